```python
import math
import jax, jax.numpy as jnp
from jax import lax
import numpy as np

D_MODEL = 1024
BATCH = 32
SEQ = 256
DEPTH = 2
DEC_BATCH = 2
DEC_SEQ = 4096
PAST_LEN = 256

GRID_W = 64
N_GROUPS = 4
GROUP_W = D_MODEL // N_GROUPS
D_MIX = N_GROUPS * GROUP_W
HEAD_DIM = 64
N_HEADS_GROUP = GROUP_W // HEAD_DIM
N_SEGMENTS = 10
W_IN_COLS = N_SEGMENTS * GROUP_W
CONV_K = 31
NA_WIN_R = 8
NA_WIN_C = 16
DIFF_SUB = HEAD_DIM // 2
ROPE_BASE = 10000.0
CHUNK = 128
Q_BLOCK = 128
PEER_HEADS = 8
PEER_NKEYS = 128
PEER_EXPERTS = PEER_NKEYS * PEER_NKEYS
PEER_DKEY = 256
PEER_TOPK = 16
TOKEN_BLOCK = 128
EPS = 1e-6
NEG = -1e30

kernel_name = 'hybrid_conv_natten_diffattn_gmlp_peer_step'


def rms_norm(x, g):
    xf = x.astype(jnp.float32)
    y = xf * lax.rsqrt(jnp.mean(xf * xf, axis=-1, keepdims=True) + EPS)
    return (y * g.astype(jnp.float32)).astype(x.dtype)


def layer_norm(x, g, b):
    xf = x.astype(jnp.float32)
    mu = jnp.mean(xf, axis=-1, keepdims=True)
    xc = xf - mu
    y = xc * lax.rsqrt(jnp.mean(xc * xc, axis=-1, keepdims=True) + EPS)
    return (y * g.astype(jnp.float32) + b.astype(jnp.float32)).astype(x.dtype)


def adaln(cond, w_mod, b_mod):
    m = jax.nn.silu(cond) @ w_mod + b_mod
    return tuple(jnp.split(m[:, None, :], 6, axis=-1))


def split_heads(x):
    B, L, _ = x.shape
    return x.reshape(B, L, N_HEADS_GROUP, HEAD_DIM).transpose(0, 2, 1, 3)


def merge_heads(x):
    B, H, L, d = x.shape
    return x.transpose(0, 2, 1, 3).reshape(B, L, H * d)


def rope_1d(x, pos):
    half = x.shape[-1] // 2
    inv = ROPE_BASE ** (-jnp.arange(half, dtype=jnp.float32) / half)
    ang = pos[:, None] * inv[None, :]
    cos = jnp.cos(ang).astype(x.dtype)
    sin = jnp.sin(ang).astype(x.dtype)
    x1, x2 = x[..., :half], x[..., half:]
    return jnp.concatenate([x1 * cos - x2 * sin, x1 * sin + x2 * cos], axis=-1)


def axial_rope(x):
    t = jnp.arange(x.shape[-2])
    rows = (t // GRID_W).astype(jnp.float32)
    cols = (t % GRID_W).astype(jnp.float32)
    h = x.shape[-1] // 2
    return jnp.concatenate([rope_1d(x[..., :h], rows), rope_1d(x[..., h:], cols)], axis=-1)


def sweep_queries(fn, *qs):
    B, H, L, _ = qs[0].shape
    nb = L // Q_BLOCK
    blocks = tuple(q.reshape(B, H, nb, Q_BLOCK, q.shape[-1]).transpose(2, 0, 1, 3, 4) for q in qs)
    out = lax.map(lambda qb: fn(*qb), blocks)
    return out.transpose(1, 2, 0, 3, 4).reshape(B, H, L, out.shape[-1])


def softmax_attend(q, k, v):
    scale = q.shape[-1] ** -0.5
    def blk(qb):
        s = jnp.einsum('bhqd,bhkd->bhqk', qb, k).astype(jnp.float32) * scale
        p = jax.nn.softmax(s, axis=-1)
        return jnp.einsum('bhqk,bhkd->bhqd', p.astype(v.dtype), v)
    return sweep_queries(blk, q)


def diff_attend(q1, q2, k1, k2, v, lam):
    scale = DIFF_SUB ** -0.5
    def blk(q1b, q2b):
        s1 = jnp.einsum('bhqd,bhkd->bhqk', q1b, k1).astype(jnp.float32) * scale
        s2 = jnp.einsum('bhqd,bhkd->bhqk', q2b, k2).astype(jnp.float32) * scale
        a = jax.nn.softmax(s1, axis=-1) - lam * jax.nn.softmax(s2, axis=-1)
        return jnp.einsum('bhqk,bhkd->bhqd', a.astype(v.dtype), v)
    return sweep_queries(blk, q1, q2)


def neighbourhood_attend(q, k, v, ck, cv, rel_bias):
    B, H, S, d = q.shape
    rows = S // GRID_W
    wr = min(NA_WIN_R, rows)
    r = jnp.arange(rows)
    r0 = jnp.clip(r - wr // 2, 0, rows - wr)
    row_idx = r0[:, None] + jnp.arange(wr)[None, :]
    dr = row_idx - r[:, None]
    c = jnp.arange(GRID_W)
    c0 = jnp.clip(c - NA_WIN_C // 2, 0, GRID_W - NA_WIN_C)
    in_win = (c[None, :] >= c0[:, None]) & (c[None, :] < c0[:, None] + NA_WIN_C)
    dc_idx = jnp.clip(c[None, :] - c[:, None] + NA_WIN_C - 1, 0, 2 * NA_WIN_C - 2)
    rb = rel_bias[:, dr + NA_WIN_R - 1]
    bias = jnp.take(rb, dc_idx, axis=-1).transpose(0, 1, 3, 2, 4)
    qg = q.reshape(B, H, rows, GRID_W, d)
    kg = k.reshape(B, H, rows, GRID_W, d)[:, :, row_idx]
    vg = v.reshape(B, H, rows, GRID_W, d)[:, :, row_idx]
    scale = d ** -0.5
    s_loc = jnp.einsum('bhrqd,bhrwkd->bhrqwk', qg, kg).astype(jnp.float32) * scale + bias[None].astype(jnp.float32)
    s_loc = jnp.where(in_win[:, None, :], s_loc, NEG)
    s_ctx = jnp.einsum('bhrqd,bhmd->bhrqm', qg, ck).astype(jnp.float32) * scale
    n_loc = wr * GRID_W
    s = jnp.concatenate([s_loc.reshape(B, H, rows, GRID_W, n_loc), s_ctx], axis=-1)
    p = jax.nn.softmax(s, axis=-1).astype(v.dtype)
    p_loc = p[..., :n_loc].reshape(B, H, rows, GRID_W, wr, GRID_W)
    o = (jnp.einsum('bhrqwk,bhrwkd->bhrqd', p_loc, vg)
         + jnp.einsum('bhrqm,bhmd->bhrqd', p[..., n_loc:], cv))
    return o.reshape(B, H, S, d)


def conv_module(a, gate, w, b, ln_g, ln_b):
    y = a * jax.nn.sigmoid(gate)
    y = lax.conv_general_dilated(y, w[:, None, :], window_strides=(1,), padding='SAME',
                                 dimension_numbers=('NWC', 'WIO', 'NWC'),
                                 feature_group_count=GROUP_W) + b
    return jax.nn.silu(layer_norm(y, ln_g, ln_b))


def chunk_gmlp(u, v, ln_g, ln_b, ws, bs):
    u = jax.nn.gelu(u)
    v = layer_norm(jax.nn.gelu(v), ln_g, ln_b)
    B, L, _ = v.shape
    vg = v.reshape(B, L // CHUNK, CHUNK, N_HEADS_GROUP, GROUP_W // N_HEADS_GROUP)
    s = jnp.einsum('gij,bnjgc->bnigc', ws, vg) + bs.T[None, None, :, :, None]
    return u * s.reshape(B, L, GROUP_W)


def peer_ffn(h, wq, sub_keys, pu, pv):
    B, L, D = h.shape
    hb = h.reshape((B * L) // TOKEN_BLOCK, TOKEN_BLOCK, D)
    def blk(xb):
        q = (xb @ wq).reshape(TOKEN_BLOCK, PEER_HEADS, 2, PEER_DKEY // 2)
        s = jnp.einsum('thpk,pnk->thpn', q, sub_keys).astype(jnp.float32)
        sv, si = lax.top_k(s, PEER_TOPK)
        cand_s = (sv[:, :, 0, :, None] + sv[:, :, 1, None, :]).reshape(TOKEN_BLOCK, PEER_HEADS, PEER_TOPK * PEER_TOPK)
        cand_i = (si[:, :, 0, :, None] * PEER_NKEYS + si[:, :, 1, None, :]).reshape(TOKEN_BLOCK, PEER_HEADS, PEER_TOPK * PEER_TOPK)
        top_s, pos = lax.top_k(cand_s, PEER_TOPK)
        eidx = jnp.take_along_axis(cand_i, pos, axis=-1)
        g = jax.nn.softmax(top_s, axis=-1)
        a = jnp.einsum('thkd,td->thk', pu[eidx], xb)
        w = (jax.nn.gelu(a.astype(jnp.float32)) * g).astype(xb.dtype)
        return jnp.einsum('thk,thkd->td', w, pv[eidx])
    return lax.map(blk, hb).reshape(B, L, D)


def trunk_layer(x, mod, p, lam_init, ctx_kv=None):
    latent = ctx_kv is not None
    B, L, _ = x.shape
    sh1, sc1, g1, sh2, sc2, g2 = mod
    h = rms_norm(x, p['norm1']) * (1 + sc1) + sh1
    parts = jnp.split(h @ p['w_in'], N_SEGMENTS, axis=-1)
    conv_o = conv_module(parts[0], parts[1], p['conv_w'], p['conv_b'], p['conv_ln_g'], p['conv_ln_b'])
    nq = rms_norm(split_heads(parts[2]), p['na_qn'])
    nk = rms_norm(split_heads(parts[3]), p['na_kn'])
    nv = split_heads(parts[4])
    if latent:
        na_o = neighbourhood_attend(nq, nk, nv, ctx_kv[0], ctx_kv[1], p['na_bias'])
    else:
        na_o = softmax_attend(nq, nk, nv)
    sub = (B, N_HEADS_GROUP, L, 2, DIFF_SUB)
    dq = rms_norm(split_heads(parts[5]).reshape(sub), p['diff_qn'])
    dk = rms_norm(split_heads(parts[6]).reshape(sub), p['diff_kn'])
    dv = split_heads(parts[7])
    dq1, dq2, dk1, dk2 = dq[..., 0, :], dq[..., 1, :], dk[..., 0, :], dk[..., 1, :]
    if latent:
        ck, cv = ctx_kv[2], ctx_kv[3]
        dq1, dq2 = axial_rope(dq1), axial_rope(dq2)
        dk1 = jnp.concatenate([axial_rope(dk1), ck[..., :DIFF_SUB]], axis=2)
        dk2 = jnp.concatenate([axial_rope(dk2), ck[..., DIFF_SUB:]], axis=2)
        dvals = jnp.concatenate([dv, cv], axis=2)
    else:
        dvals = dv
    dl = p['diff_lambda'].astype(jnp.float32)
    lam = jnp.exp(jnp.sum(dl[0] * dl[1])) - jnp.exp(jnp.sum(dl[2] * dl[3])) + lam_init
    diff_o = diff_attend(dq1, dq2, dk1, dk2, dvals, lam)
    diff_o = rms_norm(diff_o, p['diff_subln']) * (1.0 - lam_init)
    gm_o = chunk_gmlp(parts[8], parts[9], p['gmlp_ln_g'], p['gmlp_ln_b'], p['gmlp_ws'], p['gmlp_bs'])
    mixed = jnp.concatenate([conv_o, merge_heads(na_o), merge_heads(diff_o), gm_o], axis=-1) @ p['w_out']
    x = x + g1 * mixed
    h2 = rms_norm(x, p['norm2']) * (1 + sc2) + sh2
    x = x + g2 * peer_ffn(h2, p['peer_wq'], p['peer_sub_keys'], p['peer_u'], p['peer_v'])
    if latent:
        return x, None
    dk_flat = dk.reshape(B, N_HEADS_GROUP, L, HEAD_DIM)
    return x, (jnp.stack([nk, nv], axis=1), jnp.stack([dk_flat, dv], axis=1))


def setup_inputs(seed: int = 0) -> dict:
    key = jax.random.key(seed)
    ks = iter(jax.random.split(key, 40))
    def nrm(shape, s):
        return jax.random.normal(next(ks), shape, jnp.float32) * s
    cache_shape = (DEC_BATCH, DEPTH, 2, N_HEADS_GROUP, PAST_LEN, HEAD_DIM)
    return {
        'x_prompt': nrm((BATCH, SEQ, D_MODEL), 1.0),
        'x_sample': nrm((DEC_BATCH, DEC_SEQ, D_MODEL), 1.0),
        'cache_na_kv': nrm(cache_shape, 1.0),
        'cache_diff_kv': nrm(cache_shape, 1.0),
        'c': nrm((DEC_BATCH, D_MODEL), 1.0),
        'c_ctx': nrm((D_MODEL,), 1.0),
        'w_mod': nrm((DEPTH, D_MODEL, 6 * D_MODEL), 0.5 * D_MODEL ** -0.5),
        'b_mod': nrm((DEPTH, 6 * D_MODEL), 0.02),
        'norm1_g': 1.0 + nrm((DEPTH, D_MODEL), 0.1),
        'norm2_g': 1.0 + nrm((DEPTH, D_MODEL), 0.1),
        'w_in': nrm((DEPTH, D_MODEL, W_IN_COLS), D_MODEL ** -0.5),
        'conv_w': nrm((DEPTH, CONV_K, GROUP_W), CONV_K ** -0.5),
        'conv_b': nrm((DEPTH, GROUP_W), 0.02),
        'conv_ln_g': 1.0 + nrm((DEPTH, GROUP_W), 0.1),
        'conv_ln_b': nrm((DEPTH, GROUP_W), 0.02),
        'na_qn_g': 1.0 + nrm((DEPTH, HEAD_DIM), 0.1),
        'na_kn_g': 1.0 + nrm((DEPTH, HEAD_DIM), 0.1),
        'na_rel_bias': nrm((DEPTH, N_HEADS_GROUP, 2 * NA_WIN_R - 1, 2 * NA_WIN_C - 1), 0.1),
        'diff_qn_g': 1.0 + nrm((DEPTH, DIFF_SUB), 0.1),
        'diff_kn_g': 1.0 + nrm((DEPTH, DIFF_SUB), 0.1),
        'diff_lambda': nrm((DEPTH, 4, DIFF_SUB), 0.1),
        'diff_subln_g': 1.0 + nrm((DEPTH, HEAD_DIM), 0.1),
        'gmlp_ln_g': 1.0 + nrm((DEPTH, GROUP_W), 0.1),
        'gmlp_ln_b': nrm((DEPTH, GROUP_W), 0.02),
        'gmlp_ws': nrm((DEPTH, N_HEADS_GROUP, CHUNK, CHUNK), CHUNK ** -0.5),
        'gmlp_bs': 1.0 + nrm((DEPTH, N_HEADS_GROUP, CHUNK), 0.1),
        'w_out': nrm((DEPTH, D_MIX, D_MODEL), D_MIX ** -0.5),
        'peer_wq': nrm((DEPTH, D_MODEL, PEER_HEADS * PEER_DKEY), D_MODEL ** -0.5),
        'peer_sub_keys': nrm((DEPTH, 2, PEER_NKEYS, PEER_DKEY // 2), (PEER_DKEY // 2) ** -0.5),
        'peer_u': nrm((DEPTH, PEER_EXPERTS, D_MODEL), D_MODEL ** -0.5),
        'peer_v': nrm((DEPTH, PEER_EXPERTS, D_MODEL), 0.25),
    }


def reference(x_prompt, x_sample, cache_na_kv, cache_diff_kv, c, c_ctx, w_mod, b_mod, norm1_g, norm2_g,
              w_in, conv_w, conv_b, conv_ln_g, conv_ln_b, na_qn_g, na_kn_g, na_rel_bias, diff_qn_g,
              diff_kn_g, diff_lambda, diff_subln_g, gmlp_ln_g, gmlp_ln_b, gmlp_ws, gmlp_bs, w_out,
              peer_wq, peer_sub_keys, peer_u, peer_v):
    y_prompt, y_sample = x_prompt, x_sample
    na_states, diff_states = [], []
    for l in range(DEPTH):
        p = {
            'norm1': norm1_g[l], 'norm2': norm2_g[l], 'w_in': w_in[l],
            'conv_w': conv_w[l], 'conv_b': conv_b[l], 'conv_ln_g': conv_ln_g[l], 'conv_ln_b': conv_ln_b[l],
            'na_qn': na_qn_g[l], 'na_kn': na_kn_g[l], 'na_bias': na_rel_bias[l],
            'diff_qn': diff_qn_g[l], 'diff_kn': diff_kn_g[l], 'diff_lambda': diff_lambda[l],
            'diff_subln': diff_subln_g[l],
            'gmlp_ln_g': gmlp_ln_g[l], 'gmlp_ln_b': gmlp_ln_b[l], 'gmlp_ws': gmlp_ws[l], 'gmlp_bs': gmlp_bs[l],
            'w_out': w_out[l], 'peer_wq': peer_wq[l], 'peer_sub_keys': peer_sub_keys[l],
            'peer_u': peer_u[l], 'peer_v': peer_v[l],
        }
        lam_init = 0.8 - 0.6 * math.exp(-0.3 * l)
        mod_ctx = adaln(c_ctx[None, :], w_mod[l], b_mod[l])
        y_prompt, (na_kv, diff_kv) = trunk_layer(y_prompt, mod_ctx, p, lam_init)
        na_states.append(na_kv)
        diff_states.append(diff_kv)
        mod_lat = adaln(c, w_mod[l], b_mod[l])
        ctx = (cache_na_kv[:, l, 0], cache_na_kv[:, l, 1], cache_diff_kv[:, l, 0], cache_diff_kv[:, l, 1])
        y_sample, _ = trunk_layer(y_sample, mod_lat, p, lam_init, ctx)
    new_na_kv = jnp.stack(na_states, axis=1)
    new_diff_kv = jnp.stack(diff_states, axis=1)
    return (y_prompt, y_sample, new_na_kv, new_diff_kv)
```

```python
import functools
import math

import numpy as np
import jax
import jax.numpy as jnp
from jax import lax
from jax.experimental import pallas as pl
from jax.experimental.pallas import tpu as pltpu

F32 = jnp.float32
BF16 = jnp.bfloat16

D_MODEL = 1024
DEPTH = 2
GRID_W = 64
GROUP_W = 256
HEAD_DIM = 64
N_HEADS = 4
DIFF_SUB = 32
CONV_K = 31
CONV_HALO = 16
NA_WIN_R = 8
NA_WIN_C = 16
ROPE_BASE = 10000.0
CHUNK = 128
PEER_HEADS = 8
PEER_NKEYS = 128
PEER_TOPK = 16
EPS = 1e-6
NEG = -1e30

LANES = 128
SUBLANES = 8
VMEM_LIMIT = 48 * 1024 * 1024

SEQ_BLOCK = 256
ROW_BLOCK = 256
ROUTE_BLOCK = 128
PEER_TOKENS = 512
PEER_IBLOCK = 8
DIFF_QBLOCK = 128


def _params(*sem):
    return pltpu.CompilerParams(dimension_semantics=sem, vmem_limit_bytes=VMEM_LIMIT)


def _dot(a, b):
    return jnp.dot(a, b, preferred_element_type=F32)


def _dot_nt(a, b):
    return lax.dot_general(a, b, (((1,), (1,)), ((), ())), preferred_element_type=F32)


def _split_dot(a, b):
    a1 = a.astype(BF16)
    r1 = a - a1.astype(F32)
    a2 = r1.astype(BF16)
    a3 = (r1 - a2.astype(F32)).astype(BF16)
    return _dot(a1, b) + _dot(a2, b) + _dot(a3, b)


def _group_ones(n, group):
    r = lax.broadcasted_iota(jnp.int32, (n, n), 0) // group
    c = lax.broadcasted_iota(jnp.int32, (n, n), 1) // group
    return (r == c).astype(BF16)


def _group_rms(x, gain, group):
    ssq = _split_dot(x * x, _group_ones(x.shape[-1], group))
    return x * lax.rsqrt(ssq * (1.0 / group) + EPS) * gain


def _layer_norm(y, g, b):
    mu = jnp.mean(y, axis=-1, keepdims=True)
    yc = y - mu
    return yc * lax.rsqrt(jnp.mean(yc * yc, axis=-1, keepdims=True) + EPS) * g + b


def _mod_spec(rows_per_block, layout):
    t_ctx, lat_seq = layout
    assert t_ctx % rows_per_block == 0 and lat_seq % rows_per_block == 0
    def index(i, *_):
        return (jnp.maximum((i * rows_per_block - t_ctx) // lat_seq + 1, 0), 0, 0)
    return pl.BlockSpec((1, 6, D_MODEL), index)


def _mod_kernel(c_ref, w_ref, b_ref, o_ref):
    c = c_ref[...]
    s = c * jax.nn.sigmoid(c)
    o_ref[0] = jnp.dot(s, w_ref[0], preferred_element_type=F32,
                       precision=lax.Precision.HIGHEST) + b_ref[0]


def _modulation(cond, w_mod, b_mod):
    nb = 4
    cols = 6 * D_MODEL // nb
    return pl.pallas_call(
        _mod_kernel,
        grid=(DEPTH, nb),
        in_specs=[pl.BlockSpec((SUBLANES, D_MODEL), lambda l, j: (0, 0)),
                  pl.BlockSpec((1, D_MODEL, cols), lambda l, j: (l, 0, j)),
                  pl.BlockSpec((1, 1, cols), lambda l, j: (l, 0, j))],
        out_specs=pl.BlockSpec((1, SUBLANES, cols), lambda l, j: (l, 0, j)),
        out_shape=jax.ShapeDtypeStruct((DEPTH, SUBLANES, 6 * D_MODEL), F32),
        name="modulation",
        compiler_params=_params("arbitrary", "arbitrary"),
    )(cond, w_mod, b_mod.reshape(DEPTH, 1, 6 * D_MODEL))


def _in_proj_kernel(x_ref, mod_ref, g_ref, w_ref, o_ref):
    x = x_ref[...]
    y = x * lax.rsqrt(jnp.mean(x * x, axis=-1, keepdims=True) + EPS) * g_ref[...]
    h = y * (1.0 + mod_ref[0, 1:2, :]) + mod_ref[0, 0:1, :]
    o_ref[...] = _dot(h.astype(BF16), w_ref[...])


def _in_proj(x, mod, g, w, layout):
    t = x.shape[0]
    n = w.shape[1]
    return pl.pallas_call(
        _in_proj_kernel,
        grid=(t // ROW_BLOCK,),
        in_specs=[pl.BlockSpec((ROW_BLOCK, D_MODEL), lambda i: (i, 0)),
                  _mod_spec(ROW_BLOCK, layout),
                  pl.BlockSpec((1, D_MODEL), lambda i: (0, 0)),
                  pl.BlockSpec((D_MODEL, n), lambda i: (0, 0))],
        out_specs=pl.BlockSpec((ROW_BLOCK, n), lambda i: (i, 0)),
        out_shape=jax.ShapeDtypeStruct((t, n), F32),
        name="in_proj",
        compiler_params=_params("arbitrary"),
    )(x, mod, g, w)


def _conv_kernel(ac, gc, ap, gp, an, gn, w_ref, b_ref, lg_ref, lb_ref, o_ref, pad_ref, *, ctx_blocks, seq_blocks):
    i = pl.program_id(0)
    is_ctx = i < ctx_blocks
    j = i % seq_blocks
    first = jnp.logical_or(is_ctx, j == 0)
    last = jnp.logical_or(is_ctx, j == seq_blocks - 1)
    yp = ap[...] * jax.nn.sigmoid(gp[...])
    yn = an[...] * jax.nn.sigmoid(gn[...])
    pad_ref[0:CONV_HALO, :] = jnp.where(first, 0.0, yp)
    pad_ref[CONV_HALO:CONV_HALO + SEQ_BLOCK, :] = ac[...] * jax.nn.sigmoid(gc[...])
    pad_ref[CONV_HALO + SEQ_BLOCK:, :] = jnp.where(last, 0.0, yn)
    off = CONV_HALO - CONV_K // 2
    acc = jnp.zeros((SEQ_BLOCK, GROUP_W), F32)
    for k in range(CONV_K):
        acc = acc + pad_ref[off + k:off + k + SEQ_BLOCK, :] * w_ref[k:k + 1, :]
    y = _layer_norm(acc + b_ref[...], lg_ref[...], lb_ref[...])
    o_ref[...] = y * jax.nn.sigmoid(y)


def _conv_module(parts, w, b, lg, lb, ctx_tokens, lat_seq):
    t = parts.shape[0]
    nblk = t // SEQ_BLOCK
    hb = SEQ_BLOCK // CONV_HALO
    last_halo = t // CONV_HALO - 1
    vec = pl.BlockSpec((1, GROUP_W), lambda i: (0, 0))
    kern = functools.partial(_conv_kernel, ctx_blocks=ctx_tokens // SEQ_BLOCK, seq_blocks=lat_seq // SEQ_BLOCK)
    return pl.pallas_call(
        kern,
        grid=(nblk,),
        in_specs=[pl.BlockSpec((SEQ_BLOCK, GROUP_W), lambda i: (i, 0)),
                  pl.BlockSpec((SEQ_BLOCK, GROUP_W), lambda i: (i, 1)),
                  pl.BlockSpec((CONV_HALO, GROUP_W), lambda i: (jnp.maximum(i * hb - 1, 0), 0)),
                  pl.BlockSpec((CONV_HALO, GROUP_W), lambda i: (jnp.maximum(i * hb - 1, 0), 1)),
                  pl.BlockSpec((CONV_HALO, GROUP_W), lambda i: (jnp.minimum((i + 1) * hb, last_halo), 0)),
                  pl.BlockSpec((CONV_HALO, GROUP_W), lambda i: (jnp.minimum((i + 1) * hb, last_halo), 1)),
                  pl.BlockSpec((CONV_K, GROUP_W), lambda i: (0, 0)),
                  vec, vec, vec],
        out_specs=pl.BlockSpec((SEQ_BLOCK, GROUP_W), lambda i: (i, 0)),
        out_shape=jax.ShapeDtypeStruct((t, GROUP_W), F32),
        scratch_shapes=[pltpu.VMEM((SEQ_BLOCK + 2 * CONV_HALO, GROUP_W), F32)],
        name="conv_module",
        compiler_params=_params("arbitrary"),
    )(parts, parts, parts, parts, parts, parts, w, b, lg, lb)


def _gmlp_kernel(u_ref, v_ref, lg_ref, lb_ref, ws_ref, bias_ref, o_ref):
    u = jax.nn.gelu(u_ref[...])
    v = _layer_norm(jax.nn.gelu(v_ref[...]), lg_ref[...], lb_ref[...]).astype(BF16)
    lane_g = lax.broadcasted_iota(jnp.int32, (1, GROUP_W), 1) // (GROUP_W // N_HEADS)
    for ch in range(SEQ_BLOCK // CHUNK):
        rows = slice(ch * CHUNK, (ch + 1) * CHUNK)
        s = bias_ref[...]
        for g in range(N_HEADS):
            s = s + jnp.where(lane_g == g, _dot(ws_ref[g], v[rows]), 0.0)
        o_ref[rows, :] = u[rows] * s


def _gmlp(parts, lg, lb, ws, bias):
    t = parts.shape[0]
    vec = pl.BlockSpec((1, GROUP_W), lambda i: (0, 0))
    return pl.pallas_call(
        _gmlp_kernel,
        grid=(t // SEQ_BLOCK,),
        in_specs=[pl.BlockSpec((SEQ_BLOCK, GROUP_W), lambda i: (i, 8)),
                  pl.BlockSpec((SEQ_BLOCK, GROUP_W), lambda i: (i, 9)),
                  vec, vec,
                  pl.BlockSpec((N_HEADS, CHUNK, CHUNK), lambda i: (0, 0, 0)),
                  pl.BlockSpec((CHUNK, GROUP_W), lambda i: (0, 0))],
        out_specs=pl.BlockSpec((SEQ_BLOCK, GROUP_W), lambda i: (i, 0)),
        out_shape=jax.ShapeDtypeStruct((t, GROUP_W), F32),
        compiler_params=_params("arbitrary"),
        name="gmlp",
    )(parts, parts, lg, lb, ws, bias)


def _lane_group_id(width, group):
    return lax.broadcasted_iota(jnp.int32, (1, width), 1) // group


def _stack_masked(q, group, ids):
    lane = _lane_group_id(q.shape[-1], group)
    return jnp.concatenate([jnp.where(lane == g, q, jnp.zeros_like(q)) for g in ids], axis=0)


def _diff_lambda(dl_ref, lam_init):
    dl = dl_ref[...]
    a = jnp.sum(dl[0:1, :] * dl[1:2, :], axis=-1, keepdims=True)
    b = jnp.sum(dl[2:3, :] * dl[3:4, :], axis=-1, keepdims=True)
    return jnp.exp(a) - jnp.exp(b) + lam_init


def _ctx_attn_kernel(q_ref, k_ref, v_ref, dq_ref, dk_ref, dv_ref, qn_ref, kn_ref, dqn_ref, dkn_ref, sub_ref,
                     dl_ref, na_ref, df_ref, nk_ref, dkf_ref, *, lam_init):
    n = q_ref.shape[0]
    lane_h = _lane_group_id(GROUP_W, HEAD_DIM)
    qn = _group_rms(q_ref[...], qn_ref[...], HEAD_DIM)
    kn = _group_rms(k_ref[...], kn_ref[...], HEAD_DIM)
    nk_ref[...] = kn
    qs = _stack_masked(qn.astype(BF16), HEAD_DIM, range(N_HEADS))
    s = _dot_nt(qs, kn.astype(BF16)) * (HEAD_DIM ** -0.5)
    e = jnp.exp(s - jnp.max(s, axis=-1, keepdims=True))
    r = _dot(e.astype(BF16), v_ref[...].astype(BF16)) / jnp.sum(e, axis=-1, keepdims=True)
    o = jnp.zeros((n, GROUP_W), F32)
    for h in range(N_HEADS):
        o = o + jnp.where(lane_h == h, r[h * n:(h + 1) * n], 0.0)
    na_ref[...] = o
    lam = _diff_lambda(dl_ref, lam_init)
    dq = _group_rms(dq_ref[...], dqn_ref[...], DIFF_SUB)
    dk = _group_rms(dk_ref[...], dkn_ref[...], DIFF_SUB)
    dkf_ref[...] = dk
    dkb = dk.astype(BF16)
    dvb = dv_ref[...].astype(BF16)
    o = jnp.zeros((n, GROUP_W), F32)
    for h in range(N_HEADS):
        qs = _stack_masked(dq.astype(BF16), DIFF_SUB, (2 * h, 2 * h + 1))
        s = _dot_nt(qs, dkb) * (DIFF_SUB ** -0.5)
        e = jnp.exp(s - jnp.max(s, axis=-1, keepdims=True))
        inv = 1.0 / jnp.sum(e, axis=-1, keepdims=True)
        a = e[:n] * inv[:n] - e[n:] * (lam * inv[n:])
        o = o + jnp.where(lane_h == h, _dot(a.astype(BF16), dvb), 0.0)
    df_ref[...] = _group_rms(o, sub_ref[...], HEAD_DIM) * (1.0 - lam_init)


def _ctx_attention(parts, batch, seq, qn, kn, dqn, dkn, sub, dl, lam_init):
    t = batch * seq
    vec = pl.BlockSpec((1, GROUP_W), lambda b: (0, 0))
    col = lambda c: pl.BlockSpec((seq, GROUP_W), lambda b, c=c: (b, c))
    out = jax.ShapeDtypeStruct((t, GROUP_W), F32)
    ob = pl.BlockSpec((seq, GROUP_W), lambda b: (b, 0))
    return pl.pallas_call(
        functools.partial(_ctx_attn_kernel, lam_init=lam_init),
        grid=(batch,),
        in_specs=[col(2), col(3), col(4), col(5), col(6), col(7), vec, vec, vec, vec, vec,
                  pl.BlockSpec((4, DIFF_SUB), lambda b: (0, 0))],
        out_specs=[ob, ob, ob, ob],
        out_shape=[out, out, out, out],
        compiler_params=_params("arbitrary"),
        name="ctx_attention",
    )(parts, parts, parts, parts, parts, parts, qn, kn, dqn, dkn, sub, dl)


def _lat_prep_kernel(q_ref, k_ref, v_ref, dq_ref, dk_ref, dv_ref, qn_ref, kn_ref, dqn_ref, dkn_ref,
                     cos_ref, sin_ref, oq, ok, ov, odq, odk, odv):
    oq[...] = _group_rms(q_ref[...], qn_ref[...], HEAD_DIM).astype(BF16)
    ok[...] = _group_rms(k_ref[...], kn_ref[...], HEAD_DIM).astype(BF16)
    ov[...] = v_ref[...].astype(BF16)
    odv[...] = dv_ref[...].astype(BF16)
    half = DIFF_SUB // 4
    lane = lax.broadcasted_iota(jnp.int32, (1, GROUP_W), 1)
    lower = (lane % (2 * half)) < half
    cos = cos_ref[...]
    sin = sin_ref[...]

    def rope(x):
        partner = jnp.where(lower, pltpu.roll(x, GROUP_W - half, 1), pltpu.roll(x, half, 1))
        return x * cos + partner * sin

    odq[...] = rope(_group_rms(dq_ref[...], dqn_ref[...], DIFF_SUB)).astype(BF16)
    odk[...] = rope(_group_rms(dk_ref[...], dkn_ref[...], DIFF_SUB)).astype(BF16)


def _rope_tables(seq):
    half = DIFF_SUB // 4
    t = jnp.arange(seq)
    rows = (t // GRID_W).astype(F32)
    cols = (t % GRID_W).astype(F32)
    inv = ROPE_BASE ** (-jnp.arange(half, dtype=F32) / half)
    ang_r = rows[:, None] * inv[None, :]
    ang_c = cols[:, None] * inv[None, :]
    cos32 = jnp.concatenate([jnp.cos(ang_r), jnp.cos(ang_r), jnp.cos(ang_c), jnp.cos(ang_c)], axis=-1)
    sin32 = jnp.concatenate([-jnp.sin(ang_r), jnp.sin(ang_r), -jnp.sin(ang_c), jnp.sin(ang_c)], axis=-1)
    reps = GROUP_W // DIFF_SUB
    return jnp.tile(cos32, (1, reps)), jnp.tile(sin32, (1, reps))


def _lat_prep(parts, row0, batch, seq, qn, kn, dqn, dkn, cos, sin):
    t = batch * seq
    nb = seq // SEQ_BLOCK
    r0 = row0 // SEQ_BLOCK
    vec = pl.BlockSpec((1, GROUP_W), lambda i: (0, 0))
    col = lambda c: pl.BlockSpec((SEQ_BLOCK, GROUP_W), lambda i, c=c: (r0 + i, c))
    tab = pl.BlockSpec((SEQ_BLOCK, GROUP_W), lambda i: (i % nb, 0))
    out = jax.ShapeDtypeStruct((t, GROUP_W), BF16)
    ob = pl.BlockSpec((SEQ_BLOCK, GROUP_W), lambda i: (i, 0))
    return pl.pallas_call(
        _lat_prep_kernel,
        grid=(t // SEQ_BLOCK,),
        in_specs=[col(2), col(3), col(4), col(5), col(6), col(7), vec, vec, vec, vec, tab, tab],
        out_specs=[ob] * 6,
        out_shape=[out] * 6,
        compiler_params=_params("arbitrary"),
        name="lat_prep",
    )(parts, parts, parts, parts, parts, parts, qn, kn, dqn, dkn, cos, sin)


def _na_row_start(r, rows):
    return jnp.clip(r - NA_WIN_R // 2, 0, rows - NA_WIN_R)


def _na_lat_kernel(q_ref, k_ref, v_ref, ck_ref, cv_ref, bt_ref, o_ref, *, rows):
    r = pl.program_id(1)
    start = pl.multiple_of(_na_row_start(r, rows) * GRID_W, GRID_W)
    n_loc = NA_WIN_R * GRID_W
    kl = k_ref[pl.ds(start, n_loc), :]
    vl = v_ref[pl.ds(start, n_loc), :]
    qs = _stack_masked(q_ref[...], HEAD_DIM, range(N_HEADS))
    scale = HEAD_DIM ** -0.5
    s_loc = _dot_nt(qs, kl) * scale + bt_ref[0]
    s_ctx = _dot_nt(qs, ck_ref[0]) * scale
    m = jnp.maximum(jnp.max(s_loc, axis=-1, keepdims=True), jnp.max(s_ctx, axis=-1, keepdims=True))
    e_loc = jnp.exp(s_loc - m)
    e_ctx = jnp.exp(s_ctx - m)
    l = jnp.sum(e_loc, axis=-1, keepdims=True) + jnp.sum(e_ctx, axis=-1, keepdims=True)
    res = (_dot(e_loc.astype(BF16), vl) + _dot(e_ctx.astype(BF16), cv_ref[0])) / l
    lane_h = _lane_group_id(GROUP_W, HEAD_DIM)
    o = jnp.zeros((GRID_W, GROUP_W), F32)
    for h in range(N_HEADS):
        o = o + jnp.where(lane_h == h, res[h * GRID_W:(h + 1) * GRID_W], 0.0)
    o_ref[...] = o


def _na_bias_table(rel_bias, rows):
    wr = NA_WIN_R
    c = np.arange(GRID_W)
    c0 = np.clip(c - NA_WIN_C // 2, 0, GRID_W - NA_WIN_C)
    in_win = (c[None, :] >= c0[:, None]) & (c[None, :] < c0[:, None] + NA_WIN_C)
    dc_idx = np.clip(c[None, :] - c[:, None] + NA_WIN_C - 1, 0, 2 * NA_WIN_C - 2)
    off = np.arange(wr)
    dr_idx = (np.arange(wr)[None, :] - off[:, None]) + NA_WIN_R - 1
    rb = rel_bias[:, dr_idx]
    bias = jnp.take(rb, jnp.asarray(dc_idx), axis=-1)
    bias = jnp.where(jnp.asarray(in_win)[None, None, None], bias, NEG)
    bias = bias.transpose(1, 0, 3, 2, 4)
    return bias.reshape(wr, N_HEADS * GRID_W, wr * GRID_W)


def _na_latent(qn, kn, v, ck, cv, bias_tab, batch, seq):
    rows = seq // GRID_W
    nctx = ck.shape[1]
    full = pl.BlockSpec((seq, GROUP_W), lambda b, r: (b, 0))
    ctx = pl.BlockSpec((1, nctx, GROUP_W), lambda b, r: (b, 0, 0))
    return pl.pallas_call(
        functools.partial(_na_lat_kernel, rows=rows),
        grid=(batch, rows),
        in_specs=[pl.BlockSpec((GRID_W, GROUP_W), lambda b, r: (b * rows + r, 0)),
                  full, full, ctx, ctx,
                  pl.BlockSpec((1, N_HEADS * GRID_W, NA_WIN_R * GRID_W),
                               lambda b, r: (r - _na_row_start(r, rows), 0, 0))],
        out_specs=pl.BlockSpec((GRID_W, GROUP_W), lambda b, r: (b * rows + r, 0)),
        out_shape=jax.ShapeDtypeStruct((batch * seq, GROUP_W), F32),
        compiler_params=_params("arbitrary", "arbitrary"),
        name="na_latent",
    )(qn, kn, v, ck, cv, bias_tab)


def _diff_lat_kernel(q_ref, k_ref, v_ref, ck_ref, cv_ref, sub_ref, dl_ref, o_ref, *, lam_init):
    n = q_ref.shape[0]
    lam = _diff_lambda(dl_ref, lam_init)
    lane_s = _lane_group_id(GROUP_W, DIFF_SUB)
    q = q_ref[...]
    scale = DIFF_SUB ** -0.5

    def head(h, o):
        qs = jnp.concatenate([jnp.where(lane_s == 2 * h, q, jnp.zeros_like(q)),
                              jnp.where(lane_s == 2 * h + 1, q, jnp.zeros_like(q))], axis=0)
        s_l = _dot_nt(qs, k_ref[...]) * scale
        s_c = _dot_nt(qs, ck_ref[0]) * scale
        m = jnp.maximum(jnp.max(s_l, axis=-1, keepdims=True), jnp.max(s_c, axis=-1, keepdims=True))
        e_l = jnp.exp(s_l - m)
        e_c = jnp.exp(s_c - m)
        inv = 1.0 / (jnp.sum(e_l, axis=-1, keepdims=True) + jnp.sum(e_c, axis=-1, keepdims=True))
        c1 = inv[:n]
        c2 = lam * inv[n:]
        a_l = e_l[:n] * c1 - e_l[n:] * c2
        a_c = e_c[:n] * c1 - e_c[n:] * c2
        res = _dot(a_l.astype(BF16), v_ref[...]) + _dot(a_c.astype(BF16), cv_ref[0])
        return o + jnp.where(lane_s // 2 == h, res, 0.0)

    o = lax.fori_loop(0, N_HEADS, head, jnp.zeros((n, GROUP_W), F32))
    o_ref[...] = _group_rms(o, sub_ref[...], HEAD_DIM) * (1.0 - lam_init)


def _diff_latent(dq, dk, dv, ck, cv, sub, dl, batch, seq, lam_init):
    nq = seq // DIFF_QBLOCK
    nctx = ck.shape[1]
    full = pl.BlockSpec((seq, GROUP_W), lambda b, i: (b, 0))
    ctx = pl.BlockSpec((1, nctx, GROUP_W), lambda b, i: (b, 0, 0))
    return pl.pallas_call(
        functools.partial(_diff_lat_kernel, lam_init=lam_init),
        grid=(batch, nq),
        in_specs=[pl.BlockSpec((DIFF_QBLOCK, GROUP_W), lambda b, i: (b * nq + i, 0)),
                  full, full, ctx, ctx,
                  pl.BlockSpec((1, GROUP_W), lambda b, i: (0, 0)),
                  pl.BlockSpec((4, DIFF_SUB), lambda b, i: (0, 0))],
        out_specs=pl.BlockSpec((DIFF_QBLOCK, GROUP_W), lambda b, i: (b * nq + i, 0)),
        out_shape=jax.ShapeDtypeStruct((batch * seq, GROUP_W), F32),
        compiler_params=_params("arbitrary", "arbitrary"),
        name="diff_latent",
    )(dq, dk, dv, ck, cv, sub, dl)


def _out_proj_kernel(a_ref, b_ref, c_ref, d_ref, x_ref, mod_ref, w_ref, o_ref):
    mixed = jnp.zeros(o_ref.shape, F32)
    for g, ref in enumerate((a_ref, b_ref, c_ref, d_ref)):
        mixed = mixed + _dot(ref[...].astype(BF16), w_ref[g * GROUP_W:(g + 1) * GROUP_W, :])
    o_ref[...] = x_ref[...] + mod_ref[0, 2:3, :] * mixed


def _out_proj(conv_o, na_o, diff_o, gm_o, x, mod, w, layout):
    t = x.shape[0]
    part = pl.BlockSpec((ROW_BLOCK, GROUP_W), lambda i: (i, 0))
    row = pl.BlockSpec((ROW_BLOCK, D_MODEL), lambda i: (i, 0))
    return pl.pallas_call(
        _out_proj_kernel,
        grid=(t // ROW_BLOCK,),
        in_specs=[part, part, part, part, row,
                  _mod_spec(ROW_BLOCK, layout),
                  pl.BlockSpec((D_MODEL, D_MODEL), lambda i: (0, 0))],
        out_specs=row,
        out_shape=jax.ShapeDtypeStruct((t, D_MODEL), F32),
        compiler_params=_params("arbitrary"),
        name="out_proj",
    )(conv_o, na_o, diff_o, gm_o, x, mod, w)


def _sort_desc(vals):
    v = list(vals)
    n = len(v)
    k = 2
    while k <= n:
        j = k // 2
        while j >= 1:
            for i in range(n):
                l = i ^ j
                if l > i:
                    hi, lo = jnp.maximum(v[i], v[l]), jnp.minimum(v[i], v[l])
                    v[i], v[l] = (hi, lo) if (i & k) == 0 else (lo, hi)
            j //= 2
        k *= 2
    return v


def _merge_desc(v):
    v = list(v)
    n = len(v)
    j = n // 2
    while j >= 1:
        for i in range(n):
            l = i ^ j
            if l > i:
                v[i], v[l] = jnp.maximum(v[i], v[l]), jnp.minimum(v[i], v[l])
        j //= 2
    return v


def _top_half(a, b):
    n = len(a)
    return [jnp.maximum(a[i], b[n - 1 - i]) for i in range(n)]


def _top16_keys(s):
    k = PEER_TOPK
    v = _sort_desc([s[a * SUBLANES:(a + 1) * SUBLANES] for a in range(PEER_NKEYS // SUBLANES)])
    shift = SUBLANES // 2
    while shift >= 1:
        rolled = [pltpu.roll(x, shift, 0) for x in v]
        v = _merge_desc(_top_half(v, rolled))
        shift //= 2
    return v[:k]


def _route_kernel(x_ref, mod_ref, g_ref, wq_ref, sk_ref, h2_ref, s0_ref, s1_ref, e0_ref, e1_ref, thr_ref):
    x = x_ref[...]
    y = x * lax.rsqrt(jnp.mean(x * x, axis=-1, keepdims=True) + EPS) * g_ref[...]
    h2 = y * (1.0 + mod_ref[0, 4:5, :]) + mod_ref[0, 3:4, :]
    h2t = h2.T.astype(BF16)
    h2_ref[...] = h2t
    qt = _dot(wq_ref[...], h2t)
    tb = x.shape[0]
    sub = lax.broadcasted_iota(jnp.int32, (SUBLANES, tb), 0)
    k = PEER_TOPK
    tops = [[jnp.zeros((SUBLANES, tb), F32)] * k, [jnp.zeros((SUBLANES, tb), F32)] * k]
    for h in range(PEER_HEADS):
        for p in range(2):
            base = (2 * h + p) * PEER_NKEYS
            s = _dot(sk_ref[p], qt[base:base + PEER_NKEYS].astype(BF16))
            (s0_ref if p == 0 else s1_ref)[h] = s
            top = _top16_keys(s)
            tops[p] = [jnp.where(sub == h, top[a], tops[p][a]) for a in range(k)]
    cand = [tops[0][a] + tops[1][b] for a in range(k) for b in range(k) if (a + 1) * (b + 1) <= k]
    pad = [jnp.full((SUBLANES, tb), NEG, F32)] * (4 * k - len(cand))
    groups = [_sort_desc((cand + pad)[g * k:(g + 1) * k]) for g in range(4)]
    best = _top_half(_merge_desc(_top_half(groups[0], groups[1])), _merge_desc(_top_half(groups[2], groups[3])))
    thr = functools.reduce(jnp.minimum, best)
    m = tops[0][0] + tops[1][0]
    z = functools.reduce(jnp.add, [jnp.where(c >= thr, jnp.exp(c - m), 0.0) for c in cand])
    zinv = 1.0 / z
    thr_ref[...] = thr
    for h in range(PEER_HEADS):
        e0_ref[h] = jnp.exp(s0_ref[h] - tops[0][0][h:h + 1]) * zinv[h:h + 1]
        e1_ref[h] = jnp.exp(s1_ref[h] - tops[1][0][h:h + 1])


def _route(x1, mod, g, wq_t, sk, layout):
    t = x1.shape[0]
    tb = ROUTE_BLOCK
    sc = jax.ShapeDtypeStruct((PEER_HEADS, PEER_NKEYS, t), F32)
    scb = pl.BlockSpec((PEER_HEADS, PEER_NKEYS, tb), lambda i: (0, 0, i))
    return pl.pallas_call(
        _route_kernel,
        grid=(t // tb,),
        in_specs=[pl.BlockSpec((tb, D_MODEL), lambda i: (i, 0)),
                  _mod_spec(tb, layout),
                  pl.BlockSpec((1, D_MODEL), lambda i: (0, 0)),
                  pl.BlockSpec(wq_t.shape, lambda i: (0, 0)),
                  pl.BlockSpec(sk.shape, lambda i: (0, 0, 0))],
        out_specs=[pl.BlockSpec((D_MODEL, tb), lambda i: (0, i)), scb, scb, scb, scb,
                   pl.BlockSpec((PEER_HEADS, tb), lambda i: (0, i))],
        out_shape=[jax.ShapeDtypeStruct((D_MODEL, t), BF16), sc, sc, sc, sc,
                   jax.ShapeDtypeStruct((PEER_HEADS, t), F32)],
        compiler_params=_params("arbitrary"),
        name="peer_route",
    )(x1, mod, g, wq_t, sk)


def _peer_kernel(h2_ref, pu_ref, pv_ref, s0_ref, s1_ref, e0_ref, e1_ref, thr_ref, x_ref, mod_ref, o_ref,
                 acc_ref, a_ref, w_ref):
    e = pl.program_id(1)

    @pl.when(e == 0)
    def _():
        acc_ref[...] = jnp.zeros_like(acc_ref)

    a_ref[...] = jax.nn.gelu(_dot(pu_ref[...], h2_ref[...]))
    tb = h2_ref.shape[1]
    assert PEER_IBLOCK == SUBLANES
    first_keys = pl.ds(pl.multiple_of(e * PEER_IBLOCK, PEER_IBLOCK), PEER_IBLOCK)
    for c in range(tb // LANES):
        cols = slice(c * LANES, (c + 1) * LANES)
        s0 = [s0_ref[h, first_keys, cols] for h in range(PEER_HEADS)]
        e0 = [e0_ref[h, first_keys, cols] for h in range(PEER_HEADS)]
        for ii in range(PEER_IBLOCK):
            rows = slice(ii * PEER_NKEYS, (ii + 1) * PEER_NKEYS)
            gate = jnp.zeros((PEER_NKEYS, LANES), F32)
            for h in range(PEER_HEADS):
                sel = (s1_ref[h, :, cols] + s0[h][ii:ii + 1]) >= thr_ref[h:h + 1, cols]
                gate = gate + jnp.where(sel, e1_ref[h, :, cols] * e0[h][ii:ii + 1], 0.0)
            w_ref[rows, cols] = (a_ref[rows, cols] * gate).astype(BF16)
    acc_ref[...] += _dot(pv_ref[...], w_ref[...])

    @pl.when(e == pl.num_programs(1) - 1)
    def _():
        o_ref[...] = x_ref[...] + mod_ref[0, 5:6, :] * acc_ref[...].T


def _peer(h2t, pu, pv_t, s0, s1, e0, e1, thr, x1, mod, layout):
    t = x1.shape[0]
    tb = PEER_TOKENS
    eb = PEER_IBLOCK * PEER_NKEYS
    n_exp = pu.shape[0]
    sc = pl.BlockSpec((PEER_HEADS, PEER_NKEYS, tb), lambda i, e: (0, 0, i))
    return pl.pallas_call(
        _peer_kernel,
        grid=(t // tb, n_exp // eb),
        in_specs=[pl.BlockSpec((D_MODEL, tb), lambda i, e: (0, i)),
                  pl.BlockSpec((eb, D_MODEL), lambda i, e: (e, 0)),
                  pl.BlockSpec((D_MODEL, eb), lambda i, e: (0, e)),
                  sc, sc, sc, sc,
                  pl.BlockSpec((PEER_HEADS, tb), lambda i, e: (0, i)),
                  pl.BlockSpec((tb, D_MODEL), lambda i, e: (i, 0)),
                  _mod_spec(tb, layout)],
        out_specs=pl.BlockSpec((tb, D_MODEL), lambda i, e: (i, 0)),
        out_shape=jax.ShapeDtypeStruct((t, D_MODEL), F32),
        scratch_shapes=[pltpu.VMEM((D_MODEL, tb), F32),
                        pltpu.VMEM((eb, tb), F32),
                        pltpu.VMEM((eb, tb), BF16)],
        compiler_params=_params("arbitrary", "arbitrary"),
        name="peer_dense",
    )(h2t, pu, pv_t, s0, s1, e0, e1, thr, x1, mod)


def _merge_cache(kv):
    b, h, l, d = kv.shape
    return kv.transpose(0, 2, 1, 3).reshape(b, l, h * d).astype(BF16)


def _split_heads(x, batch, seq):
    return x.reshape(batch, seq, N_HEADS, HEAD_DIM).transpose(0, 2, 1, 3)


def kernel(x_prompt, x_sample, cache_na_kv, cache_diff_kv, c, c_ctx, w_mod, b_mod, norm1_g, norm2_g, w_in, conv_w, conv_b, conv_ln_g, conv_ln_b, na_qn_g, na_kn_g, na_rel_bias, diff_qn_g, diff_kn_g, diff_lambda, diff_subln_g, gmlp_ln_g, gmlp_ln_b, gmlp_ws, gmlp_bs, w_out, peer_wq, peer_sub_keys, peer_u, peer_v):
    batch, seq, _ = x_prompt.shape
    dbatch, dseq, _ = x_sample.shape
    t_ctx = batch * seq
    layout = (t_ctx, dseq)
    x = jnp.concatenate([x_prompt.reshape(t_ctx, D_MODEL), x_sample.reshape(dbatch * dseq, D_MODEL)], axis=0)
    cond = jnp.concatenate([c_ctx[None, :], c, jnp.zeros((SUBLANES - 1 - dbatch, D_MODEL), F32)], axis=0)
    mod_all = _modulation(cond, w_mod, b_mod).reshape(DEPTH, SUBLANES, 6, D_MODEL)
    cos, sin = _rope_tables(dseq)
    row = lambda v: v.reshape(1, -1)
    tile = lambda v: jnp.tile(v, GROUP_W // v.shape[0]).reshape(1, GROUP_W)
    na_states, diff_states = [], []
    for l in range(DEPTH):
        lam_init = 0.8 - 0.6 * math.exp(-0.3 * l)
        mod = mod_all[l]
        parts = _in_proj(x, mod, row(norm1_g[l]), w_in[l].astype(BF16), layout)
        conv_o = _conv_module(parts, conv_w[l], row(conv_b[l]), row(conv_ln_g[l]), row(conv_ln_b[l]), t_ctx, dseq)
        gm_bias = jnp.repeat(gmlp_bs[l].T, GROUP_W // N_HEADS, axis=1)
        gm_o = _gmlp(parts, row(gmlp_ln_g[l]), row(gmlp_ln_b[l]), gmlp_ws[l].astype(BF16), gm_bias)
        qn, kn = tile(na_qn_g[l]), tile(na_kn_g[l])
        dqn, dkn, sub = tile(diff_qn_g[l]), tile(diff_kn_g[l]), tile(diff_subln_g[l])
        na_c, df_c, nk_c, dk_c = _ctx_attention(parts, batch, seq, qn, kn, dqn, dkn, sub, diff_lambda[l], lam_init)
        lq, lk, lv, ldq, ldk, ldv = _lat_prep(parts, t_ctx, dbatch, dseq, qn, kn, dqn, dkn, cos, sin)
        na_l = _na_latent(lq, lk, lv, _merge_cache(cache_na_kv[:, l, 0]), _merge_cache(cache_na_kv[:, l, 1]),
                          _na_bias_table(na_rel_bias[l], dseq // GRID_W), dbatch, dseq)
        ldk_all = ldk
        df_l = _diff_latent(ldq, ldk_all, ldv, _merge_cache(cache_diff_kv[:, l, 0]),
                            _merge_cache(cache_diff_kv[:, l, 1]), sub, diff_lambda[l], dbatch, dseq, lam_init)
        na_o = jnp.concatenate([na_c, na_l], axis=0)
        diff_o = jnp.concatenate([df_c, df_l], axis=0)
        x1 = _out_proj(conv_o, na_o, diff_o, gm_o, x, mod, w_out[l].astype(BF16), layout)
        h2t, s0, s1, e0, e1, thr = _route(x1, mod, row(norm2_g[l]), peer_wq[l].T.astype(BF16),
                                          peer_sub_keys[l].astype(BF16), layout)
        x = _peer(h2t, peer_u[l].astype(BF16), peer_v[l].T.astype(BF16), s0, s1, e0, e1, thr, x1, mod, layout)
        nv_c = parts[:t_ctx, 4 * GROUP_W:5 * GROUP_W]
        dv_c = parts[:t_ctx, 7 * GROUP_W:8 * GROUP_W]
        na_states.append(jnp.stack([_split_heads(nk_c, batch, seq), _split_heads(nv_c, batch, seq)], axis=1))
        diff_states.append(jnp.stack([_split_heads(dk_c, batch, seq), _split_heads(dv_c, batch, seq)], axis=1))
    y_prompt = x[:t_ctx].reshape(batch, seq, D_MODEL)
    y_sample = x[t_ctx:].reshape(dbatch, dseq, D_MODEL)
    return (y_prompt, y_sample, jnp.stack(na_states, axis=1), jnp.stack(diff_states, axis=1))
```

```python
import functools
import math

import numpy as np
import jax
import jax.numpy as jnp
from jax import lax
from jax.experimental import pallas as pl
from jax.experimental.pallas import tpu as pltpu

F32 = jnp.float32
BF16 = jnp.bfloat16

D_MODEL = 1024
DEPTH = 2
GRID_W = 64
GROUP_W = 256
HEAD_DIM = 64
N_HEADS = 4
DIFF_SUB = 32
CONV_K = 31
CONV_HALO = 16
NA_WIN_R = 8
NA_WIN_C = 16
ROPE_BASE = 10000.0
CHUNK = 128
PEER_HEADS = 8
PEER_NKEYS = 128
PEER_TOPK = 16
EPS = 1e-6
NEG = -1e30

LANES = 128
SUBLANES = 8
MXU_DIM = 256
VMEM_LIMIT = 48 * 1024 * 1024

SEQ_BLOCK = 256
ROW_BLOCK = 256
ROUTE_BLOCK = 128
PEER_TOKENS = 512
PEER_IBLOCK = 16
PEER_VMEM_LIMIT = 56 * 1024 * 1024
DIFF_QBLOCK = 128


def _params(*sem):
    return pltpu.CompilerParams(dimension_semantics=sem, vmem_limit_bytes=VMEM_LIMIT)


def _dot(a, b):
    return jnp.dot(a, b, preferred_element_type=F32)


def _dot_nt(a, b):
    return lax.dot_general(a, b, (((1,), (1,)), ((), ())), preferred_element_type=F32)


def _split_dot(a, b):
    a1 = a.astype(BF16)
    r1 = a - a1.astype(F32)
    a2 = r1.astype(BF16)
    a3 = (r1 - a2.astype(F32)).astype(BF16)
    return _dot(a1, b) + _dot(a2, b) + _dot(a3, b)


def _group_ones(n, group):
    r = lax.broadcasted_iota(jnp.int32, (n, n), 0) // group
    c = lax.broadcasted_iota(jnp.int32, (n, n), 1) // group
    return (r == c).astype(BF16)


def _group_rms(x, gain, group):
    ssq = _split_dot(x * x, _group_ones(x.shape[-1], group))
    return x * lax.rsqrt(ssq * (1.0 / group) + EPS) * gain


def _layer_norm(y, g, b):
    mu = jnp.mean(y, axis=-1, keepdims=True)
    yc = y - mu
    return yc * lax.rsqrt(jnp.mean(yc * yc, axis=-1, keepdims=True) + EPS) * g + b


def _mod_spec(rows_per_block, layout):
    t_ctx, lat_seq = layout
    assert t_ctx % rows_per_block == 0 and lat_seq % rows_per_block == 0
    def index(i, *_):
        return (jnp.maximum((i * rows_per_block - t_ctx) // lat_seq + 1, 0), 0, 0)
    return pl.BlockSpec((1, 6, D_MODEL), index)


def _mod_kernel(c_ref, w_ref, b_ref, o_ref):
    c = c_ref[...]
    s = c * jax.nn.sigmoid(c)
    o_ref[0] = jnp.dot(s, w_ref[0], preferred_element_type=F32,
                       precision=lax.Precision.HIGHEST) + b_ref[0]


def _modulation(cond, w_mod, b_mod):
    nb = 4
    cols = 6 * D_MODEL // nb
    return pl.pallas_call(
        _mod_kernel,
        grid=(DEPTH, nb),
        in_specs=[pl.BlockSpec((SUBLANES, D_MODEL), lambda l, j: (0, 0)),
                  pl.BlockSpec((1, D_MODEL, cols), lambda l, j: (l, 0, j)),
                  pl.BlockSpec((1, 1, cols), lambda l, j: (l, 0, j))],
        out_specs=pl.BlockSpec((1, SUBLANES, cols), lambda l, j: (l, 0, j)),
        out_shape=jax.ShapeDtypeStruct((DEPTH, SUBLANES, 6 * D_MODEL), F32),
        name="modulation",
        compiler_params=_params("arbitrary", "arbitrary"),
    )(cond, w_mod, b_mod.reshape(DEPTH, 1, 6 * D_MODEL))


def _in_proj_kernel(x_ref, mod_ref, g_ref, w_ref, o_ref):
    x = x_ref[...]
    y = x * lax.rsqrt(jnp.mean(x * x, axis=-1, keepdims=True) + EPS) * g_ref[...]
    h = y * (1.0 + mod_ref[0, 1:2, :]) + mod_ref[0, 0:1, :]
    o_ref[...] = _dot(h.astype(BF16), w_ref[...])


def _in_proj(x, mod, g, w, layout):
    t = x.shape[0]
    n = w.shape[1]
    return pl.pallas_call(
        _in_proj_kernel,
        grid=(t // ROW_BLOCK,),
        in_specs=[pl.BlockSpec((ROW_BLOCK, D_MODEL), lambda i: (i, 0)),
                  _mod_spec(ROW_BLOCK, layout),
                  pl.BlockSpec((1, D_MODEL), lambda i: (0, 0)),
                  pl.BlockSpec((D_MODEL, n), lambda i: (0, 0))],
        out_specs=pl.BlockSpec((ROW_BLOCK, n), lambda i: (i, 0)),
        out_shape=jax.ShapeDtypeStruct((t, n), F32),
        name="in_proj",
        compiler_params=_params("arbitrary"),
    )(x, mod, g, w)


def _conv_kernel(ac, gc, ap, gp, an, gn, w_ref, b_ref, lg_ref, lb_ref, o_ref, pad_ref, *, ctx_blocks, seq_blocks):
    i = pl.program_id(0)
    is_ctx = i < ctx_blocks
    j = i % seq_blocks
    first = jnp.logical_or(is_ctx, j == 0)
    last = jnp.logical_or(is_ctx, j == seq_blocks - 1)
    yp = ap[...] * jax.nn.sigmoid(gp[...])
    yn = an[...] * jax.nn.sigmoid(gn[...])
    pad_ref[0:CONV_HALO, :] = jnp.where(first, 0.0, yp)
    pad_ref[CONV_HALO:CONV_HALO + SEQ_BLOCK, :] = ac[...] * jax.nn.sigmoid(gc[...])
    pad_ref[CONV_HALO + SEQ_BLOCK:, :] = jnp.where(last, 0.0, yn)
    off = CONV_HALO - CONV_K // 2
    acc = jnp.zeros((SEQ_BLOCK, GROUP_W), F32)
    for k in range(CONV_K):
        acc = acc + pad_ref[off + k:off + k + SEQ_BLOCK, :] * w_ref[k:k + 1, :]
    y = _layer_norm(acc + b_ref[...], lg_ref[...], lb_ref[...])
    o_ref[...] = y * jax.nn.sigmoid(y)


def _conv_module(parts, w, b, lg, lb, ctx_tokens, lat_seq):
    t = parts.shape[0]
    nblk = t // SEQ_BLOCK
    hb = SEQ_BLOCK // CONV_HALO
    last_halo = t // CONV_HALO - 1
    vec = pl.BlockSpec((1, GROUP_W), lambda i: (0, 0))
    kern = functools.partial(_conv_kernel, ctx_blocks=ctx_tokens // SEQ_BLOCK, seq_blocks=lat_seq // SEQ_BLOCK)
    return pl.pallas_call(
        kern,
        grid=(nblk,),
        in_specs=[pl.BlockSpec((SEQ_BLOCK, GROUP_W), lambda i: (i, 0)),
                  pl.BlockSpec((SEQ_BLOCK, GROUP_W), lambda i: (i, 1)),
                  pl.BlockSpec((CONV_HALO, GROUP_W), lambda i: (jnp.maximum(i * hb - 1, 0), 0)),
                  pl.BlockSpec((CONV_HALO, GROUP_W), lambda i: (jnp.maximum(i * hb - 1, 0), 1)),
                  pl.BlockSpec((CONV_HALO, GROUP_W), lambda i: (jnp.minimum((i + 1) * hb, last_halo), 0)),
                  pl.BlockSpec((CONV_HALO, GROUP_W), lambda i: (jnp.minimum((i + 1) * hb, last_halo), 1)),
                  pl.BlockSpec((CONV_K, GROUP_W), lambda i: (0, 0)),
                  vec, vec, vec],
        out_specs=pl.BlockSpec((SEQ_BLOCK, GROUP_W), lambda i: (i, 0)),
        out_shape=jax.ShapeDtypeStruct((t, GROUP_W), F32),
        scratch_shapes=[pltpu.VMEM((SEQ_BLOCK + 2 * CONV_HALO, GROUP_W), F32)],
        name="conv_module",
        compiler_params=_params("arbitrary"),
    )(parts, parts, parts, parts, parts, parts, w, b, lg, lb)


def _gmlp_kernel(u_ref, v_ref, lg_ref, lb_ref, ws_ref, bias_ref, o_ref):
    u = jax.nn.gelu(u_ref[...])
    v = _layer_norm(jax.nn.gelu(v_ref[...]), lg_ref[...], lb_ref[...]).astype(BF16)
    lane_g = lax.broadcasted_iota(jnp.int32, (1, GROUP_W), 1) // (GROUP_W // N_HEADS)
    for ch in range(SEQ_BLOCK // CHUNK):
        rows = slice(ch * CHUNK, (ch + 1) * CHUNK)
        s = bias_ref[...]
        for g in range(N_HEADS):
            s = s + jnp.where(lane_g == g, _dot(ws_ref[g], v[rows]), 0.0)
        o_ref[rows, :] = u[rows] * s


def _gmlp(parts, lg, lb, ws, bias):
    t = parts.shape[0]
    vec = pl.BlockSpec((1, GROUP_W), lambda i: (0, 0))
    return pl.pallas_call(
        _gmlp_kernel,
        grid=(t // SEQ_BLOCK,),
        in_specs=[pl.BlockSpec((SEQ_BLOCK, GROUP_W), lambda i: (i, 8)),
                  pl.BlockSpec((SEQ_BLOCK, GROUP_W), lambda i: (i, 9)),
                  vec, vec,
                  pl.BlockSpec((N_HEADS, CHUNK, CHUNK), lambda i: (0, 0, 0)),
                  pl.BlockSpec((CHUNK, GROUP_W), lambda i: (0, 0))],
        out_specs=pl.BlockSpec((SEQ_BLOCK, GROUP_W), lambda i: (i, 0)),
        out_shape=jax.ShapeDtypeStruct((t, GROUP_W), F32),
        compiler_params=_params("arbitrary"),
        name="gmlp",
    )(parts, parts, lg, lb, ws, bias)


def _lane_group_id(width, group):
    return lax.broadcasted_iota(jnp.int32, (1, width), 1) // group


def _stack_masked(q, group, ids):
    lane = _lane_group_id(q.shape[-1], group)
    return jnp.concatenate([jnp.where(lane == g, q, jnp.zeros_like(q)) for g in ids], axis=0)


def _diff_lambda(dl_ref, lam_init):
    dl = dl_ref[...]
    a = jnp.sum(dl[0:1, :] * dl[1:2, :], axis=-1, keepdims=True)
    b = jnp.sum(dl[2:3, :] * dl[3:4, :], axis=-1, keepdims=True)
    return jnp.exp(a) - jnp.exp(b) + lam_init


def _ctx_attn_kernel(q_ref, k_ref, v_ref, dq_ref, dk_ref, dv_ref, qn_ref, kn_ref, dqn_ref, dkn_ref, sub_ref,
                     dl_ref, na_ref, df_ref, nk_ref, dkf_ref, *, lam_init):
    n = q_ref.shape[0]
    lane_h = _lane_group_id(GROUP_W, HEAD_DIM)
    qn = _group_rms(q_ref[...], qn_ref[...], HEAD_DIM)
    kn = _group_rms(k_ref[...], kn_ref[...], HEAD_DIM)
    nk_ref[...] = kn
    qs = _stack_masked(qn.astype(BF16), HEAD_DIM, range(N_HEADS))
    s = _dot_nt(qs, kn.astype(BF16)) * (HEAD_DIM ** -0.5)
    e = jnp.exp(s - jnp.max(s, axis=-1, keepdims=True))
    r = _dot(e.astype(BF16), v_ref[...].astype(BF16)) / jnp.sum(e, axis=-1, keepdims=True)
    o = jnp.zeros((n, GROUP_W), F32)
    for h in range(N_HEADS):
        o = o + jnp.where(lane_h == h, r[h * n:(h + 1) * n], 0.0)
    na_ref[...] = o
    lam = _diff_lambda(dl_ref, lam_init)
    dq = _group_rms(dq_ref[...], dqn_ref[...], DIFF_SUB)
    dk = _group_rms(dk_ref[...], dkn_ref[...], DIFF_SUB)
    dkf_ref[...] = dk
    dkb = dk.astype(BF16)
    dvb = dv_ref[...].astype(BF16)
    o = jnp.zeros((n, GROUP_W), F32)
    for h in range(N_HEADS):
        qs = _stack_masked(dq.astype(BF16), DIFF_SUB, (2 * h, 2 * h + 1))
        s = _dot_nt(qs, dkb) * (DIFF_SUB ** -0.5)
        e = jnp.exp(s - jnp.max(s, axis=-1, keepdims=True))
        inv = 1.0 / jnp.sum(e, axis=-1, keepdims=True)
        a = e[:n] * inv[:n] - e[n:] * (lam * inv[n:])
        o = o + jnp.where(lane_h == h, _dot(a.astype(BF16), dvb), 0.0)
    df_ref[...] = _group_rms(o, sub_ref[...], HEAD_DIM) * (1.0 - lam_init)


def _ctx_attention(parts, batch, seq, qn, kn, dqn, dkn, sub, dl, lam_init):
    t = batch * seq
    vec = pl.BlockSpec((1, GROUP_W), lambda b: (0, 0))
    col = lambda c: pl.BlockSpec((seq, GROUP_W), lambda b, c=c: (b, c))
    out = jax.ShapeDtypeStruct((t, GROUP_W), F32)
    ob = pl.BlockSpec((seq, GROUP_W), lambda b: (b, 0))
    return pl.pallas_call(
        functools.partial(_ctx_attn_kernel, lam_init=lam_init),
        grid=(batch,),
        in_specs=[col(2), col(3), col(4), col(5), col(6), col(7), vec, vec, vec, vec, vec,
                  pl.BlockSpec((4, DIFF_SUB), lambda b: (0, 0))],
        out_specs=[ob, ob, ob, ob],
        out_shape=[out, out, out, out],
        compiler_params=_params("arbitrary"),
        name="ctx_attention",
    )(parts, parts, parts, parts, parts, parts, qn, kn, dqn, dkn, sub, dl)


def _lat_prep_kernel(q_ref, k_ref, v_ref, dq_ref, dk_ref, dv_ref, qn_ref, kn_ref, dqn_ref, dkn_ref,
                     cos_ref, sin_ref, oq, ok, ov, odq, odk, odv):
    oq[...] = _group_rms(q_ref[...], qn_ref[...], HEAD_DIM).astype(BF16)
    ok[...] = _group_rms(k_ref[...], kn_ref[...], HEAD_DIM).astype(BF16)
    ov[...] = v_ref[...].astype(BF16)
    odv[...] = dv_ref[...].astype(BF16)
    half = DIFF_SUB // 4
    lane = lax.broadcasted_iota(jnp.int32, (1, GROUP_W), 1)
    lower = (lane % (2 * half)) < half
    cos = cos_ref[...]
    sin = sin_ref[...]

    def rope(x):
        partner = jnp.where(lower, pltpu.roll(x, GROUP_W - half, 1), pltpu.roll(x, half, 1))
        return x * cos + partner * sin

    odq[...] = rope(_group_rms(dq_ref[...], dqn_ref[...], DIFF_SUB)).astype(BF16)
    odk[...] = rope(_group_rms(dk_ref[...], dkn_ref[...], DIFF_SUB)).astype(BF16)


def _rope_tables(seq):
    half = DIFF_SUB // 4
    t = jnp.arange(seq)
    rows = (t // GRID_W).astype(F32)
    cols = (t % GRID_W).astype(F32)
    inv = ROPE_BASE ** (-jnp.arange(half, dtype=F32) / half)
    ang_r = rows[:, None] * inv[None, :]
    ang_c = cols[:, None] * inv[None, :]
    cos32 = jnp.concatenate([jnp.cos(ang_r), jnp.cos(ang_r), jnp.cos(ang_c), jnp.cos(ang_c)], axis=-1)
    sin32 = jnp.concatenate([-jnp.sin(ang_r), jnp.sin(ang_r), -jnp.sin(ang_c), jnp.sin(ang_c)], axis=-1)
    reps = GROUP_W // DIFF_SUB
    return jnp.tile(cos32, (1, reps)), jnp.tile(sin32, (1, reps))


def _lat_prep(parts, row0, batch, seq, qn, kn, dqn, dkn, cos, sin):
    t = batch * seq
    nb = seq // SEQ_BLOCK
    r0 = row0 // SEQ_BLOCK
    vec = pl.BlockSpec((1, GROUP_W), lambda i: (0, 0))
    col = lambda c: pl.BlockSpec((SEQ_BLOCK, GROUP_W), lambda i, c=c: (r0 + i, c))
    tab = pl.BlockSpec((SEQ_BLOCK, GROUP_W), lambda i: (i % nb, 0))
    out = jax.ShapeDtypeStruct((t, GROUP_W), BF16)
    ob = pl.BlockSpec((SEQ_BLOCK, GROUP_W), lambda i: (i, 0))
    return pl.pallas_call(
        _lat_prep_kernel,
        grid=(t // SEQ_BLOCK,),
        in_specs=[col(2), col(3), col(4), col(5), col(6), col(7), vec, vec, vec, vec, tab, tab],
        out_specs=[ob] * 6,
        out_shape=[out] * 6,
        compiler_params=_params("arbitrary"),
        name="lat_prep",
    )(parts, parts, parts, parts, parts, parts, qn, kn, dqn, dkn, cos, sin)


def _na_row_start(r, rows):
    return jnp.clip(r - NA_WIN_R // 2, 0, rows - NA_WIN_R)


def _na_lat_kernel(q_ref, k_ref, v_ref, ck_ref, cv_ref, bt_ref, o_ref, *, rows):
    r = pl.program_id(1)
    start = pl.multiple_of(_na_row_start(r, rows) * GRID_W, GRID_W)
    n_loc = NA_WIN_R * GRID_W
    kl = k_ref[pl.ds(start, n_loc), :]
    vl = v_ref[pl.ds(start, n_loc), :]
    qs = _stack_masked(q_ref[...], HEAD_DIM, range(N_HEADS))
    scale = HEAD_DIM ** -0.5
    s_loc = _dot_nt(qs, kl) * scale + bt_ref[0]
    s_ctx = _dot_nt(qs, ck_ref[0]) * scale
    m = jnp.maximum(jnp.max(s_loc, axis=-1, keepdims=True), jnp.max(s_ctx, axis=-1, keepdims=True))
    e_loc = jnp.exp(s_loc - m)
    e_ctx = jnp.exp(s_ctx - m)
    l = jnp.sum(e_loc, axis=-1, keepdims=True) + jnp.sum(e_ctx, axis=-1, keepdims=True)
    res = (_dot(e_loc.astype(BF16), vl) + _dot(e_ctx.astype(BF16), cv_ref[0])) / l
    lane_h = _lane_group_id(GROUP_W, HEAD_DIM)
    o = jnp.zeros((GRID_W, GROUP_W), F32)
    for h in range(N_HEADS):
        o = o + jnp.where(lane_h == h, res[h * GRID_W:(h + 1) * GRID_W], 0.0)
    o_ref[...] = o


def _na_bias_table(rel_bias, rows):
    wr = NA_WIN_R
    c = np.arange(GRID_W)
    c0 = np.clip(c - NA_WIN_C // 2, 0, GRID_W - NA_WIN_C)
    in_win = (c[None, :] >= c0[:, None]) & (c[None, :] < c0[:, None] + NA_WIN_C)
    dc_idx = np.clip(c[None, :] - c[:, None] + NA_WIN_C - 1, 0, 2 * NA_WIN_C - 2)
    off = np.arange(wr)
    dr_idx = (np.arange(wr)[None, :] - off[:, None]) + NA_WIN_R - 1
    rb = rel_bias[:, dr_idx]
    bias = jnp.take(rb, jnp.asarray(dc_idx), axis=-1)
    bias = jnp.where(jnp.asarray(in_win)[None, None, None], bias, NEG)
    bias = bias.transpose(1, 0, 3, 2, 4)
    return bias.reshape(wr, N_HEADS * GRID_W, wr * GRID_W)


def _na_latent(qn, kn, v, ck, cv, bias_tab, batch, seq):
    rows = seq // GRID_W
    nctx = ck.shape[1]
    full = pl.BlockSpec((seq, GROUP_W), lambda b, r: (b, 0))
    ctx = pl.BlockSpec((1, nctx, GROUP_W), lambda b, r: (b, 0, 0))
    return pl.pallas_call(
        functools.partial(_na_lat_kernel, rows=rows),
        grid=(batch, rows),
        in_specs=[pl.BlockSpec((GRID_W, GROUP_W), lambda b, r: (b * rows + r, 0)),
                  full, full, ctx, ctx,
                  pl.BlockSpec((1, N_HEADS * GRID_W, NA_WIN_R * GRID_W),
                               lambda b, r: (r - _na_row_start(r, rows), 0, 0))],
        out_specs=pl.BlockSpec((GRID_W, GROUP_W), lambda b, r: (b * rows + r, 0)),
        out_shape=jax.ShapeDtypeStruct((batch * seq, GROUP_W), F32),
        compiler_params=_params("arbitrary", "arbitrary"),
        name="na_latent",
    )(qn, kn, v, ck, cv, bias_tab)


def _diff_lat_kernel(q_ref, k_ref, v_ref, ck_ref, cv_ref, sub_ref, dl_ref, o_ref, *, lam_init):
    n = q_ref.shape[0]
    lam = _diff_lambda(dl_ref, lam_init)
    lane_s = _lane_group_id(GROUP_W, DIFF_SUB)
    q = q_ref[...]
    scale = DIFF_SUB ** -0.5

    def head(h, o):
        qs = jnp.concatenate([jnp.where(lane_s == 2 * h, q, jnp.zeros_like(q)),
                              jnp.where(lane_s == 2 * h + 1, q, jnp.zeros_like(q))], axis=0)
        s_l = _dot_nt(qs, k_ref[...]) * scale
        s_c = _dot_nt(qs, ck_ref[0]) * scale
        m = jnp.maximum(jnp.max(s_l, axis=-1, keepdims=True), jnp.max(s_c, axis=-1, keepdims=True))
        e_l = jnp.exp(s_l - m)
        e_c = jnp.exp(s_c - m)
        inv = 1.0 / (jnp.sum(e_l, axis=-1, keepdims=True) + jnp.sum(e_c, axis=-1, keepdims=True))
        c1 = inv[:n]
        c2 = lam * inv[n:]
        a_l = e_l[:n] * c1 - e_l[n:] * c2
        a_c = e_c[:n] * c1 - e_c[n:] * c2
        res = _dot(a_l.astype(BF16), v_ref[...]) + _dot(a_c.astype(BF16), cv_ref[0])
        return o + jnp.where(lane_s // 2 == h, res, 0.0)

    o = lax.fori_loop(0, N_HEADS, head, jnp.zeros((n, GROUP_W), F32))
    o_ref[...] = _group_rms(o, sub_ref[...], HEAD_DIM) * (1.0 - lam_init)


def _diff_latent(dq, dk, dv, ck, cv, sub, dl, batch, seq, lam_init):
    nq = seq // DIFF_QBLOCK
    nctx = ck.shape[1]
    full = pl.BlockSpec((seq, GROUP_W), lambda b, i: (b, 0))
    ctx = pl.BlockSpec((1, nctx, GROUP_W), lambda b, i: (b, 0, 0))
    return pl.pallas_call(
        functools.partial(_diff_lat_kernel, lam_init=lam_init),
        grid=(batch, nq),
        in_specs=[pl.BlockSpec((DIFF_QBLOCK, GROUP_W), lambda b, i: (b * nq + i, 0)),
                  full, full, ctx, ctx,
                  pl.BlockSpec((1, GROUP_W), lambda b, i: (0, 0)),
                  pl.BlockSpec((4, DIFF_SUB), lambda b, i: (0, 0))],
        out_specs=pl.BlockSpec((DIFF_QBLOCK, GROUP_W), lambda b, i: (b * nq + i, 0)),
        out_shape=jax.ShapeDtypeStruct((batch * seq, GROUP_W), F32),
        compiler_params=_params("arbitrary", "arbitrary"),
        name="diff_latent",
    )(dq, dk, dv, ck, cv, sub, dl)


def _out_proj_kernel(a_ref, b_ref, c_ref, d_ref, x_ref, mod_ref, w_ref, o_ref):
    mixed = jnp.zeros(o_ref.shape, F32)
    for g, ref in enumerate((a_ref, b_ref, c_ref, d_ref)):
        mixed = mixed + _dot(ref[...].astype(BF16), w_ref[g * GROUP_W:(g + 1) * GROUP_W, :])
    o_ref[...] = x_ref[...] + mod_ref[0, 2:3, :] * mixed


def _out_proj(conv_o, na_o, diff_o, gm_o, x, mod, w, layout):
    t = x.shape[0]
    part = pl.BlockSpec((ROW_BLOCK, GROUP_W), lambda i: (i, 0))
    row = pl.BlockSpec((ROW_BLOCK, D_MODEL), lambda i: (i, 0))
    return pl.pallas_call(
        _out_proj_kernel,
        grid=(t // ROW_BLOCK,),
        in_specs=[part, part, part, part, row,
                  _mod_spec(ROW_BLOCK, layout),
                  pl.BlockSpec((D_MODEL, D_MODEL), lambda i: (0, 0))],
        out_specs=row,
        out_shape=jax.ShapeDtypeStruct((t, D_MODEL), F32),
        compiler_params=_params("arbitrary"),
        name="out_proj",
    )(conv_o, na_o, diff_o, gm_o, x, mod, w)


def _sort_desc(vals):
    v = list(vals)
    n = len(v)
    k = 2
    while k <= n:
        j = k // 2
        while j >= 1:
            for i in range(n):
                l = i ^ j
                if l > i:
                    hi, lo = jnp.maximum(v[i], v[l]), jnp.minimum(v[i], v[l])
                    v[i], v[l] = (hi, lo) if (i & k) == 0 else (lo, hi)
            j //= 2
        k *= 2
    return v


def _merge_desc(v):
    v = list(v)
    n = len(v)
    j = n // 2
    while j >= 1:
        for i in range(n):
            l = i ^ j
            if l > i:
                v[i], v[l] = jnp.maximum(v[i], v[l]), jnp.minimum(v[i], v[l])
        j //= 2
    return v


def _top_half(a, b):
    n = len(a)
    return [jnp.maximum(a[i], b[n - 1 - i]) for i in range(n)]


def _top16_keys(s):
    k = PEER_TOPK
    v = _sort_desc([s[a * SUBLANES:(a + 1) * SUBLANES] for a in range(PEER_NKEYS // SUBLANES)])
    shift = SUBLANES // 2
    while shift >= 1:
        rolled = [pltpu.roll(x, shift, 0) for x in v]
        v = _merge_desc(_top_half(v, rolled))
        shift //= 2
    return v[:k]


def _paired_bf16(x):
    hi = pltpu.bitcast(x.astype(BF16).astype(F32), jnp.uint32)
    return hi | (hi >> 16)


def _route_kernel(x_ref, mod_ref, g_ref, wq_ref, sk_ref, h2_ref, n0_ref, r1_ref, e0_ref, e1_ref, s0_ref, s1_ref):
    x = x_ref[...]
    y = x * lax.rsqrt(jnp.mean(x * x, axis=-1, keepdims=True) + EPS) * g_ref[...]
    h2 = y * (1.0 + mod_ref[0, 4:5, :]) + mod_ref[0, 3:4, :]
    h2t = h2.T.astype(BF16)
    h2_ref[...] = h2t
    qt = _dot(wq_ref[...], h2t)
    tb = x.shape[0]
    sub = lax.broadcasted_iota(jnp.int32, (SUBLANES, tb), 0)
    k = PEER_TOPK
    tops = [[jnp.zeros((SUBLANES, tb), F32)] * k, [jnp.zeros((SUBLANES, tb), F32)] * k]
    for h in range(PEER_HEADS):
        for p in range(2):
            base = (2 * h + p) * PEER_NKEYS
            s = _dot(sk_ref[p], qt[base:base + PEER_NKEYS].astype(BF16))
            (s0_ref if p == 0 else s1_ref)[h] = s
            top = _top16_keys(s)
            tops[p] = [jnp.where(sub == h, top[a], tops[p][a]) for a in range(k)]
    cand = [tops[0][a] + tops[1][b] for a in range(k) for b in range(k) if (a + 1) * (b + 1) <= k]
    pad = [jnp.full((SUBLANES, tb), NEG, F32)] * (4 * k - len(cand))
    groups = [_sort_desc((cand + pad)[g * k:(g + 1) * k]) for g in range(4)]
    best = _top_half(_merge_desc(_top_half(groups[0], groups[1])), _merge_desc(_top_half(groups[2], groups[3])))
    thr = functools.reduce(jnp.minimum, best)
    m = tops[0][0] + tops[1][0]
    z = functools.reduce(jnp.add, [jnp.where(c >= thr, jnp.exp(c - m), 0.0) for c in cand])
    zinv = 1.0 / z
    counts, first = [], 0
    for a in range(k):
        n_b = k // (a + 1)
        counts.append(functools.reduce(jnp.add, [jnp.where(c >= thr, 1.0, 0.0) for c in cand[first:first + n_b]]))
        first += n_b
    for h in range(PEER_HEADS):
        s0 = s0_ref[h]
        s1 = s1_ref[h]
        n0 = jnp.zeros_like(s0)
        r1 = jnp.full_like(s1, float(k))
        for a in reversed(range(k)):
            n0 = jnp.where(s0 >= tops[0][a][h:h + 1], counts[a][h:h + 1], n0)
            r1 = jnp.where(s1 >= tops[1][a][h:h + 1], float(a), r1)
        n0_ref[h, 0] = _paired_bf16(n0)
        r1_ref[h, 0] = pltpu.bitcast(r1.astype(BF16), jnp.uint32)
        e0_ref[h, 0] = _paired_bf16(jnp.exp(s0 - tops[0][0][h:h + 1]) * zinv[h:h + 1])
        e1_ref[h, 0] = pltpu.bitcast(jnp.exp(s1 - tops[1][0][h:h + 1]).astype(BF16), jnp.uint32)


def _route(x1, mod, g, wq_t, sk, layout):
    t = x1.shape[0]
    tb = ROUTE_BLOCK
    assert tb == LANES
    tiles = t // LANES
    first_key = jax.ShapeDtypeStruct((PEER_HEADS, tiles, PEER_NKEYS, LANES), jnp.uint32)
    second_key = jax.ShapeDtypeStruct((PEER_HEADS, tiles, PEER_NKEYS // 2, LANES), jnp.uint32)
    fkb = pl.BlockSpec((PEER_HEADS, 1, PEER_NKEYS, LANES), lambda i: (0, i, 0, 0))
    skb = pl.BlockSpec((PEER_HEADS, 1, PEER_NKEYS // 2, LANES), lambda i: (0, i, 0, 0))
    scores = pltpu.VMEM((PEER_HEADS, PEER_NKEYS, tb), F32)
    return pl.pallas_call(
        _route_kernel,
        grid=(t // tb,),
        in_specs=[pl.BlockSpec((tb, D_MODEL), lambda i: (i, 0)),
                  _mod_spec(tb, layout),
                  pl.BlockSpec((1, D_MODEL), lambda i: (0, 0)),
                  pl.BlockSpec(wq_t.shape, lambda i: (0, 0)),
                  pl.BlockSpec(sk.shape, lambda i: (0, 0, 0))],
        out_specs=[pl.BlockSpec((D_MODEL, tb), lambda i: (0, i)), fkb, skb, fkb, skb],
        out_shape=[jax.ShapeDtypeStruct((D_MODEL, t), BF16), first_key, second_key, first_key, second_key],
        scratch_shapes=[scores, scores],
        compiler_params=_params("arbitrary"),
        name="peer_route",
    )(x1, mod, g, wq_t, sk)


def _rows_as_bf16(rows):
    tile = pltpu.bitcast(rows, BF16)
    return jnp.concatenate([tile] * (PEER_NKEYS // tile.shape[0]), axis=0)


def _peer_kernel(h2_ref, pu_ref, pv_ref, n0_ref, r1_ref, e0_ref, e1_ref, x_ref, mod_ref, o_ref,
                 acc_ref, a0_ref, a1_ref, w0_ref, w1_ref):
    e = pl.program_id(1)

    @pl.when(e == 0)
    def _():
        acc_ref[...] = jnp.zeros_like(acc_ref)

    tb = h2_ref.shape[1]
    n_sub = pv_ref.shape[0]
    sub_keys = MXU_DIM // PEER_NKEYS
    a_refs = (a0_ref, a1_ref)
    w_refs = (w0_ref, w1_ref)

    def first_matmul(k, slot):
        start = k * MXU_DIM if isinstance(k, int) else pl.multiple_of(k * MXU_DIM, MXU_DIM)
        a_refs[slot][...] = _dot(pu_ref[pl.ds(start, MXU_DIM), :], h2_ref[...])

    def second_matmul(k, slot):
        acc_ref[...] += _dot(pv_ref[k], w_refs[slot][...])

    def gate(k, slot):
        key = e * PEER_IBLOCK + sub_keys * k
        for c in range(tb // LANES):
            cols = slice(c * LANES, (c + 1) * LANES)
            gates = [jnp.zeros((PEER_NKEYS, LANES), BF16)] * sub_keys
            for h in range(PEER_HEADS):
                r1 = pltpu.bitcast(r1_ref[h, c], BF16)
                e1 = pltpu.bitcast(e1_ref[h, c], BF16)
                for d in range(sub_keys):
                    row = pl.ds(key + d, SUBLANES, stride=0)
                    n0_d = _rows_as_bf16(n0_ref[h, c, row, :])
                    e0_d = _rows_as_bf16(e0_ref[h, c, row, :])
                    gates[d] = gates[d] + jnp.where(r1 < n0_d, e1 * e0_d, jnp.zeros_like(e1))
            for d in range(sub_keys):
                rows = slice(d * PEER_NKEYS, (d + 1) * PEER_NKEYS)
                w_refs[slot][rows, cols] = jax.nn.gelu(a_refs[slot][rows, cols]).astype(BF16) * gates[d]

    def steady(j, carry):
        k = 2 * j + 1
        first_matmul(k + 1, 0)
        gate(k, 1)
        second_matmul(k - 1, 0)
        first_matmul(k + 2, 1)
        gate(k + 1, 0)
        second_matmul(k, 1)
        return carry

    assert n_sub % 2 == 0 and n_sub >= 4
    first_matmul(0, 0)
    first_matmul(1, 1)
    gate(0, 0)
    lax.fori_loop(0, n_sub // 2 - 1, steady, 0)
    gate(n_sub - 1, 1)
    second_matmul(n_sub - 2, 0)
    second_matmul(n_sub - 1, 1)

    @pl.when(e == pl.num_programs(1) - 1)
    def _():
        o_ref[...] = x_ref[...] + mod_ref[0, 5:6, :] * acc_ref[...].T


def _peer(h2t, pu, pv_t, n0, r1, e0, e1, x1, mod, layout):
    t = x1.shape[0]
    tb = PEER_TOKENS
    eb = PEER_IBLOCK * PEER_NKEYS
    n_exp = pu.shape[0]
    n_sub = eb // MXU_DIM
    assert pv_t.shape == (n_exp // MXU_DIM, D_MODEL, MXU_DIM)
    fk = pl.BlockSpec((PEER_HEADS, tb // LANES, PEER_NKEYS, LANES), lambda i, e: (0, i, 0, 0))
    sk = pl.BlockSpec((PEER_HEADS, tb // LANES, PEER_NKEYS // 2, LANES), lambda i, e: (0, i, 0, 0))
    return pl.pallas_call(
        _peer_kernel,
        grid=(t // tb, n_exp // eb),
        in_specs=[pl.BlockSpec((D_MODEL, tb), lambda i, e: (0, i)),
                  pl.BlockSpec((eb, D_MODEL), lambda i, e: (e, 0)),
                  pl.BlockSpec((n_sub, D_MODEL, MXU_DIM), lambda i, e: (e, 0, 0)),
                  fk, sk, fk, sk,
                  pl.BlockSpec((tb, D_MODEL), lambda i, e: (i, 0)),
                  _mod_spec(tb, layout)],
        out_specs=pl.BlockSpec((tb, D_MODEL), lambda i, e: (i, 0)),
        out_shape=jax.ShapeDtypeStruct((t, D_MODEL), F32),
        scratch_shapes=[pltpu.VMEM((D_MODEL, tb), F32),
                        pltpu.VMEM((MXU_DIM, tb), F32), pltpu.VMEM((MXU_DIM, tb), F32),
                        pltpu.VMEM((MXU_DIM, tb), BF16), pltpu.VMEM((MXU_DIM, tb), BF16)],
        compiler_params=pltpu.CompilerParams(dimension_semantics=("arbitrary", "arbitrary"),
                                             vmem_limit_bytes=PEER_VMEM_LIMIT),
        name="peer_dense",
    )(h2t, pu, pv_t, n0, r1, e0, e1, x1, mod)


def _merge_cache(kv):
    b, h, l, d = kv.shape
    return kv.transpose(0, 2, 1, 3).reshape(b, l, h * d).astype(BF16)


def _split_heads(x, batch, seq):
    return x.reshape(batch, seq, N_HEADS, HEAD_DIM).transpose(0, 2, 1, 3)


def kernel(x_prompt, x_sample, cache_na_kv, cache_diff_kv, c, c_ctx, w_mod, b_mod, norm1_g, norm2_g, w_in, conv_w, conv_b, conv_ln_g, conv_ln_b, na_qn_g, na_kn_g, na_rel_bias, diff_qn_g, diff_kn_g, diff_lambda, diff_subln_g, gmlp_ln_g, gmlp_ln_b, gmlp_ws, gmlp_bs, w_out, peer_wq, peer_sub_keys, peer_u, peer_v):
    batch, seq, _ = x_prompt.shape
    dbatch, dseq, _ = x_sample.shape
    t_ctx = batch * seq
    layout = (t_ctx, dseq)
    x = jnp.concatenate([x_prompt.reshape(t_ctx, D_MODEL), x_sample.reshape(dbatch * dseq, D_MODEL)], axis=0)
    cond = jnp.concatenate([c_ctx[None, :], c, jnp.zeros((SUBLANES - 1 - dbatch, D_MODEL), F32)], axis=0)
    mod_all = _modulation(cond, w_mod, b_mod).reshape(DEPTH, SUBLANES, 6, D_MODEL)
    cos, sin = _rope_tables(dseq)
    row = lambda v: v.reshape(1, -1)
    tile = lambda v: jnp.tile(v, GROUP_W // v.shape[0]).reshape(1, GROUP_W)
    na_states, diff_states = [], []
    for l in range(DEPTH):
        lam_init = 0.8 - 0.6 * math.exp(-0.3 * l)
        mod = mod_all[l]
        parts = _in_proj(x, mod, row(norm1_g[l]), w_in[l].astype(BF16), layout)
        conv_o = _conv_module(parts, conv_w[l], row(conv_b[l]), row(conv_ln_g[l]), row(conv_ln_b[l]), t_ctx, dseq)
        gm_bias = jnp.repeat(gmlp_bs[l].T, GROUP_W // N_HEADS, axis=1)
        gm_o = _gmlp(parts, row(gmlp_ln_g[l]), row(gmlp_ln_b[l]), gmlp_ws[l].astype(BF16), gm_bias)
        qn, kn = tile(na_qn_g[l]), tile(na_kn_g[l])
        dqn, dkn, sub = tile(diff_qn_g[l]), tile(diff_kn_g[l]), tile(diff_subln_g[l])
        na_c, df_c, nk_c, dk_c = _ctx_attention(parts, batch, seq, qn, kn, dqn, dkn, sub, diff_lambda[l], lam_init)
        lq, lk, lv, ldq, ldk, ldv = _lat_prep(parts, t_ctx, dbatch, dseq, qn, kn, dqn, dkn, cos, sin)
        na_l = _na_latent(lq, lk, lv, _merge_cache(cache_na_kv[:, l, 0]), _merge_cache(cache_na_kv[:, l, 1]),
                          _na_bias_table(na_rel_bias[l], dseq // GRID_W), dbatch, dseq)
        ldk_all = ldk
        df_l = _diff_latent(ldq, ldk_all, ldv, _merge_cache(cache_diff_kv[:, l, 0]),
                            _merge_cache(cache_diff_kv[:, l, 1]), sub, diff_lambda[l], dbatch, dseq, lam_init)
        na_o = jnp.concatenate([na_c, na_l], axis=0)
        diff_o = jnp.concatenate([df_c, df_l], axis=0)
        x1 = _out_proj(conv_o, na_o, diff_o, gm_o, x, mod, w_out[l].astype(BF16), layout)
        h2t, n0, r1, e0, e1 = _route(x1, mod, row(norm2_g[l]), peer_wq[l].T.astype(BF16),
                                          peer_sub_keys[l].astype(BF16), layout)
        pv_t = peer_v[l].astype(BF16).reshape(-1, MXU_DIM, D_MODEL).transpose(0, 2, 1)
        x = _peer(h2t, peer_u[l].astype(BF16), pv_t, n0, r1, e0, e1, x1, mod, layout)
        nv_c = parts[:t_ctx, 4 * GROUP_W:5 * GROUP_W]
        dv_c = parts[:t_ctx, 7 * GROUP_W:8 * GROUP_W]
        na_states.append(jnp.stack([_split_heads(nk_c, batch, seq), _split_heads(nv_c, batch, seq)], axis=1))
        diff_states.append(jnp.stack([_split_heads(dk_c, batch, seq), _split_heads(dv_c, batch, seq)], axis=1))
    y_prompt = x[:t_ctx].reshape(batch, seq, D_MODEL)
    y_sample = x[t_ctx:].reshape(dbatch, dseq, D_MODEL)
    return (y_prompt, y_sample, jnp.stack(na_states, axis=1), jnp.stack(diff_states, axis=1))
```

```python
import functools
import math

import numpy as np
import jax
import jax.numpy as jnp
from jax import lax
from jax.experimental import pallas as pl
from jax.experimental.pallas import tpu as pltpu

F32 = jnp.float32
BF16 = jnp.bfloat16

D_MODEL = 1024
DEPTH = 2
GRID_W = 64
GROUP_W = 256
HEAD_DIM = 64
N_HEADS = 4
DIFF_SUB = 32
CONV_K = 31
CONV_HALO = 16
NA_WIN_R = 8
NA_WIN_C = 16
ROPE_BASE = 10000.0
CHUNK = 128
PEER_HEADS = 8
PEER_NKEYS = 128
PEER_TOPK = 16
EPS = 1e-6
NEG = -1e30

LANES = 128
SUBLANES = 8
MXU_DIM = 256
VMEM_LIMIT = 48 * 1024 * 1024

SEQ_BLOCK = 256
ROW_BLOCK = 256
ROUTE_BLOCK = 128
PEER_TOKENS = 512
PEER_IBLOCK = 16
PEER_VMEM_LIMIT = 56 * 1024 * 1024
DIFF_QBLOCK = 128


def _params(*sem):
    return pltpu.CompilerParams(dimension_semantics=sem, vmem_limit_bytes=VMEM_LIMIT)


def _dot(a, b):
    return jnp.dot(a, b, preferred_element_type=F32)


def _dot_nt(a, b):
    return lax.dot_general(a, b, (((1,), (1,)), ((), ())), preferred_element_type=F32)


def _split_dot(a, b):
    a1 = a.astype(BF16)
    r1 = a - a1.astype(F32)
    a2 = r1.astype(BF16)
    a3 = (r1 - a2.astype(F32)).astype(BF16)
    return _dot(a1, b) + _dot(a2, b) + _dot(a3, b)


def _group_ones(n, group):
    r = lax.broadcasted_iota(jnp.int32, (n, n), 0) // group
    c = lax.broadcasted_iota(jnp.int32, (n, n), 1) // group
    return (r == c).astype(BF16)


def _group_rms(x, gain, group):
    ssq = _split_dot(x * x, _group_ones(x.shape[-1], group))
    return x * lax.rsqrt(ssq * (1.0 / group) + EPS) * gain


def _gelu_tanh(x):
    k = -2.0 * math.sqrt(2.0 / math.pi) * math.log2(math.e)
    return x / (1.0 + jnp.exp2(x * (k + (k * 0.044715) * (x * x))))


def _layer_norm(y, g, b):
    mu = jnp.mean(y, axis=-1, keepdims=True)
    yc = y - mu
    return yc * lax.rsqrt(jnp.mean(yc * yc, axis=-1, keepdims=True) + EPS) * g + b


def _mod_spec(rows_per_block, layout):
    t_ctx, lat_seq = layout
    assert t_ctx % rows_per_block == 0 and lat_seq % rows_per_block == 0
    def index(i, *_):
        return (jnp.maximum((i * rows_per_block - t_ctx) // lat_seq + 1, 0), 0, 0)
    return pl.BlockSpec((1, 6, D_MODEL), index)


def _mod_kernel(c_ref, w_ref, b_ref, o_ref):
    c = c_ref[...]
    s = c * jax.nn.sigmoid(c)
    o_ref[0] = jnp.dot(s, w_ref[0], preferred_element_type=F32,
                       precision=lax.Precision.HIGHEST) + b_ref[0]


def _modulation(cond, w_mod, b_mod):
    nb = 4
    cols = 6 * D_MODEL // nb
    return pl.pallas_call(
        _mod_kernel,
        grid=(DEPTH, nb),
        in_specs=[pl.BlockSpec((SUBLANES, D_MODEL), lambda l, j: (0, 0)),
                  pl.BlockSpec((1, D_MODEL, cols), lambda l, j: (l, 0, j)),
                  pl.BlockSpec((1, 1, cols), lambda l, j: (l, 0, j))],
        out_specs=pl.BlockSpec((1, SUBLANES, cols), lambda l, j: (l, 0, j)),
        out_shape=jax.ShapeDtypeStruct((DEPTH, SUBLANES, 6 * D_MODEL), F32),
        name="modulation",
        compiler_params=_params("arbitrary", "arbitrary"),
    )(cond, w_mod, b_mod.reshape(DEPTH, 1, 6 * D_MODEL))


def _in_proj_kernel(x_ref, mod_ref, g_ref, w_ref, o_ref):
    x = x_ref[...]
    y = x * lax.rsqrt(jnp.mean(x * x, axis=-1, keepdims=True) + EPS) * g_ref[...]
    h = y * (1.0 + mod_ref[0, 1:2, :]) + mod_ref[0, 0:1, :]
    o_ref[...] = _dot(h.astype(BF16), w_ref[...])


def _in_proj(x, mod, g, w, layout):
    t = x.shape[0]
    n = w.shape[1]
    return pl.pallas_call(
        _in_proj_kernel,
        grid=(t // ROW_BLOCK,),
        in_specs=[pl.BlockSpec((ROW_BLOCK, D_MODEL), lambda i: (i, 0)),
                  _mod_spec(ROW_BLOCK, layout),
                  pl.BlockSpec((1, D_MODEL), lambda i: (0, 0)),
                  pl.BlockSpec((D_MODEL, n), lambda i: (0, 0))],
        out_specs=pl.BlockSpec((ROW_BLOCK, n), lambda i: (i, 0)),
        out_shape=jax.ShapeDtypeStruct((t, n), F32),
        name="in_proj",
        compiler_params=_params("arbitrary"),
    )(x, mod, g, w)


def _conv_kernel(ac, gc, ap, gp, an, gn, w_ref, b_ref, lg_ref, lb_ref, o_ref, pad_ref, *, ctx_blocks, seq_blocks):
    i = pl.program_id(0)
    is_ctx = i < ctx_blocks
    j = i % seq_blocks
    first = jnp.logical_or(is_ctx, j == 0)
    last = jnp.logical_or(is_ctx, j == seq_blocks - 1)
    yp = ap[...] * jax.nn.sigmoid(gp[...])
    yn = an[...] * jax.nn.sigmoid(gn[...])
    pad_ref[0:CONV_HALO, :] = jnp.where(first, 0.0, yp)
    pad_ref[CONV_HALO:CONV_HALO + SEQ_BLOCK, :] = ac[...] * jax.nn.sigmoid(gc[...])
    pad_ref[CONV_HALO + SEQ_BLOCK:, :] = jnp.where(last, 0.0, yn)
    off = CONV_HALO - CONV_K // 2
    acc = jnp.zeros((SEQ_BLOCK, GROUP_W), F32)
    for k in range(CONV_K):
        acc = acc + pad_ref[off + k:off + k + SEQ_BLOCK, :] * w_ref[k:k + 1, :]
    y = _layer_norm(acc + b_ref[...], lg_ref[...], lb_ref[...])
    o_ref[...] = y * jax.nn.sigmoid(y)


def _conv_module(parts, w, b, lg, lb, ctx_tokens, lat_seq):
    t = parts.shape[0]
    nblk = t // SEQ_BLOCK
    hb = SEQ_BLOCK // CONV_HALO
    last_halo = t // CONV_HALO - 1
    vec = pl.BlockSpec((1, GROUP_W), lambda i: (0, 0))
    kern = functools.partial(_conv_kernel, ctx_blocks=ctx_tokens // SEQ_BLOCK, seq_blocks=lat_seq // SEQ_BLOCK)
    return pl.pallas_call(
        kern,
        grid=(nblk,),
        in_specs=[pl.BlockSpec((SEQ_BLOCK, GROUP_W), lambda i: (i, 0)),
                  pl.BlockSpec((SEQ_BLOCK, GROUP_W), lambda i: (i, 1)),
                  pl.BlockSpec((CONV_HALO, GROUP_W), lambda i: (jnp.maximum(i * hb - 1, 0), 0)),
                  pl.BlockSpec((CONV_HALO, GROUP_W), lambda i: (jnp.maximum(i * hb - 1, 0), 1)),
                  pl.BlockSpec((CONV_HALO, GROUP_W), lambda i: (jnp.minimum((i + 1) * hb, last_halo), 0)),
                  pl.BlockSpec((CONV_HALO, GROUP_W), lambda i: (jnp.minimum((i + 1) * hb, last_halo), 1)),
                  pl.BlockSpec((CONV_K, GROUP_W), lambda i: (0, 0)),
                  vec, vec, vec],
        out_specs=pl.BlockSpec((SEQ_BLOCK, GROUP_W), lambda i: (i, 0)),
        out_shape=jax.ShapeDtypeStruct((t, GROUP_W), F32),
        scratch_shapes=[pltpu.VMEM((SEQ_BLOCK + 2 * CONV_HALO, GROUP_W), F32)],
        name="conv_module",
        compiler_params=_params("arbitrary"),
    )(parts, parts, parts, parts, parts, parts, w, b, lg, lb)


def _gmlp_kernel(u_ref, v_ref, lg_ref, lb_ref, ws_ref, bias_ref, o_ref):
    u = jax.nn.gelu(u_ref[...])
    v = _layer_norm(jax.nn.gelu(v_ref[...]), lg_ref[...], lb_ref[...]).astype(BF16)
    lane_g = lax.broadcasted_iota(jnp.int32, (1, GROUP_W), 1) // (GROUP_W // N_HEADS)
    for ch in range(SEQ_BLOCK // CHUNK):
        rows = slice(ch * CHUNK, (ch + 1) * CHUNK)
        s = bias_ref[...]
        for g in range(N_HEADS):
            s = s + jnp.where(lane_g == g, _dot(ws_ref[g], v[rows]), 0.0)
        o_ref[rows, :] = u[rows] * s


def _gmlp(parts, lg, lb, ws, bias):
    t = parts.shape[0]
    vec = pl.BlockSpec((1, GROUP_W), lambda i: (0, 0))
    return pl.pallas_call(
        _gmlp_kernel,
        grid=(t // SEQ_BLOCK,),
        in_specs=[pl.BlockSpec((SEQ_BLOCK, GROUP_W), lambda i: (i, 8)),
                  pl.BlockSpec((SEQ_BLOCK, GROUP_W), lambda i: (i, 9)),
                  vec, vec,
                  pl.BlockSpec((N_HEADS, CHUNK, CHUNK), lambda i: (0, 0, 0)),
                  pl.BlockSpec((CHUNK, GROUP_W), lambda i: (0, 0))],
        out_specs=pl.BlockSpec((SEQ_BLOCK, GROUP_W), lambda i: (i, 0)),
        out_shape=jax.ShapeDtypeStruct((t, GROUP_W), F32),
        compiler_params=_params("arbitrary"),
        name="gmlp",
    )(parts, parts, lg, lb, ws, bias)


def _lane_group_id(width, group):
    return lax.broadcasted_iota(jnp.int32, (1, width), 1) // group


def _stack_masked(q, group, ids):
    lane = _lane_group_id(q.shape[-1], group)
    return jnp.concatenate([jnp.where(lane == g, q, jnp.zeros_like(q)) for g in ids], axis=0)


def _diff_lambda(dl_ref, lam_init):
    dl = dl_ref[...]
    a = jnp.sum(dl[0:1, :] * dl[1:2, :], axis=-1, keepdims=True)
    b = jnp.sum(dl[2:3, :] * dl[3:4, :], axis=-1, keepdims=True)
    return jnp.exp(a) - jnp.exp(b) + lam_init


def _ctx_attn_kernel(q_ref, k_ref, v_ref, dq_ref, dk_ref, dv_ref, qn_ref, kn_ref, dqn_ref, dkn_ref, sub_ref,
                     dl_ref, na_ref, df_ref, nk_ref, dkf_ref, *, lam_init):
    n = q_ref.shape[0]
    lane_h = _lane_group_id(GROUP_W, HEAD_DIM)
    qn = _group_rms(q_ref[...], qn_ref[...], HEAD_DIM)
    kn = _group_rms(k_ref[...], kn_ref[...], HEAD_DIM)
    nk_ref[...] = kn
    qs = _stack_masked(qn.astype(BF16), HEAD_DIM, range(N_HEADS))
    s = _dot_nt(qs, kn.astype(BF16)) * (HEAD_DIM ** -0.5)
    e = jnp.exp(s - jnp.max(s, axis=-1, keepdims=True))
    r = _dot(e.astype(BF16), v_ref[...].astype(BF16)) / jnp.sum(e, axis=-1, keepdims=True)
    o = jnp.zeros((n, GROUP_W), F32)
    for h in range(N_HEADS):
        o = o + jnp.where(lane_h == h, r[h * n:(h + 1) * n], 0.0)
    na_ref[...] = o
    lam = _diff_lambda(dl_ref, lam_init)
    dq = _group_rms(dq_ref[...], dqn_ref[...], DIFF_SUB)
    dk = _group_rms(dk_ref[...], dkn_ref[...], DIFF_SUB)
    dkf_ref[...] = dk
    dkb = dk.astype(BF16)
    dvb = dv_ref[...].astype(BF16)
    o = jnp.zeros((n, GROUP_W), F32)
    for h in range(N_HEADS):
        qs = _stack_masked(dq.astype(BF16), DIFF_SUB, (2 * h, 2 * h + 1))
        s = _dot_nt(qs, dkb) * (DIFF_SUB ** -0.5)
        e = jnp.exp(s - jnp.max(s, axis=-1, keepdims=True))
        inv = 1.0 / jnp.sum(e, axis=-1, keepdims=True)
        a = e[:n] * inv[:n] - e[n:] * (lam * inv[n:])
        o = o + jnp.where(lane_h == h, _dot(a.astype(BF16), dvb), 0.0)
    df_ref[...] = _group_rms(o, sub_ref[...], HEAD_DIM) * (1.0 - lam_init)


def _ctx_attention(parts, batch, seq, qn, kn, dqn, dkn, sub, dl, lam_init):
    t = batch * seq
    vec = pl.BlockSpec((1, GROUP_W), lambda b: (0, 0))
    col = lambda c: pl.BlockSpec((seq, GROUP_W), lambda b, c=c: (b, c))
    out = jax.ShapeDtypeStruct((t, GROUP_W), F32)
    ob = pl.BlockSpec((seq, GROUP_W), lambda b: (b, 0))
    return pl.pallas_call(
        functools.partial(_ctx_attn_kernel, lam_init=lam_init),
        grid=(batch,),
        in_specs=[col(2), col(3), col(4), col(5), col(6), col(7), vec, vec, vec, vec, vec,
                  pl.BlockSpec((4, DIFF_SUB), lambda b: (0, 0))],
        out_specs=[ob, ob, ob, ob],
        out_shape=[out, out, out, out],
        compiler_params=_params("arbitrary"),
        name="ctx_attention",
    )(parts, parts, parts, parts, parts, parts, qn, kn, dqn, dkn, sub, dl)


def _lat_prep_kernel(q_ref, k_ref, v_ref, dq_ref, dk_ref, dv_ref, qn_ref, kn_ref, dqn_ref, dkn_ref,
                     cos_ref, sin_ref, oq, ok, ov, odq, odk, odv):
    oq[...] = _group_rms(q_ref[...], qn_ref[...], HEAD_DIM).astype(BF16)
    ok[...] = _group_rms(k_ref[...], kn_ref[...], HEAD_DIM).astype(BF16)
    ov[...] = v_ref[...].astype(BF16)
    odv[...] = dv_ref[...].astype(BF16)
    half = DIFF_SUB // 4
    lane = lax.broadcasted_iota(jnp.int32, (1, GROUP_W), 1)
    lower = (lane % (2 * half)) < half
    cos = cos_ref[...]
    sin = sin_ref[...]

    def rope(x):
        partner = jnp.where(lower, pltpu.roll(x, GROUP_W - half, 1), pltpu.roll(x, half, 1))
        return x * cos + partner * sin

    odq[...] = rope(_group_rms(dq_ref[...], dqn_ref[...], DIFF_SUB)).astype(BF16)
    odk[...] = rope(_group_rms(dk_ref[...], dkn_ref[...], DIFF_SUB)).astype(BF16)


def _rope_tables(seq):
    half = DIFF_SUB // 4
    t = jnp.arange(seq)
    rows = (t // GRID_W).astype(F32)
    cols = (t % GRID_W).astype(F32)
    inv = ROPE_BASE ** (-jnp.arange(half, dtype=F32) / half)
    ang_r = rows[:, None] * inv[None, :]
    ang_c = cols[:, None] * inv[None, :]
    cos32 = jnp.concatenate([jnp.cos(ang_r), jnp.cos(ang_r), jnp.cos(ang_c), jnp.cos(ang_c)], axis=-1)
    sin32 = jnp.concatenate([-jnp.sin(ang_r), jnp.sin(ang_r), -jnp.sin(ang_c), jnp.sin(ang_c)], axis=-1)
    reps = GROUP_W // DIFF_SUB
    return jnp.tile(cos32, (1, reps)), jnp.tile(sin32, (1, reps))


def _lat_prep(parts, row0, batch, seq, qn, kn, dqn, dkn, cos, sin):
    t = batch * seq
    nb = seq // SEQ_BLOCK
    r0 = row0 // SEQ_BLOCK
    vec = pl.BlockSpec((1, GROUP_W), lambda i: (0, 0))
    col = lambda c: pl.BlockSpec((SEQ_BLOCK, GROUP_W), lambda i, c=c: (r0 + i, c))
    tab = pl.BlockSpec((SEQ_BLOCK, GROUP_W), lambda i: (i % nb, 0))
    out = jax.ShapeDtypeStruct((t, GROUP_W), BF16)
    ob = pl.BlockSpec((SEQ_BLOCK, GROUP_W), lambda i: (i, 0))
    return pl.pallas_call(
        _lat_prep_kernel,
        grid=(t // SEQ_BLOCK,),
        in_specs=[col(2), col(3), col(4), col(5), col(6), col(7), vec, vec, vec, vec, tab, tab],
        out_specs=[ob] * 6,
        out_shape=[out] * 6,
        compiler_params=_params("arbitrary"),
        name="lat_prep",
    )(parts, parts, parts, parts, parts, parts, qn, kn, dqn, dkn, cos, sin)


def _na_row_start(r, rows):
    return jnp.clip(r - NA_WIN_R // 2, 0, rows - NA_WIN_R)


def _na_lat_kernel(q_ref, k_ref, v_ref, ck_ref, cv_ref, bt_ref, o_ref, *, rows):
    r = pl.program_id(1)
    start = pl.multiple_of(_na_row_start(r, rows) * GRID_W, GRID_W)
    n_loc = NA_WIN_R * GRID_W
    kl = k_ref[pl.ds(start, n_loc), :]
    vl = v_ref[pl.ds(start, n_loc), :]
    qs = _stack_masked(q_ref[...], HEAD_DIM, range(N_HEADS))
    scale = HEAD_DIM ** -0.5
    s_loc = _dot_nt(qs, kl) * scale + bt_ref[0]
    s_ctx = _dot_nt(qs, ck_ref[0]) * scale
    m = jnp.maximum(jnp.max(s_loc, axis=-1, keepdims=True), jnp.max(s_ctx, axis=-1, keepdims=True))
    e_loc = jnp.exp(s_loc - m)
    e_ctx = jnp.exp(s_ctx - m)
    l = jnp.sum(e_loc, axis=-1, keepdims=True) + jnp.sum(e_ctx, axis=-1, keepdims=True)
    res = (_dot(e_loc.astype(BF16), vl) + _dot(e_ctx.astype(BF16), cv_ref[0])) / l
    lane_h = _lane_group_id(GROUP_W, HEAD_DIM)
    o = jnp.zeros((GRID_W, GROUP_W), F32)
    for h in range(N_HEADS):
        o = o + jnp.where(lane_h == h, res[h * GRID_W:(h + 1) * GRID_W], 0.0)
    o_ref[...] = o


def _na_bias_table(rel_bias, rows):
    wr = NA_WIN_R
    c = np.arange(GRID_W)
    c0 = np.clip(c - NA_WIN_C // 2, 0, GRID_W - NA_WIN_C)
    in_win = (c[None, :] >= c0[:, None]) & (c[None, :] < c0[:, None] + NA_WIN_C)
    dc_idx = np.clip(c[None, :] - c[:, None] + NA_WIN_C - 1, 0, 2 * NA_WIN_C - 2)
    off = np.arange(wr)
    dr_idx = (np.arange(wr)[None, :] - off[:, None]) + NA_WIN_R - 1
    rb = rel_bias[:, dr_idx]
    bias = jnp.take(rb, jnp.asarray(dc_idx), axis=-1)
    bias = jnp.where(jnp.asarray(in_win)[None, None, None], bias, NEG)
    bias = bias.transpose(1, 0, 3, 2, 4)
    return bias.reshape(wr, N_HEADS * GRID_W, wr * GRID_W)


def _na_latent(qn, kn, v, ck, cv, bias_tab, batch, seq):
    rows = seq // GRID_W
    nctx = ck.shape[1]
    full = pl.BlockSpec((seq, GROUP_W), lambda b, r: (b, 0))
    ctx = pl.BlockSpec((1, nctx, GROUP_W), lambda b, r: (b, 0, 0))
    return pl.pallas_call(
        functools.partial(_na_lat_kernel, rows=rows),
        grid=(batch, rows),
        in_specs=[pl.BlockSpec((GRID_W, GROUP_W), lambda b, r: (b * rows + r, 0)),
                  full, full, ctx, ctx,
                  pl.BlockSpec((1, N_HEADS * GRID_W, NA_WIN_R * GRID_W),
                               lambda b, r: (r - _na_row_start(r, rows), 0, 0))],
        out_specs=pl.BlockSpec((GRID_W, GROUP_W), lambda b, r: (b * rows + r, 0)),
        out_shape=jax.ShapeDtypeStruct((batch * seq, GROUP_W), F32),
        compiler_params=_params("arbitrary", "arbitrary"),
        name="na_latent",
    )(qn, kn, v, ck, cv, bias_tab)


def _diff_lat_kernel(q_ref, k_ref, v_ref, ck_ref, cv_ref, sub_ref, dl_ref, o_ref, *, lam_init):
    n = q_ref.shape[0]
    lam = _diff_lambda(dl_ref, lam_init)
    lane_s = _lane_group_id(GROUP_W, DIFF_SUB)
    q = q_ref[...]
    scale = DIFF_SUB ** -0.5

    def head(h, o):
        qs = jnp.concatenate([jnp.where(lane_s == 2 * h, q, jnp.zeros_like(q)),
                              jnp.where(lane_s == 2 * h + 1, q, jnp.zeros_like(q))], axis=0)
        s_l = _dot_nt(qs, k_ref[...]) * scale
        s_c = _dot_nt(qs, ck_ref[0]) * scale
        m = jnp.maximum(jnp.max(s_l, axis=-1, keepdims=True), jnp.max(s_c, axis=-1, keepdims=True))
        e_l = jnp.exp(s_l - m)
        e_c = jnp.exp(s_c - m)
        inv = 1.0 / (jnp.sum(e_l, axis=-1, keepdims=True) + jnp.sum(e_c, axis=-1, keepdims=True))
        c1 = inv[:n]
        c2 = lam * inv[n:]
        a_l = e_l[:n] * c1 - e_l[n:] * c2
        a_c = e_c[:n] * c1 - e_c[n:] * c2
        res = _dot(a_l.astype(BF16), v_ref[...]) + _dot(a_c.astype(BF16), cv_ref[0])
        return o + jnp.where(lane_s // 2 == h, res, 0.0)

    o = lax.fori_loop(0, N_HEADS, head, jnp.zeros((n, GROUP_W), F32))
    o_ref[...] = _group_rms(o, sub_ref[...], HEAD_DIM) * (1.0 - lam_init)


def _diff_latent(dq, dk, dv, ck, cv, sub, dl, batch, seq, lam_init):
    nq = seq // DIFF_QBLOCK
    nctx = ck.shape[1]
    full = pl.BlockSpec((seq, GROUP_W), lambda b, i: (b, 0))
    ctx = pl.BlockSpec((1, nctx, GROUP_W), lambda b, i: (b, 0, 0))
    return pl.pallas_call(
        functools.partial(_diff_lat_kernel, lam_init=lam_init),
        grid=(batch, nq),
        in_specs=[pl.BlockSpec((DIFF_QBLOCK, GROUP_W), lambda b, i: (b * nq + i, 0)),
                  full, full, ctx, ctx,
                  pl.BlockSpec((1, GROUP_W), lambda b, i: (0, 0)),
                  pl.BlockSpec((4, DIFF_SUB), lambda b, i: (0, 0))],
        out_specs=pl.BlockSpec((DIFF_QBLOCK, GROUP_W), lambda b, i: (b * nq + i, 0)),
        out_shape=jax.ShapeDtypeStruct((batch * seq, GROUP_W), F32),
        compiler_params=_params("arbitrary", "arbitrary"),
        name="diff_latent",
    )(dq, dk, dv, ck, cv, sub, dl)


def _out_proj_kernel(a_ref, b_ref, c_ref, d_ref, x_ref, mod_ref, w_ref, o_ref):
    mixed = jnp.zeros(o_ref.shape, F32)
    for g, ref in enumerate((a_ref, b_ref, c_ref, d_ref)):
        mixed = mixed + _dot(ref[...].astype(BF16), w_ref[g * GROUP_W:(g + 1) * GROUP_W, :])
    o_ref[...] = x_ref[...] + mod_ref[0, 2:3, :] * mixed


def _out_proj(conv_o, na_o, diff_o, gm_o, x, mod, w, layout):
    t = x.shape[0]
    part = pl.BlockSpec((ROW_BLOCK, GROUP_W), lambda i: (i, 0))
    row = pl.BlockSpec((ROW_BLOCK, D_MODEL), lambda i: (i, 0))
    return pl.pallas_call(
        _out_proj_kernel,
        grid=(t // ROW_BLOCK,),
        in_specs=[part, part, part, part, row,
                  _mod_spec(ROW_BLOCK, layout),
                  pl.BlockSpec((D_MODEL, D_MODEL), lambda i: (0, 0))],
        out_specs=row,
        out_shape=jax.ShapeDtypeStruct((t, D_MODEL), F32),
        compiler_params=_params("arbitrary"),
        name="out_proj",
    )(conv_o, na_o, diff_o, gm_o, x, mod, w)


def _sort_desc(vals):
    v = list(vals)
    n = len(v)
    k = 2
    while k <= n:
        j = k // 2
        while j >= 1:
            for i in range(n):
                l = i ^ j
                if l > i:
                    hi, lo = jnp.maximum(v[i], v[l]), jnp.minimum(v[i], v[l])
                    v[i], v[l] = (hi, lo) if (i & k) == 0 else (lo, hi)
            j //= 2
        k *= 2
    return v


def _merge_desc(v):
    v = list(v)
    n = len(v)
    j = n // 2
    while j >= 1:
        for i in range(n):
            l = i ^ j
            if l > i:
                v[i], v[l] = jnp.maximum(v[i], v[l]), jnp.minimum(v[i], v[l])
        j //= 2
    return v


def _top_half(a, b):
    n = len(a)
    return [jnp.maximum(a[i], b[n - 1 - i]) for i in range(n)]


def _top16_keys(s):
    k = PEER_TOPK
    v = _sort_desc([s[a * SUBLANES:(a + 1) * SUBLANES] for a in range(PEER_NKEYS // SUBLANES)])
    shift = SUBLANES // 2
    while shift >= 1:
        rolled = [pltpu.roll(x, shift, 0) for x in v]
        v = _merge_desc(_top_half(v, rolled))
        shift //= 2
    return v[:k]


def _paired_bf16(x):
    hi = pltpu.bitcast(x.astype(BF16).astype(F32), jnp.uint32)
    return hi | (hi >> 16)


def _route_kernel(x_ref, mod_ref, g_ref, wq_ref, sk_ref, h2_ref, n0_ref, r1_ref, e0_ref, e1_ref, s0_ref, s1_ref):
    x = x_ref[...]
    y = x * lax.rsqrt(jnp.mean(x * x, axis=-1, keepdims=True) + EPS) * g_ref[...]
    h2 = y * (1.0 + mod_ref[0, 4:5, :]) + mod_ref[0, 3:4, :]
    h2t = h2.T.astype(BF16)
    h2_ref[...] = h2t
    qt = _dot(wq_ref[...], h2t)
    tb = x.shape[0]
    sub = lax.broadcasted_iota(jnp.int32, (SUBLANES, tb), 0)
    k = PEER_TOPK
    tops = [[jnp.zeros((SUBLANES, tb), F32)] * k, [jnp.zeros((SUBLANES, tb), F32)] * k]
    for h in range(PEER_HEADS):
        for p in range(2):
            base = (2 * h + p) * PEER_NKEYS
            s = _dot(sk_ref[p], qt[base:base + PEER_NKEYS].astype(BF16))
            (s0_ref if p == 0 else s1_ref)[h] = s
            top = _top16_keys(s)
            tops[p] = [jnp.where(sub == h, top[a], tops[p][a]) for a in range(k)]
    cand = [tops[0][a] + tops[1][b] for a in range(k) for b in range(k) if (a + 1) * (b + 1) <= k]
    pad = [jnp.full((SUBLANES, tb), NEG, F32)] * (4 * k - len(cand))
    groups = [_sort_desc((cand + pad)[g * k:(g + 1) * k]) for g in range(4)]
    best = _top_half(_merge_desc(_top_half(groups[0], groups[1])), _merge_desc(_top_half(groups[2], groups[3])))
    thr = functools.reduce(jnp.minimum, best)
    m = tops[0][0] + tops[1][0]
    z = functools.reduce(jnp.add, [jnp.where(c >= thr, jnp.exp(c - m), 0.0) for c in cand])
    zinv = 1.0 / z
    counts, first = [], 0
    for a in range(k):
        n_b = k // (a + 1)
        counts.append(functools.reduce(jnp.add, [jnp.where(c >= thr, 1.0, 0.0) for c in cand[first:first + n_b]]))
        first += n_b
    for h in range(PEER_HEADS):
        s0 = s0_ref[h]
        s1 = s1_ref[h]
        n0 = jnp.zeros_like(s0)
        r1 = jnp.full_like(s1, float(k))
        for a in reversed(range(k)):
            n0 = jnp.where(s0 >= tops[0][a][h:h + 1], counts[a][h:h + 1], n0)
            r1 = jnp.where(s1 >= tops[1][a][h:h + 1], float(a), r1)
        n0_ref[h, 0] = _paired_bf16(n0)
        r1_ref[h, 0] = pltpu.bitcast(r1.astype(BF16), jnp.uint32)
        e0_ref[h, 0] = _paired_bf16(jnp.exp(s0 - tops[0][0][h:h + 1]) * zinv[h:h + 1])
        e1_ref[h, 0] = pltpu.bitcast(jnp.exp(s1 - tops[1][0][h:h + 1]).astype(BF16), jnp.uint32)


def _route(x1, mod, g, wq_t, sk, layout):
    t = x1.shape[0]
    tb = ROUTE_BLOCK
    assert tb == LANES
    tiles = t // LANES
    first_key = jax.ShapeDtypeStruct((PEER_HEADS, tiles, PEER_NKEYS, LANES), jnp.uint32)
    second_key = jax.ShapeDtypeStruct((PEER_HEADS, tiles, PEER_NKEYS // 2, LANES), jnp.uint32)
    fkb = pl.BlockSpec((PEER_HEADS, 1, PEER_NKEYS, LANES), lambda i: (0, i, 0, 0))
    skb = pl.BlockSpec((PEER_HEADS, 1, PEER_NKEYS // 2, LANES), lambda i: (0, i, 0, 0))
    scores = pltpu.VMEM((PEER_HEADS, PEER_NKEYS, tb), F32)
    return pl.pallas_call(
        _route_kernel,
        grid=(t // tb,),
        in_specs=[pl.BlockSpec((tb, D_MODEL), lambda i: (i, 0)),
                  _mod_spec(tb, layout),
                  pl.BlockSpec((1, D_MODEL), lambda i: (0, 0)),
                  pl.BlockSpec(wq_t.shape, lambda i: (0, 0)),
                  pl.BlockSpec(sk.shape, lambda i: (0, 0, 0))],
        out_specs=[pl.BlockSpec((D_MODEL, tb), lambda i: (0, i)), fkb, skb, fkb, skb],
        out_shape=[jax.ShapeDtypeStruct((D_MODEL, t), BF16), first_key, second_key, first_key, second_key],
        scratch_shapes=[scores, scores],
        compiler_params=_params("arbitrary"),
        name="peer_route",
    )(x1, mod, g, wq_t, sk)


def _rows_as_bf16(rows):
    tile = pltpu.bitcast(rows, BF16)
    return jnp.concatenate([tile] * (PEER_NKEYS // tile.shape[0]), axis=0)


def _peer_kernel(h2_ref, pu_ref, pv_ref, n0_ref, r1_ref, e0_ref, e1_ref, x_ref, mod_ref, o_ref,
                 acc_ref, a0_ref, a1_ref, w0_ref, w1_ref):
    e = pl.program_id(1)

    @pl.when(e == 0)
    def _():
        acc_ref[...] = jnp.zeros_like(acc_ref)

    tb = h2_ref.shape[1]
    n_sub = pv_ref.shape[0]
    sub_keys = MXU_DIM // PEER_NKEYS
    a_refs = (a0_ref, a1_ref)
    w_refs = (w0_ref, w1_ref)

    def first_matmul(k, slot):
        start = k * MXU_DIM if isinstance(k, int) else pl.multiple_of(k * MXU_DIM, MXU_DIM)
        a_refs[slot][...] = _dot(pu_ref[pl.ds(start, MXU_DIM), :], h2_ref[...])

    def second_matmul(k, slot):
        acc_ref[...] += _dot(pv_ref[k], w_refs[slot][...])

    def gate(k, slot):
        key = e * PEER_IBLOCK + sub_keys * k
        for c in range(tb // LANES):
            cols = slice(c * LANES, (c + 1) * LANES)
            gates = [jnp.zeros((PEER_NKEYS, LANES), BF16)] * sub_keys
            for h in range(PEER_HEADS):
                r1 = pltpu.bitcast(r1_ref[h, c], BF16)
                e1 = pltpu.bitcast(e1_ref[h, c], BF16)
                for d in range(sub_keys):
                    row = pl.ds(key + d, SUBLANES, stride=0)
                    n0_d = _rows_as_bf16(n0_ref[h, c, row, :])
                    e0_d = _rows_as_bf16(e0_ref[h, c, row, :])
                    gates[d] = gates[d] + jnp.where(r1 < n0_d, e1 * e0_d, jnp.zeros_like(e1))
            for d in range(sub_keys):
                rows = slice(d * PEER_NKEYS, (d + 1) * PEER_NKEYS)
                w_refs[slot][rows, cols] = _gelu_tanh(a_refs[slot][rows, cols]).astype(BF16) * gates[d]

    def steady(j, carry):
        k = 2 * j + 1
        first_matmul(k + 1, 0)
        gate(k, 1)
        second_matmul(k - 1, 0)
        first_matmul(k + 2, 1)
        gate(k + 1, 0)
        second_matmul(k, 1)
        return carry

    assert n_sub % 2 == 0 and n_sub >= 4
    first_matmul(0, 0)
    first_matmul(1, 1)
    gate(0, 0)
    lax.fori_loop(0, n_sub // 2 - 1, steady, 0)
    gate(n_sub - 1, 1)
    second_matmul(n_sub - 2, 0)
    second_matmul(n_sub - 1, 1)

    @pl.when(e == pl.num_programs(1) - 1)
    def _():
        o_ref[...] = x_ref[...] + mod_ref[0, 5:6, :] * acc_ref[...].T


def _peer(h2t, pu, pv_t, n0, r1, e0, e1, x1, mod, layout):
    t = x1.shape[0]
    tb = PEER_TOKENS
    eb = PEER_IBLOCK * PEER_NKEYS
    n_exp = pu.shape[0]
    n_sub = eb // MXU_DIM
    assert pv_t.shape == (n_exp // MXU_DIM, D_MODEL, MXU_DIM)
    fk = pl.BlockSpec((PEER_HEADS, tb // LANES, PEER_NKEYS, LANES), lambda i, e: (0, i, 0, 0))
    sk = pl.BlockSpec((PEER_HEADS, tb // LANES, PEER_NKEYS // 2, LANES), lambda i, e: (0, i, 0, 0))
    return pl.pallas_call(
        _peer_kernel,
        grid=(t // tb, n_exp // eb),
        in_specs=[pl.BlockSpec((D_MODEL, tb), lambda i, e: (0, i)),
                  pl.BlockSpec((eb, D_MODEL), lambda i, e: (e, 0)),
                  pl.BlockSpec((n_sub, D_MODEL, MXU_DIM), lambda i, e: (e, 0, 0)),
                  fk, sk, fk, sk,
                  pl.BlockSpec((tb, D_MODEL), lambda i, e: (i, 0)),
                  _mod_spec(tb, layout)],
        out_specs=pl.BlockSpec((tb, D_MODEL), lambda i, e: (i, 0)),
        out_shape=jax.ShapeDtypeStruct((t, D_MODEL), F32),
        scratch_shapes=[pltpu.VMEM((D_MODEL, tb), F32),
                        pltpu.VMEM((MXU_DIM, tb), F32), pltpu.VMEM((MXU_DIM, tb), F32),
                        pltpu.VMEM((MXU_DIM, tb), BF16), pltpu.VMEM((MXU_DIM, tb), BF16)],
        compiler_params=pltpu.CompilerParams(dimension_semantics=("arbitrary", "arbitrary"),
                                             vmem_limit_bytes=PEER_VMEM_LIMIT),
        name="peer_dense",
    )(h2t, pu, pv_t, n0, r1, e0, e1, x1, mod)


def _merge_cache(kv):
    b, h, l, d = kv.shape
    return kv.transpose(0, 2, 1, 3).reshape(b, l, h * d).astype(BF16)


def _split_heads(x, batch, seq):
    return x.reshape(batch, seq, N_HEADS, HEAD_DIM).transpose(0, 2, 1, 3)


def kernel(x_prompt, x_sample, cache_na_kv, cache_diff_kv, c, c_ctx, w_mod, b_mod, norm1_g, norm2_g, w_in, conv_w, conv_b, conv_ln_g, conv_ln_b, na_qn_g, na_kn_g, na_rel_bias, diff_qn_g, diff_kn_g, diff_lambda, diff_subln_g, gmlp_ln_g, gmlp_ln_b, gmlp_ws, gmlp_bs, w_out, peer_wq, peer_sub_keys, peer_u, peer_v):
    batch, seq, _ = x_prompt.shape
    dbatch, dseq, _ = x_sample.shape
    t_ctx = batch * seq
    layout = (t_ctx, dseq)
    x = jnp.concatenate([x_prompt.reshape(t_ctx, D_MODEL), x_sample.reshape(dbatch * dseq, D_MODEL)], axis=0)
    cond = jnp.concatenate([c_ctx[None, :], c, jnp.zeros((SUBLANES - 1 - dbatch, D_MODEL), F32)], axis=0)
    mod_all = _modulation(cond, w_mod, b_mod).reshape(DEPTH, SUBLANES, 6, D_MODEL)
    cos, sin = _rope_tables(dseq)
    row = lambda v: v.reshape(1, -1)
    tile = lambda v: jnp.tile(v, GROUP_W // v.shape[0]).reshape(1, GROUP_W)
    na_states, diff_states = [], []
    for l in range(DEPTH):
        lam_init = 0.8 - 0.6 * math.exp(-0.3 * l)
        mod = mod_all[l]
        parts = _in_proj(x, mod, row(norm1_g[l]), w_in[l].astype(BF16), layout)
        conv_o = _conv_module(parts, conv_w[l], row(conv_b[l]), row(conv_ln_g[l]), row(conv_ln_b[l]), t_ctx, dseq)
        gm_bias = jnp.repeat(gmlp_bs[l].T, GROUP_W // N_HEADS, axis=1)
        gm_o = _gmlp(parts, row(gmlp_ln_g[l]), row(gmlp_ln_b[l]), gmlp_ws[l].astype(BF16), gm_bias)
        qn, kn = tile(na_qn_g[l]), tile(na_kn_g[l])
        dqn, dkn, sub = tile(diff_qn_g[l]), tile(diff_kn_g[l]), tile(diff_subln_g[l])
        na_c, df_c, nk_c, dk_c = _ctx_attention(parts, batch, seq, qn, kn, dqn, dkn, sub, diff_lambda[l], lam_init)
        lq, lk, lv, ldq, ldk, ldv = _lat_prep(parts, t_ctx, dbatch, dseq, qn, kn, dqn, dkn, cos, sin)
        na_l = _na_latent(lq, lk, lv, _merge_cache(cache_na_kv[:, l, 0]), _merge_cache(cache_na_kv[:, l, 1]),
                          _na_bias_table(na_rel_bias[l], dseq // GRID_W), dbatch, dseq)
        ldk_all = ldk
        df_l = _diff_latent(ldq, ldk_all, ldv, _merge_cache(cache_diff_kv[:, l, 0]),
                            _merge_cache(cache_diff_kv[:, l, 1]), sub, diff_lambda[l], dbatch, dseq, lam_init)
        na_o = jnp.concatenate([na_c, na_l], axis=0)
        diff_o = jnp.concatenate([df_c, df_l], axis=0)
        x1 = _out_proj(conv_o, na_o, diff_o, gm_o, x, mod, w_out[l].astype(BF16), layout)
        h2t, n0, r1, e0, e1 = _route(x1, mod, row(norm2_g[l]), peer_wq[l].T.astype(BF16),
                                          peer_sub_keys[l].astype(BF16), layout)
        pv_t = peer_v[l].astype(BF16).reshape(-1, MXU_DIM, D_MODEL).transpose(0, 2, 1)
        x = _peer(h2t, peer_u[l].astype(BF16), pv_t, n0, r1, e0, e1, x1, mod, layout)
        nv_c = parts[:t_ctx, 4 * GROUP_W:5 * GROUP_W]
        dv_c = parts[:t_ctx, 7 * GROUP_W:8 * GROUP_W]
        na_states.append(jnp.stack([_split_heads(nk_c, batch, seq), _split_heads(nv_c, batch, seq)], axis=1))
        diff_states.append(jnp.stack([_split_heads(dk_c, batch, seq), _split_heads(dv_c, batch, seq)], axis=1))
    y_prompt = x[:t_ctx].reshape(batch, seq, D_MODEL)
    y_sample = x[t_ctx:].reshape(dbatch, dseq, D_MODEL)
    return (y_prompt, y_sample, jnp.stack(na_states, axis=1), jnp.stack(diff_states, axis=1))
```

```python
import functools
import math

import numpy as np
import jax
import jax.numpy as jnp
from jax import lax
from jax.experimental import pallas as pl
from jax.experimental.pallas import tpu as pltpu

F32 = jnp.float32
BF16 = jnp.bfloat16

D_MODEL = 1024
DEPTH = 2
GRID_W = 64
GROUP_W = 256
HEAD_DIM = 64
N_HEADS = 4
DIFF_SUB = 32
CONV_K = 31
CONV_HALO = 16
NA_WIN_R = 8
NA_WIN_C = 16
ROPE_BASE = 10000.0
CHUNK = 128
PEER_HEADS = 8
PEER_NKEYS = 128
PEER_TOPK = 16
EPS = 1e-6
NEG = -1e30

LANES = 128
SUBLANES = 8
MXU_DIM = 256
VMEM_LIMIT = 48 * 1024 * 1024

SEQ_BLOCK = 256
ROW_BLOCK = 256
ROUTE_BLOCK = 128
PEER_TOKENS = 512
PEER_IBLOCK = 16
PEER_VMEM_LIMIT = 56 * 1024 * 1024
DIFF_QBLOCK = 128
DIFF_KCHUNK = 512


def _params(*sem):
    return pltpu.CompilerParams(dimension_semantics=sem, vmem_limit_bytes=VMEM_LIMIT)


def _dot(a, b):
    return jnp.dot(a, b, preferred_element_type=F32)


def _dot_nt(a, b):
    return lax.dot_general(a, b, (((1,), (1,)), ((), ())), preferred_element_type=F32)


def _split_dot(a, b):
    a1 = a.astype(BF16)
    r1 = a - a1.astype(F32)
    a2 = r1.astype(BF16)
    a3 = (r1 - a2.astype(F32)).astype(BF16)
    return _dot(a1, b) + _dot(a2, b) + _dot(a3, b)


def _group_ones(n, group):
    r = lax.broadcasted_iota(jnp.int32, (n, n), 0) // group
    c = lax.broadcasted_iota(jnp.int32, (n, n), 1) // group
    return (r == c).astype(BF16)


def _group_rms(x, gain, group):
    ssq = _split_dot(x * x, _group_ones(x.shape[-1], group))
    return x * lax.rsqrt(ssq * (1.0 / group) + EPS) * gain


def _gelu_tanh(x):
    k = -2.0 * math.sqrt(2.0 / math.pi) * math.log2(math.e)
    return x / (1.0 + jnp.exp2(x * (k + (k * 0.044715) * (x * x))))


def _layer_norm(y, g, b):
    mu = jnp.mean(y, axis=-1, keepdims=True)
    yc = y - mu
    return yc * lax.rsqrt(jnp.mean(yc * yc, axis=-1, keepdims=True) + EPS) * g + b


def _mod_spec(rows_per_block, layout):
    t_ctx, lat_seq = layout
    assert t_ctx % rows_per_block == 0 and lat_seq % rows_per_block == 0
    def index(i, *_):
        return (jnp.maximum((i * rows_per_block - t_ctx) // lat_seq + 1, 0), 0, 0)
    return pl.BlockSpec((1, 6, D_MODEL), index)


def _mod_kernel(c_ref, w_ref, b_ref, o_ref):
    c = c_ref[...]
    s = c * jax.nn.sigmoid(c)
    o_ref[0] = jnp.dot(s, w_ref[0], preferred_element_type=F32,
                       precision=lax.Precision.HIGHEST) + b_ref[0]


def _modulation(cond, w_mod, b_mod):
    nb = 4
    cols = 6 * D_MODEL // nb
    return pl.pallas_call(
        _mod_kernel,
        grid=(DEPTH, nb),
        in_specs=[pl.BlockSpec((SUBLANES, D_MODEL), lambda l, j: (0, 0)),
                  pl.BlockSpec((1, D_MODEL, cols), lambda l, j: (l, 0, j)),
                  pl.BlockSpec((1, 1, cols), lambda l, j: (l, 0, j))],
        out_specs=pl.BlockSpec((1, SUBLANES, cols), lambda l, j: (l, 0, j)),
        out_shape=jax.ShapeDtypeStruct((DEPTH, SUBLANES, 6 * D_MODEL), F32),
        name="modulation",
        compiler_params=_params("arbitrary", "arbitrary"),
    )(cond, w_mod, b_mod.reshape(DEPTH, 1, 6 * D_MODEL))


def _in_proj_kernel(x_ref, mod_ref, g_ref, w_ref, o_ref):
    x = x_ref[...]
    y = x * lax.rsqrt(jnp.mean(x * x, axis=-1, keepdims=True) + EPS) * g_ref[...]
    h = y * (1.0 + mod_ref[0, 1:2, :]) + mod_ref[0, 0:1, :]
    o_ref[...] = _dot(h.astype(BF16), w_ref[...])


def _in_proj(x, mod, g, w, layout):
    t = x.shape[0]
    n = w.shape[1]
    return pl.pallas_call(
        _in_proj_kernel,
        grid=(t // ROW_BLOCK,),
        in_specs=[pl.BlockSpec((ROW_BLOCK, D_MODEL), lambda i: (i, 0)),
                  _mod_spec(ROW_BLOCK, layout),
                  pl.BlockSpec((1, D_MODEL), lambda i: (0, 0)),
                  pl.BlockSpec((D_MODEL, n), lambda i: (0, 0))],
        out_specs=pl.BlockSpec((ROW_BLOCK, n), lambda i: (i, 0)),
        out_shape=jax.ShapeDtypeStruct((t, n), F32),
        name="in_proj",
        compiler_params=_params("arbitrary"),
    )(x, mod, g, w)


def _conv_kernel(ac, gc, ap, gp, an, gn, w_ref, b_ref, lg_ref, lb_ref, o_ref, pad_ref, *, ctx_blocks, seq_blocks):
    i = pl.program_id(0)
    is_ctx = i < ctx_blocks
    j = i % seq_blocks
    first = jnp.logical_or(is_ctx, j == 0)
    last = jnp.logical_or(is_ctx, j == seq_blocks - 1)
    yp = ap[...] * jax.nn.sigmoid(gp[...])
    yn = an[...] * jax.nn.sigmoid(gn[...])
    pad_ref[0:CONV_HALO, :] = jnp.where(first, 0.0, yp)
    pad_ref[CONV_HALO:CONV_HALO + SEQ_BLOCK, :] = ac[...] * jax.nn.sigmoid(gc[...])
    pad_ref[CONV_HALO + SEQ_BLOCK:, :] = jnp.where(last, 0.0, yn)
    off = CONV_HALO - CONV_K // 2
    acc = jnp.zeros((SEQ_BLOCK, GROUP_W), F32)
    for k in range(CONV_K):
        acc = acc + pad_ref[off + k:off + k + SEQ_BLOCK, :] * w_ref[k:k + 1, :]
    y = _layer_norm(acc + b_ref[...], lg_ref[...], lb_ref[...])
    o_ref[...] = y * jax.nn.sigmoid(y)


def _conv_module(parts, w, b, lg, lb, ctx_tokens, lat_seq):
    t = parts.shape[0]
    nblk = t // SEQ_BLOCK
    hb = SEQ_BLOCK // CONV_HALO
    last_halo = t // CONV_HALO - 1
    vec = pl.BlockSpec((1, GROUP_W), lambda i: (0, 0))
    kern = functools.partial(_conv_kernel, ctx_blocks=ctx_tokens // SEQ_BLOCK, seq_blocks=lat_seq // SEQ_BLOCK)
    return pl.pallas_call(
        kern,
        grid=(nblk,),
        in_specs=[pl.BlockSpec((SEQ_BLOCK, GROUP_W), lambda i: (i, 0)),
                  pl.BlockSpec((SEQ_BLOCK, GROUP_W), lambda i: (i, 1)),
                  pl.BlockSpec((CONV_HALO, GROUP_W), lambda i: (jnp.maximum(i * hb - 1, 0), 0)),
                  pl.BlockSpec((CONV_HALO, GROUP_W), lambda i: (jnp.maximum(i * hb - 1, 0), 1)),
                  pl.BlockSpec((CONV_HALO, GROUP_W), lambda i: (jnp.minimum((i + 1) * hb, last_halo), 0)),
                  pl.BlockSpec((CONV_HALO, GROUP_W), lambda i: (jnp.minimum((i + 1) * hb, last_halo), 1)),
                  pl.BlockSpec((CONV_K, GROUP_W), lambda i: (0, 0)),
                  vec, vec, vec],
        out_specs=pl.BlockSpec((SEQ_BLOCK, GROUP_W), lambda i: (i, 0)),
        out_shape=jax.ShapeDtypeStruct((t, GROUP_W), F32),
        scratch_shapes=[pltpu.VMEM((SEQ_BLOCK + 2 * CONV_HALO, GROUP_W), F32)],
        name="conv_module",
        compiler_params=_params("arbitrary"),
    )(parts, parts, parts, parts, parts, parts, w, b, lg, lb)


def _gmlp_kernel(u_ref, v_ref, lg_ref, lb_ref, ws_ref, bias_ref, o_ref):
    u = jax.nn.gelu(u_ref[...])
    v = _layer_norm(jax.nn.gelu(v_ref[...]), lg_ref[...], lb_ref[...]).astype(BF16)
    lane_g = lax.broadcasted_iota(jnp.int32, (1, GROUP_W), 1) // (GROUP_W // N_HEADS)
    for ch in range(SEQ_BLOCK // CHUNK):
        rows = slice(ch * CHUNK, (ch + 1) * CHUNK)
        s = bias_ref[...]
        for g in range(N_HEADS):
            s = s + jnp.where(lane_g == g, _dot(ws_ref[g], v[rows]), 0.0)
        o_ref[rows, :] = u[rows] * s


def _gmlp(parts, lg, lb, ws, bias):
    t = parts.shape[0]
    vec = pl.BlockSpec((1, GROUP_W), lambda i: (0, 0))
    return pl.pallas_call(
        _gmlp_kernel,
        grid=(t // SEQ_BLOCK,),
        in_specs=[pl.BlockSpec((SEQ_BLOCK, GROUP_W), lambda i: (i, 8)),
                  pl.BlockSpec((SEQ_BLOCK, GROUP_W), lambda i: (i, 9)),
                  vec, vec,
                  pl.BlockSpec((N_HEADS, CHUNK, CHUNK), lambda i: (0, 0, 0)),
                  pl.BlockSpec((CHUNK, GROUP_W), lambda i: (0, 0))],
        out_specs=pl.BlockSpec((SEQ_BLOCK, GROUP_W), lambda i: (i, 0)),
        out_shape=jax.ShapeDtypeStruct((t, GROUP_W), F32),
        compiler_params=_params("arbitrary"),
        name="gmlp",
    )(parts, parts, lg, lb, ws, bias)


def _lane_group_id(width, group):
    return lax.broadcasted_iota(jnp.int32, (1, width), 1) // group


def _stack_masked(q, group, ids):
    lane = _lane_group_id(q.shape[-1], group)
    return jnp.concatenate([jnp.where(lane == g, q, jnp.zeros_like(q)) for g in ids], axis=0)


def _diff_lambda(dl_ref, lam_init):
    dl = dl_ref[...]
    a = jnp.sum(dl[0:1, :] * dl[1:2, :], axis=-1, keepdims=True)
    b = jnp.sum(dl[2:3, :] * dl[3:4, :], axis=-1, keepdims=True)
    return jnp.exp(a) - jnp.exp(b) + lam_init


def _ctx_attn_kernel(q_ref, k_ref, v_ref, dq_ref, dk_ref, dv_ref, qn_ref, kn_ref, dqn_ref, dkn_ref, sub_ref,
                     dl_ref, na_ref, df_ref, nk_ref, dkf_ref, *, lam_init):
    n = q_ref.shape[0]
    lane_h = _lane_group_id(GROUP_W, HEAD_DIM)
    qn = _group_rms(q_ref[...], qn_ref[...], HEAD_DIM)
    kn = _group_rms(k_ref[...], kn_ref[...], HEAD_DIM)
    nk_ref[...] = kn
    qs = _stack_masked(qn.astype(BF16), HEAD_DIM, range(N_HEADS))
    s = _dot_nt(qs, kn.astype(BF16)) * (HEAD_DIM ** -0.5)
    e = jnp.exp(s - jnp.max(s, axis=-1, keepdims=True))
    r = _dot(e.astype(BF16), v_ref[...].astype(BF16)) / jnp.sum(e, axis=-1, keepdims=True)
    o = jnp.zeros((n, GROUP_W), F32)
    for h in range(N_HEADS):
        o = o + jnp.where(lane_h == h, r[h * n:(h + 1) * n], 0.0)
    na_ref[...] = o
    lam = _diff_lambda(dl_ref, lam_init)
    dq = _group_rms(dq_ref[...], dqn_ref[...], DIFF_SUB)
    dk = _group_rms(dk_ref[...], dkn_ref[...], DIFF_SUB)
    dkf_ref[...] = dk
    dkb = dk.astype(BF16)
    dvb = dv_ref[...].astype(BF16)
    o = jnp.zeros((n, GROUP_W), F32)
    for h in range(N_HEADS):
        qs = _stack_masked(dq.astype(BF16), DIFF_SUB, (2 * h, 2 * h + 1))
        s = _dot_nt(qs, dkb) * (DIFF_SUB ** -0.5)
        e = jnp.exp(s - jnp.max(s, axis=-1, keepdims=True))
        inv = 1.0 / jnp.sum(e, axis=-1, keepdims=True)
        a = e[:n] * inv[:n] - e[n:] * (lam * inv[n:])
        o = o + jnp.where(lane_h == h, _dot(a.astype(BF16), dvb), 0.0)
    df_ref[...] = _group_rms(o, sub_ref[...], HEAD_DIM) * (1.0 - lam_init)


def _ctx_attention(parts, batch, seq, qn, kn, dqn, dkn, sub, dl, lam_init):
    t = batch * seq
    vec = pl.BlockSpec((1, GROUP_W), lambda b: (0, 0))
    col = lambda c: pl.BlockSpec((seq, GROUP_W), lambda b, c=c: (b, c))
    out = jax.ShapeDtypeStruct((t, GROUP_W), F32)
    ob = pl.BlockSpec((seq, GROUP_W), lambda b: (b, 0))
    return pl.pallas_call(
        functools.partial(_ctx_attn_kernel, lam_init=lam_init),
        grid=(batch,),
        in_specs=[col(2), col(3), col(4), col(5), col(6), col(7), vec, vec, vec, vec, vec,
                  pl.BlockSpec((4, DIFF_SUB), lambda b: (0, 0))],
        out_specs=[ob, ob, ob, ob],
        out_shape=[out, out, out, out],
        compiler_params=_params("arbitrary"),
        name="ctx_attention",
    )(parts, parts, parts, parts, parts, parts, qn, kn, dqn, dkn, sub, dl)


def _lat_prep_kernel(q_ref, k_ref, v_ref, dq_ref, dk_ref, dv_ref, qn_ref, kn_ref, dqn_ref, dkn_ref,
                     cos_ref, sin_ref, oq, ok, ov, odq, odk, odv):
    oq[...] = _group_rms(q_ref[...], qn_ref[...], HEAD_DIM).astype(BF16)
    ok[...] = _group_rms(k_ref[...], kn_ref[...], HEAD_DIM).astype(BF16)
    ov[...] = v_ref[...].astype(BF16)
    odv[...] = dv_ref[...].astype(BF16)
    half = DIFF_SUB // 4
    lane = lax.broadcasted_iota(jnp.int32, (1, GROUP_W), 1)
    lower = (lane % (2 * half)) < half
    cos = cos_ref[...]
    sin = sin_ref[...]

    def rope(x):
        partner = jnp.where(lower, pltpu.roll(x, GROUP_W - half, 1), pltpu.roll(x, half, 1))
        return x * cos + partner * sin

    odq[...] = rope(_group_rms(dq_ref[...], dqn_ref[...], DIFF_SUB)).astype(BF16)
    odk[...] = rope(_group_rms(dk_ref[...], dkn_ref[...], DIFF_SUB)).astype(BF16)


def _rope_tables(seq):
    half = DIFF_SUB // 4
    t = jnp.arange(seq)
    rows = (t // GRID_W).astype(F32)
    cols = (t % GRID_W).astype(F32)
    inv = ROPE_BASE ** (-jnp.arange(half, dtype=F32) / half)
    ang_r = rows[:, None] * inv[None, :]
    ang_c = cols[:, None] * inv[None, :]
    cos32 = jnp.concatenate([jnp.cos(ang_r), jnp.cos(ang_r), jnp.cos(ang_c), jnp.cos(ang_c)], axis=-1)
    sin32 = jnp.concatenate([-jnp.sin(ang_r), jnp.sin(ang_r), -jnp.sin(ang_c), jnp.sin(ang_c)], axis=-1)
    reps = GROUP_W // DIFF_SUB
    return jnp.tile(cos32, (1, reps)), jnp.tile(sin32, (1, reps))


def _lat_prep(parts, row0, batch, seq, qn, kn, dqn, dkn, cos, sin):
    t = batch * seq
    nb = seq // SEQ_BLOCK
    r0 = row0 // SEQ_BLOCK
    vec = pl.BlockSpec((1, GROUP_W), lambda i: (0, 0))
    col = lambda c: pl.BlockSpec((SEQ_BLOCK, GROUP_W), lambda i, c=c: (r0 + i, c))
    tab = pl.BlockSpec((SEQ_BLOCK, GROUP_W), lambda i: (i % nb, 0))
    out = jax.ShapeDtypeStruct((t, GROUP_W), BF16)
    ob = pl.BlockSpec((SEQ_BLOCK, GROUP_W), lambda i: (i, 0))
    return pl.pallas_call(
        _lat_prep_kernel,
        grid=(t // SEQ_BLOCK,),
        in_specs=[col(2), col(3), col(4), col(5), col(6), col(7), vec, vec, vec, vec, tab, tab],
        out_specs=[ob] * 6,
        out_shape=[out] * 6,
        compiler_params=_params("arbitrary"),
        name="lat_prep",
    )(parts, parts, parts, parts, parts, parts, qn, kn, dqn, dkn, cos, sin)


def _na_row_start(r, rows):
    return jnp.clip(r - NA_WIN_R // 2, 0, rows - NA_WIN_R)


def _na_lat_kernel(q_ref, k_ref, v_ref, ck_ref, cv_ref, bt_ref, o_ref, *, rows):
    r = pl.program_id(1)
    start = pl.multiple_of(_na_row_start(r, rows) * GRID_W, GRID_W)
    n_loc = NA_WIN_R * GRID_W
    kl = k_ref[pl.ds(start, n_loc), :]
    vl = v_ref[pl.ds(start, n_loc), :]
    qs = _stack_masked(q_ref[...], HEAD_DIM, range(N_HEADS))
    scale = HEAD_DIM ** -0.5
    s_loc = _dot_nt(qs, kl) * scale + bt_ref[0]
    s_ctx = _dot_nt(qs, ck_ref[0]) * scale
    m = jnp.maximum(jnp.max(s_loc, axis=-1, keepdims=True), jnp.max(s_ctx, axis=-1, keepdims=True))
    e_loc = jnp.exp(s_loc - m)
    e_ctx = jnp.exp(s_ctx - m)
    l = jnp.sum(e_loc, axis=-1, keepdims=True) + jnp.sum(e_ctx, axis=-1, keepdims=True)
    res = (_dot(e_loc.astype(BF16), vl) + _dot(e_ctx.astype(BF16), cv_ref[0])) / l
    lane_h = _lane_group_id(GROUP_W, HEAD_DIM)
    o = jnp.zeros((GRID_W, GROUP_W), F32)
    for h in range(N_HEADS):
        o = o + jnp.where(lane_h == h, res[h * GRID_W:(h + 1) * GRID_W], 0.0)
    o_ref[...] = o


def _na_bias_table(rel_bias, rows):
    wr = NA_WIN_R
    c = np.arange(GRID_W)
    c0 = np.clip(c - NA_WIN_C // 2, 0, GRID_W - NA_WIN_C)
    in_win = (c[None, :] >= c0[:, None]) & (c[None, :] < c0[:, None] + NA_WIN_C)
    dc_idx = np.clip(c[None, :] - c[:, None] + NA_WIN_C - 1, 0, 2 * NA_WIN_C - 2)
    off = np.arange(wr)
    dr_idx = (np.arange(wr)[None, :] - off[:, None]) + NA_WIN_R - 1
    rb = rel_bias[:, dr_idx]
    bias = jnp.take(rb, jnp.asarray(dc_idx), axis=-1)
    bias = jnp.where(jnp.asarray(in_win)[None, None, None], bias, NEG)
    bias = bias.transpose(1, 0, 3, 2, 4)
    return bias.reshape(wr, N_HEADS * GRID_W, wr * GRID_W)


def _na_latent(qn, kn, v, ck, cv, bias_tab, batch, seq):
    rows = seq // GRID_W
    nctx = ck.shape[1]
    full = pl.BlockSpec((seq, GROUP_W), lambda b, r: (b, 0))
    ctx = pl.BlockSpec((1, nctx, GROUP_W), lambda b, r: (b, 0, 0))
    return pl.pallas_call(
        functools.partial(_na_lat_kernel, rows=rows),
        grid=(batch, rows),
        in_specs=[pl.BlockSpec((GRID_W, GROUP_W), lambda b, r: (b * rows + r, 0)),
                  full, full, ctx, ctx,
                  pl.BlockSpec((1, N_HEADS * GRID_W, NA_WIN_R * GRID_W),
                               lambda b, r: (r - _na_row_start(r, rows), 0, 0))],
        out_specs=pl.BlockSpec((GRID_W, GROUP_W), lambda b, r: (b * rows + r, 0)),
        out_shape=jax.ShapeDtypeStruct((batch * seq, GROUP_W), F32),
        compiler_params=_params("arbitrary", "arbitrary"),
        name="na_latent",
    )(qn, kn, v, ck, cv, bias_tab)


def _lane_fold(fn, acc, x):
    for g in range(x.shape[-1] // LANES):
        acc = fn(acc, x[:, g * LANES:(g + 1) * LANES])
    return acc


def _diff_lat_kernel(q_ref, k_ref, v_ref, ck_ref, cv_ref, sub_ref, dl_ref, o_ref, s_ref, sc_ref, *, lam_init):
    n = q_ref.shape[0]
    n_chunks, rows, chunk = s_ref.shape
    n_sub = rows // n
    lam = _diff_lambda(dl_ref, lam_init)
    qs = _stack_masked(q_ref[...], DIFF_SUB, range(n_sub))
    c2 = (DIFF_SUB ** -0.5) * math.log2(math.e)
    step = pl.program_id(1)
    top = jnp.full((rows, LANES), NEG, F32)
    for j in range(n_chunks):
        s = _dot(qs, k_ref[0, :, j * chunk:(j + 1) * chunk])
        s_ref[(j + step) % n_chunks] = s
        top = _lane_fold(jnp.maximum, top, s)
    s = _dot(qs, ck_ref[0])
    sc_ref[...] = s
    top = _lane_fold(jnp.maximum, top, s)
    m = jnp.max(top, axis=-1, keepdims=True) * c2
    part = jnp.zeros((rows, LANES), F32)
    half = rows // 2
    res = [jnp.zeros((half, GROUP_W), F32)] * 2
    for j in range(n_chunks + 1):
        scores = s_ref[(j + step) % n_chunks] if j < n_chunks else sc_ref[...]
        values = v_ref[j * chunk:(j + 1) * chunk, :] if j < n_chunks else cv_ref[0]
        e = jnp.exp2(scores * c2 - m)
        part = _lane_fold(jnp.add, part, e)
        eb = e.astype(BF16)
        res = [res[r] + _dot(eb[r * half:(r + 1) * half], values) for r in range(2)]
    res = jnp.concatenate(res, axis=0) * (1.0 / jnp.sum(part, axis=-1, keepdims=True))
    lane_h = _lane_group_id(GROUP_W, HEAD_DIM)
    o = jnp.zeros((n, GROUP_W), F32)
    for h in range(N_HEADS):
        first, second = res[2 * h * n:(2 * h + 1) * n], res[(2 * h + 1) * n:(2 * h + 2) * n]
        o = o + jnp.where(lane_h == h, first - lam * second, 0.0)
    o_ref[...] = _group_rms(o, sub_ref[...], HEAD_DIM) * (1.0 - lam_init)


def _diff_latent(dq, dk_t, dv, ck_t, cv, sub, dl, batch, seq, lam_init):
    nq = seq // DIFF_QBLOCK
    nctx = cv.shape[1]
    full = pl.BlockSpec((seq, GROUP_W), lambda b, i: (b, 0))
    ctx = pl.BlockSpec((1, nctx, GROUP_W), lambda b, i: (b, 0, 0))
    return pl.pallas_call(
        functools.partial(_diff_lat_kernel, lam_init=lam_init),
        grid=(batch, nq),
        in_specs=[pl.BlockSpec((DIFF_QBLOCK, GROUP_W), lambda b, i: (b * nq + i, 0)),
                  pl.BlockSpec((1, GROUP_W, seq), lambda b, i: (b, 0, 0)), full,
                  pl.BlockSpec((1, GROUP_W, nctx), lambda b, i: (b, 0, 0)), ctx,
                  pl.BlockSpec((1, GROUP_W), lambda b, i: (0, 0)),
                  pl.BlockSpec((4, DIFF_SUB), lambda b, i: (0, 0))],
        out_specs=pl.BlockSpec((DIFF_QBLOCK, GROUP_W), lambda b, i: (b * nq + i, 0)),
        out_shape=jax.ShapeDtypeStruct((batch * seq, GROUP_W), F32),
        scratch_shapes=[pltpu.VMEM((seq // DIFF_KCHUNK, 2 * N_HEADS * DIFF_QBLOCK, DIFF_KCHUNK), F32),
                        pltpu.VMEM((2 * N_HEADS * DIFF_QBLOCK, nctx), F32)],
        compiler_params=_params("arbitrary", "arbitrary"),
        name="diff_latent",
    )(dq, dk_t, dv, ck_t, cv, sub, dl)


def _out_proj_kernel(a_ref, b_ref, c_ref, d_ref, x_ref, mod_ref, w_ref, o_ref):
    mixed = jnp.zeros(o_ref.shape, F32)
    for g, ref in enumerate((a_ref, b_ref, c_ref, d_ref)):
        mixed = mixed + _dot(ref[...].astype(BF16), w_ref[g * GROUP_W:(g + 1) * GROUP_W, :])
    o_ref[...] = x_ref[...] + mod_ref[0, 2:3, :] * mixed


def _out_proj(conv_o, na_o, diff_o, gm_o, x, mod, w, layout):
    t = x.shape[0]
    part = pl.BlockSpec((ROW_BLOCK, GROUP_W), lambda i: (i, 0))
    row = pl.BlockSpec((ROW_BLOCK, D_MODEL), lambda i: (i, 0))
    return pl.pallas_call(
        _out_proj_kernel,
        grid=(t // ROW_BLOCK,),
        in_specs=[part, part, part, part, row,
                  _mod_spec(ROW_BLOCK, layout),
                  pl.BlockSpec((D_MODEL, D_MODEL), lambda i: (0, 0))],
        out_specs=row,
        out_shape=jax.ShapeDtypeStruct((t, D_MODEL), F32),
        compiler_params=_params("arbitrary"),
        name="out_proj",
    )(conv_o, na_o, diff_o, gm_o, x, mod, w)


def _sort_desc(vals):
    v = list(vals)
    n = len(v)
    k = 2
    while k <= n:
        j = k // 2
        while j >= 1:
            for i in range(n):
                l = i ^ j
                if l > i:
                    hi, lo = jnp.maximum(v[i], v[l]), jnp.minimum(v[i], v[l])
                    v[i], v[l] = (hi, lo) if (i & k) == 0 else (lo, hi)
            j //= 2
        k *= 2
    return v


def _merge_desc(v):
    v = list(v)
    n = len(v)
    j = n // 2
    while j >= 1:
        for i in range(n):
            l = i ^ j
            if l > i:
                v[i], v[l] = jnp.maximum(v[i], v[l]), jnp.minimum(v[i], v[l])
        j //= 2
    return v


def _top_half(a, b):
    n = len(a)
    return [jnp.maximum(a[i], b[n - 1 - i]) for i in range(n)]


def _top16_keys(s):
    k = PEER_TOPK
    v = _sort_desc([s[a * SUBLANES:(a + 1) * SUBLANES] for a in range(PEER_NKEYS // SUBLANES)])
    shift = SUBLANES // 2
    while shift >= 1:
        rolled = [pltpu.roll(x, shift, 0) for x in v]
        v = _merge_desc(_top_half(v, rolled))
        shift //= 2
    return v[:k]


def _paired_bf16(x):
    hi = pltpu.bitcast(x.astype(BF16).astype(F32), jnp.uint32)
    return hi | (hi >> 16)


def _route_kernel(x_ref, mod_ref, g_ref, wq_ref, sk_ref, h2_ref, n0_ref, r1_ref, e0_ref, e1_ref, s0_ref, s1_ref):
    x = x_ref[...]
    y = x * lax.rsqrt(jnp.mean(x * x, axis=-1, keepdims=True) + EPS) * g_ref[...]
    h2 = y * (1.0 + mod_ref[0, 4:5, :]) + mod_ref[0, 3:4, :]
    h2t = h2.T.astype(BF16)
    h2_ref[...] = h2t
    qt = _dot(wq_ref[...], h2t)
    tb = x.shape[0]
    sub = lax.broadcasted_iota(jnp.int32, (SUBLANES, tb), 0)
    k = PEER_TOPK
    tops = [[jnp.zeros((SUBLANES, tb), F32)] * k, [jnp.zeros((SUBLANES, tb), F32)] * k]
    for h in range(PEER_HEADS):
        for p in range(2):
            base = (2 * h + p) * PEER_NKEYS
            s = _dot(sk_ref[p], qt[base:base + PEER_NKEYS].astype(BF16))
            (s0_ref if p == 0 else s1_ref)[h] = s
            top = _top16_keys(s)
            tops[p] = [jnp.where(sub == h, top[a], tops[p][a]) for a in range(k)]
    cand = [tops[0][a] + tops[1][b] for a in range(k) for b in range(k) if (a + 1) * (b + 1) <= k]
    pad = [jnp.full((SUBLANES, tb), NEG, F32)] * (4 * k - len(cand))
    groups = [_sort_desc((cand + pad)[g * k:(g + 1) * k]) for g in range(4)]
    best = _top_half(_merge_desc(_top_half(groups[0], groups[1])), _merge_desc(_top_half(groups[2], groups[3])))
    thr = functools.reduce(jnp.minimum, best)
    m = tops[0][0] + tops[1][0]
    z = functools.reduce(jnp.add, [jnp.where(c >= thr, jnp.exp(c - m), 0.0) for c in cand])
    zinv = 1.0 / z
    counts, first = [], 0
    for a in range(k):
        n_b = k // (a + 1)
        counts.append(functools.reduce(jnp.add, [jnp.where(c >= thr, 1.0, 0.0) for c in cand[first:first + n_b]]))
        first += n_b
    for h in range(PEER_HEADS):
        s0 = s0_ref[h]
        s1 = s1_ref[h]
        n0 = jnp.zeros_like(s0)
        r1 = jnp.full_like(s1, float(k))
        for a in reversed(range(k)):
            n0 = jnp.where(s0 >= tops[0][a][h:h + 1], counts[a][h:h + 1], n0)
            r1 = jnp.where(s1 >= tops[1][a][h:h + 1], float(a), r1)
        n0_ref[h, 0] = _paired_bf16(n0)
        r1_ref[h, 0] = pltpu.bitcast(r1.astype(BF16), jnp.uint32)
        e0_ref[h, 0] = _paired_bf16(jnp.exp(s0 - tops[0][0][h:h + 1]) * zinv[h:h + 1])
        e1_ref[h, 0] = pltpu.bitcast(jnp.exp(s1 - tops[1][0][h:h + 1]).astype(BF16), jnp.uint32)


def _route(x1, mod, g, wq_t, sk, layout):
    t = x1.shape[0]
    tb = ROUTE_BLOCK
    assert tb == LANES
    tiles = t // LANES
    first_key = jax.ShapeDtypeStruct((PEER_HEADS, tiles, PEER_NKEYS, LANES), jnp.uint32)
    second_key = jax.ShapeDtypeStruct((PEER_HEADS, tiles, PEER_NKEYS // 2, LANES), jnp.uint32)
    fkb = pl.BlockSpec((PEER_HEADS, 1, PEER_NKEYS, LANES), lambda i: (0, i, 0, 0))
    skb = pl.BlockSpec((PEER_HEADS, 1, PEER_NKEYS // 2, LANES), lambda i: (0, i, 0, 0))
    scores = pltpu.VMEM((PEER_HEADS, PEER_NKEYS, tb), F32)
    return pl.pallas_call(
        _route_kernel,
        grid=(t // tb,),
        in_specs=[pl.BlockSpec((tb, D_MODEL), lambda i: (i, 0)),
                  _mod_spec(tb, layout),
                  pl.BlockSpec((1, D_MODEL), lambda i: (0, 0)),
                  pl.BlockSpec(wq_t.shape, lambda i: (0, 0)),
                  pl.BlockSpec(sk.shape, lambda i: (0, 0, 0))],
        out_specs=[pl.BlockSpec((D_MODEL, tb), lambda i: (0, i)), fkb, skb, fkb, skb],
        out_shape=[jax.ShapeDtypeStruct((D_MODEL, t), BF16), first_key, second_key, first_key, second_key],
        scratch_shapes=[scores, scores],
        compiler_params=_params("arbitrary"),
        name="peer_route",
    )(x1, mod, g, wq_t, sk)


def _rows_as_bf16(rows):
    tile = pltpu.bitcast(rows, BF16)
    return jnp.concatenate([tile] * (PEER_NKEYS // tile.shape[0]), axis=0)


def _peer_kernel(h2_ref, pu_ref, pv_ref, n0_ref, r1_ref, e0_ref, e1_ref, x_ref, mod_ref, o_ref,
                 acc_ref, a0_ref, a1_ref, w0_ref, w1_ref):
    e = pl.program_id(1)

    @pl.when(e == 0)
    def _():
        acc_ref[...] = jnp.zeros_like(acc_ref)

    tb = h2_ref.shape[1]
    n_sub = pv_ref.shape[0]
    sub_keys = MXU_DIM // PEER_NKEYS
    a_refs = (a0_ref, a1_ref)
    w_refs = (w0_ref, w1_ref)

    def first_matmul(k, slot):
        start = k * MXU_DIM if isinstance(k, int) else pl.multiple_of(k * MXU_DIM, MXU_DIM)
        a_refs[slot][...] = _dot(pu_ref[pl.ds(start, MXU_DIM), :], h2_ref[...])

    def second_matmul(k, slot):
        acc_ref[...] += _dot(pv_ref[k], w_refs[slot][...])

    def gate(k, slot):
        key = e * PEER_IBLOCK + sub_keys * k
        for c in range(tb // LANES):
            cols = slice(c * LANES, (c + 1) * LANES)
            gates = [jnp.zeros((PEER_NKEYS, LANES), BF16)] * sub_keys
            for h in range(PEER_HEADS):
                r1 = pltpu.bitcast(r1_ref[h, c], BF16)
                e1 = pltpu.bitcast(e1_ref[h, c], BF16)
                for d in range(sub_keys):
                    row = pl.ds(key + d, SUBLANES, stride=0)
                    n0_d = _rows_as_bf16(n0_ref[h, c, row, :])
                    e0_d = _rows_as_bf16(e0_ref[h, c, row, :])
                    gates[d] = gates[d] + jnp.where(r1 < n0_d, e1 * e0_d, jnp.zeros_like(e1))
            for d in range(sub_keys):
                rows = slice(d * PEER_NKEYS, (d + 1) * PEER_NKEYS)
                w_refs[slot][rows, cols] = _gelu_tanh(a_refs[slot][rows, cols]).astype(BF16) * gates[d]

    def steady(j, carry):
        k = 2 * j + 1
        first_matmul(k + 1, 0)
        gate(k, 1)
        second_matmul(k - 1, 0)
        first_matmul(k + 2, 1)
        gate(k + 1, 0)
        second_matmul(k, 1)
        return carry

    assert n_sub % 2 == 0 and n_sub >= 4
    first_matmul(0, 0)
    first_matmul(1, 1)
    gate(0, 0)
    lax.fori_loop(0, n_sub // 2 - 1, steady, 0)
    gate(n_sub - 1, 1)
    second_matmul(n_sub - 2, 0)
    second_matmul(n_sub - 1, 1)

    @pl.when(e == pl.num_programs(1) - 1)
    def _():
        o_ref[...] = x_ref[...] + mod_ref[0, 5:6, :] * acc_ref[...].T


def _peer(h2t, pu, pv_t, n0, r1, e0, e1, x1, mod, layout):
    t = x1.shape[0]
    tb = PEER_TOKENS
    eb = PEER_IBLOCK * PEER_NKEYS
    n_exp = pu.shape[0]
    n_sub = eb // MXU_DIM
    assert pv_t.shape == (n_exp // MXU_DIM, D_MODEL, MXU_DIM)
    fk = pl.BlockSpec((PEER_HEADS, tb // LANES, PEER_NKEYS, LANES), lambda i, e: (0, i, 0, 0))
    sk = pl.BlockSpec((PEER_HEADS, tb // LANES, PEER_NKEYS // 2, LANES), lambda i, e: (0, i, 0, 0))
    return pl.pallas_call(
        _peer_kernel,
        grid=(t // tb, n_exp // eb),
        in_specs=[pl.BlockSpec((D_MODEL, tb), lambda i, e: (0, i)),
                  pl.BlockSpec((eb, D_MODEL), lambda i, e: (e, 0)),
                  pl.BlockSpec((n_sub, D_MODEL, MXU_DIM), lambda i, e: (e, 0, 0)),
                  fk, sk, fk, sk,
                  pl.BlockSpec((tb, D_MODEL), lambda i, e: (i, 0)),
                  _mod_spec(tb, layout)],
        out_specs=pl.BlockSpec((tb, D_MODEL), lambda i, e: (i, 0)),
        out_shape=jax.ShapeDtypeStruct((t, D_MODEL), F32),
        scratch_shapes=[pltpu.VMEM((D_MODEL, tb), F32),
                        pltpu.VMEM((MXU_DIM, tb), F32), pltpu.VMEM((MXU_DIM, tb), F32),
                        pltpu.VMEM((MXU_DIM, tb), BF16), pltpu.VMEM((MXU_DIM, tb), BF16)],
        compiler_params=pltpu.CompilerParams(dimension_semantics=("arbitrary", "arbitrary"),
                                             vmem_limit_bytes=PEER_VMEM_LIMIT),
        name="peer_dense",
    )(h2t, pu, pv_t, n0, r1, e0, e1, x1, mod)


def _merge_cache(kv):
    b, h, l, d = kv.shape
    return kv.transpose(0, 2, 1, 3).reshape(b, l, h * d).astype(BF16)


def _split_heads(x, batch, seq):
    return x.reshape(batch, seq, N_HEADS, HEAD_DIM).transpose(0, 2, 1, 3)


def kernel(x_prompt, x_sample, cache_na_kv, cache_diff_kv, c, c_ctx, w_mod, b_mod, norm1_g, norm2_g, w_in, conv_w, conv_b, conv_ln_g, conv_ln_b, na_qn_g, na_kn_g, na_rel_bias, diff_qn_g, diff_kn_g, diff_lambda, diff_subln_g, gmlp_ln_g, gmlp_ln_b, gmlp_ws, gmlp_bs, w_out, peer_wq, peer_sub_keys, peer_u, peer_v):
    batch, seq, _ = x_prompt.shape
    dbatch, dseq, _ = x_sample.shape
    t_ctx = batch * seq
    layout = (t_ctx, dseq)
    x = jnp.concatenate([x_prompt.reshape(t_ctx, D_MODEL), x_sample.reshape(dbatch * dseq, D_MODEL)], axis=0)
    cond = jnp.concatenate([c_ctx[None, :], c, jnp.zeros((SUBLANES - 1 - dbatch, D_MODEL), F32)], axis=0)
    mod_all = _modulation(cond, w_mod, b_mod).reshape(DEPTH, SUBLANES, 6, D_MODEL)
    cos, sin = _rope_tables(dseq)
    row = lambda v: v.reshape(1, -1)
    tile = lambda v: jnp.tile(v, GROUP_W // v.shape[0]).reshape(1, GROUP_W)
    na_states, diff_states = [], []
    for l in range(DEPTH):
        lam_init = 0.8 - 0.6 * math.exp(-0.3 * l)
        mod = mod_all[l]
        parts = _in_proj(x, mod, row(norm1_g[l]), w_in[l].astype(BF16), layout)
        conv_o = _conv_module(parts, conv_w[l], row(conv_b[l]), row(conv_ln_g[l]), row(conv_ln_b[l]), t_ctx, dseq)
        gm_bias = jnp.repeat(gmlp_bs[l].T, GROUP_W // N_HEADS, axis=1)
        gm_o = _gmlp(parts, row(gmlp_ln_g[l]), row(gmlp_ln_b[l]), gmlp_ws[l].astype(BF16), gm_bias)
        qn, kn = tile(na_qn_g[l]), tile(na_kn_g[l])
        dqn, dkn, sub = tile(diff_qn_g[l]), tile(diff_kn_g[l]), tile(diff_subln_g[l])
        na_c, df_c, nk_c, dk_c = _ctx_attention(parts, batch, seq, qn, kn, dqn, dkn, sub, diff_lambda[l], lam_init)
        lq, lk, lv, ldq, ldk, ldv = _lat_prep(parts, t_ctx, dbatch, dseq, qn, kn, dqn, dkn, cos, sin)
        na_l = _na_latent(lq, lk, lv, _merge_cache(cache_na_kv[:, l, 0]), _merge_cache(cache_na_kv[:, l, 1]),
                          _na_bias_table(na_rel_bias[l], dseq // GRID_W), dbatch, dseq)
        ldk_t = ldk.reshape(dbatch, dseq, GROUP_W).transpose(0, 2, 1)
        ck_t = _merge_cache(cache_diff_kv[:, l, 0]).transpose(0, 2, 1)
        df_l = _diff_latent(ldq, ldk_t, ldv, ck_t, _merge_cache(cache_diff_kv[:, l, 1]), sub, diff_lambda[l],
                            dbatch, dseq, lam_init)
        na_o = jnp.concatenate([na_c, na_l], axis=0)
        diff_o = jnp.concatenate([df_c, df_l], axis=0)
        x1 = _out_proj(conv_o, na_o, diff_o, gm_o, x, mod, w_out[l].astype(BF16), layout)
        h2t, n0, r1, e0, e1 = _route(x1, mod, row(norm2_g[l]), peer_wq[l].T.astype(BF16),
                                          peer_sub_keys[l].astype(BF16), layout)
        pv_t = peer_v[l].astype(BF16).reshape(-1, MXU_DIM, D_MODEL).transpose(0, 2, 1)
        x = _peer(h2t, peer_u[l].astype(BF16), pv_t, n0, r1, e0, e1, x1, mod, layout)
        nv_c = parts[:t_ctx, 4 * GROUP_W:5 * GROUP_W]
        dv_c = parts[:t_ctx, 7 * GROUP_W:8 * GROUP_W]
        na_states.append(jnp.stack([_split_heads(nk_c, batch, seq), _split_heads(nv_c, batch, seq)], axis=1))
        diff_states.append(jnp.stack([_split_heads(dk_c, batch, seq), _split_heads(dv_c, batch, seq)], axis=1))
    y_prompt = x[:t_ctx].reshape(batch, seq, D_MODEL)
    y_sample = x[t_ctx:].reshape(dbatch, dseq, D_MODEL)
    return (y_prompt, y_sample, jnp.stack(na_states, axis=1), jnp.stack(diff_states, axis=1))
```

```python
import functools
import math

import numpy as np
import jax
import jax.numpy as jnp
from jax import lax
from jax.experimental import pallas as pl
from jax.experimental.pallas import tpu as pltpu

F32 = jnp.float32
BF16 = jnp.bfloat16

D_MODEL = 1024
DEPTH = 2
GRID_W = 64
GROUP_W = 256
HEAD_DIM = 64
N_HEADS = 4
DIFF_SUB = 32
CONV_K = 31
CONV_HALO = 16
NA_WIN_R = 8
NA_WIN_C = 16
ROPE_BASE = 10000.0
CHUNK = 128
PEER_HEADS = 8
PEER_NKEYS = 128
PEER_TOPK = 16
EPS = 1e-6
NEG = -1e30

LANES = 128
SUBLANES = 8
MXU_DIM = 256
VMEM_LIMIT = 48 * 1024 * 1024

SEQ_BLOCK = 256
ROW_BLOCK = 256
ROUTE_BLOCK = 128
PEER_TOKENS = 512
PEER_IBLOCK = 16
PEER_VMEM_LIMIT = 56 * 1024 * 1024
DIFF_QBLOCK = 128
DIFF_KCHUNK = 512


def _params(*sem):
    return pltpu.CompilerParams(dimension_semantics=sem, vmem_limit_bytes=VMEM_LIMIT)


def _dot(a, b):
    return jnp.dot(a, b, preferred_element_type=F32)


def _dot_nt(a, b):
    return lax.dot_general(a, b, (((1,), (1,)), ((), ())), preferred_element_type=F32)


def _split_dot(a, b):
    a1 = a.astype(BF16)
    r1 = a - a1.astype(F32)
    a2 = r1.astype(BF16)
    a3 = (r1 - a2.astype(F32)).astype(BF16)
    return _dot(a1, b) + _dot(a2, b) + _dot(a3, b)


def _group_ones(n, group):
    r = lax.broadcasted_iota(jnp.int32, (n, n), 0) // group
    c = lax.broadcasted_iota(jnp.int32, (n, n), 1) // group
    return (r == c).astype(BF16)


def _group_rms(x, gain, group):
    ssq = _split_dot(x * x, _group_ones(x.shape[-1], group))
    return x * lax.rsqrt(ssq * (1.0 / group) + EPS) * gain


def _gelu_tanh(x):
    k = -2.0 * math.sqrt(2.0 / math.pi) * math.log2(math.e)
    return x / (1.0 + jnp.exp2(x * (k + (k * 0.044715) * (x * x))))


def _layer_norm(y, g, b):
    mu = jnp.mean(y, axis=-1, keepdims=True)
    yc = y - mu
    return yc * lax.rsqrt(jnp.mean(yc * yc, axis=-1, keepdims=True) + EPS) * g + b


def _mod_spec(rows_per_block, layout):
    t_ctx, lat_seq = layout
    assert t_ctx % rows_per_block == 0 and lat_seq % rows_per_block == 0
    def index(i, *_):
        return (jnp.maximum((i * rows_per_block - t_ctx) // lat_seq + 1, 0), 0, 0)
    return pl.BlockSpec((1, 6, D_MODEL), index)


def _mod_kernel(c_ref, w_ref, b_ref, o_ref):
    c = c_ref[...]
    s = c * jax.nn.sigmoid(c)
    o_ref[0] = jnp.dot(s, w_ref[0], preferred_element_type=F32,
                       precision=lax.Precision.HIGHEST) + b_ref[0]


def _modulation(cond, w_mod, b_mod):
    nb = 4
    cols = 6 * D_MODEL // nb
    return pl.pallas_call(
        _mod_kernel,
        grid=(DEPTH, nb),
        in_specs=[pl.BlockSpec((SUBLANES, D_MODEL), lambda l, j: (0, 0)),
                  pl.BlockSpec((1, D_MODEL, cols), lambda l, j: (l, 0, j)),
                  pl.BlockSpec((1, 1, cols), lambda l, j: (l, 0, j))],
        out_specs=pl.BlockSpec((1, SUBLANES, cols), lambda l, j: (l, 0, j)),
        out_shape=jax.ShapeDtypeStruct((DEPTH, SUBLANES, 6 * D_MODEL), F32),
        name="modulation",
        compiler_params=_params("arbitrary", "arbitrary"),
    )(cond, w_mod, b_mod.reshape(DEPTH, 1, 6 * D_MODEL))


def _in_proj_kernel(x_ref, mod_ref, g_ref, w_ref, o_ref):
    x = x_ref[...]
    y = x * lax.rsqrt(jnp.mean(x * x, axis=-1, keepdims=True) + EPS) * g_ref[...]
    h = y * (1.0 + mod_ref[0, 1:2, :]) + mod_ref[0, 0:1, :]
    o_ref[...] = _dot(h.astype(BF16), w_ref[...])


def _in_proj(x, mod, g, w, layout):
    t = x.shape[0]
    n = w.shape[1]
    return pl.pallas_call(
        _in_proj_kernel,
        grid=(t // ROW_BLOCK,),
        in_specs=[pl.BlockSpec((ROW_BLOCK, D_MODEL), lambda i: (i, 0)),
                  _mod_spec(ROW_BLOCK, layout),
                  pl.BlockSpec((1, D_MODEL), lambda i: (0, 0)),
                  pl.BlockSpec((D_MODEL, n), lambda i: (0, 0))],
        out_specs=pl.BlockSpec((ROW_BLOCK, n), lambda i: (i, 0)),
        out_shape=jax.ShapeDtypeStruct((t, n), F32),
        name="in_proj",
        compiler_params=_params("arbitrary"),
    )(x, mod, g, w)


def _conv_kernel(ac, gc, ap, gp, an, gn, w_ref, b_ref, lg_ref, lb_ref, o_ref, pad_ref, *, ctx_blocks, seq_blocks):
    i = pl.program_id(0)
    is_ctx = i < ctx_blocks
    j = i % seq_blocks
    first = jnp.logical_or(is_ctx, j == 0)
    last = jnp.logical_or(is_ctx, j == seq_blocks - 1)
    yp = ap[...] * jax.nn.sigmoid(gp[...])
    yn = an[...] * jax.nn.sigmoid(gn[...])
    pad_ref[0:CONV_HALO, :] = jnp.where(first, 0.0, yp)
    pad_ref[CONV_HALO:CONV_HALO + SEQ_BLOCK, :] = ac[...] * jax.nn.sigmoid(gc[...])
    pad_ref[CONV_HALO + SEQ_BLOCK:, :] = jnp.where(last, 0.0, yn)
    off = CONV_HALO - CONV_K // 2
    acc = jnp.zeros((SEQ_BLOCK, GROUP_W), F32)
    for k in range(CONV_K):
        acc = acc + pad_ref[off + k:off + k + SEQ_BLOCK, :] * w_ref[k:k + 1, :]
    y = _layer_norm(acc + b_ref[...], lg_ref[...], lb_ref[...])
    o_ref[...] = y * jax.nn.sigmoid(y)


def _conv_module(parts, w, b, lg, lb, ctx_tokens, lat_seq):
    t = parts.shape[0]
    nblk = t // SEQ_BLOCK
    hb = SEQ_BLOCK // CONV_HALO
    last_halo = t // CONV_HALO - 1
    vec = pl.BlockSpec((1, GROUP_W), lambda i: (0, 0))
    kern = functools.partial(_conv_kernel, ctx_blocks=ctx_tokens // SEQ_BLOCK, seq_blocks=lat_seq // SEQ_BLOCK)
    return pl.pallas_call(
        kern,
        grid=(nblk,),
        in_specs=[pl.BlockSpec((SEQ_BLOCK, GROUP_W), lambda i: (i, 0)),
                  pl.BlockSpec((SEQ_BLOCK, GROUP_W), lambda i: (i, 1)),
                  pl.BlockSpec((CONV_HALO, GROUP_W), lambda i: (jnp.maximum(i * hb - 1, 0), 0)),
                  pl.BlockSpec((CONV_HALO, GROUP_W), lambda i: (jnp.maximum(i * hb - 1, 0), 1)),
                  pl.BlockSpec((CONV_HALO, GROUP_W), lambda i: (jnp.minimum((i + 1) * hb, last_halo), 0)),
                  pl.BlockSpec((CONV_HALO, GROUP_W), lambda i: (jnp.minimum((i + 1) * hb, last_halo), 1)),
                  pl.BlockSpec((CONV_K, GROUP_W), lambda i: (0, 0)),
                  vec, vec, vec],
        out_specs=pl.BlockSpec((SEQ_BLOCK, GROUP_W), lambda i: (i, 0)),
        out_shape=jax.ShapeDtypeStruct((t, GROUP_W), F32),
        scratch_shapes=[pltpu.VMEM((SEQ_BLOCK + 2 * CONV_HALO, GROUP_W), F32)],
        name="conv_module",
        compiler_params=_params("arbitrary"),
    )(parts, parts, parts, parts, parts, parts, w, b, lg, lb)


def _gmlp_kernel(u_ref, v_ref, lg_ref, lb_ref, ws_ref, bias_ref, o_ref):
    u = jax.nn.gelu(u_ref[...])
    v = _layer_norm(jax.nn.gelu(v_ref[...]), lg_ref[...], lb_ref[...]).astype(BF16)
    lane_g = lax.broadcasted_iota(jnp.int32, (1, GROUP_W), 1) // (GROUP_W // N_HEADS)
    for ch in range(SEQ_BLOCK // CHUNK):
        rows = slice(ch * CHUNK, (ch + 1) * CHUNK)
        s = bias_ref[...]
        for g in range(N_HEADS):
            s = s + jnp.where(lane_g == g, _dot(ws_ref[g], v[rows]), 0.0)
        o_ref[rows, :] = u[rows] * s


def _gmlp(parts, lg, lb, ws, bias):
    t = parts.shape[0]
    vec = pl.BlockSpec((1, GROUP_W), lambda i: (0, 0))
    return pl.pallas_call(
        _gmlp_kernel,
        grid=(t // SEQ_BLOCK,),
        in_specs=[pl.BlockSpec((SEQ_BLOCK, GROUP_W), lambda i: (i, 8)),
                  pl.BlockSpec((SEQ_BLOCK, GROUP_W), lambda i: (i, 9)),
                  vec, vec,
                  pl.BlockSpec((N_HEADS, CHUNK, CHUNK), lambda i: (0, 0, 0)),
                  pl.BlockSpec((CHUNK, GROUP_W), lambda i: (0, 0))],
        out_specs=pl.BlockSpec((SEQ_BLOCK, GROUP_W), lambda i: (i, 0)),
        out_shape=jax.ShapeDtypeStruct((t, GROUP_W), F32),
        compiler_params=_params("arbitrary"),
        name="gmlp",
    )(parts, parts, lg, lb, ws, bias)


def _lane_group_id(width, group):
    return lax.broadcasted_iota(jnp.int32, (1, width), 1) // group


def _stack_masked(q, group, ids):
    lane = _lane_group_id(q.shape[-1], group)
    return jnp.concatenate([jnp.where(lane == g, q, jnp.zeros_like(q)) for g in ids], axis=0)


def _diff_lambda(dl_ref, lam_init):
    dl = dl_ref[...]
    a = jnp.sum(dl[0:1, :] * dl[1:2, :], axis=-1, keepdims=True)
    b = jnp.sum(dl[2:3, :] * dl[3:4, :], axis=-1, keepdims=True)
    return jnp.exp(a) - jnp.exp(b) + lam_init


def _ctx_attn_kernel(q_ref, k_ref, v_ref, dq_ref, dk_ref, dv_ref, qn_ref, kn_ref, dqn_ref, dkn_ref, sub_ref,
                     dl_ref, na_ref, df_ref, nk_ref, dkf_ref, *, lam_init):
    n = q_ref.shape[0]
    lane_h = _lane_group_id(GROUP_W, HEAD_DIM)
    qn = _group_rms(q_ref[...], qn_ref[...], HEAD_DIM)
    kn = _group_rms(k_ref[...], kn_ref[...], HEAD_DIM)
    nk_ref[...] = kn
    qs = _stack_masked(qn.astype(BF16), HEAD_DIM, range(N_HEADS))
    s = _dot_nt(qs, kn.astype(BF16)) * (HEAD_DIM ** -0.5)
    e = jnp.exp(s - jnp.max(s, axis=-1, keepdims=True))
    r = _dot(e.astype(BF16), v_ref[...].astype(BF16)) / jnp.sum(e, axis=-1, keepdims=True)
    o = jnp.zeros((n, GROUP_W), F32)
    for h in range(N_HEADS):
        o = o + jnp.where(lane_h == h, r[h * n:(h + 1) * n], 0.0)
    na_ref[...] = o
    lam = _diff_lambda(dl_ref, lam_init)
    dq = _group_rms(dq_ref[...], dqn_ref[...], DIFF_SUB)
    dk = _group_rms(dk_ref[...], dkn_ref[...], DIFF_SUB)
    dkf_ref[...] = dk
    dkb = dk.astype(BF16)
    dvb = dv_ref[...].astype(BF16)
    o = jnp.zeros((n, GROUP_W), F32)
    for h in range(N_HEADS):
        qs = _stack_masked(dq.astype(BF16), DIFF_SUB, (2 * h, 2 * h + 1))
        s = _dot_nt(qs, dkb) * (DIFF_SUB ** -0.5)
        e = jnp.exp(s - jnp.max(s, axis=-1, keepdims=True))
        inv = 1.0 / jnp.sum(e, axis=-1, keepdims=True)
        a = e[:n] * inv[:n] - e[n:] * (lam * inv[n:])
        o = o + jnp.where(lane_h == h, _dot(a.astype(BF16), dvb), 0.0)
    df_ref[...] = _group_rms(o, sub_ref[...], HEAD_DIM) * (1.0 - lam_init)


def _ctx_attention(parts, batch, seq, qn, kn, dqn, dkn, sub, dl, lam_init):
    t = batch * seq
    vec = pl.BlockSpec((1, GROUP_W), lambda b: (0, 0))
    col = lambda c: pl.BlockSpec((seq, GROUP_W), lambda b, c=c: (b, c))
    out = jax.ShapeDtypeStruct((t, GROUP_W), F32)
    ob = pl.BlockSpec((seq, GROUP_W), lambda b: (b, 0))
    return pl.pallas_call(
        functools.partial(_ctx_attn_kernel, lam_init=lam_init),
        grid=(batch,),
        in_specs=[col(2), col(3), col(4), col(5), col(6), col(7), vec, vec, vec, vec, vec,
                  pl.BlockSpec((4, DIFF_SUB), lambda b: (0, 0))],
        out_specs=[ob, ob, ob, ob],
        out_shape=[out, out, out, out],
        compiler_params=_params("arbitrary"),
        name="ctx_attention",
    )(parts, parts, parts, parts, parts, parts, qn, kn, dqn, dkn, sub, dl)


def _lat_prep_kernel(q_ref, k_ref, v_ref, dq_ref, dk_ref, dv_ref, qn_ref, kn_ref, dqn_ref, dkn_ref,
                     cos_ref, sin_ref, oq, ok, ov, odq, odk, odv):
    oq[...] = _group_rms(q_ref[...], qn_ref[...], HEAD_DIM).astype(BF16)
    ok[...] = _group_rms(k_ref[...], kn_ref[...], HEAD_DIM).astype(BF16)
    ov[...] = v_ref[...].astype(BF16)
    odv[...] = dv_ref[...].astype(BF16)
    half = DIFF_SUB // 4
    lane = lax.broadcasted_iota(jnp.int32, (1, GROUP_W), 1)
    lower = (lane % (2 * half)) < half
    cos = cos_ref[...]
    sin = sin_ref[...]

    def rope(x):
        partner = jnp.where(lower, pltpu.roll(x, GROUP_W - half, 1), pltpu.roll(x, half, 1))
        return x * cos + partner * sin

    odq[...] = rope(_group_rms(dq_ref[...], dqn_ref[...], DIFF_SUB)).astype(BF16)
    odk[...] = rope(_group_rms(dk_ref[...], dkn_ref[...], DIFF_SUB)).astype(BF16)


def _rope_tables(seq):
    half = DIFF_SUB // 4
    t = jnp.arange(seq)
    rows = (t // GRID_W).astype(F32)
    cols = (t % GRID_W).astype(F32)
    inv = ROPE_BASE ** (-jnp.arange(half, dtype=F32) / half)
    ang_r = rows[:, None] * inv[None, :]
    ang_c = cols[:, None] * inv[None, :]
    cos32 = jnp.concatenate([jnp.cos(ang_r), jnp.cos(ang_r), jnp.cos(ang_c), jnp.cos(ang_c)], axis=-1)
    sin32 = jnp.concatenate([-jnp.sin(ang_r), jnp.sin(ang_r), -jnp.sin(ang_c), jnp.sin(ang_c)], axis=-1)
    reps = GROUP_W // DIFF_SUB
    return jnp.tile(cos32, (1, reps)), jnp.tile(sin32, (1, reps))


def _lat_prep(parts, row0, batch, seq, qn, kn, dqn, dkn, cos, sin):
    t = batch * seq
    nb = seq // SEQ_BLOCK
    r0 = row0 // SEQ_BLOCK
    vec = pl.BlockSpec((1, GROUP_W), lambda i: (0, 0))
    col = lambda c: pl.BlockSpec((SEQ_BLOCK, GROUP_W), lambda i, c=c: (r0 + i, c))
    tab = pl.BlockSpec((SEQ_BLOCK, GROUP_W), lambda i: (i % nb, 0))
    out = jax.ShapeDtypeStruct((t, GROUP_W), BF16)
    ob = pl.BlockSpec((SEQ_BLOCK, GROUP_W), lambda i: (i, 0))
    return pl.pallas_call(
        _lat_prep_kernel,
        grid=(t // SEQ_BLOCK,),
        in_specs=[col(2), col(3), col(4), col(5), col(6), col(7), vec, vec, vec, vec, tab, tab],
        out_specs=[ob] * 6,
        out_shape=[out] * 6,
        compiler_params=_params("arbitrary"),
        name="lat_prep",
    )(parts, parts, parts, parts, parts, parts, qn, kn, dqn, dkn, cos, sin)


def _na_row_start(r, rows):
    return jnp.clip(r - NA_WIN_R // 2, 0, rows - NA_WIN_R)


def _na_lat_kernel(q_ref, k_ref, v_ref, ck_ref, cv_ref, bt_ref, o_ref, *, rows):
    r = pl.program_id(1)
    start = pl.multiple_of(_na_row_start(r, rows) * GRID_W, GRID_W)
    n_loc = NA_WIN_R * GRID_W
    kl = k_ref[pl.ds(start, n_loc), :]
    vl = v_ref[pl.ds(start, n_loc), :]
    qs = _stack_masked(q_ref[...], HEAD_DIM, range(N_HEADS))
    scale = HEAD_DIM ** -0.5
    s_loc = _dot_nt(qs, kl) * scale + bt_ref[0]
    s_ctx = _dot_nt(qs, ck_ref[0]) * scale
    m = jnp.maximum(jnp.max(s_loc, axis=-1, keepdims=True), jnp.max(s_ctx, axis=-1, keepdims=True))
    e_loc = jnp.exp(s_loc - m)
    e_ctx = jnp.exp(s_ctx - m)
    l = jnp.sum(e_loc, axis=-1, keepdims=True) + jnp.sum(e_ctx, axis=-1, keepdims=True)
    res = (_dot(e_loc.astype(BF16), vl) + _dot(e_ctx.astype(BF16), cv_ref[0])) / l
    lane_h = _lane_group_id(GROUP_W, HEAD_DIM)
    o = jnp.zeros((GRID_W, GROUP_W), F32)
    for h in range(N_HEADS):
        o = o + jnp.where(lane_h == h, res[h * GRID_W:(h + 1) * GRID_W], 0.0)
    o_ref[...] = o


def _na_bias_table(rel_bias, rows):
    wr = NA_WIN_R
    c = np.arange(GRID_W)
    c0 = np.clip(c - NA_WIN_C // 2, 0, GRID_W - NA_WIN_C)
    in_win = (c[None, :] >= c0[:, None]) & (c[None, :] < c0[:, None] + NA_WIN_C)
    dc_idx = np.clip(c[None, :] - c[:, None] + NA_WIN_C - 1, 0, 2 * NA_WIN_C - 2)
    off = np.arange(wr)
    dr_idx = (np.arange(wr)[None, :] - off[:, None]) + NA_WIN_R - 1
    rb = rel_bias[:, dr_idx]
    bias = jnp.take(rb, jnp.asarray(dc_idx), axis=-1)
    bias = jnp.where(jnp.asarray(in_win)[None, None, None], bias, NEG)
    bias = bias.transpose(1, 0, 3, 2, 4)
    return bias.reshape(wr, N_HEADS * GRID_W, wr * GRID_W)


def _na_latent(qn, kn, v, ck, cv, bias_tab, batch, seq):
    rows = seq // GRID_W
    nctx = ck.shape[1]
    full = pl.BlockSpec((seq, GROUP_W), lambda b, r: (b, 0))
    ctx = pl.BlockSpec((1, nctx, GROUP_W), lambda b, r: (b, 0, 0))
    return pl.pallas_call(
        functools.partial(_na_lat_kernel, rows=rows),
        grid=(batch, rows),
        in_specs=[pl.BlockSpec((GRID_W, GROUP_W), lambda b, r: (b * rows + r, 0)),
                  full, full, ctx, ctx,
                  pl.BlockSpec((1, N_HEADS * GRID_W, NA_WIN_R * GRID_W),
                               lambda b, r: (r - _na_row_start(r, rows), 0, 0))],
        out_specs=pl.BlockSpec((GRID_W, GROUP_W), lambda b, r: (b * rows + r, 0)),
        out_shape=jax.ShapeDtypeStruct((batch * seq, GROUP_W), F32),
        compiler_params=_params("arbitrary", "arbitrary"),
        name="na_latent",
    )(qn, kn, v, ck, cv, bias_tab)


def _lane_fold(fn, acc, x):
    for g in range(x.shape[-1] // LANES):
        acc = fn(acc, x[:, g * LANES:(g + 1) * LANES])
    return acc


def _diff_lat_kernel(q_ref, k_ref, v_ref, ck_ref, cv_ref, sub_ref, dl_ref, o_ref, s_ref, sc_ref, *, lam_init):
    n = q_ref.shape[0]
    n_chunks, rows, chunk = s_ref.shape
    n_sub = rows // n
    lam = _diff_lambda(dl_ref, lam_init)
    qs = _stack_masked(q_ref[...], DIFF_SUB, range(n_sub))
    c2 = (DIFF_SUB ** -0.5) * math.log2(math.e)
    step = pl.program_id(1)
    top = jnp.full((rows, LANES), NEG, F32)
    for j in range(n_chunks):
        s = _dot(qs, k_ref[0, :, j * chunk:(j + 1) * chunk])
        s_ref[(j + step) % n_chunks] = s
        top = _lane_fold(jnp.maximum, top, s)
    s = _dot(qs, ck_ref[0])
    sc_ref[...] = s
    top = _lane_fold(jnp.maximum, top, s)
    m = jnp.max(top, axis=-1, keepdims=True) * c2
    part = jnp.zeros((rows, LANES), F32)
    half = rows // 2
    res = [jnp.zeros((half, GROUP_W), F32)] * 2
    for j in range(n_chunks + 1):
        scores = s_ref[(j + step) % n_chunks] if j < n_chunks else sc_ref[...]
        values = v_ref[j * chunk:(j + 1) * chunk, :] if j < n_chunks else cv_ref[0]
        e = jnp.exp2(scores * c2 - m)
        part = _lane_fold(jnp.add, part, e)
        eb = e.astype(BF16)
        res = [res[r] + _dot(eb[r * half:(r + 1) * half], values) for r in range(2)]
    res = jnp.concatenate(res, axis=0) * (1.0 / jnp.sum(part, axis=-1, keepdims=True))
    lane_h = _lane_group_id(GROUP_W, HEAD_DIM)
    o = jnp.zeros((n, GROUP_W), F32)
    for h in range(N_HEADS):
        first, second = res[2 * h * n:(2 * h + 1) * n], res[(2 * h + 1) * n:(2 * h + 2) * n]
        o = o + jnp.where(lane_h == h, first - lam * second, 0.0)
    o_ref[...] = _group_rms(o, sub_ref[...], HEAD_DIM) * (1.0 - lam_init)


def _diff_latent(dq, dk_t, dv, ck_t, cv, sub, dl, batch, seq, lam_init):
    nq = seq // DIFF_QBLOCK
    nctx = cv.shape[1]
    full = pl.BlockSpec((seq, GROUP_W), lambda b, i: (b, 0))
    ctx = pl.BlockSpec((1, nctx, GROUP_W), lambda b, i: (b, 0, 0))
    return pl.pallas_call(
        functools.partial(_diff_lat_kernel, lam_init=lam_init),
        grid=(batch, nq),
        in_specs=[pl.BlockSpec((DIFF_QBLOCK, GROUP_W), lambda b, i: (b * nq + i, 0)),
                  pl.BlockSpec((1, GROUP_W, seq), lambda b, i: (b, 0, 0)), full,
                  pl.BlockSpec((1, GROUP_W, nctx), lambda b, i: (b, 0, 0)), ctx,
                  pl.BlockSpec((1, GROUP_W), lambda b, i: (0, 0)),
                  pl.BlockSpec((4, DIFF_SUB), lambda b, i: (0, 0))],
        out_specs=pl.BlockSpec((DIFF_QBLOCK, GROUP_W), lambda b, i: (b * nq + i, 0)),
        out_shape=jax.ShapeDtypeStruct((batch * seq, GROUP_W), F32),
        scratch_shapes=[pltpu.VMEM((seq // DIFF_KCHUNK, 2 * N_HEADS * DIFF_QBLOCK, DIFF_KCHUNK), F32),
                        pltpu.VMEM((2 * N_HEADS * DIFF_QBLOCK, nctx), F32)],
        compiler_params=_params("arbitrary", "arbitrary"),
        name="diff_latent",
    )(dq, dk_t, dv, ck_t, cv, sub, dl)


def _out_proj_kernel(a_ref, b_ref, c_ref, d_ref, x_ref, mod_ref, w_ref, o_ref):
    mixed = jnp.zeros(o_ref.shape, F32)
    for g, ref in enumerate((a_ref, b_ref, c_ref, d_ref)):
        mixed = mixed + _dot(ref[...].astype(BF16), w_ref[g * GROUP_W:(g + 1) * GROUP_W, :])
    o_ref[...] = x_ref[...] + mod_ref[0, 2:3, :] * mixed


def _out_proj(conv_o, na_o, diff_o, gm_o, x, mod, w, layout):
    t = x.shape[0]
    part = pl.BlockSpec((ROW_BLOCK, GROUP_W), lambda i: (i, 0))
    row = pl.BlockSpec((ROW_BLOCK, D_MODEL), lambda i: (i, 0))
    return pl.pallas_call(
        _out_proj_kernel,
        grid=(t // ROW_BLOCK,),
        in_specs=[part, part, part, part, row,
                  _mod_spec(ROW_BLOCK, layout),
                  pl.BlockSpec((D_MODEL, D_MODEL), lambda i: (0, 0))],
        out_specs=row,
        out_shape=jax.ShapeDtypeStruct((t, D_MODEL), F32),
        compiler_params=_params("arbitrary"),
        name="out_proj",
    )(conv_o, na_o, diff_o, gm_o, x, mod, w)


def _sort_desc(vals):
    v = list(vals)
    n = len(v)
    k = 2
    while k <= n:
        j = k // 2
        while j >= 1:
            for i in range(n):
                l = i ^ j
                if l > i:
                    hi, lo = jnp.maximum(v[i], v[l]), jnp.minimum(v[i], v[l])
                    v[i], v[l] = (hi, lo) if (i & k) == 0 else (lo, hi)
            j //= 2
        k *= 2
    return v


def _merge_desc(v):
    v = list(v)
    n = len(v)
    j = n // 2
    while j >= 1:
        for i in range(n):
            l = i ^ j
            if l > i:
                v[i], v[l] = jnp.maximum(v[i], v[l]), jnp.minimum(v[i], v[l])
        j //= 2
    return v


def _top_half(a, b):
    n = len(a)
    return [jnp.maximum(a[i], b[n - 1 - i]) for i in range(n)]


def _top16_keys(s):
    k = PEER_TOPK
    v = _sort_desc([s[a * SUBLANES:(a + 1) * SUBLANES] for a in range(PEER_NKEYS // SUBLANES)])
    shift = SUBLANES // 2
    while shift >= 1:
        rolled = [pltpu.roll(x, shift, 0) for x in v]
        v = _merge_desc(_top_half(v, rolled))
        shift //= 2
    return v[:k]


def _paired_bf16(x):
    hi = pltpu.bitcast(x.astype(BF16).astype(F32), jnp.uint32)
    return hi | (hi >> 16)


def _route_kernel(x_ref, mod_ref, g_ref, wq_ref, sk_ref, h2_ref, n0_ref, r1_ref, e0_ref, e1_ref, s0_ref, s1_ref):
    x = x_ref[...]
    y = x * lax.rsqrt(jnp.mean(x * x, axis=-1, keepdims=True) + EPS) * g_ref[...]
    h2 = y * (1.0 + mod_ref[0, 4:5, :]) + mod_ref[0, 3:4, :]
    h2t = h2.T.astype(BF16)
    h2_ref[...] = h2t
    qt = _dot(wq_ref[...], h2t)
    tb = x.shape[0]
    sub = lax.broadcasted_iota(jnp.int32, (SUBLANES, tb), 0)
    k = PEER_TOPK
    tops = [[jnp.zeros((SUBLANES, tb), F32)] * k, [jnp.zeros((SUBLANES, tb), F32)] * k]
    for h in range(PEER_HEADS):
        for p in range(2):
            base = (2 * h + p) * PEER_NKEYS
            s = _dot(sk_ref[p], qt[base:base + PEER_NKEYS].astype(BF16))
            (s0_ref if p == 0 else s1_ref)[h] = s
            top = _top16_keys(s)
            tops[p] = [jnp.where(sub == h, top[a], tops[p][a]) for a in range(k)]
    cand = [tops[0][a] + tops[1][b] for a in range(k) for b in range(k) if (a + 1) * (b + 1) <= k]
    pad = [jnp.full((SUBLANES, tb), NEG, F32)] * (4 * k - len(cand))
    groups = [_sort_desc((cand + pad)[g * k:(g + 1) * k]) for g in range(4)]
    best = _top_half(_merge_desc(_top_half(groups[0], groups[1])), _merge_desc(_top_half(groups[2], groups[3])))
    thr = functools.reduce(jnp.minimum, best)
    m = tops[0][0] + tops[1][0]
    z = functools.reduce(jnp.add, [jnp.where(c >= thr, jnp.exp(c - m), 0.0) for c in cand])
    zinv = 1.0 / z
    counts, first = [], 0
    for a in range(k):
        n_b = k // (a + 1)
        counts.append(functools.reduce(jnp.add, [jnp.where(c >= thr, 1.0, 0.0) for c in cand[first:first + n_b]]))
        first += n_b
    for h in range(PEER_HEADS):
        s0 = s0_ref[h]
        s1 = s1_ref[h]
        n0 = jnp.zeros_like(s0)
        r1 = jnp.full_like(s1, float(k))
        for a in reversed(range(k)):
            n0 = jnp.where(s0 >= tops[0][a][h:h + 1], counts[a][h:h + 1], n0)
            r1 = jnp.where(s1 >= tops[1][a][h:h + 1], float(a), r1)
        n0_ref[h, 0] = _paired_bf16(n0)
        r1_ref[h, 0] = pltpu.bitcast(r1.astype(BF16), jnp.uint32)
        e0_ref[h, 0] = _paired_bf16(jnp.exp(s0 - tops[0][0][h:h + 1]) * zinv[h:h + 1])
        e1_ref[h, 0] = pltpu.bitcast(jnp.exp(s1 - tops[1][0][h:h + 1]).astype(BF16), jnp.uint32)


def _route(x1, mod, g, wq_t, sk, layout):
    t = x1.shape[0]
    tb = ROUTE_BLOCK
    assert tb == LANES
    tiles = t // LANES
    first_key = jax.ShapeDtypeStruct((PEER_HEADS, tiles, PEER_NKEYS, LANES), jnp.uint32)
    second_key = jax.ShapeDtypeStruct((PEER_HEADS, tiles, PEER_NKEYS // 2, LANES), jnp.uint32)
    fkb = pl.BlockSpec((PEER_HEADS, 1, PEER_NKEYS, LANES), lambda i: (0, i, 0, 0))
    skb = pl.BlockSpec((PEER_HEADS, 1, PEER_NKEYS // 2, LANES), lambda i: (0, i, 0, 0))
    scores = pltpu.VMEM((PEER_HEADS, PEER_NKEYS, tb), F32)
    return pl.pallas_call(
        _route_kernel,
        grid=(t // tb,),
        in_specs=[pl.BlockSpec((tb, D_MODEL), lambda i: (i, 0)),
                  _mod_spec(tb, layout),
                  pl.BlockSpec((1, D_MODEL), lambda i: (0, 0)),
                  pl.BlockSpec(wq_t.shape, lambda i: (0, 0)),
                  pl.BlockSpec(sk.shape, lambda i: (0, 0, 0))],
        out_specs=[pl.BlockSpec((D_MODEL, tb), lambda i: (0, i)), fkb, skb, fkb, skb],
        out_shape=[jax.ShapeDtypeStruct((D_MODEL, t), BF16), first_key, second_key, first_key, second_key],
        scratch_shapes=[scores, scores],
        compiler_params=_params("arbitrary"),
        name="peer_route",
    )(x1, mod, g, wq_t, sk)


def _rows_as_bf16(rows):
    tile = pltpu.bitcast(rows, BF16)
    return jnp.concatenate([tile] * (PEER_NKEYS // tile.shape[0]), axis=0)


def _peer_kernel(h2_ref, pu_ref, pv_ref, n0_ref, r1_ref, e0_ref, e1_ref, x_ref, mod_ref, o_ref,
                 acc_ref, a_ref, w_ref):
    e = pl.program_id(1)

    @pl.when(e == 0)
    def _():
        acc_ref[...] = jnp.zeros_like(acc_ref)
        w_ref[...] = jnp.zeros_like(w_ref)

    tb = h2_ref.shape[1]
    n_sub = pv_ref.shape[1] - 1
    sub_keys = MXU_DIM // PEER_NKEYS
    n_tiles = tb // LANES

    def first_matmul(k):
        start = k * MXU_DIM if isinstance(k, int) else pl.multiple_of(k * MXU_DIM, MXU_DIM)
        parts = [_dot(pu_ref[pl.ds(start, MXU_DIM), j * MXU_DIM:(j + 1) * MXU_DIM],
                      h2_ref[j * MXU_DIM:(j + 1) * MXU_DIM, :]) for j in range(D_MODEL // MXU_DIM)]
        a = functools.reduce(jnp.add, parts)
        a_ref[...] = pltpu.bitcast(_gelu_tanh(a).astype(BF16), jnp.uint32)
        return parts

    def second_matmul(k):
        part = _dot(pv_ref[0, k], w_ref[...])
        acc_ref[...] += part
        return part

    def gate(k, anchors):
        key = e * PEER_IBLOCK + sub_keys * k
        out = []
        for c in range(n_tiles):
            cols = slice(c * LANES, (c + 1) * LANES)
            words = pltpu.bitcast(anchors[c], jnp.uint32)
            gates = [_rows_as_bf16((words >> 16) >> 16)] * sub_keys
            for h in range(PEER_HEADS):
                r1 = pltpu.bitcast(r1_ref[h, c], BF16)
                e1 = pltpu.bitcast(e1_ref[h, c], BF16)
                for d in range(sub_keys):
                    row = pl.ds(key + d, SUBLANES, stride=0)
                    n0_d = _rows_as_bf16(n0_ref[h, c, row, :])
                    e0_d = _rows_as_bf16(e0_ref[h, c, row, :])
                    gates[d] = gates[d] + jnp.where(r1 < n0_d, e1 * e0_d, jnp.zeros_like(e1))
            out.append(gates)
        return out

    def write_weights(gates):
        for c in range(n_tiles):
            cols = slice(c * LANES, (c + 1) * LANES)
            for d in range(sub_keys):
                rows = slice(d * PEER_NKEYS, (d + 1) * PEER_NKEYS)
                words = slice(d * PEER_NKEYS // 2, (d + 1) * PEER_NKEYS // 2)
                w_ref[rows, cols] = pltpu.bitcast(a_ref[words, cols], BF16) * gates[c][d]

    def sub_block(k):
        firsts = first_matmul(k)
        second = second_matmul(k)
        tile = lambda x, row: x[row:row + SUBLANES, :LANES]
        assert n_tiles == 4
        anchors = [tile(firsts[0], 0), tile(firsts[2], 0), tile(second, 0), tile(second, D_MODEL // 2)]
        write_weights(gate(k, anchors))

    def pair(j, carry):
        sub_block(2 * j)
        sub_block(2 * j + 1)
        return carry

    assert n_sub % 2 == 0
    lax.fori_loop(0, n_sub // 2, pair, 0)

    @pl.when(e == pl.num_programs(1) - 1)
    def _():
        second_matmul(n_sub)
        o_ref[...] = x_ref[...] + mod_ref[0, 5:6, :] * acc_ref[...].T


def _peer(h2t, pu, pv_t, n0, r1, e0, e1, x1, mod, layout):
    t = x1.shape[0]
    tb = PEER_TOKENS
    eb = PEER_IBLOCK * PEER_NKEYS
    n_exp = pu.shape[0]
    n_sub = eb // MXU_DIM
    assert pv_t.shape == (n_exp // MXU_DIM, D_MODEL, MXU_DIM)
    padded = jnp.concatenate([jnp.zeros_like(pv_t[:1]), pv_t], axis=0)
    window = (jnp.arange(n_exp // eb) * n_sub)[:, None] + jnp.arange(n_sub + 1)[None, :]
    pv_t = padded[window]
    fk = pl.BlockSpec((PEER_HEADS, tb // LANES, PEER_NKEYS, LANES), lambda i, e: (0, i, 0, 0))
    sk = pl.BlockSpec((PEER_HEADS, tb // LANES, PEER_NKEYS // 2, LANES), lambda i, e: (0, i, 0, 0))
    return pl.pallas_call(
        _peer_kernel,
        grid=(t // tb, n_exp // eb),
        in_specs=[pl.BlockSpec((D_MODEL, tb), lambda i, e: (0, i)),
                  pl.BlockSpec((eb, D_MODEL), lambda i, e: (e, 0)),
                  pl.BlockSpec((1, n_sub + 1, D_MODEL, MXU_DIM), lambda i, e: (e, 0, 0, 0)),
                  fk, sk, fk, sk,
                  pl.BlockSpec((tb, D_MODEL), lambda i, e: (i, 0)),
                  _mod_spec(tb, layout)],
        out_specs=pl.BlockSpec((tb, D_MODEL), lambda i, e: (i, 0)),
        out_shape=jax.ShapeDtypeStruct((t, D_MODEL), F32),
        scratch_shapes=[pltpu.VMEM((D_MODEL, tb), F32),
                        pltpu.VMEM((MXU_DIM // 2, tb), jnp.uint32),
                        pltpu.VMEM((MXU_DIM, tb), BF16)],
        compiler_params=pltpu.CompilerParams(dimension_semantics=("arbitrary", "arbitrary"),
                                             vmem_limit_bytes=PEER_VMEM_LIMIT),
        name="peer_dense",
    )(h2t, pu, pv_t, n0, r1, e0, e1, x1, mod)


def _merge_cache(kv):
    b, h, l, d = kv.shape
    return kv.transpose(0, 2, 1, 3).reshape(b, l, h * d).astype(BF16)


def _split_heads(x, batch, seq):
    return x.reshape(batch, seq, N_HEADS, HEAD_DIM).transpose(0, 2, 1, 3)


def kernel(x_prompt, x_sample, cache_na_kv, cache_diff_kv, c, c_ctx, w_mod, b_mod, norm1_g, norm2_g, w_in, conv_w, conv_b, conv_ln_g, conv_ln_b, na_qn_g, na_kn_g, na_rel_bias, diff_qn_g, diff_kn_g, diff_lambda, diff_subln_g, gmlp_ln_g, gmlp_ln_b, gmlp_ws, gmlp_bs, w_out, peer_wq, peer_sub_keys, peer_u, peer_v):
    batch, seq, _ = x_prompt.shape
    dbatch, dseq, _ = x_sample.shape
    t_ctx = batch * seq
    layout = (t_ctx, dseq)
    x = jnp.concatenate([x_prompt.reshape(t_ctx, D_MODEL), x_sample.reshape(dbatch * dseq, D_MODEL)], axis=0)
    cond = jnp.concatenate([c_ctx[None, :], c, jnp.zeros((SUBLANES - 1 - dbatch, D_MODEL), F32)], axis=0)
    mod_all = _modulation(cond, w_mod, b_mod).reshape(DEPTH, SUBLANES, 6, D_MODEL)
    cos, sin = _rope_tables(dseq)
    row = lambda v: v.reshape(1, -1)
    tile = lambda v: jnp.tile(v, GROUP_W // v.shape[0]).reshape(1, GROUP_W)
    na_states, diff_states = [], []
    for l in range(DEPTH):
        lam_init = 0.8 - 0.6 * math.exp(-0.3 * l)
        mod = mod_all[l]
        parts = _in_proj(x, mod, row(norm1_g[l]), w_in[l].astype(BF16), layout)
        conv_o = _conv_module(parts, conv_w[l], row(conv_b[l]), row(conv_ln_g[l]), row(conv_ln_b[l]), t_ctx, dseq)
        gm_bias = jnp.repeat(gmlp_bs[l].T, GROUP_W // N_HEADS, axis=1)
        gm_o = _gmlp(parts, row(gmlp_ln_g[l]), row(gmlp_ln_b[l]), gmlp_ws[l].astype(BF16), gm_bias)
        qn, kn = tile(na_qn_g[l]), tile(na_kn_g[l])
        dqn, dkn, sub = tile(diff_qn_g[l]), tile(diff_kn_g[l]), tile(diff_subln_g[l])
        na_c, df_c, nk_c, dk_c = _ctx_attention(parts, batch, seq, qn, kn, dqn, dkn, sub, diff_lambda[l], lam_init)
        lq, lk, lv, ldq, ldk, ldv = _lat_prep(parts, t_ctx, dbatch, dseq, qn, kn, dqn, dkn, cos, sin)
        na_l = _na_latent(lq, lk, lv, _merge_cache(cache_na_kv[:, l, 0]), _merge_cache(cache_na_kv[:, l, 1]),
                          _na_bias_table(na_rel_bias[l], dseq // GRID_W), dbatch, dseq)
        ldk_t = ldk.reshape(dbatch, dseq, GROUP_W).transpose(0, 2, 1)
        ck_t = _merge_cache(cache_diff_kv[:, l, 0]).transpose(0, 2, 1)
        df_l = _diff_latent(ldq, ldk_t, ldv, ck_t, _merge_cache(cache_diff_kv[:, l, 1]), sub, diff_lambda[l],
                            dbatch, dseq, lam_init)
        na_o = jnp.concatenate([na_c, na_l], axis=0)
        diff_o = jnp.concatenate([df_c, df_l], axis=0)
        x1 = _out_proj(conv_o, na_o, diff_o, gm_o, x, mod, w_out[l].astype(BF16), layout)
        h2t, n0, r1, e0, e1 = _route(x1, mod, row(norm2_g[l]), peer_wq[l].T.astype(BF16),
                                          peer_sub_keys[l].astype(BF16), layout)
        pv_t = peer_v[l].astype(BF16).reshape(-1, MXU_DIM, D_MODEL).transpose(0, 2, 1)
        x = _peer(h2t, peer_u[l].astype(BF16), pv_t, n0, r1, e0, e1, x1, mod, layout)
        nv_c = parts[:t_ctx, 4 * GROUP_W:5 * GROUP_W]
        dv_c = parts[:t_ctx, 7 * GROUP_W:8 * GROUP_W]
        na_states.append(jnp.stack([_split_heads(nk_c, batch, seq), _split_heads(nv_c, batch, seq)], axis=1))
        diff_states.append(jnp.stack([_split_heads(dk_c, batch, seq), _split_heads(dv_c, batch, seq)], axis=1))
    y_prompt = x[:t_ctx].reshape(batch, seq, D_MODEL)
    y_sample = x[t_ctx:].reshape(dbatch, dseq, D_MODEL)
    return (y_prompt, y_sample, jnp.stack(na_states, axis=1), jnp.stack(diff_states, axis=1))
```

```python
import functools
import math

import numpy as np
import jax
import jax.numpy as jnp
from jax import lax
from jax.experimental import pallas as pl
from jax.experimental.pallas import tpu as pltpu

F32 = jnp.float32
BF16 = jnp.bfloat16

D_MODEL = 1024
DEPTH = 2
GRID_W = 64
GROUP_W = 256
HEAD_DIM = 64
N_HEADS = 4
DIFF_SUB = 32
CONV_K = 31
CONV_HALO = 16
NA_WIN_R = 8
NA_WIN_C = 16
ROPE_BASE = 10000.0
CHUNK = 128
PEER_HEADS = 8
PEER_NKEYS = 128
PEER_TOPK = 16
EPS = 1e-6
NEG = -1e30

LANES = 128
SUBLANES = 8
MXU_DIM = 256
VMEM_LIMIT = 48 * 1024 * 1024

SEQ_BLOCK = 256
ROW_BLOCK = 256
ROUTE_BLOCK = 128
PEER_TOKENS = 512
PEER_IBLOCK = 16
PEER_VMEM_LIMIT = 56 * 1024 * 1024
DIFF_QBLOCK = 128
DIFF_KCHUNK = 512


def _params(*sem):
    return pltpu.CompilerParams(dimension_semantics=sem, vmem_limit_bytes=VMEM_LIMIT)


def _dot(a, b):
    return jnp.dot(a, b, preferred_element_type=F32)


def _dot_nt(a, b):
    return lax.dot_general(a, b, (((1,), (1,)), ((), ())), preferred_element_type=F32)


def _split_dot(a, b):
    a1 = a.astype(BF16)
    r1 = a - a1.astype(F32)
    a2 = r1.astype(BF16)
    a3 = (r1 - a2.astype(F32)).astype(BF16)
    return _dot(a1, b) + _dot(a2, b) + _dot(a3, b)


def _group_ones(n, group):
    r = lax.broadcasted_iota(jnp.int32, (n, n), 0) // group
    c = lax.broadcasted_iota(jnp.int32, (n, n), 1) // group
    return (r == c).astype(BF16)


def _group_rms(x, gain, group):
    ssq = _split_dot(x * x, _group_ones(x.shape[-1], group))
    return x * lax.rsqrt(ssq * (1.0 / group) + EPS) * gain


def _gelu_tanh(x):
    k = -2.0 * math.sqrt(2.0 / math.pi) * math.log2(math.e)
    return x / (1.0 + jnp.exp2(x * (k + (k * 0.044715) * (x * x))))


def _layer_norm(y, g, b):
    mu = jnp.mean(y, axis=-1, keepdims=True)
    yc = y - mu
    return yc * lax.rsqrt(jnp.mean(yc * yc, axis=-1, keepdims=True) + EPS) * g + b


def _mod_spec(rows_per_block, layout):
    t_ctx, lat_seq = layout
    assert t_ctx % rows_per_block == 0 and lat_seq % rows_per_block == 0
    def index(i, *_):
        return (jnp.maximum((i * rows_per_block - t_ctx) // lat_seq + 1, 0), 0, 0)
    return pl.BlockSpec((1, 6, D_MODEL), index)


def _mod_kernel(c_ref, w_ref, b_ref, o_ref):
    c = c_ref[...]
    s = c * jax.nn.sigmoid(c)
    o_ref[0] = jnp.dot(s, w_ref[0], preferred_element_type=F32,
                       precision=lax.Precision.HIGHEST) + b_ref[0]


def _modulation(cond, w_mod, b_mod):
    nb = 4
    cols = 6 * D_MODEL // nb
    return pl.pallas_call(
        _mod_kernel,
        grid=(DEPTH, nb),
        in_specs=[pl.BlockSpec((SUBLANES, D_MODEL), lambda l, j: (0, 0)),
                  pl.BlockSpec((1, D_MODEL, cols), lambda l, j: (l, 0, j)),
                  pl.BlockSpec((1, 1, cols), lambda l, j: (l, 0, j))],
        out_specs=pl.BlockSpec((1, SUBLANES, cols), lambda l, j: (l, 0, j)),
        out_shape=jax.ShapeDtypeStruct((DEPTH, SUBLANES, 6 * D_MODEL), F32),
        name="modulation",
        compiler_params=_params("arbitrary", "arbitrary"),
    )(cond, w_mod, b_mod.reshape(DEPTH, 1, 6 * D_MODEL))


def _in_proj_kernel(x_ref, mod_ref, g_ref, w_ref, o_ref):
    x = x_ref[...]
    y = x * lax.rsqrt(jnp.mean(x * x, axis=-1, keepdims=True) + EPS) * g_ref[...]
    h = y * (1.0 + mod_ref[0, 1:2, :]) + mod_ref[0, 0:1, :]
    o_ref[...] = _dot(h.astype(BF16), w_ref[...])


def _in_proj(x, mod, g, w, layout):
    t = x.shape[0]
    n = w.shape[1]
    return pl.pallas_call(
        _in_proj_kernel,
        grid=(t // ROW_BLOCK,),
        in_specs=[pl.BlockSpec((ROW_BLOCK, D_MODEL), lambda i: (i, 0)),
                  _mod_spec(ROW_BLOCK, layout),
                  pl.BlockSpec((1, D_MODEL), lambda i: (0, 0)),
                  pl.BlockSpec((D_MODEL, n), lambda i: (0, 0))],
        out_specs=pl.BlockSpec((ROW_BLOCK, n), lambda i: (i, 0)),
        out_shape=jax.ShapeDtypeStruct((t, n), F32),
        name="in_proj",
        compiler_params=_params("arbitrary"),
    )(x, mod, g, w)


def _conv_kernel(ac, gc, ap, gp, an, gn, w_ref, b_ref, lg_ref, lb_ref, o_ref, pad_ref, *, ctx_blocks, seq_blocks):
    i = pl.program_id(0)
    is_ctx = i < ctx_blocks
    j = i % seq_blocks
    first = jnp.logical_or(is_ctx, j == 0)
    last = jnp.logical_or(is_ctx, j == seq_blocks - 1)
    yp = ap[...] * jax.nn.sigmoid(gp[...])
    yn = an[...] * jax.nn.sigmoid(gn[...])
    pad_ref[0:CONV_HALO, :] = jnp.where(first, 0.0, yp)
    pad_ref[CONV_HALO:CONV_HALO + SEQ_BLOCK, :] = ac[...] * jax.nn.sigmoid(gc[...])
    pad_ref[CONV_HALO + SEQ_BLOCK:, :] = jnp.where(last, 0.0, yn)
    off = CONV_HALO - CONV_K // 2
    acc = jnp.zeros((SEQ_BLOCK, GROUP_W), F32)
    for k in range(CONV_K):
        acc = acc + pad_ref[off + k:off + k + SEQ_BLOCK, :] * w_ref[k:k + 1, :]
    y = _layer_norm(acc + b_ref[...], lg_ref[...], lb_ref[...])
    o_ref[...] = y * jax.nn.sigmoid(y)


def _conv_module(parts, w, b, lg, lb, ctx_tokens, lat_seq):
    t = parts.shape[0]
    nblk = t // SEQ_BLOCK
    hb = SEQ_BLOCK // CONV_HALO
    last_halo = t // CONV_HALO - 1
    vec = pl.BlockSpec((1, GROUP_W), lambda i: (0, 0))
    kern = functools.partial(_conv_kernel, ctx_blocks=ctx_tokens // SEQ_BLOCK, seq_blocks=lat_seq // SEQ_BLOCK)
    return pl.pallas_call(
        kern,
        grid=(nblk,),
        in_specs=[pl.BlockSpec((SEQ_BLOCK, GROUP_W), lambda i: (i, 0)),
                  pl.BlockSpec((SEQ_BLOCK, GROUP_W), lambda i: (i, 1)),
                  pl.BlockSpec((CONV_HALO, GROUP_W), lambda i: (jnp.maximum(i * hb - 1, 0), 0)),
                  pl.BlockSpec((CONV_HALO, GROUP_W), lambda i: (jnp.maximum(i * hb - 1, 0), 1)),
                  pl.BlockSpec((CONV_HALO, GROUP_W), lambda i: (jnp.minimum((i + 1) * hb, last_halo), 0)),
                  pl.BlockSpec((CONV_HALO, GROUP_W), lambda i: (jnp.minimum((i + 1) * hb, last_halo), 1)),
                  pl.BlockSpec((CONV_K, GROUP_W), lambda i: (0, 0)),
                  vec, vec, vec],
        out_specs=pl.BlockSpec((SEQ_BLOCK, GROUP_W), lambda i: (i, 0)),
        out_shape=jax.ShapeDtypeStruct((t, GROUP_W), F32),
        scratch_shapes=[pltpu.VMEM((SEQ_BLOCK + 2 * CONV_HALO, GROUP_W), F32)],
        name="conv_module",
        compiler_params=_params("arbitrary"),
    )(parts, parts, parts, parts, parts, parts, w, b, lg, lb)


def _gmlp_kernel(u_ref, v_ref, lg_ref, lb_ref, ws_ref, bias_ref, o_ref):
    u = jax.nn.gelu(u_ref[...])
    v = _layer_norm(jax.nn.gelu(v_ref[...]), lg_ref[...], lb_ref[...]).astype(BF16)
    lane_g = lax.broadcasted_iota(jnp.int32, (1, GROUP_W), 1) // (GROUP_W // N_HEADS)
    for ch in range(SEQ_BLOCK // CHUNK):
        rows = slice(ch * CHUNK, (ch + 1) * CHUNK)
        s = bias_ref[...]
        for g in range(N_HEADS):
            s = s + jnp.where(lane_g == g, _dot(ws_ref[g], v[rows]), 0.0)
        o_ref[rows, :] = u[rows] * s


def _gmlp(parts, lg, lb, ws, bias):
    t = parts.shape[0]
    vec = pl.BlockSpec((1, GROUP_W), lambda i: (0, 0))
    return pl.pallas_call(
        _gmlp_kernel,
        grid=(t // SEQ_BLOCK,),
        in_specs=[pl.BlockSpec((SEQ_BLOCK, GROUP_W), lambda i: (i, 8)),
                  pl.BlockSpec((SEQ_BLOCK, GROUP_W), lambda i: (i, 9)),
                  vec, vec,
                  pl.BlockSpec((N_HEADS, CHUNK, CHUNK), lambda i: (0, 0, 0)),
                  pl.BlockSpec((CHUNK, GROUP_W), lambda i: (0, 0))],
        out_specs=pl.BlockSpec((SEQ_BLOCK, GROUP_W), lambda i: (i, 0)),
        out_shape=jax.ShapeDtypeStruct((t, GROUP_W), F32),
        compiler_params=_params("arbitrary"),
        name="gmlp",
    )(parts, parts, lg, lb, ws, bias)


def _lane_group_id(width, group):
    return lax.broadcasted_iota(jnp.int32, (1, width), 1) // group


def _stack_masked(q, group, ids):
    lane = _lane_group_id(q.shape[-1], group)
    return jnp.concatenate([jnp.where(lane == g, q, jnp.zeros_like(q)) for g in ids], axis=0)


def _diff_lambda(dl_ref, lam_init):
    dl = dl_ref[...]
    a = jnp.sum(dl[0:1, :] * dl[1:2, :], axis=-1, keepdims=True)
    b = jnp.sum(dl[2:3, :] * dl[3:4, :], axis=-1, keepdims=True)
    return jnp.exp(a) - jnp.exp(b) + lam_init


def _ctx_attn_kernel(q_ref, k_ref, v_ref, dq_ref, dk_ref, dv_ref, qn_ref, kn_ref, dqn_ref, dkn_ref, sub_ref,
                     dl_ref, na_ref, df_ref, na_kv_ref, df_kv_ref, *, lam_init):
    n = q_ref.shape[0]
    lane_h = _lane_group_id(GROUP_W, HEAD_DIM)

    def write_state(ref, keys, values):
        for h in range(N_HEADS):
            ref[0, 0, h] = keys[:, h * HEAD_DIM:(h + 1) * HEAD_DIM]
            ref[0, 1, h] = values[:, h * HEAD_DIM:(h + 1) * HEAD_DIM]

    qn = _group_rms(q_ref[...], qn_ref[...], HEAD_DIM)
    kn = _group_rms(k_ref[...], kn_ref[...], HEAD_DIM)
    v = v_ref[...]
    write_state(na_kv_ref, kn, v)
    qs = _stack_masked(qn.astype(BF16), HEAD_DIM, range(N_HEADS))
    s = _dot_nt(qs, kn.astype(BF16)) * (HEAD_DIM ** -0.5)
    e = jnp.exp(s - jnp.max(s, axis=-1, keepdims=True))
    r = _dot(e.astype(BF16), v.astype(BF16)) / jnp.sum(e, axis=-1, keepdims=True)
    o = jnp.zeros((n, GROUP_W), F32)
    for h in range(N_HEADS):
        o = o + jnp.where(lane_h == h, r[h * n:(h + 1) * n], 0.0)
    na_ref[...] = o
    lam = _diff_lambda(dl_ref, lam_init)
    dq = _group_rms(dq_ref[...], dqn_ref[...], DIFF_SUB)
    dk = _group_rms(dk_ref[...], dkn_ref[...], DIFF_SUB)
    dv = dv_ref[...]
    write_state(df_kv_ref, dk, dv)
    dkb = dk.astype(BF16)
    dvb = dv.astype(BF16)
    o = jnp.zeros((n, GROUP_W), F32)
    for h in range(N_HEADS):
        qs = _stack_masked(dq.astype(BF16), DIFF_SUB, (2 * h, 2 * h + 1))
        s = _dot_nt(qs, dkb) * (DIFF_SUB ** -0.5)
        e = jnp.exp(s - jnp.max(s, axis=-1, keepdims=True))
        inv = 1.0 / jnp.sum(e, axis=-1, keepdims=True)
        a = e[:n] * inv[:n] - e[n:] * (lam * inv[n:])
        o = o + jnp.where(lane_h == h, _dot(a.astype(BF16), dvb), 0.0)
    df_ref[...] = _group_rms(o, sub_ref[...], HEAD_DIM) * (1.0 - lam_init)


def _ctx_attention(parts, batch, seq, qn, kn, dqn, dkn, sub, dl, lam_init):
    t = batch * seq
    vec = pl.BlockSpec((1, GROUP_W), lambda b: (0, 0))
    col = lambda c: pl.BlockSpec((seq, GROUP_W), lambda b, c=c: (b, c))
    out = jax.ShapeDtypeStruct((t, GROUP_W), F32)
    ob = pl.BlockSpec((seq, GROUP_W), lambda b: (b, 0))
    state = jax.ShapeDtypeStruct((batch, 2, N_HEADS, seq, HEAD_DIM), F32)
    sb = pl.BlockSpec((1, 2, N_HEADS, seq, HEAD_DIM), lambda b: (b, 0, 0, 0, 0))
    return pl.pallas_call(
        functools.partial(_ctx_attn_kernel, lam_init=lam_init),
        grid=(batch,),
        in_specs=[col(2), col(3), col(4), col(5), col(6), col(7), vec, vec, vec, vec, vec,
                  pl.BlockSpec((4, DIFF_SUB), lambda b: (0, 0))],
        out_specs=[ob, ob, sb, sb],
        out_shape=[out, out, state, state],
        compiler_params=_params("arbitrary"),
        name="ctx_attention",
    )(parts, parts, parts, parts, parts, parts, qn, kn, dqn, dkn, sub, dl)


def _lat_prep_kernel(q_ref, k_ref, v_ref, dq_ref, dk_ref, dv_ref, qn_ref, kn_ref, dqn_ref, dkn_ref,
                     cos_ref, sin_ref, oq, ok, ov, odq, odk, odv):
    oq[...] = _group_rms(q_ref[...], qn_ref[...], HEAD_DIM).astype(BF16)
    ok[...] = _group_rms(k_ref[...], kn_ref[...], HEAD_DIM).astype(BF16)
    ov[...] = v_ref[...].astype(BF16)
    odv[...] = dv_ref[...].astype(BF16)
    half = DIFF_SUB // 4
    lane = lax.broadcasted_iota(jnp.int32, (1, GROUP_W), 1)
    lower = (lane % (2 * half)) < half
    cos = cos_ref[...]
    sin = sin_ref[...]

    def rope(x):
        partner = jnp.where(lower, pltpu.roll(x, GROUP_W - half, 1), pltpu.roll(x, half, 1))
        return x * cos + partner * sin

    odq[...] = rope(_group_rms(dq_ref[...], dqn_ref[...], DIFF_SUB)).astype(BF16)
    odk[...] = rope(_group_rms(dk_ref[...], dkn_ref[...], DIFF_SUB)).astype(BF16)


def _rope_tables(seq):
    half = DIFF_SUB // 4
    t = jnp.arange(seq)
    rows = (t // GRID_W).astype(F32)
    cols = (t % GRID_W).astype(F32)
    inv = ROPE_BASE ** (-jnp.arange(half, dtype=F32) / half)
    ang_r = rows[:, None] * inv[None, :]
    ang_c = cols[:, None] * inv[None, :]
    cos32 = jnp.concatenate([jnp.cos(ang_r), jnp.cos(ang_r), jnp.cos(ang_c), jnp.cos(ang_c)], axis=-1)
    sin32 = jnp.concatenate([-jnp.sin(ang_r), jnp.sin(ang_r), -jnp.sin(ang_c), jnp.sin(ang_c)], axis=-1)
    reps = GROUP_W // DIFF_SUB
    return jnp.tile(cos32, (1, reps)), jnp.tile(sin32, (1, reps))


def _lat_prep(parts, row0, batch, seq, qn, kn, dqn, dkn, cos, sin):
    t = batch * seq
    nb = seq // SEQ_BLOCK
    r0 = row0 // SEQ_BLOCK
    vec = pl.BlockSpec((1, GROUP_W), lambda i: (0, 0))
    col = lambda c: pl.BlockSpec((SEQ_BLOCK, GROUP_W), lambda i, c=c: (r0 + i, c))
    tab = pl.BlockSpec((SEQ_BLOCK, GROUP_W), lambda i: (i % nb, 0))
    out = jax.ShapeDtypeStruct((t, GROUP_W), BF16)
    ob = pl.BlockSpec((SEQ_BLOCK, GROUP_W), lambda i: (i, 0))
    return pl.pallas_call(
        _lat_prep_kernel,
        grid=(t // SEQ_BLOCK,),
        in_specs=[col(2), col(3), col(4), col(5), col(6), col(7), vec, vec, vec, vec, tab, tab],
        out_specs=[ob] * 6,
        out_shape=[out] * 6,
        compiler_params=_params("arbitrary"),
        name="lat_prep",
    )(parts, parts, parts, parts, parts, parts, qn, kn, dqn, dkn, cos, sin)


def _na_row_start(r, rows):
    return jnp.clip(r - NA_WIN_R // 2, 0, rows - NA_WIN_R)


def _na_lat_kernel(q_ref, k_ref, v_ref, ck_ref, cv_ref, bt_ref, o_ref, *, rows):
    r = pl.program_id(1)
    start = pl.multiple_of(_na_row_start(r, rows) * GRID_W, GRID_W)
    n_loc = NA_WIN_R * GRID_W
    kl = k_ref[pl.ds(start, n_loc), :]
    vl = v_ref[pl.ds(start, n_loc), :]
    qs = _stack_masked(q_ref[...], HEAD_DIM, range(N_HEADS))
    scale = HEAD_DIM ** -0.5
    s_loc = _dot_nt(qs, kl) * scale + bt_ref[0]
    s_ctx = _dot_nt(qs, ck_ref[0]) * scale
    m = jnp.maximum(jnp.max(s_loc, axis=-1, keepdims=True), jnp.max(s_ctx, axis=-1, keepdims=True))
    e_loc = jnp.exp(s_loc - m)
    e_ctx = jnp.exp(s_ctx - m)
    l = jnp.sum(e_loc, axis=-1, keepdims=True) + jnp.sum(e_ctx, axis=-1, keepdims=True)
    res = (_dot(e_loc.astype(BF16), vl) + _dot(e_ctx.astype(BF16), cv_ref[0])) / l
    lane_h = _lane_group_id(GROUP_W, HEAD_DIM)
    o = jnp.zeros((GRID_W, GROUP_W), F32)
    for h in range(N_HEADS):
        o = o + jnp.where(lane_h == h, res[h * GRID_W:(h + 1) * GRID_W], 0.0)
    o_ref[...] = o


def _na_bias_table(rel_bias, rows):
    wr = NA_WIN_R
    c = np.arange(GRID_W)
    c0 = np.clip(c - NA_WIN_C // 2, 0, GRID_W - NA_WIN_C)
    in_win = (c[None, :] >= c0[:, None]) & (c[None, :] < c0[:, None] + NA_WIN_C)
    dc_idx = np.clip(c[None, :] - c[:, None] + NA_WIN_C - 1, 0, 2 * NA_WIN_C - 2)
    off = np.arange(wr)
    dr_idx = (np.arange(wr)[None, :] - off[:, None]) + NA_WIN_R - 1
    rb = rel_bias[:, dr_idx]
    bias = jnp.take(rb, jnp.asarray(dc_idx), axis=-1)
    bias = jnp.where(jnp.asarray(in_win)[None, None, None], bias, NEG)
    bias = bias.transpose(1, 0, 3, 2, 4)
    return bias.reshape(wr, N_HEADS * GRID_W, wr * GRID_W)


def _na_latent(qn, kn, v, ck, cv, bias_tab, batch, seq):
    rows = seq // GRID_W
    nctx = ck.shape[1]
    full = pl.BlockSpec((seq, GROUP_W), lambda b, r: (b, 0))
    ctx = pl.BlockSpec((1, nctx, GROUP_W), lambda b, r: (b, 0, 0))
    return pl.pallas_call(
        functools.partial(_na_lat_kernel, rows=rows),
        grid=(batch, rows),
        in_specs=[pl.BlockSpec((GRID_W, GROUP_W), lambda b, r: (b * rows + r, 0)),
                  full, full, ctx, ctx,
                  pl.BlockSpec((1, N_HEADS * GRID_W, NA_WIN_R * GRID_W),
                               lambda b, r: (r - _na_row_start(r, rows), 0, 0))],
        out_specs=pl.BlockSpec((GRID_W, GROUP_W), lambda b, r: (b * rows + r, 0)),
        out_shape=jax.ShapeDtypeStruct((batch * seq, GROUP_W), F32),
        compiler_params=_params("arbitrary", "arbitrary"),
        name="na_latent",
    )(qn, kn, v, ck, cv, bias_tab)


def _lane_fold(fn, acc, x):
    for g in range(x.shape[-1] // LANES):
        acc = fn(acc, x[:, g * LANES:(g + 1) * LANES])
    return acc


def _diff_lat_kernel(q_ref, k_ref, v_ref, ck_ref, cv_ref, sub_ref, dl_ref, o_ref, s_ref, sc_ref, *, lam_init):
    n = q_ref.shape[0]
    n_chunks, rows, chunk = s_ref.shape
    n_sub = rows // n
    lam = _diff_lambda(dl_ref, lam_init)
    qs = _stack_masked(q_ref[...], DIFF_SUB, range(n_sub))
    c2 = (DIFF_SUB ** -0.5) * math.log2(math.e)
    step = pl.program_id(1)
    top = jnp.full((rows, LANES), NEG, F32)
    for j in range(n_chunks):
        s = _dot(qs, k_ref[0, :, j * chunk:(j + 1) * chunk])
        s_ref[(j + step) % n_chunks] = s
        top = _lane_fold(jnp.maximum, top, s)
    s = _dot(qs, ck_ref[0])
    sc_ref[...] = s
    top = _lane_fold(jnp.maximum, top, s)
    m = jnp.max(top, axis=-1, keepdims=True) * c2
    part = jnp.zeros((rows, LANES), F32)
    half = rows // 2
    res = [jnp.zeros((half, GROUP_W), F32)] * 2
    for j in range(n_chunks + 1):
        scores = s_ref[(j + step) % n_chunks] if j < n_chunks else sc_ref[...]
        values = v_ref[j * chunk:(j + 1) * chunk, :] if j < n_chunks else cv_ref[0]
        e = jnp.exp2(scores * c2 - m)
        part = _lane_fold(jnp.add, part, e)
        eb = e.astype(BF16)
        res = [res[r] + _dot(eb[r * half:(r + 1) * half], values) for r in range(2)]
    res = jnp.concatenate(res, axis=0) * (1.0 / jnp.sum(part, axis=-1, keepdims=True))
    lane_h = _lane_group_id(GROUP_W, HEAD_DIM)
    o = jnp.zeros((n, GROUP_W), F32)
    for h in range(N_HEADS):
        first, second = res[2 * h * n:(2 * h + 1) * n], res[(2 * h + 1) * n:(2 * h + 2) * n]
        o = o + jnp.where(lane_h == h, first - lam * second, 0.0)
    o_ref[...] = _group_rms(o, sub_ref[...], HEAD_DIM) * (1.0 - lam_init)


def _diff_latent(dq, dk_t, dv, ck_t, cv, sub, dl, batch, seq, lam_init):
    nq = seq // DIFF_QBLOCK
    nctx = cv.shape[1]
    full = pl.BlockSpec((seq, GROUP_W), lambda b, i: (b, 0))
    ctx = pl.BlockSpec((1, nctx, GROUP_W), lambda b, i: (b, 0, 0))
    return pl.pallas_call(
        functools.partial(_diff_lat_kernel, lam_init=lam_init),
        grid=(batch, nq),
        in_specs=[pl.BlockSpec((DIFF_QBLOCK, GROUP_W), lambda b, i: (b * nq + i, 0)),
                  pl.BlockSpec((1, GROUP_W, seq), lambda b, i: (b, 0, 0)), full,
                  pl.BlockSpec((1, GROUP_W, nctx), lambda b, i: (b, 0, 0)), ctx,
                  pl.BlockSpec((1, GROUP_W), lambda b, i: (0, 0)),
                  pl.BlockSpec((4, DIFF_SUB), lambda b, i: (0, 0))],
        out_specs=pl.BlockSpec((DIFF_QBLOCK, GROUP_W), lambda b, i: (b * nq + i, 0)),
        out_shape=jax.ShapeDtypeStruct((batch * seq, GROUP_W), F32),
        scratch_shapes=[pltpu.VMEM((seq // DIFF_KCHUNK, 2 * N_HEADS * DIFF_QBLOCK, DIFF_KCHUNK), F32),
                        pltpu.VMEM((2 * N_HEADS * DIFF_QBLOCK, nctx), F32)],
        compiler_params=_params("arbitrary", "arbitrary"),
        name="diff_latent",
    )(dq, dk_t, dv, ck_t, cv, sub, dl)


def _out_proj_kernel(conv_ref, na_c_ref, na_l_ref, df_c_ref, df_l_ref, gm_ref, x_ref, mod_ref, w_ref, o_ref, *,
                     ctx_blocks):
    is_ctx = pl.program_id(0) < ctx_blocks
    na = jnp.where(is_ctx, na_c_ref[...], na_l_ref[...])
    df = jnp.where(is_ctx, df_c_ref[...], df_l_ref[...])
    mixed = jnp.zeros(o_ref.shape, F32)
    for g, part in enumerate((conv_ref[...], na, df, gm_ref[...])):
        mixed = mixed + _dot(part.astype(BF16), w_ref[g * GROUP_W:(g + 1) * GROUP_W, :])
    o_ref[...] = x_ref[...] + mod_ref[0, 2:3, :] * mixed


def _out_proj(conv_o, na_c, na_l, df_c, df_l, gm_o, x, mod, w, layout):
    t = x.shape[0]
    ctx_blocks = layout[0] // ROW_BLOCK
    last_lat = (t - layout[0]) // ROW_BLOCK - 1
    part = pl.BlockSpec((ROW_BLOCK, GROUP_W), lambda i: (i, 0))
    ctx = pl.BlockSpec((ROW_BLOCK, GROUP_W), lambda i: (jnp.minimum(i, ctx_blocks - 1), 0))
    lat = pl.BlockSpec((ROW_BLOCK, GROUP_W), lambda i: (jnp.clip(i - ctx_blocks, 0, last_lat), 0))
    row = pl.BlockSpec((ROW_BLOCK, D_MODEL), lambda i: (i, 0))
    return pl.pallas_call(
        functools.partial(_out_proj_kernel, ctx_blocks=ctx_blocks),
        grid=(t // ROW_BLOCK,),
        in_specs=[part, ctx, lat, ctx, lat, part, row,
                  _mod_spec(ROW_BLOCK, layout),
                  pl.BlockSpec((D_MODEL, D_MODEL), lambda i: (0, 0))],
        out_specs=row,
        out_shape=jax.ShapeDtypeStruct((t, D_MODEL), F32),
        compiler_params=_params("arbitrary"),
        name="out_proj",
    )(conv_o, na_c, na_l, df_c, df_l, gm_o, x, mod, w)


def _sort_desc(vals):
    v = list(vals)
    n = len(v)
    k = 2
    while k <= n:
        j = k // 2
        while j >= 1:
            for i in range(n):
                l = i ^ j
                if l > i:
                    hi, lo = jnp.maximum(v[i], v[l]), jnp.minimum(v[i], v[l])
                    v[i], v[l] = (hi, lo) if (i & k) == 0 else (lo, hi)
            j //= 2
        k *= 2
    return v


def _merge_desc(v):
    v = list(v)
    n = len(v)
    j = n // 2
    while j >= 1:
        for i in range(n):
            l = i ^ j
            if l > i:
                v[i], v[l] = jnp.maximum(v[i], v[l]), jnp.minimum(v[i], v[l])
        j //= 2
    return v


def _top_half(a, b):
    n = len(a)
    return [jnp.maximum(a[i], b[n - 1 - i]) for i in range(n)]


def _top16_keys(s):
    k = PEER_TOPK
    v = _sort_desc([s[a * SUBLANES:(a + 1) * SUBLANES] for a in range(PEER_NKEYS // SUBLANES)])
    shift = SUBLANES // 2
    while shift >= 1:
        rolled = [pltpu.roll(x, shift, 0) for x in v]
        v = _merge_desc(_top_half(v, rolled))
        shift //= 2
    return v[:k]


def _paired_bf16(x):
    hi = pltpu.bitcast(x.astype(BF16).astype(F32), jnp.uint32)
    return hi | (hi >> 16)


def _route_kernel(x_ref, mod_ref, g_ref, wq_ref, sk_ref, h2_ref, n0_ref, r1_ref, e0_ref, e1_ref, s0_ref, s1_ref):
    x = x_ref[...]
    y = x * lax.rsqrt(jnp.mean(x * x, axis=-1, keepdims=True) + EPS) * g_ref[...]
    h2 = y * (1.0 + mod_ref[0, 4:5, :]) + mod_ref[0, 3:4, :]
    h2t = h2.T.astype(BF16)
    h2_ref[...] = h2t
    qt = _dot(wq_ref[...], h2t)
    tb = x.shape[0]
    sub = lax.broadcasted_iota(jnp.int32, (SUBLANES, tb), 0)
    k = PEER_TOPK
    tops = [[jnp.zeros((SUBLANES, tb), F32)] * k, [jnp.zeros((SUBLANES, tb), F32)] * k]
    for h in range(PEER_HEADS):
        for p in range(2):
            base = (2 * h + p) * PEER_NKEYS
            s = _dot(sk_ref[p], qt[base:base + PEER_NKEYS].astype(BF16))
            (s0_ref if p == 0 else s1_ref)[h] = s
            top = _top16_keys(s)
            tops[p] = [jnp.where(sub == h, top[a], tops[p][a]) for a in range(k)]
    cand = [tops[0][a] + tops[1][b] for a in range(k) for b in range(k) if (a + 1) * (b + 1) <= k]
    pad = [jnp.full((SUBLANES, tb), NEG, F32)] * (4 * k - len(cand))
    groups = [_sort_desc((cand + pad)[g * k:(g + 1) * k]) for g in range(4)]
    best = _top_half(_merge_desc(_top_half(groups[0], groups[1])), _merge_desc(_top_half(groups[2], groups[3])))
    thr = functools.reduce(jnp.minimum, best)
    m = tops[0][0] + tops[1][0]
    z = functools.reduce(jnp.add, [jnp.where(c >= thr, jnp.exp(c - m), 0.0) for c in cand])
    zinv = 1.0 / z
    counts, first = [], 0
    for a in range(k):
        n_b = k // (a + 1)
        counts.append(functools.reduce(jnp.add, [jnp.where(c >= thr, 1.0, 0.0) for c in cand[first:first + n_b]]))
        first += n_b
    for h in range(PEER_HEADS):
        s0 = s0_ref[h]
        s1 = s1_ref[h]
        n0 = jnp.zeros_like(s0)
        r1 = jnp.full_like(s1, float(k))
        for a in reversed(range(k)):
            n0 = jnp.where(s0 >= tops[0][a][h:h + 1], counts[a][h:h + 1], n0)
            r1 = jnp.where(s1 >= tops[1][a][h:h + 1], float(a), r1)
        n0_ref[h, 0] = _paired_bf16(n0)
        r1_ref[h, 0] = pltpu.bitcast(r1.astype(BF16), jnp.uint32)
        e0_ref[h, 0] = _paired_bf16(jnp.exp(s0 - tops[0][0][h:h + 1]) * zinv[h:h + 1])
        e1_ref[h, 0] = pltpu.bitcast(jnp.exp(s1 - tops[1][0][h:h + 1]).astype(BF16), jnp.uint32)


def _route(x1, mod, g, wq_t, sk, layout):
    t = x1.shape[0]
    tb = ROUTE_BLOCK
    assert tb == LANES
    tiles = t // LANES
    first_key = jax.ShapeDtypeStruct((PEER_HEADS, tiles, PEER_NKEYS, LANES), jnp.uint32)
    second_key = jax.ShapeDtypeStruct((PEER_HEADS, tiles, PEER_NKEYS // 2, LANES), jnp.uint32)
    fkb = pl.BlockSpec((PEER_HEADS, 1, PEER_NKEYS, LANES), lambda i: (0, i, 0, 0))
    skb = pl.BlockSpec((PEER_HEADS, 1, PEER_NKEYS // 2, LANES), lambda i: (0, i, 0, 0))
    scores = pltpu.VMEM((PEER_HEADS, PEER_NKEYS, tb), F32)
    return pl.pallas_call(
        _route_kernel,
        grid=(t // tb,),
        in_specs=[pl.BlockSpec((tb, D_MODEL), lambda i: (i, 0)),
                  _mod_spec(tb, layout),
                  pl.BlockSpec((1, D_MODEL), lambda i: (0, 0)),
                  pl.BlockSpec(wq_t.shape, lambda i: (0, 0)),
                  pl.BlockSpec(sk.shape, lambda i: (0, 0, 0))],
        out_specs=[pl.BlockSpec((D_MODEL, tb), lambda i: (0, i)), fkb, skb, fkb, skb],
        out_shape=[jax.ShapeDtypeStruct((D_MODEL, t), BF16), first_key, second_key, first_key, second_key],
        scratch_shapes=[scores, scores],
        compiler_params=_params("arbitrary"),
        name="peer_route",
    )(x1, mod, g, wq_t, sk)


def _rows_as_bf16(rows):
    tile = pltpu.bitcast(rows, BF16)
    return jnp.concatenate([tile] * (PEER_NKEYS // tile.shape[0]), axis=0)


def _peer_kernel(h2_ref, pu_ref, pv_ref, pv_prev_ref, n0_ref, r1_ref, e0_ref, e1_ref, x_ref, mod_ref, o_ref,
                 acc_ref, a_ref, w_ref):
    e = pl.program_id(1)

    @pl.when(e == 0)
    def _():
        acc_ref[...] = jnp.zeros_like(acc_ref)
        w_ref[...] = jnp.zeros_like(w_ref)

    tb = h2_ref.shape[1]
    n_sub = pv_ref.shape[0]
    sub_keys = MXU_DIM // PEER_NKEYS
    n_tiles = tb // LANES

    def first_matmul(k):
        start = k * MXU_DIM if isinstance(k, int) else pl.multiple_of(k * MXU_DIM, MXU_DIM)
        parts = [_dot(pu_ref[pl.ds(start, MXU_DIM), j * MXU_DIM:(j + 1) * MXU_DIM],
                      h2_ref[j * MXU_DIM:(j + 1) * MXU_DIM, :]) for j in range(D_MODEL // MXU_DIM)]
        a = functools.reduce(jnp.add, parts)
        a_ref[...] = pltpu.bitcast(_gelu_tanh(a).astype(BF16), jnp.uint32)
        return parts

    def second_matmul(k):
        values = pv_prev_ref[0] if isinstance(k, int) and k == 0 else pv_ref[k - 1]
        part = _dot(values, w_ref[...])
        acc_ref[...] += part
        return part

    def gate(k, anchors):
        key = e * PEER_IBLOCK + sub_keys * k
        out = []
        for c in range(n_tiles):
            cols = slice(c * LANES, (c + 1) * LANES)
            words = pltpu.bitcast(anchors[c], jnp.uint32)
            gates = [_rows_as_bf16((words >> 16) >> 16)] * sub_keys
            for h in range(PEER_HEADS):
                r1 = pltpu.bitcast(r1_ref[h, c], BF16)
                e1 = pltpu.bitcast(e1_ref[h, c], BF16)
                for d in range(sub_keys):
                    row = pl.ds(key + d, SUBLANES, stride=0)
                    n0_d = _rows_as_bf16(n0_ref[h, c, row, :])
                    e0_d = _rows_as_bf16(e0_ref[h, c, row, :])
                    gates[d] = gates[d] + jnp.where(r1 < n0_d, e1 * e0_d, jnp.zeros_like(e1))
            out.append(gates)
        return out

    def write_weights(gates):
        for c in range(n_tiles):
            cols = slice(c * LANES, (c + 1) * LANES)
            for d in range(sub_keys):
                rows = slice(d * PEER_NKEYS, (d + 1) * PEER_NKEYS)
                words = slice(d * PEER_NKEYS // 2, (d + 1) * PEER_NKEYS // 2)
                w_ref[rows, cols] = pltpu.bitcast(a_ref[words, cols], BF16) * gates[c][d]

    def sub_block(k):
        firsts = first_matmul(k)
        second = second_matmul(k)
        tile = lambda x, row: x[row:row + SUBLANES, :LANES]
        assert n_tiles == 4
        anchors = [tile(firsts[0], 0), tile(firsts[2], 0), tile(second, 0), tile(second, D_MODEL // 2)]
        write_weights(gate(k, anchors))

    def pair(j, carry):
        sub_block(2 * j)
        sub_block(2 * j + 1)
        return carry

    assert n_sub % 2 == 0
    pair(0, 0)
    lax.fori_loop(1, n_sub // 2, pair, 0)

    @pl.when(e == pl.num_programs(1) - 1)
    def _():
        second_matmul(n_sub)
        o_ref[...] = x_ref[...] + mod_ref[0, 5:6, :] * acc_ref[...].T


def _peer(h2t, pu, pv_t, n0, r1, e0, e1, x1, mod, layout):
    t = x1.shape[0]
    tb = PEER_TOKENS
    eb = PEER_IBLOCK * PEER_NKEYS
    n_exp = pu.shape[0]
    n_sub = eb // MXU_DIM
    assert pv_t.shape == (n_exp // MXU_DIM, D_MODEL, MXU_DIM)
    fk = pl.BlockSpec((PEER_HEADS, tb // LANES, PEER_NKEYS, LANES), lambda i, e: (0, i, 0, 0))
    sk = pl.BlockSpec((PEER_HEADS, tb // LANES, PEER_NKEYS // 2, LANES), lambda i, e: (0, i, 0, 0))
    return pl.pallas_call(
        _peer_kernel,
        grid=(t // tb, n_exp // eb),
        in_specs=[pl.BlockSpec((D_MODEL, tb), lambda i, e: (0, i)),
                  pl.BlockSpec((eb, D_MODEL), lambda i, e: (e, 0)),
                  pl.BlockSpec((n_sub, D_MODEL, MXU_DIM), lambda i, e: (e, 0, 0)),
                  pl.BlockSpec((1, D_MODEL, MXU_DIM), lambda i, e: (jnp.maximum(e * n_sub - 1, 0), 0, 0)),
                  fk, sk, fk, sk,
                  pl.BlockSpec((tb, D_MODEL), lambda i, e: (i, 0)),
                  _mod_spec(tb, layout)],
        out_specs=pl.BlockSpec((tb, D_MODEL), lambda i, e: (i, 0)),
        out_shape=jax.ShapeDtypeStruct((t, D_MODEL), F32),
        scratch_shapes=[pltpu.VMEM((D_MODEL, tb), F32),
                        pltpu.VMEM((MXU_DIM // 2, tb), jnp.uint32),
                        pltpu.VMEM((MXU_DIM, tb), BF16)],
        compiler_params=pltpu.CompilerParams(dimension_semantics=("arbitrary", "arbitrary"),
                                             vmem_limit_bytes=PEER_VMEM_LIMIT),
        name="peer_dense",
    )(h2t, pu, pv_t, pv_t, n0, r1, e0, e1, x1, mod)


def _merge_cache(kv):
    b, h, l, d = kv.shape
    return kv.transpose(0, 2, 1, 3).reshape(b, l, h * d).astype(BF16)


def kernel(x_prompt, x_sample, cache_na_kv, cache_diff_kv, c, c_ctx, w_mod, b_mod, norm1_g, norm2_g, w_in, conv_w, conv_b, conv_ln_g, conv_ln_b, na_qn_g, na_kn_g, na_rel_bias, diff_qn_g, diff_kn_g, diff_lambda, diff_subln_g, gmlp_ln_g, gmlp_ln_b, gmlp_ws, gmlp_bs, w_out, peer_wq, peer_sub_keys, peer_u, peer_v):
    batch, seq, _ = x_prompt.shape
    dbatch, dseq, _ = x_sample.shape
    t_ctx = batch * seq
    layout = (t_ctx, dseq)
    x = jnp.concatenate([x_prompt.reshape(t_ctx, D_MODEL), x_sample.reshape(dbatch * dseq, D_MODEL)], axis=0)
    cond = jnp.concatenate([c_ctx[None, :], c, jnp.zeros((SUBLANES - 1 - dbatch, D_MODEL), F32)], axis=0)
    mod_all = _modulation(cond, w_mod, b_mod).reshape(DEPTH, SUBLANES, 6, D_MODEL)
    cos, sin = _rope_tables(dseq)
    row = lambda v: v.reshape(1, -1)
    tile = lambda v: jnp.tile(v, GROUP_W // v.shape[0]).reshape(1, GROUP_W)
    na_states, diff_states = [], []
    for l in range(DEPTH):
        lam_init = 0.8 - 0.6 * math.exp(-0.3 * l)
        mod = mod_all[l]
        parts = _in_proj(x, mod, row(norm1_g[l]), w_in[l].astype(BF16), layout)
        conv_o = _conv_module(parts, conv_w[l], row(conv_b[l]), row(conv_ln_g[l]), row(conv_ln_b[l]), t_ctx, dseq)
        gm_bias = jnp.repeat(gmlp_bs[l].T, GROUP_W // N_HEADS, axis=1)
        gm_o = _gmlp(parts, row(gmlp_ln_g[l]), row(gmlp_ln_b[l]), gmlp_ws[l].astype(BF16), gm_bias)
        qn, kn = tile(na_qn_g[l]), tile(na_kn_g[l])
        dqn, dkn, sub = tile(diff_qn_g[l]), tile(diff_kn_g[l]), tile(diff_subln_g[l])
        na_c, df_c, na_kv, df_kv = _ctx_attention(parts, batch, seq, qn, kn, dqn, dkn, sub, diff_lambda[l], lam_init)
        lq, lk, lv, ldq, ldk, ldv = _lat_prep(parts, t_ctx, dbatch, dseq, qn, kn, dqn, dkn, cos, sin)
        na_l = _na_latent(lq, lk, lv, _merge_cache(cache_na_kv[:, l, 0]), _merge_cache(cache_na_kv[:, l, 1]),
                          _na_bias_table(na_rel_bias[l], dseq // GRID_W), dbatch, dseq)
        ldk_t = ldk.reshape(dbatch, dseq, GROUP_W).transpose(0, 2, 1)
        ck_t = _merge_cache(cache_diff_kv[:, l, 0]).transpose(0, 2, 1)
        df_l = _diff_latent(ldq, ldk_t, ldv, ck_t, _merge_cache(cache_diff_kv[:, l, 1]), sub, diff_lambda[l],
                            dbatch, dseq, lam_init)
        x1 = _out_proj(conv_o, na_c, na_l, df_c, df_l, gm_o, x, mod, w_out[l].astype(BF16), layout)
        h2t, n0, r1, e0, e1 = _route(x1, mod, row(norm2_g[l]), peer_wq[l].T.astype(BF16),
                                          peer_sub_keys[l].astype(BF16), layout)
        pv_t = peer_v[l].astype(BF16).reshape(-1, MXU_DIM, D_MODEL).transpose(0, 2, 1)
        x = _peer(h2t, peer_u[l].astype(BF16), pv_t, n0, r1, e0, e1, x1, mod, layout)
        na_states.append(na_kv)
        diff_states.append(df_kv)
    y_prompt = x[:t_ctx].reshape(batch, seq, D_MODEL)
    y_sample = x[t_ctx:].reshape(dbatch, dseq, D_MODEL)
    return (y_prompt, y_sample, jnp.stack(na_states, axis=1), jnp.stack(diff_states, axis=1))
```

```python
import functools
import math

import numpy as np
import jax
import jax.numpy as jnp
from jax import lax
from jax.experimental import pallas as pl
from jax.experimental.pallas import tpu as pltpu

F32 = jnp.float32
BF16 = jnp.bfloat16

D_MODEL = 1024
DEPTH = 2
GRID_W = 64
GROUP_W = 256
HEAD_DIM = 64
N_HEADS = 4
DIFF_SUB = 32
CONV_K = 31
CONV_HALO = 16
NA_WIN_R = 8
NA_WIN_C = 16
ROPE_BASE = 10000.0
CHUNK = 128
PEER_HEADS = 8
PEER_NKEYS = 128
PEER_TOPK = 16
EPS = 1e-6
NEG = -1e30

LANES = 128
SUBLANES = 8
MXU_DIM = 256
VMEM_LIMIT = 48 * 1024 * 1024

SEQ_BLOCK = 256
ROW_BLOCK = 256
ROUTE_BLOCK = 128
PEER_TOKENS = 512
PEER_IBLOCK = 16
PEER_VMEM_LIMIT = 56 * 1024 * 1024
DIFF_QBLOCK = 128
DIFF_KCHUNK = 512


def _params(*sem):
    return pltpu.CompilerParams(dimension_semantics=sem, vmem_limit_bytes=VMEM_LIMIT)


def _dot(a, b):
    return jnp.dot(a, b, preferred_element_type=F32)


def _dot_nt(a, b):
    return lax.dot_general(a, b, (((1,), (1,)), ((), ())), preferred_element_type=F32)


def _split_dot(a, b):
    a1 = a.astype(BF16)
    r1 = a - a1.astype(F32)
    a2 = r1.astype(BF16)
    a3 = (r1 - a2.astype(F32)).astype(BF16)
    return _dot(a1, b) + _dot(a2, b) + _dot(a3, b)


def _group_ones(n, group):
    r = lax.broadcasted_iota(jnp.int32, (n, n), 0) // group
    c = lax.broadcasted_iota(jnp.int32, (n, n), 1) // group
    return (r == c).astype(BF16)


def _group_rms(x, gain, group):
    ssq = _split_dot(x * x, _group_ones(x.shape[-1], group))
    return x * lax.rsqrt(ssq * (1.0 / group) + EPS) * gain


def _gelu_tanh(x):
    k = -2.0 * math.sqrt(2.0 / math.pi) * math.log2(math.e)
    return x / (1.0 + jnp.exp2(x * (k + (k * 0.044715) * (x * x))))


def _layer_norm(y, g, b):
    mu = jnp.mean(y, axis=-1, keepdims=True)
    yc = y - mu
    return yc * lax.rsqrt(jnp.mean(yc * yc, axis=-1, keepdims=True) + EPS) * g + b


def _mod_spec(rows_per_block, layout):
    t_ctx, lat_seq = layout
    assert t_ctx % rows_per_block == 0 and lat_seq % rows_per_block == 0
    def index(i, *_):
        return (jnp.maximum((i * rows_per_block - t_ctx) // lat_seq + 1, 0), 0, 0)
    return pl.BlockSpec((1, 6, D_MODEL), index)


def _mod_kernel(c_ref, w_ref, b_ref, o_ref):
    c = c_ref[...]
    s = c * jax.nn.sigmoid(c)
    o_ref[0] = jnp.dot(s, w_ref[0], preferred_element_type=F32,
                       precision=lax.Precision.HIGHEST) + b_ref[0]


def _modulation(cond, w_mod, b_mod):
    nb = 4
    cols = 6 * D_MODEL // nb
    return pl.pallas_call(
        _mod_kernel,
        grid=(DEPTH, nb),
        in_specs=[pl.BlockSpec((SUBLANES, D_MODEL), lambda l, j: (0, 0)),
                  pl.BlockSpec((1, D_MODEL, cols), lambda l, j: (l, 0, j)),
                  pl.BlockSpec((1, 1, cols), lambda l, j: (l, 0, j))],
        out_specs=pl.BlockSpec((1, SUBLANES, cols), lambda l, j: (l, 0, j)),
        out_shape=jax.ShapeDtypeStruct((DEPTH, SUBLANES, 6 * D_MODEL), F32),
        name="modulation",
        compiler_params=_params("arbitrary", "arbitrary"),
    )(cond, w_mod, b_mod.reshape(DEPTH, 1, 6 * D_MODEL))


def _in_proj_kernel(x_ref, mod_ref, g_ref, w_ref, o_ref):
    x = x_ref[...]
    y = x * lax.rsqrt(jnp.mean(x * x, axis=-1, keepdims=True) + EPS) * g_ref[...]
    h = y * (1.0 + mod_ref[0, 1:2, :]) + mod_ref[0, 0:1, :]
    o_ref[...] = _dot(h.astype(BF16), w_ref[...])


def _in_proj(x, mod, g, w, layout):
    t = x.shape[0]
    n = w.shape[1]
    return pl.pallas_call(
        _in_proj_kernel,
        grid=(t // ROW_BLOCK,),
        in_specs=[pl.BlockSpec((ROW_BLOCK, D_MODEL), lambda i: (i, 0)),
                  _mod_spec(ROW_BLOCK, layout),
                  pl.BlockSpec((1, D_MODEL), lambda i: (0, 0)),
                  pl.BlockSpec((D_MODEL, n), lambda i: (0, 0))],
        out_specs=pl.BlockSpec((ROW_BLOCK, n), lambda i: (i, 0)),
        out_shape=jax.ShapeDtypeStruct((t, n), F32),
        name="in_proj",
        compiler_params=_params("arbitrary"),
    )(x, mod, g, w)


def _conv_kernel(ac, gc, ap, gp, an, gn, w_ref, b_ref, lg_ref, lb_ref, o_ref, pad_ref, *, ctx_blocks, seq_blocks):
    i = pl.program_id(0)
    is_ctx = i < ctx_blocks
    j = i % seq_blocks
    first = jnp.logical_or(is_ctx, j == 0)
    last = jnp.logical_or(is_ctx, j == seq_blocks - 1)
    yp = ap[...] * jax.nn.sigmoid(gp[...])
    yn = an[...] * jax.nn.sigmoid(gn[...])
    pad_ref[0:CONV_HALO, :] = jnp.where(first, 0.0, yp)
    pad_ref[CONV_HALO:CONV_HALO + SEQ_BLOCK, :] = ac[...] * jax.nn.sigmoid(gc[...])
    pad_ref[CONV_HALO + SEQ_BLOCK:, :] = jnp.where(last, 0.0, yn)
    off = CONV_HALO - CONV_K // 2
    acc = jnp.zeros((SEQ_BLOCK, GROUP_W), F32)
    for k in range(CONV_K):
        acc = acc + pad_ref[off + k:off + k + SEQ_BLOCK, :] * w_ref[k:k + 1, :]
    y = _layer_norm(acc + b_ref[...], lg_ref[...], lb_ref[...])
    o_ref[...] = y * jax.nn.sigmoid(y)


def _conv_module(parts, w, b, lg, lb, ctx_tokens, lat_seq):
    t = parts.shape[0]
    nblk = t // SEQ_BLOCK
    hb = SEQ_BLOCK // CONV_HALO
    last_halo = t // CONV_HALO - 1
    vec = pl.BlockSpec((1, GROUP_W), lambda i: (0, 0))
    kern = functools.partial(_conv_kernel, ctx_blocks=ctx_tokens // SEQ_BLOCK, seq_blocks=lat_seq // SEQ_BLOCK)
    return pl.pallas_call(
        kern,
        grid=(nblk,),
        in_specs=[pl.BlockSpec((SEQ_BLOCK, GROUP_W), lambda i: (i, 0)),
                  pl.BlockSpec((SEQ_BLOCK, GROUP_W), lambda i: (i, 1)),
                  pl.BlockSpec((CONV_HALO, GROUP_W), lambda i: (jnp.maximum(i * hb - 1, 0), 0)),
                  pl.BlockSpec((CONV_HALO, GROUP_W), lambda i: (jnp.maximum(i * hb - 1, 0), 1)),
                  pl.BlockSpec((CONV_HALO, GROUP_W), lambda i: (jnp.minimum((i + 1) * hb, last_halo), 0)),
                  pl.BlockSpec((CONV_HALO, GROUP_W), lambda i: (jnp.minimum((i + 1) * hb, last_halo), 1)),
                  pl.BlockSpec((CONV_K, GROUP_W), lambda i: (0, 0)),
                  vec, vec, vec],
        out_specs=pl.BlockSpec((SEQ_BLOCK, GROUP_W), lambda i: (i, 0)),
        out_shape=jax.ShapeDtypeStruct((t, GROUP_W), F32),
        scratch_shapes=[pltpu.VMEM((SEQ_BLOCK + 2 * CONV_HALO, GROUP_W), F32)],
        name="conv_module",
        compiler_params=_params("arbitrary"),
    )(parts, parts, parts, parts, parts, parts, w, b, lg, lb)


def _gmlp_kernel(u_ref, v_ref, lg_ref, lb_ref, ws_ref, bias_ref, o_ref):
    u = jax.nn.gelu(u_ref[...])
    v = _layer_norm(jax.nn.gelu(v_ref[...]), lg_ref[...], lb_ref[...]).astype(BF16)
    lane_g = lax.broadcasted_iota(jnp.int32, (1, GROUP_W), 1) // (GROUP_W // N_HEADS)
    for ch in range(SEQ_BLOCK // CHUNK):
        rows = slice(ch * CHUNK, (ch + 1) * CHUNK)
        s = bias_ref[...]
        for g in range(N_HEADS):
            s = s + jnp.where(lane_g == g, _dot(ws_ref[g], v[rows]), 0.0)
        o_ref[rows, :] = u[rows] * s


def _gmlp(parts, lg, lb, ws, bias):
    t = parts.shape[0]
    vec = pl.BlockSpec((1, GROUP_W), lambda i: (0, 0))
    return pl.pallas_call(
        _gmlp_kernel,
        grid=(t // SEQ_BLOCK,),
        in_specs=[pl.BlockSpec((SEQ_BLOCK, GROUP_W), lambda i: (i, 8)),
                  pl.BlockSpec((SEQ_BLOCK, GROUP_W), lambda i: (i, 9)),
                  vec, vec,
                  pl.BlockSpec((N_HEADS, CHUNK, CHUNK), lambda i: (0, 0, 0)),
                  pl.BlockSpec((CHUNK, GROUP_W), lambda i: (0, 0))],
        out_specs=pl.BlockSpec((SEQ_BLOCK, GROUP_W), lambda i: (i, 0)),
        out_shape=jax.ShapeDtypeStruct((t, GROUP_W), F32),
        compiler_params=_params("arbitrary"),
        name="gmlp",
    )(parts, parts, lg, lb, ws, bias)


def _lane_group_id(width, group):
    return lax.broadcasted_iota(jnp.int32, (1, width), 1) // group


def _stack_masked(q, group, ids):
    lane = _lane_group_id(q.shape[-1], group)
    return jnp.concatenate([jnp.where(lane == g, q, jnp.zeros_like(q)) for g in ids], axis=0)


def _diff_lambda(dl_ref, lam_init):
    dl = dl_ref[...]
    a = jnp.sum(dl[0:1, :] * dl[1:2, :], axis=-1, keepdims=True)
    b = jnp.sum(dl[2:3, :] * dl[3:4, :], axis=-1, keepdims=True)
    return jnp.exp(a) - jnp.exp(b) + lam_init


def _ctx_attn_kernel(q_ref, k_ref, v_ref, dq_ref, dk_ref, dv_ref, qn_ref, kn_ref, dqn_ref, dkn_ref, sub_ref,
                     dl_ref, na_ref, df_ref, na_kv_ref, df_kv_ref, *, lam_init):
    n = q_ref.shape[0]
    lane_h = _lane_group_id(GROUP_W, HEAD_DIM)

    def write_state(ref, keys, values):
        for h in range(N_HEADS):
            ref[0, 0, h] = keys[:, h * HEAD_DIM:(h + 1) * HEAD_DIM]
            ref[0, 1, h] = values[:, h * HEAD_DIM:(h + 1) * HEAD_DIM]

    qn = _group_rms(q_ref[...], qn_ref[...], HEAD_DIM)
    kn = _group_rms(k_ref[...], kn_ref[...], HEAD_DIM)
    v = v_ref[...]
    write_state(na_kv_ref, kn, v)
    qs = _stack_masked(qn.astype(BF16), HEAD_DIM, range(N_HEADS))
    s = _dot_nt(qs, kn.astype(BF16)) * (HEAD_DIM ** -0.5)
    e = jnp.exp(s - jnp.max(s, axis=-1, keepdims=True))
    r = _dot(e.astype(BF16), v.astype(BF16)) / jnp.sum(e, axis=-1, keepdims=True)
    o = jnp.zeros((n, GROUP_W), F32)
    for h in range(N_HEADS):
        o = o + jnp.where(lane_h == h, r[h * n:(h + 1) * n], 0.0)
    na_ref[...] = o
    lam = _diff_lambda(dl_ref, lam_init)
    dq = _group_rms(dq_ref[...], dqn_ref[...], DIFF_SUB)
    dk = _group_rms(dk_ref[...], dkn_ref[...], DIFF_SUB)
    dv = dv_ref[...]
    write_state(df_kv_ref, dk, dv)
    dkb = dk.astype(BF16)
    dvb = dv.astype(BF16)
    o = jnp.zeros((n, GROUP_W), F32)
    for h in range(N_HEADS):
        qs = _stack_masked(dq.astype(BF16), DIFF_SUB, (2 * h, 2 * h + 1))
        s = _dot_nt(qs, dkb) * (DIFF_SUB ** -0.5)
        e = jnp.exp(s - jnp.max(s, axis=-1, keepdims=True))
        inv = 1.0 / jnp.sum(e, axis=-1, keepdims=True)
        a = e[:n] * inv[:n] - e[n:] * (lam * inv[n:])
        o = o + jnp.where(lane_h == h, _dot(a.astype(BF16), dvb), 0.0)
    df_ref[...] = _group_rms(o, sub_ref[...], HEAD_DIM) * (1.0 - lam_init)


def _ctx_attention(parts, batch, seq, qn, kn, dqn, dkn, sub, dl, lam_init):
    t = batch * seq
    vec = pl.BlockSpec((1, GROUP_W), lambda b: (0, 0))
    col = lambda c: pl.BlockSpec((seq, GROUP_W), lambda b, c=c: (b, c))
    out = jax.ShapeDtypeStruct((t, GROUP_W), F32)
    ob = pl.BlockSpec((seq, GROUP_W), lambda b: (b, 0))
    state = jax.ShapeDtypeStruct((batch, 2, N_HEADS, seq, HEAD_DIM), F32)
    sb = pl.BlockSpec((1, 2, N_HEADS, seq, HEAD_DIM), lambda b: (b, 0, 0, 0, 0))
    return pl.pallas_call(
        functools.partial(_ctx_attn_kernel, lam_init=lam_init),
        grid=(batch,),
        in_specs=[col(2), col(3), col(4), col(5), col(6), col(7), vec, vec, vec, vec, vec,
                  pl.BlockSpec((4, DIFF_SUB), lambda b: (0, 0))],
        out_specs=[ob, ob, sb, sb],
        out_shape=[out, out, state, state],
        compiler_params=_params("arbitrary"),
        name="ctx_attention",
    )(parts, parts, parts, parts, parts, parts, qn, kn, dqn, dkn, sub, dl)


def _lat_prep_kernel(q_ref, k_ref, v_ref, dq_ref, dk_ref, dv_ref, qn_ref, kn_ref, dqn_ref, dkn_ref,
                     cos_ref, sin_ref, oq, ok, ov, odq, odk, odv):
    oq[...] = _group_rms(q_ref[...], qn_ref[...], HEAD_DIM).astype(BF16)
    ok[...] = _group_rms(k_ref[...], kn_ref[...], HEAD_DIM).astype(BF16)
    ov[...] = v_ref[...].astype(BF16)
    odv[...] = dv_ref[...].astype(BF16)
    half = DIFF_SUB // 4
    lane = lax.broadcasted_iota(jnp.int32, (1, GROUP_W), 1)
    lower = (lane % (2 * half)) < half
    cos = cos_ref[...]
    sin = sin_ref[...]

    def rope(x):
        partner = jnp.where(lower, pltpu.roll(x, GROUP_W - half, 1), pltpu.roll(x, half, 1))
        return x * cos + partner * sin

    odq[...] = rope(_group_rms(dq_ref[...], dqn_ref[...], DIFF_SUB)).astype(BF16)
    odk[...] = rope(_group_rms(dk_ref[...], dkn_ref[...], DIFF_SUB)).astype(BF16)


def _rope_tables(seq):
    half = DIFF_SUB // 4
    t = jnp.arange(seq)
    rows = (t // GRID_W).astype(F32)
    cols = (t % GRID_W).astype(F32)
    inv = ROPE_BASE ** (-jnp.arange(half, dtype=F32) / half)
    ang_r = rows[:, None] * inv[None, :]
    ang_c = cols[:, None] * inv[None, :]
    cos32 = jnp.concatenate([jnp.cos(ang_r), jnp.cos(ang_r), jnp.cos(ang_c), jnp.cos(ang_c)], axis=-1)
    sin32 = jnp.concatenate([-jnp.sin(ang_r), jnp.sin(ang_r), -jnp.sin(ang_c), jnp.sin(ang_c)], axis=-1)
    reps = GROUP_W // DIFF_SUB
    return jnp.tile(cos32, (1, reps)), jnp.tile(sin32, (1, reps))


def _lat_prep(parts, row0, batch, seq, qn, kn, dqn, dkn, cos, sin):
    t = batch * seq
    nb = seq // SEQ_BLOCK
    r0 = row0 // SEQ_BLOCK
    vec = pl.BlockSpec((1, GROUP_W), lambda i: (0, 0))
    col = lambda c: pl.BlockSpec((SEQ_BLOCK, GROUP_W), lambda i, c=c: (r0 + i, c))
    tab = pl.BlockSpec((SEQ_BLOCK, GROUP_W), lambda i: (i % nb, 0))
    out = jax.ShapeDtypeStruct((t, GROUP_W), BF16)
    ob = pl.BlockSpec((SEQ_BLOCK, GROUP_W), lambda i: (i, 0))
    return pl.pallas_call(
        _lat_prep_kernel,
        grid=(t // SEQ_BLOCK,),
        in_specs=[col(2), col(3), col(4), col(5), col(6), col(7), vec, vec, vec, vec, tab, tab],
        out_specs=[ob] * 6,
        out_shape=[out] * 6,
        compiler_params=_params("arbitrary"),
        name="lat_prep",
    )(parts, parts, parts, parts, parts, parts, qn, kn, dqn, dkn, cos, sin)


def _na_row_start(r, rows):
    return jnp.clip(r - NA_WIN_R // 2, 0, rows - NA_WIN_R)


def _na_lat_kernel(q_ref, k_ref, v_ref, ck_ref, cv_ref, bt_ref, o_ref, *, rows):
    r = pl.program_id(1)
    start = pl.multiple_of(_na_row_start(r, rows) * GRID_W, GRID_W)
    n_loc = NA_WIN_R * GRID_W
    kl = k_ref[pl.ds(start, n_loc), :]
    vl = v_ref[pl.ds(start, n_loc), :]
    qs = _stack_masked(q_ref[...], HEAD_DIM, range(N_HEADS))
    scale = HEAD_DIM ** -0.5
    s_loc = _dot_nt(qs, kl) * scale + bt_ref[0]
    s_ctx = _dot_nt(qs, ck_ref[0]) * scale
    m = jnp.maximum(jnp.max(s_loc, axis=-1, keepdims=True), jnp.max(s_ctx, axis=-1, keepdims=True))
    e_loc = jnp.exp(s_loc - m)
    e_ctx = jnp.exp(s_ctx - m)
    l = jnp.sum(e_loc, axis=-1, keepdims=True) + jnp.sum(e_ctx, axis=-1, keepdims=True)
    res = (_dot(e_loc.astype(BF16), vl) + _dot(e_ctx.astype(BF16), cv_ref[0])) / l
    lane_h = _lane_group_id(GROUP_W, HEAD_DIM)
    o = jnp.zeros((GRID_W, GROUP_W), F32)
    for h in range(N_HEADS):
        o = o + jnp.where(lane_h == h, res[h * GRID_W:(h + 1) * GRID_W], 0.0)
    o_ref[...] = o


def _na_bias_table(rel_bias, rows):
    wr = NA_WIN_R
    c = np.arange(GRID_W)
    c0 = np.clip(c - NA_WIN_C // 2, 0, GRID_W - NA_WIN_C)
    in_win = (c[None, :] >= c0[:, None]) & (c[None, :] < c0[:, None] + NA_WIN_C)
    dc_idx = np.clip(c[None, :] - c[:, None] + NA_WIN_C - 1, 0, 2 * NA_WIN_C - 2)
    off = np.arange(wr)
    dr_idx = (np.arange(wr)[None, :] - off[:, None]) + NA_WIN_R - 1
    rb = rel_bias[:, dr_idx]
    onehot = jnp.asarray(dc_idx[..., None] == np.arange(2 * NA_WIN_C - 1), F32)
    bias = jnp.einsum('howd,qkd->howqk', rb, onehot, precision=lax.Precision.HIGHEST)
    bias = jnp.where(jnp.asarray(in_win)[None, None, None], bias, NEG)
    bias = bias.transpose(1, 0, 3, 2, 4)
    return bias.reshape(wr, N_HEADS * GRID_W, wr * GRID_W)


def _na_latent(qn, kn, v, ck, cv, bias_tab, batch, seq):
    rows = seq // GRID_W
    nctx = ck.shape[1]
    full = pl.BlockSpec((seq, GROUP_W), lambda b, r: (b, 0))
    ctx = pl.BlockSpec((1, nctx, GROUP_W), lambda b, r: (b, 0, 0))
    return pl.pallas_call(
        functools.partial(_na_lat_kernel, rows=rows),
        grid=(batch, rows),
        in_specs=[pl.BlockSpec((GRID_W, GROUP_W), lambda b, r: (b * rows + r, 0)),
                  full, full, ctx, ctx,
                  pl.BlockSpec((1, N_HEADS * GRID_W, NA_WIN_R * GRID_W),
                               lambda b, r: (r - _na_row_start(r, rows), 0, 0))],
        out_specs=pl.BlockSpec((GRID_W, GROUP_W), lambda b, r: (b * rows + r, 0)),
        out_shape=jax.ShapeDtypeStruct((batch * seq, GROUP_W), F32),
        compiler_params=_params("arbitrary", "arbitrary"),
        name="na_latent",
    )(qn, kn, v, ck, cv, bias_tab)


def _lane_fold(fn, acc, x):
    for g in range(x.shape[-1] // LANES):
        acc = fn(acc, x[:, g * LANES:(g + 1) * LANES])
    return acc


def _diff_lat_kernel(q_ref, k_ref, v_ref, ck_ref, cv_ref, sub_ref, dl_ref, o_ref, s_ref, sc_ref, *, lam_init):
    n = q_ref.shape[0]
    n_chunks, rows, chunk = s_ref.shape
    n_sub = rows // n
    lam = _diff_lambda(dl_ref, lam_init)
    qs = _stack_masked(q_ref[...], DIFF_SUB, range(n_sub))
    c2 = (DIFF_SUB ** -0.5) * math.log2(math.e)
    step = pl.program_id(1)
    top = jnp.full((rows, LANES), NEG, F32)
    for j in range(n_chunks):
        s = _dot(qs, k_ref[0, :, j * chunk:(j + 1) * chunk])
        s_ref[(j + step) % n_chunks] = s
        top = _lane_fold(jnp.maximum, top, s)
    s = _dot(qs, ck_ref[0])
    sc_ref[...] = s
    top = _lane_fold(jnp.maximum, top, s)
    m = jnp.max(top, axis=-1, keepdims=True) * c2
    part = jnp.zeros((rows, LANES), F32)
    half = rows // 2
    res = [jnp.zeros((half, GROUP_W), F32)] * 2
    for j in range(n_chunks + 1):
        scores = s_ref[(j + step) % n_chunks] if j < n_chunks else sc_ref[...]
        values = v_ref[j * chunk:(j + 1) * chunk, :] if j < n_chunks else cv_ref[0]
        e = jnp.exp2(scores * c2 - m)
        part = _lane_fold(jnp.add, part, e)
        eb = e.astype(BF16)
        res = [res[r] + _dot(eb[r * half:(r + 1) * half], values) for r in range(2)]
    res = jnp.concatenate(res, axis=0) * (1.0 / jnp.sum(part, axis=-1, keepdims=True))
    lane_h = _lane_group_id(GROUP_W, HEAD_DIM)
    o = jnp.zeros((n, GROUP_W), F32)
    for h in range(N_HEADS):
        first, second = res[2 * h * n:(2 * h + 1) * n], res[(2 * h + 1) * n:(2 * h + 2) * n]
        o = o + jnp.where(lane_h == h, first - lam * second, 0.0)
    o_ref[...] = _group_rms(o, sub_ref[...], HEAD_DIM) * (1.0 - lam_init)


def _diff_latent(dq, dk_t, dv, ck_t, cv, sub, dl, batch, seq, lam_init):
    nq = seq // DIFF_QBLOCK
    nctx = cv.shape[1]
    full = pl.BlockSpec((seq, GROUP_W), lambda b, i: (b, 0))
    ctx = pl.BlockSpec((1, nctx, GROUP_W), lambda b, i: (b, 0, 0))
    return pl.pallas_call(
        functools.partial(_diff_lat_kernel, lam_init=lam_init),
        grid=(batch, nq),
        in_specs=[pl.BlockSpec((DIFF_QBLOCK, GROUP_W), lambda b, i: (b * nq + i, 0)),
                  pl.BlockSpec((1, GROUP_W, seq), lambda b, i: (b, 0, 0)), full,
                  pl.BlockSpec((1, GROUP_W, nctx), lambda b, i: (b, 0, 0)), ctx,
                  pl.BlockSpec((1, GROUP_W), lambda b, i: (0, 0)),
                  pl.BlockSpec((4, DIFF_SUB), lambda b, i: (0, 0))],
        out_specs=pl.BlockSpec((DIFF_QBLOCK, GROUP_W), lambda b, i: (b * nq + i, 0)),
        out_shape=jax.ShapeDtypeStruct((batch * seq, GROUP_W), F32),
        scratch_shapes=[pltpu.VMEM((seq // DIFF_KCHUNK, 2 * N_HEADS * DIFF_QBLOCK, DIFF_KCHUNK), F32),
                        pltpu.VMEM((2 * N_HEADS * DIFF_QBLOCK, nctx), F32)],
        compiler_params=_params("arbitrary", "arbitrary"),
        name="diff_latent",
    )(dq, dk_t, dv, ck_t, cv, sub, dl)


def _out_proj_kernel(conv_ref, na_c_ref, na_l_ref, df_c_ref, df_l_ref, gm_ref, x_ref, mod_ref, w_ref, o_ref, *,
                     ctx_blocks):
    is_ctx = pl.program_id(0) < ctx_blocks
    na = jnp.where(is_ctx, na_c_ref[...], na_l_ref[...])
    df = jnp.where(is_ctx, df_c_ref[...], df_l_ref[...])
    mixed = jnp.zeros(o_ref.shape, F32)
    for g, part in enumerate((conv_ref[...], na, df, gm_ref[...])):
        mixed = mixed + _dot(part.astype(BF16), w_ref[g * GROUP_W:(g + 1) * GROUP_W, :])
    o_ref[...] = x_ref[...] + mod_ref[0, 2:3, :] * mixed


def _out_proj(conv_o, na_c, na_l, df_c, df_l, gm_o, x, mod, w, layout):
    t = x.shape[0]
    ctx_blocks = layout[0] // ROW_BLOCK
    last_lat = (t - layout[0]) // ROW_BLOCK - 1
    part = pl.BlockSpec((ROW_BLOCK, GROUP_W), lambda i: (i, 0))
    ctx = pl.BlockSpec((ROW_BLOCK, GROUP_W), lambda i: (jnp.minimum(i, ctx_blocks - 1), 0))
    lat = pl.BlockSpec((ROW_BLOCK, GROUP_W), lambda i: (jnp.clip(i - ctx_blocks, 0, last_lat), 0))
    row = pl.BlockSpec((ROW_BLOCK, D_MODEL), lambda i: (i, 0))
    return pl.pallas_call(
        functools.partial(_out_proj_kernel, ctx_blocks=ctx_blocks),
        grid=(t // ROW_BLOCK,),
        in_specs=[part, ctx, lat, ctx, lat, part, row,
                  _mod_spec(ROW_BLOCK, layout),
                  pl.BlockSpec((D_MODEL, D_MODEL), lambda i: (0, 0))],
        out_specs=row,
        out_shape=jax.ShapeDtypeStruct((t, D_MODEL), F32),
        compiler_params=_params("arbitrary"),
        name="out_proj",
    )(conv_o, na_c, na_l, df_c, df_l, gm_o, x, mod, w)


def _sort_desc(vals):
    v = list(vals)
    n = len(v)
    k = 2
    while k <= n:
        j = k // 2
        while j >= 1:
            for i in range(n):
                l = i ^ j
                if l > i:
                    hi, lo = jnp.maximum(v[i], v[l]), jnp.minimum(v[i], v[l])
                    v[i], v[l] = (hi, lo) if (i & k) == 0 else (lo, hi)
            j //= 2
        k *= 2
    return v


def _merge_desc(v):
    v = list(v)
    n = len(v)
    j = n // 2
    while j >= 1:
        for i in range(n):
            l = i ^ j
            if l > i:
                v[i], v[l] = jnp.maximum(v[i], v[l]), jnp.minimum(v[i], v[l])
        j //= 2
    return v


def _top_half(a, b):
    n = len(a)
    return [jnp.maximum(a[i], b[n - 1 - i]) for i in range(n)]


def _top16_keys(s):
    k = PEER_TOPK
    v = _sort_desc([s[a * SUBLANES:(a + 1) * SUBLANES] for a in range(PEER_NKEYS // SUBLANES)])
    shift = SUBLANES // 2
    while shift >= 1:
        rolled = [pltpu.roll(x, shift, 0) for x in v]
        v = _merge_desc(_top_half(v, rolled))
        shift //= 2
    return v[:k]


def _paired_bf16(x):
    hi = pltpu.bitcast(x.astype(BF16).astype(F32), jnp.uint32)
    return hi | (hi >> 16)


def _route_kernel(x_ref, mod_ref, g_ref, wq_ref, sk_ref, h2_ref, n0_ref, r1_ref, e0_ref, e1_ref, s0_ref, s1_ref):
    x = x_ref[...]
    y = x * lax.rsqrt(jnp.mean(x * x, axis=-1, keepdims=True) + EPS) * g_ref[...]
    h2 = y * (1.0 + mod_ref[0, 4:5, :]) + mod_ref[0, 3:4, :]
    h2t = h2.T.astype(BF16)
    h2_ref[...] = h2t
    qt = _dot(wq_ref[...], h2t)
    tb = x.shape[0]
    sub = lax.broadcasted_iota(jnp.int32, (SUBLANES, tb), 0)
    k = PEER_TOPK
    tops = [[jnp.zeros((SUBLANES, tb), F32)] * k, [jnp.zeros((SUBLANES, tb), F32)] * k]
    for h in range(PEER_HEADS):
        for p in range(2):
            base = (2 * h + p) * PEER_NKEYS
            s = _dot(sk_ref[p], qt[base:base + PEER_NKEYS].astype(BF16))
            (s0_ref if p == 0 else s1_ref)[h] = s
            top = _top16_keys(s)
            tops[p] = [jnp.where(sub == h, top[a], tops[p][a]) for a in range(k)]
    cand = [tops[0][a] + tops[1][b] for a in range(k) for b in range(k) if (a + 1) * (b + 1) <= k]
    pad = [jnp.full((SUBLANES, tb), NEG, F32)] * (4 * k - len(cand))
    groups = [_sort_desc((cand + pad)[g * k:(g + 1) * k]) for g in range(4)]
    best = _top_half(_merge_desc(_top_half(groups[0], groups[1])), _merge_desc(_top_half(groups[2], groups[3])))
    thr = functools.reduce(jnp.minimum, best)
    m = tops[0][0] + tops[1][0]
    z = functools.reduce(jnp.add, [jnp.where(c >= thr, jnp.exp(c - m), 0.0) for c in cand])
    zinv = 1.0 / z
    counts, first = [], 0
    for a in range(k):
        n_b = k // (a + 1)
        counts.append(functools.reduce(jnp.add, [jnp.where(c >= thr, 1.0, 0.0) for c in cand[first:first + n_b]]))
        first += n_b
    for h in range(PEER_HEADS):
        s0 = s0_ref[h]
        s1 = s1_ref[h]
        n0 = jnp.zeros_like(s0)
        r1 = jnp.full_like(s1, float(k))
        for a in reversed(range(k)):
            n0 = jnp.where(s0 >= tops[0][a][h:h + 1], counts[a][h:h + 1], n0)
            r1 = jnp.where(s1 >= tops[1][a][h:h + 1], float(a), r1)
        n0_ref[h, 0] = _paired_bf16(n0)
        r1_ref[h, 0] = pltpu.bitcast(r1.astype(BF16), jnp.uint32)
        e0_ref[h, 0] = _paired_bf16(jnp.exp(s0 - tops[0][0][h:h + 1]) * zinv[h:h + 1])
        e1_ref[h, 0] = pltpu.bitcast(jnp.exp(s1 - tops[1][0][h:h + 1]).astype(BF16), jnp.uint32)


def _route(x1, mod, g, wq_t, sk, layout):
    t = x1.shape[0]
    tb = ROUTE_BLOCK
    assert tb == LANES
    tiles = t // LANES
    first_key = jax.ShapeDtypeStruct((PEER_HEADS, tiles, PEER_NKEYS, LANES), jnp.uint32)
    second_key = jax.ShapeDtypeStruct((PEER_HEADS, tiles, PEER_NKEYS // 2, LANES), jnp.uint32)
    fkb = pl.BlockSpec((PEER_HEADS, 1, PEER_NKEYS, LANES), lambda i: (0, i, 0, 0))
    skb = pl.BlockSpec((PEER_HEADS, 1, PEER_NKEYS // 2, LANES), lambda i: (0, i, 0, 0))
    scores = pltpu.VMEM((PEER_HEADS, PEER_NKEYS, tb), F32)
    return pl.pallas_call(
        _route_kernel,
        grid=(t // tb,),
        in_specs=[pl.BlockSpec((tb, D_MODEL), lambda i: (i, 0)),
                  _mod_spec(tb, layout),
                  pl.BlockSpec((1, D_MODEL), lambda i: (0, 0)),
                  pl.BlockSpec(wq_t.shape, lambda i: (0, 0)),
                  pl.BlockSpec(sk.shape, lambda i: (0, 0, 0))],
        out_specs=[pl.BlockSpec((D_MODEL, tb), lambda i: (0, i)), fkb, skb, fkb, skb],
        out_shape=[jax.ShapeDtypeStruct((D_MODEL, t), BF16), first_key, second_key, first_key, second_key],
        scratch_shapes=[scores, scores],
        compiler_params=_params("arbitrary"),
        name="peer_route",
    )(x1, mod, g, wq_t, sk)


def _rows_as_bf16(rows):
    tile = pltpu.bitcast(rows, BF16)
    return jnp.concatenate([tile] * (PEER_NKEYS // tile.shape[0]), axis=0)


def _peer_kernel(h2_ref, pu_ref, pv_ref, pv_prev_ref, n0_ref, r1_ref, e0_ref, e1_ref, x_ref, mod_ref, o_ref,
                 acc_ref, a0_ref, a1_ref, w0_ref, w1_ref):
    e = pl.program_id(1)
    a_refs = (a0_ref, a1_ref)
    w_refs = (w0_ref, w1_ref)

    @pl.when(e == 0)
    def _():
        acc_ref[...] = jnp.zeros_like(acc_ref)
        for w_ref in w_refs:
            w_ref[...] = jnp.zeros_like(w_ref)

    tb = h2_ref.shape[1]
    n_sub = pv_ref.shape[0]
    sub_keys = MXU_DIM // PEER_NKEYS
    n_tiles = tb // LANES

    def first_matmul(k, slot):
        start = k * MXU_DIM if isinstance(k, int) else pl.multiple_of(k * MXU_DIM, MXU_DIM)
        parts = [_dot(pu_ref[pl.ds(start, MXU_DIM), j * MXU_DIM:(j + 1) * MXU_DIM],
                      h2_ref[j * MXU_DIM:(j + 1) * MXU_DIM, :]) for j in range(D_MODEL // MXU_DIM)]
        a = functools.reduce(jnp.add, parts)
        a_refs[slot][...] = pltpu.bitcast(_gelu_tanh(a).astype(BF16), jnp.uint32)
        return parts

    def second_matmul(values, slot):
        part = _dot(values, w_refs[slot][...])
        acc_ref[...] += part
        return part

    def gate(k, anchors):
        key = e * PEER_IBLOCK + sub_keys * k
        out = []
        for c in range(n_tiles):
            cols = slice(c * LANES, (c + 1) * LANES)
            words = pltpu.bitcast(anchors[c], jnp.uint32)
            gates = [_rows_as_bf16((words >> 16) >> 16)] * sub_keys
            for h in range(PEER_HEADS):
                r1 = pltpu.bitcast(r1_ref[h, c], BF16)
                e1 = pltpu.bitcast(e1_ref[h, c], BF16)
                for d in range(sub_keys):
                    row = pl.ds(key + d, SUBLANES, stride=0)
                    n0_d = _rows_as_bf16(n0_ref[h, c, row, :])
                    e0_d = _rows_as_bf16(e0_ref[h, c, row, :])
                    gates[d] = gates[d] + jnp.where(r1 < n0_d, e1 * e0_d, jnp.zeros_like(e1))
            out.append(gates)
        return out

    def write_weights(gates, slot):
        for c in range(n_tiles):
            cols = slice(c * LANES, (c + 1) * LANES)
            for d in range(sub_keys):
                rows = slice(d * PEER_NKEYS, (d + 1) * PEER_NKEYS)
                words = slice(d * PEER_NKEYS // 2, (d + 1) * PEER_NKEYS // 2)
                w_refs[slot][rows, cols] = pltpu.bitcast(a_refs[slot][words, cols], BF16) * gates[c][d]

    def trip(j, carry):
        tile = lambda x, row: x[row:row + SUBLANES, :LANES]
        assert n_tiles == 4
        anchors = []
        for slot in range(2):
            k = 2 * j + slot
            firsts = first_matmul(k, slot)
            previous = pv_prev_ref[slot] if isinstance(j, int) and j == 0 else pv_ref[k - 2]
            second = second_matmul(previous, slot)
            anchors.append([tile(firsts[0], 0), tile(firsts[2], 0), tile(second, 0), tile(second, D_MODEL // 2)])
        for slot in range(2):
            write_weights(gate(2 * j + slot, anchors[slot]), slot)
        return carry

    assert n_sub % 2 == 0
    trip(0, 0)
    lax.fori_loop(1, n_sub // 2, trip, 0)

    @pl.when(e == pl.num_programs(1) - 1)
    def _():
        for slot in range(2):
            second_matmul(pv_ref[n_sub - 2 + slot], slot)
        o_ref[...] = x_ref[...] + mod_ref[0, 5:6, :] * acc_ref[...].T


def _peer(h2t, pu, pv_t, n0, r1, e0, e1, x1, mod, layout):
    t = x1.shape[0]
    tb = PEER_TOKENS
    eb = PEER_IBLOCK * PEER_NKEYS
    n_exp = pu.shape[0]
    n_sub = eb // MXU_DIM
    assert pv_t.shape == (n_exp // MXU_DIM, D_MODEL, MXU_DIM)
    fk = pl.BlockSpec((PEER_HEADS, tb // LANES, PEER_NKEYS, LANES), lambda i, e: (0, i, 0, 0))
    sk = pl.BlockSpec((PEER_HEADS, tb // LANES, PEER_NKEYS // 2, LANES), lambda i, e: (0, i, 0, 0))
    return pl.pallas_call(
        _peer_kernel,
        grid=(t // tb, n_exp // eb),
        in_specs=[pl.BlockSpec((D_MODEL, tb), lambda i, e: (0, i)),
                  pl.BlockSpec((eb, D_MODEL), lambda i, e: (e, 0)),
                  pl.BlockSpec((n_sub, D_MODEL, MXU_DIM), lambda i, e: (e, 0, 0)),
                  pl.BlockSpec((2, D_MODEL, MXU_DIM), lambda i, e: (jnp.maximum(e * (n_sub // 2) - 1, 0), 0, 0)),
                  fk, sk, fk, sk,
                  pl.BlockSpec((tb, D_MODEL), lambda i, e: (i, 0)),
                  _mod_spec(tb, layout)],
        out_specs=pl.BlockSpec((tb, D_MODEL), lambda i, e: (i, 0)),
        out_shape=jax.ShapeDtypeStruct((t, D_MODEL), F32),
        scratch_shapes=[pltpu.VMEM((D_MODEL, tb), F32),
                        pltpu.VMEM((MXU_DIM // 2, tb), jnp.uint32), pltpu.VMEM((MXU_DIM // 2, tb), jnp.uint32),
                        pltpu.VMEM((MXU_DIM, tb), BF16), pltpu.VMEM((MXU_DIM, tb), BF16)],
        compiler_params=pltpu.CompilerParams(dimension_semantics=("arbitrary", "arbitrary"),
                                             vmem_limit_bytes=PEER_VMEM_LIMIT),
        name="peer_dense",
    )(h2t, pu, pv_t, pv_t, n0, r1, e0, e1, x1, mod)


def _merge_cache(kv):
    b, h, l, d = kv.shape
    return kv.transpose(0, 2, 1, 3).reshape(b, l, h * d).astype(BF16)


def kernel(x_prompt, x_sample, cache_na_kv, cache_diff_kv, c, c_ctx, w_mod, b_mod, norm1_g, norm2_g, w_in, conv_w, conv_b, conv_ln_g, conv_ln_b, na_qn_g, na_kn_g, na_rel_bias, diff_qn_g, diff_kn_g, diff_lambda, diff_subln_g, gmlp_ln_g, gmlp_ln_b, gmlp_ws, gmlp_bs, w_out, peer_wq, peer_sub_keys, peer_u, peer_v):
    batch, seq, _ = x_prompt.shape
    dbatch, dseq, _ = x_sample.shape
    t_ctx = batch * seq
    layout = (t_ctx, dseq)
    x = jnp.concatenate([x_prompt.reshape(t_ctx, D_MODEL), x_sample.reshape(dbatch * dseq, D_MODEL)], axis=0)
    cond = jnp.concatenate([c_ctx[None, :], c, jnp.zeros((SUBLANES - 1 - dbatch, D_MODEL), F32)], axis=0)
    mod_all = _modulation(cond, w_mod, b_mod).reshape(DEPTH, SUBLANES, 6, D_MODEL)
    cos, sin = _rope_tables(dseq)
    row = lambda v: v.reshape(1, -1)
    tile = lambda v: jnp.tile(v, GROUP_W // v.shape[0]).reshape(1, GROUP_W)
    na_states, diff_states = [], []
    for l in range(DEPTH):
        lam_init = 0.8 - 0.6 * math.exp(-0.3 * l)
        mod = mod_all[l]
        parts = _in_proj(x, mod, row(norm1_g[l]), w_in[l].astype(BF16), layout)
        conv_o = _conv_module(parts, conv_w[l], row(conv_b[l]), row(conv_ln_g[l]), row(conv_ln_b[l]), t_ctx, dseq)
        gm_bias = jnp.repeat(gmlp_bs[l].T, GROUP_W // N_HEADS, axis=1)
        gm_o = _gmlp(parts, row(gmlp_ln_g[l]), row(gmlp_ln_b[l]), gmlp_ws[l].astype(BF16), gm_bias)
        qn, kn = tile(na_qn_g[l]), tile(na_kn_g[l])
        dqn, dkn, sub = tile(diff_qn_g[l]), tile(diff_kn_g[l]), tile(diff_subln_g[l])
        na_c, df_c, na_kv, df_kv = _ctx_attention(parts, batch, seq, qn, kn, dqn, dkn, sub, diff_lambda[l], lam_init)
        lq, lk, lv, ldq, ldk, ldv = _lat_prep(parts, t_ctx, dbatch, dseq, qn, kn, dqn, dkn, cos, sin)
        na_l = _na_latent(lq, lk, lv, _merge_cache(cache_na_kv[:, l, 0]), _merge_cache(cache_na_kv[:, l, 1]),
                          _na_bias_table(na_rel_bias[l], dseq // GRID_W), dbatch, dseq)
        ldk_t = ldk.reshape(dbatch, dseq, GROUP_W).transpose(0, 2, 1)
        ck_t = _merge_cache(cache_diff_kv[:, l, 0]).transpose(0, 2, 1)
        df_l = _diff_latent(ldq, ldk_t, ldv, ck_t, _merge_cache(cache_diff_kv[:, l, 1]), sub, diff_lambda[l],
                            dbatch, dseq, lam_init)
        x1 = _out_proj(conv_o, na_c, na_l, df_c, df_l, gm_o, x, mod, w_out[l].astype(BF16), layout)
        h2t, n0, r1, e0, e1 = _route(x1, mod, row(norm2_g[l]), peer_wq[l].T.astype(BF16),
                                          peer_sub_keys[l].astype(BF16), layout)
        pv_t = peer_v[l].astype(BF16).reshape(-1, MXU_DIM, D_MODEL).transpose(0, 2, 1)
        x = _peer(h2t, peer_u[l].astype(BF16), pv_t, n0, r1, e0, e1, x1, mod, layout)
        na_states.append(na_kv)
        diff_states.append(df_kv)
    y_prompt = x[:t_ctx].reshape(batch, seq, D_MODEL)
    y_sample = x[t_ctx:].reshape(dbatch, dseq, D_MODEL)
    return (y_prompt, y_sample, jnp.stack(na_states, axis=1), jnp.stack(diff_states, axis=1))
```

```python
import functools
import math

import numpy as np
import jax
import jax.numpy as jnp
from jax import lax
from jax.experimental import pallas as pl
from jax.experimental.pallas import tpu as pltpu

F32 = jnp.float32
BF16 = jnp.bfloat16

D_MODEL = 1024
DEPTH = 2
GRID_W = 64
GROUP_W = 256
HEAD_DIM = 64
N_HEADS = 4
DIFF_SUB = 32
CONV_K = 31
CONV_HALO = 16
NA_WIN_R = 8
NA_WIN_C = 16
ROPE_BASE = 10000.0
CHUNK = 128
PEER_HEADS = 8
PEER_NKEYS = 128
PEER_TOPK = 16
EPS = 1e-6
NEG = -1e30

LANES = 128
SUBLANES = 8
MXU_DIM = 256
VMEM_LIMIT = 48 * 1024 * 1024

SEQ_BLOCK = 256
ROW_BLOCK = 256
ROUTE_BLOCK = 128
PEER_TOKENS = 512
PEER_IBLOCK = 16
PEER_VMEM_LIMIT = 56 * 1024 * 1024
DIFF_QBLOCK = 128
DIFF_KCHUNK = 512


def _params(*sem):
    return pltpu.CompilerParams(dimension_semantics=sem, vmem_limit_bytes=VMEM_LIMIT)


def _dot(a, b):
    return jnp.dot(a, b, preferred_element_type=F32)


def _dot_nt(a, b):
    return lax.dot_general(a, b, (((1,), (1,)), ((), ())), preferred_element_type=F32)


def _split_dot(a, b):
    a1 = a.astype(BF16)
    r1 = a - a1.astype(F32)
    a2 = r1.astype(BF16)
    a3 = (r1 - a2.astype(F32)).astype(BF16)
    return _dot(a1, b) + _dot(a2, b) + _dot(a3, b)


def _group_ones(n, group):
    r = lax.broadcasted_iota(jnp.int32, (n, n), 0) // group
    c = lax.broadcasted_iota(jnp.int32, (n, n), 1) // group
    return (r == c).astype(BF16)


def _group_rms(x, gain, group):
    ssq = _split_dot(x * x, _group_ones(x.shape[-1], group))
    return x * lax.rsqrt(ssq * (1.0 / group) + EPS) * gain


def _gelu_tanh(x):
    k = -2.0 * math.sqrt(2.0 / math.pi) * math.log2(math.e)
    return x / (1.0 + jnp.exp2(x * (k + (k * 0.044715) * (x * x))))


def _layer_norm(y, g, b):
    mu = jnp.mean(y, axis=-1, keepdims=True)
    yc = y - mu
    return yc * lax.rsqrt(jnp.mean(yc * yc, axis=-1, keepdims=True) + EPS) * g + b


def _mod_spec(rows_per_block, layout):
    t_ctx, lat_seq = layout
    assert t_ctx % rows_per_block == 0 and lat_seq % rows_per_block == 0
    def index(i, *_):
        return (jnp.maximum((i * rows_per_block - t_ctx) // lat_seq + 1, 0), 0, 0)
    return pl.BlockSpec((1, 6, D_MODEL), index)


def _mod_kernel(c_ref, w_ref, b_ref, o_ref):
    c = c_ref[...]
    s = c * jax.nn.sigmoid(c)
    o_ref[0] = jnp.dot(s, w_ref[0], preferred_element_type=F32,
                       precision=lax.Precision.HIGHEST) + b_ref[0]


def _modulation(cond, w_mod, b_mod):
    nb = 4
    cols = 6 * D_MODEL // nb
    return pl.pallas_call(
        _mod_kernel,
        grid=(DEPTH, nb),
        in_specs=[pl.BlockSpec((SUBLANES, D_MODEL), lambda l, j: (0, 0)),
                  pl.BlockSpec((1, D_MODEL, cols), lambda l, j: (l, 0, j)),
                  pl.BlockSpec((1, 1, cols), lambda l, j: (l, 0, j))],
        out_specs=pl.BlockSpec((1, SUBLANES, cols), lambda l, j: (l, 0, j)),
        out_shape=jax.ShapeDtypeStruct((DEPTH, SUBLANES, 6 * D_MODEL), F32),
        name="modulation",
        compiler_params=_params("arbitrary", "arbitrary"),
    )(cond, w_mod, b_mod.reshape(DEPTH, 1, 6 * D_MODEL))


def _in_proj_kernel(x_ref, mod_ref, g_ref, w_ref, o_ref):
    x = x_ref[...]
    y = x * lax.rsqrt(jnp.mean(x * x, axis=-1, keepdims=True) + EPS) * g_ref[...]
    h = y * (1.0 + mod_ref[0, 1:2, :]) + mod_ref[0, 0:1, :]
    o_ref[...] = _dot(h.astype(BF16), w_ref[...])


def _in_proj(x, mod, g, w, layout):
    t = x.shape[0]
    n = w.shape[1]
    return pl.pallas_call(
        _in_proj_kernel,
        grid=(t // ROW_BLOCK,),
        in_specs=[pl.BlockSpec((ROW_BLOCK, D_MODEL), lambda i: (i, 0)),
                  _mod_spec(ROW_BLOCK, layout),
                  pl.BlockSpec((1, D_MODEL), lambda i: (0, 0)),
                  pl.BlockSpec((D_MODEL, n), lambda i: (0, 0))],
        out_specs=pl.BlockSpec((ROW_BLOCK, n), lambda i: (i, 0)),
        out_shape=jax.ShapeDtypeStruct((t, n), F32),
        name="in_proj",
        compiler_params=_params("arbitrary"),
    )(x, mod, g, w)


def _conv_kernel(ac, gc, ap, gp, an, gn, w_ref, b_ref, lg_ref, lb_ref, o_ref, pad_ref, *, ctx_blocks, seq_blocks):
    i = pl.program_id(0)
    is_ctx = i < ctx_blocks
    j = i % seq_blocks
    first = jnp.logical_or(is_ctx, j == 0)
    last = jnp.logical_or(is_ctx, j == seq_blocks - 1)
    yp = ap[...] * jax.nn.sigmoid(gp[...])
    yn = an[...] * jax.nn.sigmoid(gn[...])
    pad_ref[0:CONV_HALO, :] = jnp.where(first, 0.0, yp)
    pad_ref[CONV_HALO:CONV_HALO + SEQ_BLOCK, :] = ac[...] * jax.nn.sigmoid(gc[...])
    pad_ref[CONV_HALO + SEQ_BLOCK:, :] = jnp.where(last, 0.0, yn)
    off = CONV_HALO - CONV_K // 2
    acc = jnp.zeros((SEQ_BLOCK, GROUP_W), F32)
    for k in range(CONV_K):
        acc = acc + pad_ref[off + k:off + k + SEQ_BLOCK, :] * w_ref[k:k + 1, :]
    y = _layer_norm(acc + b_ref[...], lg_ref[...], lb_ref[...])
    o_ref[...] = y * jax.nn.sigmoid(y)


def _conv_module(parts, w, b, lg, lb, ctx_tokens, lat_seq):
    t = parts.shape[0]
    nblk = t // SEQ_BLOCK
    hb = SEQ_BLOCK // CONV_HALO
    last_halo = t // CONV_HALO - 1
    vec = pl.BlockSpec((1, GROUP_W), lambda i: (0, 0))
    kern = functools.partial(_conv_kernel, ctx_blocks=ctx_tokens // SEQ_BLOCK, seq_blocks=lat_seq // SEQ_BLOCK)
    return pl.pallas_call(
        kern,
        grid=(nblk,),
        in_specs=[pl.BlockSpec((SEQ_BLOCK, GROUP_W), lambda i: (i, 0)),
                  pl.BlockSpec((SEQ_BLOCK, GROUP_W), lambda i: (i, 1)),
                  pl.BlockSpec((CONV_HALO, GROUP_W), lambda i: (jnp.maximum(i * hb - 1, 0), 0)),
                  pl.BlockSpec((CONV_HALO, GROUP_W), lambda i: (jnp.maximum(i * hb - 1, 0), 1)),
                  pl.BlockSpec((CONV_HALO, GROUP_W), lambda i: (jnp.minimum((i + 1) * hb, last_halo), 0)),
                  pl.BlockSpec((CONV_HALO, GROUP_W), lambda i: (jnp.minimum((i + 1) * hb, last_halo), 1)),
                  pl.BlockSpec((CONV_K, GROUP_W), lambda i: (0, 0)),
                  vec, vec, vec],
        out_specs=pl.BlockSpec((SEQ_BLOCK, GROUP_W), lambda i: (i, 0)),
        out_shape=jax.ShapeDtypeStruct((t, GROUP_W), F32),
        scratch_shapes=[pltpu.VMEM((SEQ_BLOCK + 2 * CONV_HALO, GROUP_W), F32)],
        name="conv_module",
        compiler_params=_params("arbitrary"),
    )(parts, parts, parts, parts, parts, parts, w, b, lg, lb)


def _gmlp_kernel(u_ref, v_ref, lg_ref, lb_ref, ws_ref, bias_ref, o_ref):
    u = jax.nn.gelu(u_ref[...])
    v = _layer_norm(jax.nn.gelu(v_ref[...]), lg_ref[...], lb_ref[...]).astype(BF16)
    lane_g = lax.broadcasted_iota(jnp.int32, (1, GROUP_W), 1) // (GROUP_W // N_HEADS)
    for ch in range(SEQ_BLOCK // CHUNK):
        rows = slice(ch * CHUNK, (ch + 1) * CHUNK)
        s = bias_ref[...]
        for g in range(N_HEADS):
            s = s + jnp.where(lane_g == g, _dot(ws_ref[g], v[rows]), 0.0)
        o_ref[rows, :] = u[rows] * s


def _gmlp(parts, lg, lb, ws, bias):
    t = parts.shape[0]
    vec = pl.BlockSpec((1, GROUP_W), lambda i: (0, 0))
    return pl.pallas_call(
        _gmlp_kernel,
        grid=(t // SEQ_BLOCK,),
        in_specs=[pl.BlockSpec((SEQ_BLOCK, GROUP_W), lambda i: (i, 8)),
                  pl.BlockSpec((SEQ_BLOCK, GROUP_W), lambda i: (i, 9)),
                  vec, vec,
                  pl.BlockSpec((N_HEADS, CHUNK, CHUNK), lambda i: (0, 0, 0)),
                  pl.BlockSpec((CHUNK, GROUP_W), lambda i: (0, 0))],
        out_specs=pl.BlockSpec((SEQ_BLOCK, GROUP_W), lambda i: (i, 0)),
        out_shape=jax.ShapeDtypeStruct((t, GROUP_W), F32),
        compiler_params=_params("arbitrary"),
        name="gmlp",
    )(parts, parts, lg, lb, ws, bias)


def _lane_group_id(width, group):
    return lax.broadcasted_iota(jnp.int32, (1, width), 1) // group


def _stack_masked(q, group, ids):
    lane = _lane_group_id(q.shape[-1], group)
    return jnp.concatenate([jnp.where(lane == g, q, jnp.zeros_like(q)) for g in ids], axis=0)


def _diff_lambda(dl_ref, lam_init):
    dl = dl_ref[...]
    a = jnp.sum(dl[0:1, :] * dl[1:2, :], axis=-1, keepdims=True)
    b = jnp.sum(dl[2:3, :] * dl[3:4, :], axis=-1, keepdims=True)
    return jnp.exp(a) - jnp.exp(b) + lam_init


def _ctx_attn_kernel(q_ref, k_ref, v_ref, dq_ref, dk_ref, dv_ref, qn_ref, kn_ref, dqn_ref, dkn_ref, sub_ref,
                     dl_ref, na_ref, df_ref, na_kv_ref, df_kv_ref, *, lam_init):
    n = q_ref.shape[0]
    lane_h = _lane_group_id(GROUP_W, HEAD_DIM)

    def write_state(ref, keys, values):
        for h in range(N_HEADS):
            ref[0, 0, h] = keys[:, h * HEAD_DIM:(h + 1) * HEAD_DIM]
            ref[0, 1, h] = values[:, h * HEAD_DIM:(h + 1) * HEAD_DIM]

    qn = _group_rms(q_ref[...], qn_ref[...], HEAD_DIM)
    kn = _group_rms(k_ref[...], kn_ref[...], HEAD_DIM)
    v = v_ref[...]
    write_state(na_kv_ref, kn, v)
    qs = _stack_masked(qn.astype(BF16), HEAD_DIM, range(N_HEADS))
    s = _dot_nt(qs, kn.astype(BF16)) * (HEAD_DIM ** -0.5)
    e = jnp.exp(s - jnp.max(s, axis=-1, keepdims=True))
    r = _dot(e.astype(BF16), v.astype(BF16)) / jnp.sum(e, axis=-1, keepdims=True)
    o = jnp.zeros((n, GROUP_W), F32)
    for h in range(N_HEADS):
        o = o + jnp.where(lane_h == h, r[h * n:(h + 1) * n], 0.0)
    na_ref[...] = o
    lam = _diff_lambda(dl_ref, lam_init)
    dq = _group_rms(dq_ref[...], dqn_ref[...], DIFF_SUB)
    dk = _group_rms(dk_ref[...], dkn_ref[...], DIFF_SUB)
    dv = dv_ref[...]
    write_state(df_kv_ref, dk, dv)
    dkb = dk.astype(BF16)
    dvb = dv.astype(BF16)
    o = jnp.zeros((n, GROUP_W), F32)
    for h in range(N_HEADS):
        qs = _stack_masked(dq.astype(BF16), DIFF_SUB, (2 * h, 2 * h + 1))
        s = _dot_nt(qs, dkb) * (DIFF_SUB ** -0.5)
        e = jnp.exp(s - jnp.max(s, axis=-1, keepdims=True))
        inv = 1.0 / jnp.sum(e, axis=-1, keepdims=True)
        a = e[:n] * inv[:n] - e[n:] * (lam * inv[n:])
        o = o + jnp.where(lane_h == h, _dot(a.astype(BF16), dvb), 0.0)
    df_ref[...] = _group_rms(o, sub_ref[...], HEAD_DIM) * (1.0 - lam_init)


def _ctx_attention(parts, batch, seq, qn, kn, dqn, dkn, sub, dl, lam_init):
    t = batch * seq
    vec = pl.BlockSpec((1, GROUP_W), lambda b: (0, 0))
    col = lambda c: pl.BlockSpec((seq, GROUP_W), lambda b, c=c: (b, c))
    out = jax.ShapeDtypeStruct((t, GROUP_W), F32)
    ob = pl.BlockSpec((seq, GROUP_W), lambda b: (b, 0))
    state = jax.ShapeDtypeStruct((batch, 2, N_HEADS, seq, HEAD_DIM), F32)
    sb = pl.BlockSpec((1, 2, N_HEADS, seq, HEAD_DIM), lambda b: (b, 0, 0, 0, 0))
    return pl.pallas_call(
        functools.partial(_ctx_attn_kernel, lam_init=lam_init),
        grid=(batch,),
        in_specs=[col(2), col(3), col(4), col(5), col(6), col(7), vec, vec, vec, vec, vec,
                  pl.BlockSpec((4, DIFF_SUB), lambda b: (0, 0))],
        out_specs=[ob, ob, sb, sb],
        out_shape=[out, out, state, state],
        compiler_params=_params("arbitrary"),
        name="ctx_attention",
    )(parts, parts, parts, parts, parts, parts, qn, kn, dqn, dkn, sub, dl)


def _lat_prep_kernel(q_ref, k_ref, v_ref, dq_ref, dk_ref, dv_ref, qn_ref, kn_ref, dqn_ref, dkn_ref,
                     cos_ref, sin_ref, oq, ok, ov, odq, odk, odv):
    oq[...] = _group_rms(q_ref[...], qn_ref[...], HEAD_DIM).astype(BF16)
    ok[...] = _group_rms(k_ref[...], kn_ref[...], HEAD_DIM).astype(BF16)
    ov[...] = v_ref[...].astype(BF16)
    odv[...] = dv_ref[...].astype(BF16)
    half = DIFF_SUB // 4
    lane = lax.broadcasted_iota(jnp.int32, (1, GROUP_W), 1)
    lower = (lane % (2 * half)) < half
    cos = cos_ref[...]
    sin = sin_ref[...]

    def rope(x):
        partner = jnp.where(lower, pltpu.roll(x, GROUP_W - half, 1), pltpu.roll(x, half, 1))
        return x * cos + partner * sin

    odq[...] = rope(_group_rms(dq_ref[...], dqn_ref[...], DIFF_SUB)).astype(BF16)
    odk[...] = rope(_group_rms(dk_ref[...], dkn_ref[...], DIFF_SUB)).astype(BF16)


def _rope_tables(seq):
    half = DIFF_SUB // 4
    t = jnp.arange(seq)
    rows = (t // GRID_W).astype(F32)
    cols = (t % GRID_W).astype(F32)
    inv = ROPE_BASE ** (-jnp.arange(half, dtype=F32) / half)
    ang_r = rows[:, None] * inv[None, :]
    ang_c = cols[:, None] * inv[None, :]
    cos32 = jnp.concatenate([jnp.cos(ang_r), jnp.cos(ang_r), jnp.cos(ang_c), jnp.cos(ang_c)], axis=-1)
    sin32 = jnp.concatenate([-jnp.sin(ang_r), jnp.sin(ang_r), -jnp.sin(ang_c), jnp.sin(ang_c)], axis=-1)
    reps = GROUP_W // DIFF_SUB
    return jnp.tile(cos32, (1, reps)), jnp.tile(sin32, (1, reps))


def _lat_prep(parts, row0, batch, seq, qn, kn, dqn, dkn, cos, sin):
    t = batch * seq
    nb = seq // SEQ_BLOCK
    r0 = row0 // SEQ_BLOCK
    vec = pl.BlockSpec((1, GROUP_W), lambda i: (0, 0))
    col = lambda c: pl.BlockSpec((SEQ_BLOCK, GROUP_W), lambda i, c=c: (r0 + i, c))
    tab = pl.BlockSpec((SEQ_BLOCK, GROUP_W), lambda i: (i % nb, 0))
    out = jax.ShapeDtypeStruct((t, GROUP_W), BF16)
    ob = pl.BlockSpec((SEQ_BLOCK, GROUP_W), lambda i: (i, 0))
    return pl.pallas_call(
        _lat_prep_kernel,
        grid=(t // SEQ_BLOCK,),
        in_specs=[col(2), col(3), col(4), col(5), col(6), col(7), vec, vec, vec, vec, tab, tab],
        out_specs=[ob] * 6,
        out_shape=[out] * 6,
        compiler_params=_params("arbitrary"),
        name="lat_prep",
    )(parts, parts, parts, parts, parts, parts, qn, kn, dqn, dkn, cos, sin)


def _na_row_start(r, rows):
    return jnp.clip(r - NA_WIN_R // 2, 0, rows - NA_WIN_R)


def _na_lat_kernel(q_ref, k_ref, v_ref, ck_ref, cv_ref, bt_ref, o_ref, *, rows):
    r = pl.program_id(1)
    start = pl.multiple_of(_na_row_start(r, rows) * GRID_W, GRID_W)
    n_loc = NA_WIN_R * GRID_W
    kl = k_ref[pl.ds(start, n_loc), :]
    vl = v_ref[pl.ds(start, n_loc), :]
    qs = _stack_masked(q_ref[...], HEAD_DIM, range(N_HEADS))
    scale = HEAD_DIM ** -0.5
    s_loc = _dot_nt(qs, kl) * scale + bt_ref[0]
    s_ctx = _dot_nt(qs, ck_ref[0]) * scale
    m = jnp.maximum(jnp.max(s_loc, axis=-1, keepdims=True), jnp.max(s_ctx, axis=-1, keepdims=True))
    e_loc = jnp.exp(s_loc - m)
    e_ctx = jnp.exp(s_ctx - m)
    l = jnp.sum(e_loc, axis=-1, keepdims=True) + jnp.sum(e_ctx, axis=-1, keepdims=True)
    res = (_dot(e_loc.astype(BF16), vl) + _dot(e_ctx.astype(BF16), cv_ref[0])) / l
    lane_h = _lane_group_id(GROUP_W, HEAD_DIM)
    o = jnp.zeros((GRID_W, GROUP_W), F32)
    for h in range(N_HEADS):
        o = o + jnp.where(lane_h == h, res[h * GRID_W:(h + 1) * GRID_W], 0.0)
    o_ref[...] = o


def _na_bias_table(rel_bias, rows):
    wr = NA_WIN_R
    c = np.arange(GRID_W)
    c0 = np.clip(c - NA_WIN_C // 2, 0, GRID_W - NA_WIN_C)
    in_win = (c[None, :] >= c0[:, None]) & (c[None, :] < c0[:, None] + NA_WIN_C)
    dc_idx = np.clip(c[None, :] - c[:, None] + NA_WIN_C - 1, 0, 2 * NA_WIN_C - 2)
    off = np.arange(wr)
    dr_idx = (np.arange(wr)[None, :] - off[:, None]) + NA_WIN_R - 1
    rb = rel_bias[:, dr_idx]
    onehot = jnp.asarray(dc_idx[..., None] == np.arange(2 * NA_WIN_C - 1), F32)
    bias = jnp.einsum('howd,qkd->howqk', rb, onehot, precision=lax.Precision.HIGHEST)
    bias = jnp.where(jnp.asarray(in_win)[None, None, None], bias, NEG)
    bias = bias.transpose(1, 0, 3, 2, 4)
    return bias.reshape(wr, N_HEADS * GRID_W, wr * GRID_W)


def _na_latent(qn, kn, v, ck, cv, bias_tab, batch, seq):
    rows = seq // GRID_W
    nctx = ck.shape[1]
    full = pl.BlockSpec((seq, GROUP_W), lambda b, r: (b, 0))
    ctx = pl.BlockSpec((1, nctx, GROUP_W), lambda b, r: (b, 0, 0))
    return pl.pallas_call(
        functools.partial(_na_lat_kernel, rows=rows),
        grid=(batch, rows),
        in_specs=[pl.BlockSpec((GRID_W, GROUP_W), lambda b, r: (b * rows + r, 0)),
                  full, full, ctx, ctx,
                  pl.BlockSpec((1, N_HEADS * GRID_W, NA_WIN_R * GRID_W),
                               lambda b, r: (r - _na_row_start(r, rows), 0, 0))],
        out_specs=pl.BlockSpec((GRID_W, GROUP_W), lambda b, r: (b * rows + r, 0)),
        out_shape=jax.ShapeDtypeStruct((batch * seq, GROUP_W), F32),
        compiler_params=_params("arbitrary", "arbitrary"),
        name="na_latent",
    )(qn, kn, v, ck, cv, bias_tab)


def _lane_fold(fn, acc, x):
    for g in range(x.shape[-1] // LANES):
        acc = fn(acc, x[:, g * LANES:(g + 1) * LANES])
    return acc


def _diff_lat_kernel(q_ref, k_ref, v_ref, ck_ref, cv_ref, sub_ref, dl_ref, o_ref, s_ref, sc_ref, *, lam_init):
    n = q_ref.shape[0]
    n_chunks, rows, chunk = s_ref.shape
    n_sub = rows // n
    lam = _diff_lambda(dl_ref, lam_init)
    qs = _stack_masked(q_ref[...], DIFF_SUB, range(n_sub))
    c2 = (DIFF_SUB ** -0.5) * math.log2(math.e)
    step = pl.program_id(1)
    top = jnp.full((rows, LANES), NEG, F32)
    for j in range(n_chunks):
        s = _dot(qs, k_ref[0, :, j * chunk:(j + 1) * chunk])
        s_ref[(j + step) % n_chunks] = s
        top = _lane_fold(jnp.maximum, top, s)
    s = _dot(qs, ck_ref[0])
    sc_ref[...] = s
    top = _lane_fold(jnp.maximum, top, s)
    m = jnp.max(top, axis=-1, keepdims=True) * c2
    part = jnp.zeros((rows, LANES), F32)
    half = rows // 2
    res = [jnp.zeros((half, GROUP_W), F32)] * 2
    products = []
    for j in range(n_chunks + 1):
        scores = s_ref[(j + step) % n_chunks] if j < n_chunks else sc_ref[...]
        values = v_ref[j * chunk:(j + 1) * chunk, :] if j < n_chunks else cv_ref[0]
        shift = m
        if j >= 2:
            bits = pltpu.bitcast(products[j - 2][:SUBLANES, :LANES], jnp.uint32)
            shift = m + pltpu.bitcast((bits >> 16) >> 16, F32)[:1, :1]
        e = jnp.exp2(scores * c2 - shift)
        part = _lane_fold(jnp.add, part, e)
        eb = e.astype(BF16)
        prods = [_dot(eb[r * half:(r + 1) * half], values) for r in range(2)]
        products.append(prods[0])
        res = [res[r] + prods[r] for r in range(2)]
    res = jnp.concatenate(res, axis=0) * (1.0 / jnp.sum(part, axis=-1, keepdims=True))
    lane_h = _lane_group_id(GROUP_W, HEAD_DIM)
    o = jnp.zeros((n, GROUP_W), F32)
    for h in range(N_HEADS):
        first, second = res[2 * h * n:(2 * h + 1) * n], res[(2 * h + 1) * n:(2 * h + 2) * n]
        o = o + jnp.where(lane_h == h, first - lam * second, 0.0)
    o_ref[...] = _group_rms(o, sub_ref[...], HEAD_DIM) * (1.0 - lam_init)


def _diff_latent(dq, dk_t, dv, ck_t, cv, sub, dl, batch, seq, lam_init):
    nq = seq // DIFF_QBLOCK
    nctx = cv.shape[1]
    full = pl.BlockSpec((seq, GROUP_W), lambda b, i: (b, 0))
    ctx = pl.BlockSpec((1, nctx, GROUP_W), lambda b, i: (b, 0, 0))
    return pl.pallas_call(
        functools.partial(_diff_lat_kernel, lam_init=lam_init),
        grid=(batch, nq),
        in_specs=[pl.BlockSpec((DIFF_QBLOCK, GROUP_W), lambda b, i: (b * nq + i, 0)),
                  pl.BlockSpec((1, GROUP_W, seq), lambda b, i: (b, 0, 0)), full,
                  pl.BlockSpec((1, GROUP_W, nctx), lambda b, i: (b, 0, 0)), ctx,
                  pl.BlockSpec((1, GROUP_W), lambda b, i: (0, 0)),
                  pl.BlockSpec((4, DIFF_SUB), lambda b, i: (0, 0))],
        out_specs=pl.BlockSpec((DIFF_QBLOCK, GROUP_W), lambda b, i: (b * nq + i, 0)),
        out_shape=jax.ShapeDtypeStruct((batch * seq, GROUP_W), F32),
        scratch_shapes=[pltpu.VMEM((seq // DIFF_KCHUNK, 2 * N_HEADS * DIFF_QBLOCK, DIFF_KCHUNK), F32),
                        pltpu.VMEM((2 * N_HEADS * DIFF_QBLOCK, nctx), F32)],
        compiler_params=_params("arbitrary", "arbitrary"),
        name="diff_latent",
    )(dq, dk_t, dv, ck_t, cv, sub, dl)


def _out_proj_kernel(conv_ref, na_c_ref, na_l_ref, df_c_ref, df_l_ref, gm_ref, x_ref, mod_ref, w_ref, o_ref, *,
                     ctx_blocks):
    is_ctx = pl.program_id(0) < ctx_blocks
    na = jnp.where(is_ctx, na_c_ref[...], na_l_ref[...])
    df = jnp.where(is_ctx, df_c_ref[...], df_l_ref[...])
    mixed = jnp.zeros(o_ref.shape, F32)
    for g, part in enumerate((conv_ref[...], na, df, gm_ref[...])):
        mixed = mixed + _dot(part.astype(BF16), w_ref[g * GROUP_W:(g + 1) * GROUP_W, :])
    o_ref[...] = x_ref[...] + mod_ref[0, 2:3, :] * mixed


def _out_proj(conv_o, na_c, na_l, df_c, df_l, gm_o, x, mod, w, layout):
    t = x.shape[0]
    ctx_blocks = layout[0] // ROW_BLOCK
    last_lat = (t - layout[0]) // ROW_BLOCK - 1
    part = pl.BlockSpec((ROW_BLOCK, GROUP_W), lambda i: (i, 0))
    ctx = pl.BlockSpec((ROW_BLOCK, GROUP_W), lambda i: (jnp.minimum(i, ctx_blocks - 1), 0))
    lat = pl.BlockSpec((ROW_BLOCK, GROUP_W), lambda i: (jnp.clip(i - ctx_blocks, 0, last_lat), 0))
    row = pl.BlockSpec((ROW_BLOCK, D_MODEL), lambda i: (i, 0))
    return pl.pallas_call(
        functools.partial(_out_proj_kernel, ctx_blocks=ctx_blocks),
        grid=(t // ROW_BLOCK,),
        in_specs=[part, ctx, lat, ctx, lat, part, row,
                  _mod_spec(ROW_BLOCK, layout),
                  pl.BlockSpec((D_MODEL, D_MODEL), lambda i: (0, 0))],
        out_specs=row,
        out_shape=jax.ShapeDtypeStruct((t, D_MODEL), F32),
        compiler_params=_params("arbitrary"),
        name="out_proj",
    )(conv_o, na_c, na_l, df_c, df_l, gm_o, x, mod, w)


def _sort_desc(vals):
    v = list(vals)
    n = len(v)
    k = 2
    while k <= n:
        j = k // 2
        while j >= 1:
            for i in range(n):
                l = i ^ j
                if l > i:
                    hi, lo = jnp.maximum(v[i], v[l]), jnp.minimum(v[i], v[l])
                    v[i], v[l] = (hi, lo) if (i & k) == 0 else (lo, hi)
            j //= 2
        k *= 2
    return v


def _merge_desc(v):
    v = list(v)
    n = len(v)
    j = n // 2
    while j >= 1:
        for i in range(n):
            l = i ^ j
            if l > i:
                v[i], v[l] = jnp.maximum(v[i], v[l]), jnp.minimum(v[i], v[l])
        j //= 2
    return v


def _top_half(a, b):
    n = len(a)
    return [jnp.maximum(a[i], b[n - 1 - i]) for i in range(n)]


def _top16_keys(s):
    k = PEER_TOPK
    v = _sort_desc([s[a * SUBLANES:(a + 1) * SUBLANES] for a in range(PEER_NKEYS // SUBLANES)])
    shift = SUBLANES // 2
    while shift >= 1:
        rolled = [pltpu.roll(x, shift, 0) for x in v]
        v = _merge_desc(_top_half(v, rolled))
        shift //= 2
    return v[:k]


def _paired_bf16(x):
    hi = pltpu.bitcast(x.astype(BF16).astype(F32), jnp.uint32)
    return hi | (hi >> 16)


def _route_kernel(x_ref, mod_ref, g_ref, wq_ref, sk_ref, h2_ref, n0_ref, r1_ref, e0_ref, e1_ref, s0_ref, s1_ref):
    x = x_ref[...]
    y = x * lax.rsqrt(jnp.mean(x * x, axis=-1, keepdims=True) + EPS) * g_ref[...]
    h2 = y * (1.0 + mod_ref[0, 4:5, :]) + mod_ref[0, 3:4, :]
    h2t = h2.T.astype(BF16)
    h2_ref[...] = h2t
    qt = _dot(wq_ref[...], h2t)
    tb = x.shape[0]
    sub = lax.broadcasted_iota(jnp.int32, (SUBLANES, tb), 0)
    k = PEER_TOPK
    tops = [[jnp.zeros((SUBLANES, tb), F32)] * k, [jnp.zeros((SUBLANES, tb), F32)] * k]
    for h in range(PEER_HEADS):
        for p in range(2):
            base = (2 * h + p) * PEER_NKEYS
            s = _dot(sk_ref[p], qt[base:base + PEER_NKEYS].astype(BF16))
            (s0_ref if p == 0 else s1_ref)[h] = s
            top = _top16_keys(s)
            tops[p] = [jnp.where(sub == h, top[a], tops[p][a]) for a in range(k)]
    cand = [tops[0][a] + tops[1][b] for a in range(k) for b in range(k) if (a + 1) * (b + 1) <= k]
    pad = [jnp.full((SUBLANES, tb), NEG, F32)] * (4 * k - len(cand))
    groups = [_sort_desc((cand + pad)[g * k:(g + 1) * k]) for g in range(4)]
    best = _top_half(_merge_desc(_top_half(groups[0], groups[1])), _merge_desc(_top_half(groups[2], groups[3])))
    thr = functools.reduce(jnp.minimum, best)
    m = tops[0][0] + tops[1][0]
    z = functools.reduce(jnp.add, [jnp.where(c >= thr, jnp.exp(c - m), 0.0) for c in cand])
    zinv = 1.0 / z
    counts, first = [], 0
    for a in range(k):
        n_b = k // (a + 1)
        counts.append(functools.reduce(jnp.add, [jnp.where(c >= thr, 1.0, 0.0) for c in cand[first:first + n_b]]))
        first += n_b
    for h in range(PEER_HEADS):
        s0 = s0_ref[h]
        s1 = s1_ref[h]
        n0 = jnp.zeros_like(s0)
        r1 = jnp.full_like(s1, float(k))
        for a in reversed(range(k)):
            n0 = jnp.where(s0 >= tops[0][a][h:h + 1], counts[a][h:h + 1], n0)
            r1 = jnp.where(s1 >= tops[1][a][h:h + 1], float(a), r1)
        n0_ref[h, 0] = _paired_bf16(n0)
        r1_ref[h, 0] = pltpu.bitcast(r1.astype(BF16), jnp.uint32)
        e0_ref[h, 0] = _paired_bf16(jnp.exp(s0 - tops[0][0][h:h + 1]) * zinv[h:h + 1])
        e1_ref[h, 0] = pltpu.bitcast(jnp.exp(s1 - tops[1][0][h:h + 1]).astype(BF16), jnp.uint32)


def _route(x1, mod, g, wq_t, sk, layout):
    t = x1.shape[0]
    tb = ROUTE_BLOCK
    assert tb == LANES
    tiles = t // LANES
    first_key = jax.ShapeDtypeStruct((PEER_HEADS, tiles, PEER_NKEYS, LANES), jnp.uint32)
    second_key = jax.ShapeDtypeStruct((PEER_HEADS, tiles, PEER_NKEYS // 2, LANES), jnp.uint32)
    fkb = pl.BlockSpec((PEER_HEADS, 1, PEER_NKEYS, LANES), lambda i: (0, i, 0, 0))
    skb = pl.BlockSpec((PEER_HEADS, 1, PEER_NKEYS // 2, LANES), lambda i: (0, i, 0, 0))
    scores = pltpu.VMEM((PEER_HEADS, PEER_NKEYS, tb), F32)
    return pl.pallas_call(
        _route_kernel,
        grid=(t // tb,),
        in_specs=[pl.BlockSpec((tb, D_MODEL), lambda i: (i, 0)),
                  _mod_spec(tb, layout),
                  pl.BlockSpec((1, D_MODEL), lambda i: (0, 0)),
                  pl.BlockSpec(wq_t.shape, lambda i: (0, 0)),
                  pl.BlockSpec(sk.shape, lambda i: (0, 0, 0))],
        out_specs=[pl.BlockSpec((D_MODEL, tb), lambda i: (0, i)), fkb, skb, fkb, skb],
        out_shape=[jax.ShapeDtypeStruct((D_MODEL, t), BF16), first_key, second_key, first_key, second_key],
        scratch_shapes=[scores, scores],
        compiler_params=_params("arbitrary"),
        name="peer_route",
    )(x1, mod, g, wq_t, sk)


def _rows_as_bf16(rows):
    tile = pltpu.bitcast(rows, BF16)
    return jnp.concatenate([tile] * (PEER_NKEYS // tile.shape[0]), axis=0)


def _peer_kernel(h2_ref, pu_ref, pv_ref, pv_prev_ref, n0_ref, r1_ref, e0_ref, e1_ref, x_ref, mod_ref, o_ref,
                 acc_ref, a0_ref, a1_ref, w0_ref, w1_ref):
    e = pl.program_id(1)
    a_refs = (a0_ref, a1_ref)
    w_refs = (w0_ref, w1_ref)

    @pl.when(e == 0)
    def _():
        acc_ref[...] = jnp.zeros_like(acc_ref)
        for w_ref in w_refs:
            w_ref[...] = jnp.zeros_like(w_ref)

    tb = h2_ref.shape[1]
    n_sub = pv_ref.shape[0]
    sub_keys = MXU_DIM // PEER_NKEYS
    n_tiles = tb // LANES

    def first_matmul(k, slot):
        start = k * MXU_DIM if isinstance(k, int) else pl.multiple_of(k * MXU_DIM, MXU_DIM)
        parts = [_dot(pu_ref[pl.ds(start, MXU_DIM), j * MXU_DIM:(j + 1) * MXU_DIM],
                      h2_ref[j * MXU_DIM:(j + 1) * MXU_DIM, :]) for j in range(D_MODEL // MXU_DIM)]
        a = functools.reduce(jnp.add, parts)
        a_refs[slot][...] = pltpu.bitcast(_gelu_tanh(a).astype(BF16), jnp.uint32)
        return parts

    def second_matmul(values, slot):
        part = _dot(values, w_refs[slot][...])
        acc_ref[...] += part
        return part

    def gate(k, anchors):
        key = e * PEER_IBLOCK + sub_keys * k
        out = []
        for c in range(n_tiles):
            cols = slice(c * LANES, (c + 1) * LANES)
            words = pltpu.bitcast(anchors[c], jnp.uint32)
            gates = [_rows_as_bf16((words >> 16) >> 16)] * sub_keys
            for h in range(PEER_HEADS):
                r1 = pltpu.bitcast(r1_ref[h, c], BF16)
                e1 = pltpu.bitcast(e1_ref[h, c], BF16)
                for d in range(sub_keys):
                    row = pl.ds(key + d, SUBLANES, stride=0)
                    n0_d = _rows_as_bf16(n0_ref[h, c, row, :])
                    e0_d = _rows_as_bf16(e0_ref[h, c, row, :])
                    gates[d] = gates[d] + jnp.where(r1 < n0_d, e1 * e0_d, jnp.zeros_like(e1))
            out.append(gates)
        return out

    def write_weights(gates, slot):
        for c in range(n_tiles):
            cols = slice(c * LANES, (c + 1) * LANES)
            for d in range(sub_keys):
                rows = slice(d * PEER_NKEYS, (d + 1) * PEER_NKEYS)
                words = slice(d * PEER_NKEYS // 2, (d + 1) * PEER_NKEYS // 2)
                w_refs[slot][rows, cols] = pltpu.bitcast(a_refs[slot][words, cols], BF16) * gates[c][d]

    def trip(j, carry):
        tile = lambda x, row: x[row:row + SUBLANES, :LANES]
        assert n_tiles == 4
        anchors = []
        for slot in range(2):
            k = 2 * j + slot
            firsts = first_matmul(k, slot)
            previous = pv_prev_ref[slot] if isinstance(j, int) and j == 0 else pv_ref[k - 2]
            second = second_matmul(previous, slot)
            anchors.append([tile(firsts[0], 0), tile(firsts[2], 0), tile(second, 0), tile(second, D_MODEL // 2)])
        for slot in range(2):
            write_weights(gate(2 * j + slot, anchors[slot]), slot)
        return carry

    assert n_sub % 2 == 0
    trip(0, 0)
    lax.fori_loop(1, n_sub // 2, trip, 0)

    @pl.when(e == pl.num_programs(1) - 1)
    def _():
        for slot in range(2):
            second_matmul(pv_ref[n_sub - 2 + slot], slot)
        o_ref[...] = x_ref[...] + mod_ref[0, 5:6, :] * acc_ref[...].T


def _peer(h2t, pu, pv_t, n0, r1, e0, e1, x1, mod, layout):
    t = x1.shape[0]
    tb = PEER_TOKENS
    eb = PEER_IBLOCK * PEER_NKEYS
    n_exp = pu.shape[0]
    n_sub = eb // MXU_DIM
    assert pv_t.shape == (n_exp // MXU_DIM, D_MODEL, MXU_DIM)
    fk = pl.BlockSpec((PEER_HEADS, tb // LANES, PEER_NKEYS, LANES), lambda i, e: (0, i, 0, 0))
    sk = pl.BlockSpec((PEER_HEADS, tb // LANES, PEER_NKEYS // 2, LANES), lambda i, e: (0, i, 0, 0))
    return pl.pallas_call(
        _peer_kernel,
        grid=(t // tb, n_exp // eb),
        in_specs=[pl.BlockSpec((D_MODEL, tb), lambda i, e: (0, i)),
                  pl.BlockSpec((eb, D_MODEL), lambda i, e: (e, 0)),
                  pl.BlockSpec((n_sub, D_MODEL, MXU_DIM), lambda i, e: (e, 0, 0)),
                  pl.BlockSpec((2, D_MODEL, MXU_DIM), lambda i, e: (jnp.maximum(e * (n_sub // 2) - 1, 0), 0, 0)),
                  fk, sk, fk, sk,
                  pl.BlockSpec((tb, D_MODEL), lambda i, e: (i, 0)),
                  _mod_spec(tb, layout)],
        out_specs=pl.BlockSpec((tb, D_MODEL), lambda i, e: (i, 0)),
        out_shape=jax.ShapeDtypeStruct((t, D_MODEL), F32),
        scratch_shapes=[pltpu.VMEM((D_MODEL, tb), F32),
                        pltpu.VMEM((MXU_DIM // 2, tb), jnp.uint32), pltpu.VMEM((MXU_DIM // 2, tb), jnp.uint32),
                        pltpu.VMEM((MXU_DIM, tb), BF16), pltpu.VMEM((MXU_DIM, tb), BF16)],
        compiler_params=pltpu.CompilerParams(dimension_semantics=("arbitrary", "arbitrary"),
                                             vmem_limit_bytes=PEER_VMEM_LIMIT),
        name="peer_dense",
    )(h2t, pu, pv_t, pv_t, n0, r1, e0, e1, x1, mod)


def _merge_cache(kv):
    b, h, l, d = kv.shape
    return kv.transpose(0, 2, 1, 3).reshape(b, l, h * d).astype(BF16)


def kernel(x_prompt, x_sample, cache_na_kv, cache_diff_kv, c, c_ctx, w_mod, b_mod, norm1_g, norm2_g, w_in, conv_w, conv_b, conv_ln_g, conv_ln_b, na_qn_g, na_kn_g, na_rel_bias, diff_qn_g, diff_kn_g, diff_lambda, diff_subln_g, gmlp_ln_g, gmlp_ln_b, gmlp_ws, gmlp_bs, w_out, peer_wq, peer_sub_keys, peer_u, peer_v):
    batch, seq, _ = x_prompt.shape
    dbatch, dseq, _ = x_sample.shape
    t_ctx = batch * seq
    layout = (t_ctx, dseq)
    x = jnp.concatenate([x_prompt.reshape(t_ctx, D_MODEL), x_sample.reshape(dbatch * dseq, D_MODEL)], axis=0)
    cond = jnp.concatenate([c_ctx[None, :], c, jnp.zeros((SUBLANES - 1 - dbatch, D_MODEL), F32)], axis=0)
    mod_all = _modulation(cond, w_mod, b_mod).reshape(DEPTH, SUBLANES, 6, D_MODEL)
    cos, sin = _rope_tables(dseq)
    row = lambda v: v.reshape(1, -1)
    tile = lambda v: jnp.tile(v, GROUP_W // v.shape[0]).reshape(1, GROUP_W)
    na_states, diff_states = [], []
    for l in range(DEPTH):
        lam_init = 0.8 - 0.6 * math.exp(-0.3 * l)
        mod = mod_all[l]
        parts = _in_proj(x, mod, row(norm1_g[l]), w_in[l].astype(BF16), layout)
        conv_o = _conv_module(parts, conv_w[l], row(conv_b[l]), row(conv_ln_g[l]), row(conv_ln_b[l]), t_ctx, dseq)
        gm_bias = jnp.repeat(gmlp_bs[l].T, GROUP_W // N_HEADS, axis=1)
        gm_o = _gmlp(parts, row(gmlp_ln_g[l]), row(gmlp_ln_b[l]), gmlp_ws[l].astype(BF16), gm_bias)
        qn, kn = tile(na_qn_g[l]), tile(na_kn_g[l])
        dqn, dkn, sub = tile(diff_qn_g[l]), tile(diff_kn_g[l]), tile(diff_subln_g[l])
        na_c, df_c, na_kv, df_kv = _ctx_attention(parts, batch, seq, qn, kn, dqn, dkn, sub, diff_lambda[l], lam_init)
        lq, lk, lv, ldq, ldk, ldv = _lat_prep(parts, t_ctx, dbatch, dseq, qn, kn, dqn, dkn, cos, sin)
        na_l = _na_latent(lq, lk, lv, _merge_cache(cache_na_kv[:, l, 0]), _merge_cache(cache_na_kv[:, l, 1]),
                          _na_bias_table(na_rel_bias[l], dseq // GRID_W), dbatch, dseq)
        ldk_t = ldk.reshape(dbatch, dseq, GROUP_W).transpose(0, 2, 1)
        ck_t = _merge_cache(cache_diff_kv[:, l, 0]).transpose(0, 2, 1)
        df_l = _diff_latent(ldq, ldk_t, ldv, ck_t, _merge_cache(cache_diff_kv[:, l, 1]), sub, diff_lambda[l],
                            dbatch, dseq, lam_init)
        x1 = _out_proj(conv_o, na_c, na_l, df_c, df_l, gm_o, x, mod, w_out[l].astype(BF16), layout)
        h2t, n0, r1, e0, e1 = _route(x1, mod, row(norm2_g[l]), peer_wq[l].T.astype(BF16),
                                          peer_sub_keys[l].astype(BF16), layout)
        pv_t = peer_v[l].astype(BF16).reshape(-1, MXU_DIM, D_MODEL).transpose(0, 2, 1)
        x = _peer(h2t, peer_u[l].astype(BF16), pv_t, n0, r1, e0, e1, x1, mod, layout)
        na_states.append(na_kv)
        diff_states.append(df_kv)
    y_prompt = x[:t_ctx].reshape(batch, seq, D_MODEL)
    y_sample = x[t_ctx:].reshape(dbatch, dseq, D_MODEL)
    return (y_prompt, y_sample, jnp.stack(na_states, axis=1), jnp.stack(diff_states, axis=1))
```

```python
import functools
import math

import numpy as np
import jax
import jax.numpy as jnp
from jax import lax
from jax.experimental import pallas as pl
from jax.experimental.pallas import tpu as pltpu

F32 = jnp.float32
BF16 = jnp.bfloat16

D_MODEL = 1024
DEPTH = 2
GRID_W = 64
GROUP_W = 256
HEAD_DIM = 64
N_HEADS = 4
DIFF_SUB = 32
CONV_K = 31
CONV_HALO = 16
NA_WIN_R = 8
NA_WIN_C = 16
ROPE_BASE = 10000.0
CHUNK = 128
PEER_HEADS = 8
PEER_NKEYS = 128
PEER_TOPK = 16
EPS = 1e-6
NEG = -1e30

LANES = 128
SUBLANES = 8
MXU_DIM = 256
VMEM_LIMIT = 48 * 1024 * 1024

SEQ_BLOCK = 256
ROW_BLOCK = 256
ROUTE_BLOCK = 256
PEER_TOKENS = 512
PEER_IBLOCK = 16
PEER_VMEM_LIMIT = 56 * 1024 * 1024
DIFF_QBLOCK = 128
DIFF_KCHUNK = 512


def _params(*sem):
    return pltpu.CompilerParams(dimension_semantics=sem, vmem_limit_bytes=VMEM_LIMIT)


def _dot(a, b):
    return jnp.dot(a, b, preferred_element_type=F32)


def _dot_nt(a, b):
    return lax.dot_general(a, b, (((1,), (1,)), ((), ())), preferred_element_type=F32)


def _split_dot(a, b):
    a1 = a.astype(BF16)
    r1 = a - a1.astype(F32)
    a2 = r1.astype(BF16)
    a3 = (r1 - a2.astype(F32)).astype(BF16)
    return _dot(a1, b) + _dot(a2, b) + _dot(a3, b)


def _group_ones(n, group):
    r = lax.broadcasted_iota(jnp.int32, (n, n), 0) // group
    c = lax.broadcasted_iota(jnp.int32, (n, n), 1) // group
    return (r == c).astype(BF16)


def _group_rms(x, gain, group):
    ssq = _split_dot(x * x, _group_ones(x.shape[-1], group))
    return x * lax.rsqrt(ssq * (1.0 / group) + EPS) * gain


def _gelu_tanh(x):
    k = -2.0 * math.sqrt(2.0 / math.pi) * math.log2(math.e)
    return x / (1.0 + jnp.exp2(x * (k + (k * 0.044715) * (x * x))))


def _layer_norm(y, g, b):
    mu = jnp.mean(y, axis=-1, keepdims=True)
    yc = y - mu
    return yc * lax.rsqrt(jnp.mean(yc * yc, axis=-1, keepdims=True) + EPS) * g + b


def _mod_spec(rows_per_block, layout):
    t_ctx, lat_seq = layout
    assert t_ctx % rows_per_block == 0 and lat_seq % rows_per_block == 0
    def index(i, *_):
        return (jnp.maximum((i * rows_per_block - t_ctx) // lat_seq + 1, 0), 0, 0)
    return pl.BlockSpec((1, 6, D_MODEL), index)


def _mod_kernel(c_ref, w_ref, b_ref, o_ref):
    c = c_ref[...]
    s = c * jax.nn.sigmoid(c)
    o_ref[0] = jnp.dot(s, w_ref[0], preferred_element_type=F32,
                       precision=lax.Precision.HIGHEST) + b_ref[0]


def _modulation(cond, w_mod, b_mod):
    nb = 4
    cols = 6 * D_MODEL // nb
    return pl.pallas_call(
        _mod_kernel,
        grid=(DEPTH, nb),
        in_specs=[pl.BlockSpec((SUBLANES, D_MODEL), lambda l, j: (0, 0)),
                  pl.BlockSpec((1, D_MODEL, cols), lambda l, j: (l, 0, j)),
                  pl.BlockSpec((1, 1, cols), lambda l, j: (l, 0, j))],
        out_specs=pl.BlockSpec((1, SUBLANES, cols), lambda l, j: (l, 0, j)),
        out_shape=jax.ShapeDtypeStruct((DEPTH, SUBLANES, 6 * D_MODEL), F32),
        name="modulation",
        compiler_params=_params("arbitrary", "arbitrary"),
    )(cond, w_mod, b_mod.reshape(DEPTH, 1, 6 * D_MODEL))


def _in_proj_kernel(x_ref, mod_ref, g_ref, w_ref, o_ref):
    x = x_ref[...]
    y = x * lax.rsqrt(jnp.mean(x * x, axis=-1, keepdims=True) + EPS) * g_ref[...]
    h = y * (1.0 + mod_ref[0, 1:2, :]) + mod_ref[0, 0:1, :]
    o_ref[...] = _dot(h.astype(BF16), w_ref[...])


def _in_proj(x, mod, g, w, layout):
    t = x.shape[0]
    n = w.shape[1]
    return pl.pallas_call(
        _in_proj_kernel,
        grid=(t // ROW_BLOCK,),
        in_specs=[pl.BlockSpec((ROW_BLOCK, D_MODEL), lambda i: (i, 0)),
                  _mod_spec(ROW_BLOCK, layout),
                  pl.BlockSpec((1, D_MODEL), lambda i: (0, 0)),
                  pl.BlockSpec((D_MODEL, n), lambda i: (0, 0))],
        out_specs=pl.BlockSpec((ROW_BLOCK, n), lambda i: (i, 0)),
        out_shape=jax.ShapeDtypeStruct((t, n), F32),
        name="in_proj",
        compiler_params=_params("arbitrary"),
    )(x, mod, g, w)


def _conv_kernel(ac, gc, ap, gp, an, gn, w_ref, b_ref, lg_ref, lb_ref, o_ref, pad_ref, *, ctx_blocks, seq_blocks):
    i = pl.program_id(0)
    is_ctx = i < ctx_blocks
    j = i % seq_blocks
    first = jnp.logical_or(is_ctx, j == 0)
    last = jnp.logical_or(is_ctx, j == seq_blocks - 1)
    yp = ap[...] * jax.nn.sigmoid(gp[...])
    yn = an[...] * jax.nn.sigmoid(gn[...])
    pad_ref[0:CONV_HALO, :] = jnp.where(first, 0.0, yp)
    pad_ref[CONV_HALO:CONV_HALO + SEQ_BLOCK, :] = ac[...] * jax.nn.sigmoid(gc[...])
    pad_ref[CONV_HALO + SEQ_BLOCK:, :] = jnp.where(last, 0.0, yn)
    off = CONV_HALO - CONV_K // 2
    acc = jnp.zeros((SEQ_BLOCK, GROUP_W), F32)
    for k in range(CONV_K):
        acc = acc + pad_ref[off + k:off + k + SEQ_BLOCK, :] * w_ref[k:k + 1, :]
    y = _layer_norm(acc + b_ref[...], lg_ref[...], lb_ref[...])
    o_ref[...] = y * jax.nn.sigmoid(y)


def _conv_module(parts, w, b, lg, lb, ctx_tokens, lat_seq):
    t = parts.shape[0]
    nblk = t // SEQ_BLOCK
    hb = SEQ_BLOCK // CONV_HALO
    last_halo = t // CONV_HALO - 1
    vec = pl.BlockSpec((1, GROUP_W), lambda i: (0, 0))
    kern = functools.partial(_conv_kernel, ctx_blocks=ctx_tokens // SEQ_BLOCK, seq_blocks=lat_seq // SEQ_BLOCK)
    return pl.pallas_call(
        kern,
        grid=(nblk,),
        in_specs=[pl.BlockSpec((SEQ_BLOCK, GROUP_W), lambda i: (i, 0)),
                  pl.BlockSpec((SEQ_BLOCK, GROUP_W), lambda i: (i, 1)),
                  pl.BlockSpec((CONV_HALO, GROUP_W), lambda i: (jnp.maximum(i * hb - 1, 0), 0)),
                  pl.BlockSpec((CONV_HALO, GROUP_W), lambda i: (jnp.maximum(i * hb - 1, 0), 1)),
                  pl.BlockSpec((CONV_HALO, GROUP_W), lambda i: (jnp.minimum((i + 1) * hb, last_halo), 0)),
                  pl.BlockSpec((CONV_HALO, GROUP_W), lambda i: (jnp.minimum((i + 1) * hb, last_halo), 1)),
                  pl.BlockSpec((CONV_K, GROUP_W), lambda i: (0, 0)),
                  vec, vec, vec],
        out_specs=pl.BlockSpec((SEQ_BLOCK, GROUP_W), lambda i: (i, 0)),
        out_shape=jax.ShapeDtypeStruct((t, GROUP_W), F32),
        scratch_shapes=[pltpu.VMEM((SEQ_BLOCK + 2 * CONV_HALO, GROUP_W), F32)],
        name="conv_module",
        compiler_params=_params("arbitrary"),
    )(parts, parts, parts, parts, parts, parts, w, b, lg, lb)


def _gmlp_kernel(u_ref, v_ref, lg_ref, lb_ref, ws_ref, bias_ref, o_ref):
    u = jax.nn.gelu(u_ref[...])
    v = _layer_norm(jax.nn.gelu(v_ref[...]), lg_ref[...], lb_ref[...]).astype(BF16)
    lane_g = lax.broadcasted_iota(jnp.int32, (1, GROUP_W), 1) // (GROUP_W // N_HEADS)
    for ch in range(SEQ_BLOCK // CHUNK):
        rows = slice(ch * CHUNK, (ch + 1) * CHUNK)
        s = bias_ref[...]
        for g in range(N_HEADS):
            s = s + jnp.where(lane_g == g, _dot(ws_ref[g], v[rows]), 0.0)
        o_ref[rows, :] = u[rows] * s


def _gmlp(parts, lg, lb, ws, bias):
    t = parts.shape[0]
    vec = pl.BlockSpec((1, GROUP_W), lambda i: (0, 0))
    return pl.pallas_call(
        _gmlp_kernel,
        grid=(t // SEQ_BLOCK,),
        in_specs=[pl.BlockSpec((SEQ_BLOCK, GROUP_W), lambda i: (i, 8)),
                  pl.BlockSpec((SEQ_BLOCK, GROUP_W), lambda i: (i, 9)),
                  vec, vec,
                  pl.BlockSpec((N_HEADS, CHUNK, CHUNK), lambda i: (0, 0, 0)),
                  pl.BlockSpec((CHUNK, GROUP_W), lambda i: (0, 0))],
        out_specs=pl.BlockSpec((SEQ_BLOCK, GROUP_W), lambda i: (i, 0)),
        out_shape=jax.ShapeDtypeStruct((t, GROUP_W), F32),
        compiler_params=_params("arbitrary"),
        name="gmlp",
    )(parts, parts, lg, lb, ws, bias)


def _lane_group_id(width, group):
    return lax.broadcasted_iota(jnp.int32, (1, width), 1) // group


def _stack_masked(q, group, ids):
    lane = _lane_group_id(q.shape[-1], group)
    return jnp.concatenate([jnp.where(lane == g, q, jnp.zeros_like(q)) for g in ids], axis=0)


def _diff_lambda(dl_ref, lam_init):
    dl = dl_ref[...]
    a = jnp.sum(dl[0:1, :] * dl[1:2, :], axis=-1, keepdims=True)
    b = jnp.sum(dl[2:3, :] * dl[3:4, :], axis=-1, keepdims=True)
    return jnp.exp(a) - jnp.exp(b) + lam_init


def _ctx_attn_kernel(q_ref, k_ref, v_ref, dq_ref, dk_ref, dv_ref, qn_ref, kn_ref, dqn_ref, dkn_ref, sub_ref,
                     dl_ref, na_ref, df_ref, na_kv_ref, df_kv_ref, *, lam_init):
    n = q_ref.shape[0]
    lane_h = _lane_group_id(GROUP_W, HEAD_DIM)

    def write_state(ref, keys, values):
        for h in range(N_HEADS):
            ref[0, 0, h] = keys[:, h * HEAD_DIM:(h + 1) * HEAD_DIM]
            ref[0, 1, h] = values[:, h * HEAD_DIM:(h + 1) * HEAD_DIM]

    qn = _group_rms(q_ref[...], qn_ref[...], HEAD_DIM)
    kn = _group_rms(k_ref[...], kn_ref[...], HEAD_DIM)
    v = v_ref[...]
    write_state(na_kv_ref, kn, v)
    qs = _stack_masked(qn.astype(BF16), HEAD_DIM, range(N_HEADS))
    s = _dot_nt(qs, kn.astype(BF16)) * (HEAD_DIM ** -0.5)
    e = jnp.exp(s - jnp.max(s, axis=-1, keepdims=True))
    r = _dot(e.astype(BF16), v.astype(BF16)) / jnp.sum(e, axis=-1, keepdims=True)
    o = jnp.zeros((n, GROUP_W), F32)
    for h in range(N_HEADS):
        o = o + jnp.where(lane_h == h, r[h * n:(h + 1) * n], 0.0)
    na_ref[...] = o
    lam = _diff_lambda(dl_ref, lam_init)
    dq = _group_rms(dq_ref[...], dqn_ref[...], DIFF_SUB)
    dk = _group_rms(dk_ref[...], dkn_ref[...], DIFF_SUB)
    dv = dv_ref[...]
    write_state(df_kv_ref, dk, dv)
    dkb = dk.astype(BF16)
    dvb = dv.astype(BF16)
    o = jnp.zeros((n, GROUP_W), F32)
    for h in range(N_HEADS):
        qs = _stack_masked(dq.astype(BF16), DIFF_SUB, (2 * h, 2 * h + 1))
        s = _dot_nt(qs, dkb) * (DIFF_SUB ** -0.5)
        e = jnp.exp(s - jnp.max(s, axis=-1, keepdims=True))
        inv = 1.0 / jnp.sum(e, axis=-1, keepdims=True)
        a = e[:n] * inv[:n] - e[n:] * (lam * inv[n:])
        o = o + jnp.where(lane_h == h, _dot(a.astype(BF16), dvb), 0.0)
    df_ref[...] = _group_rms(o, sub_ref[...], HEAD_DIM) * (1.0 - lam_init)


def _ctx_attention(parts, batch, seq, qn, kn, dqn, dkn, sub, dl, lam_init):
    t = batch * seq
    vec = pl.BlockSpec((1, GROUP_W), lambda b: (0, 0))
    col = lambda c: pl.BlockSpec((seq, GROUP_W), lambda b, c=c: (b, c))
    out = jax.ShapeDtypeStruct((t, GROUP_W), F32)
    ob = pl.BlockSpec((seq, GROUP_W), lambda b: (b, 0))
    state = jax.ShapeDtypeStruct((batch, 2, N_HEADS, seq, HEAD_DIM), F32)
    sb = pl.BlockSpec((1, 2, N_HEADS, seq, HEAD_DIM), lambda b: (b, 0, 0, 0, 0))
    return pl.pallas_call(
        functools.partial(_ctx_attn_kernel, lam_init=lam_init),
        grid=(batch,),
        in_specs=[col(2), col(3), col(4), col(5), col(6), col(7), vec, vec, vec, vec, vec,
                  pl.BlockSpec((4, DIFF_SUB), lambda b: (0, 0))],
        out_specs=[ob, ob, sb, sb],
        out_shape=[out, out, state, state],
        compiler_params=_params("arbitrary"),
        name="ctx_attention",
    )(parts, parts, parts, parts, parts, parts, qn, kn, dqn, dkn, sub, dl)


def _lat_prep_kernel(q_ref, k_ref, v_ref, dq_ref, dk_ref, dv_ref, qn_ref, kn_ref, dqn_ref, dkn_ref,
                     cos_ref, sin_ref, oq, ok, ov, odq, odk, odv):
    oq[...] = _group_rms(q_ref[...], qn_ref[...], HEAD_DIM).astype(BF16)
    ok[...] = _group_rms(k_ref[...], kn_ref[...], HEAD_DIM).astype(BF16)
    ov[...] = v_ref[...].astype(BF16)
    odv[...] = dv_ref[...].astype(BF16)
    half = DIFF_SUB // 4
    lane = lax.broadcasted_iota(jnp.int32, (1, GROUP_W), 1)
    lower = (lane % (2 * half)) < half
    cos = cos_ref[...]
    sin = sin_ref[...]

    def rope(x):
        partner = jnp.where(lower, pltpu.roll(x, GROUP_W - half, 1), pltpu.roll(x, half, 1))
        return x * cos + partner * sin

    odq[...] = rope(_group_rms(dq_ref[...], dqn_ref[...], DIFF_SUB)).astype(BF16)
    odk[...] = rope(_group_rms(dk_ref[...], dkn_ref[...], DIFF_SUB)).astype(BF16)


def _rope_tables(seq):
    half = DIFF_SUB // 4
    t = jnp.arange(seq)
    rows = (t // GRID_W).astype(F32)
    cols = (t % GRID_W).astype(F32)
    inv = ROPE_BASE ** (-jnp.arange(half, dtype=F32) / half)
    ang_r = rows[:, None] * inv[None, :]
    ang_c = cols[:, None] * inv[None, :]
    cos32 = jnp.concatenate([jnp.cos(ang_r), jnp.cos(ang_r), jnp.cos(ang_c), jnp.cos(ang_c)], axis=-1)
    sin32 = jnp.concatenate([-jnp.sin(ang_r), jnp.sin(ang_r), -jnp.sin(ang_c), jnp.sin(ang_c)], axis=-1)
    reps = GROUP_W // DIFF_SUB
    return jnp.tile(cos32, (1, reps)), jnp.tile(sin32, (1, reps))


def _lat_prep(parts, row0, batch, seq, qn, kn, dqn, dkn, cos, sin):
    t = batch * seq
    nb = seq // SEQ_BLOCK
    r0 = row0 // SEQ_BLOCK
    vec = pl.BlockSpec((1, GROUP_W), lambda i: (0, 0))
    col = lambda c: pl.BlockSpec((SEQ_BLOCK, GROUP_W), lambda i, c=c: (r0 + i, c))
    tab = pl.BlockSpec((SEQ_BLOCK, GROUP_W), lambda i: (i % nb, 0))
    out = jax.ShapeDtypeStruct((t, GROUP_W), BF16)
    ob = pl.BlockSpec((SEQ_BLOCK, GROUP_W), lambda i: (i, 0))
    return pl.pallas_call(
        _lat_prep_kernel,
        grid=(t // SEQ_BLOCK,),
        in_specs=[col(2), col(3), col(4), col(5), col(6), col(7), vec, vec, vec, vec, tab, tab],
        out_specs=[ob] * 6,
        out_shape=[out] * 6,
        compiler_params=_params("arbitrary"),
        name="lat_prep",
    )(parts, parts, parts, parts, parts, parts, qn, kn, dqn, dkn, cos, sin)


def _na_row_start(r, rows):
    return jnp.clip(r - NA_WIN_R // 2, 0, rows - NA_WIN_R)


def _na_lat_kernel(q_ref, k_ref, v_ref, ck_ref, cv_ref, bt_ref, o_ref, *, rows):
    r = pl.program_id(1)
    start = pl.multiple_of(_na_row_start(r, rows) * GRID_W, GRID_W)
    n_loc = NA_WIN_R * GRID_W
    kl = k_ref[pl.ds(start, n_loc), :]
    vl = v_ref[pl.ds(start, n_loc), :]
    qs = _stack_masked(q_ref[...], HEAD_DIM, range(N_HEADS))
    scale = HEAD_DIM ** -0.5
    s_loc = _dot_nt(qs, kl) * scale + bt_ref[0]
    s_ctx = _dot_nt(qs, ck_ref[0]) * scale
    m = jnp.maximum(jnp.max(s_loc, axis=-1, keepdims=True), jnp.max(s_ctx, axis=-1, keepdims=True))
    e_loc = jnp.exp(s_loc - m)
    e_ctx = jnp.exp(s_ctx - m)
    l = jnp.sum(e_loc, axis=-1, keepdims=True) + jnp.sum(e_ctx, axis=-1, keepdims=True)
    res = (_dot(e_loc.astype(BF16), vl) + _dot(e_ctx.astype(BF16), cv_ref[0])) / l
    lane_h = _lane_group_id(GROUP_W, HEAD_DIM)
    o = jnp.zeros((GRID_W, GROUP_W), F32)
    for h in range(N_HEADS):
        o = o + jnp.where(lane_h == h, res[h * GRID_W:(h + 1) * GRID_W], 0.0)
    o_ref[...] = o


def _na_bias_table(rel_bias, rows):
    wr = NA_WIN_R
    c = np.arange(GRID_W)
    c0 = np.clip(c - NA_WIN_C // 2, 0, GRID_W - NA_WIN_C)
    in_win = (c[None, :] >= c0[:, None]) & (c[None, :] < c0[:, None] + NA_WIN_C)
    dc_idx = np.clip(c[None, :] - c[:, None] + NA_WIN_C - 1, 0, 2 * NA_WIN_C - 2)
    off = np.arange(wr)
    dr_idx = (np.arange(wr)[None, :] - off[:, None]) + NA_WIN_R - 1
    rb = rel_bias[:, dr_idx]
    onehot = jnp.asarray(dc_idx[..., None] == np.arange(2 * NA_WIN_C - 1), F32)
    bias = jnp.einsum('howd,qkd->howqk', rb, onehot, precision=lax.Precision.HIGHEST)
    bias = jnp.where(jnp.asarray(in_win)[None, None, None], bias, NEG)
    bias = bias.transpose(1, 0, 3, 2, 4)
    return bias.reshape(wr, N_HEADS * GRID_W, wr * GRID_W)


def _na_latent(qn, kn, v, ck, cv, bias_tab, batch, seq):
    rows = seq // GRID_W
    nctx = ck.shape[1]
    full = pl.BlockSpec((seq, GROUP_W), lambda b, r: (b, 0))
    ctx = pl.BlockSpec((1, nctx, GROUP_W), lambda b, r: (b, 0, 0))
    return pl.pallas_call(
        functools.partial(_na_lat_kernel, rows=rows),
        grid=(batch, rows),
        in_specs=[pl.BlockSpec((GRID_W, GROUP_W), lambda b, r: (b * rows + r, 0)),
                  full, full, ctx, ctx,
                  pl.BlockSpec((1, N_HEADS * GRID_W, NA_WIN_R * GRID_W),
                               lambda b, r: (r - _na_row_start(r, rows), 0, 0))],
        out_specs=pl.BlockSpec((GRID_W, GROUP_W), lambda b, r: (b * rows + r, 0)),
        out_shape=jax.ShapeDtypeStruct((batch * seq, GROUP_W), F32),
        compiler_params=_params("arbitrary", "arbitrary"),
        name="na_latent",
    )(qn, kn, v, ck, cv, bias_tab)


def _lane_fold(fn, acc, x):
    for g in range(x.shape[-1] // LANES):
        acc = fn(acc, x[:, g * LANES:(g + 1) * LANES])
    return acc


def _diff_lat_kernel(q_ref, k_ref, v_ref, ck_ref, cv_ref, sub_ref, dl_ref, o_ref, s_ref, sc_ref, *, lam_init):
    n = q_ref.shape[0]
    n_chunks, rows, chunk = s_ref.shape
    n_sub = rows // n
    lam = _diff_lambda(dl_ref, lam_init)
    qs = _stack_masked(q_ref[...], DIFF_SUB, range(n_sub))
    c2 = (DIFF_SUB ** -0.5) * math.log2(math.e)
    step = pl.program_id(1)
    top = jnp.full((rows, LANES), NEG, F32)
    for j in range(n_chunks):
        s = _dot(qs, k_ref[0, :, j * chunk:(j + 1) * chunk])
        s_ref[(j + step) % n_chunks] = s
        top = _lane_fold(jnp.maximum, top, s)
    s = _dot(qs, ck_ref[0])
    sc_ref[...] = s
    top = _lane_fold(jnp.maximum, top, s)
    m = jnp.max(top, axis=-1, keepdims=True) * c2
    part = jnp.zeros((rows, LANES), F32)
    half = rows // 2
    res = [jnp.zeros((half, GROUP_W), F32)] * 2
    products = []
    for j in range(n_chunks + 1):
        scores = s_ref[(j + step) % n_chunks] if j < n_chunks else sc_ref[...]
        values = v_ref[j * chunk:(j + 1) * chunk, :] if j < n_chunks else cv_ref[0]
        shift = m
        if j >= 2:
            bits = pltpu.bitcast(products[j - 2][:SUBLANES, :LANES], jnp.uint32)
            shift = m + pltpu.bitcast((bits >> 16) >> 16, F32)[:1, :1]
        e = jnp.exp2(scores * c2 - shift)
        part = _lane_fold(jnp.add, part, e)
        eb = e.astype(BF16)
        prods = [_dot(eb[r * half:(r + 1) * half], values) for r in range(2)]
        products.append(prods[0])
        res = [res[r] + prods[r] for r in range(2)]
    res = jnp.concatenate(res, axis=0) * (1.0 / jnp.sum(part, axis=-1, keepdims=True))
    lane_h = _lane_group_id(GROUP_W, HEAD_DIM)
    o = jnp.zeros((n, GROUP_W), F32)
    for h in range(N_HEADS):
        first, second = res[2 * h * n:(2 * h + 1) * n], res[(2 * h + 1) * n:(2 * h + 2) * n]
        o = o + jnp.where(lane_h == h, first - lam * second, 0.0)
    o_ref[...] = _group_rms(o, sub_ref[...], HEAD_DIM) * (1.0 - lam_init)


def _diff_latent(dq, dk_t, dv, ck_t, cv, sub, dl, batch, seq, lam_init):
    nq = seq // DIFF_QBLOCK
    nctx = cv.shape[1]
    full = pl.BlockSpec((seq, GROUP_W), lambda b, i: (b, 0))
    ctx = pl.BlockSpec((1, nctx, GROUP_W), lambda b, i: (b, 0, 0))
    return pl.pallas_call(
        functools.partial(_diff_lat_kernel, lam_init=lam_init),
        grid=(batch, nq),
        in_specs=[pl.BlockSpec((DIFF_QBLOCK, GROUP_W), lambda b, i: (b * nq + i, 0)),
                  pl.BlockSpec((1, GROUP_W, seq), lambda b, i: (b, 0, 0)), full,
                  pl.BlockSpec((1, GROUP_W, nctx), lambda b, i: (b, 0, 0)), ctx,
                  pl.BlockSpec((1, GROUP_W), lambda b, i: (0, 0)),
                  pl.BlockSpec((4, DIFF_SUB), lambda b, i: (0, 0))],
        out_specs=pl.BlockSpec((DIFF_QBLOCK, GROUP_W), lambda b, i: (b * nq + i, 0)),
        out_shape=jax.ShapeDtypeStruct((batch * seq, GROUP_W), F32),
        scratch_shapes=[pltpu.VMEM((seq // DIFF_KCHUNK, 2 * N_HEADS * DIFF_QBLOCK, DIFF_KCHUNK), F32),
                        pltpu.VMEM((2 * N_HEADS * DIFF_QBLOCK, nctx), F32)],
        compiler_params=_params("arbitrary", "arbitrary"),
        name="diff_latent",
    )(dq, dk_t, dv, ck_t, cv, sub, dl)


def _out_proj_kernel(conv_ref, na_c_ref, na_l_ref, df_c_ref, df_l_ref, gm_ref, x_ref, mod_ref, w_ref, o_ref, *,
                     ctx_blocks):
    is_ctx = pl.program_id(0) < ctx_blocks
    na = jnp.where(is_ctx, na_c_ref[...], na_l_ref[...])
    df = jnp.where(is_ctx, df_c_ref[...], df_l_ref[...])
    mixed = jnp.zeros(o_ref.shape, F32)
    for g, part in enumerate((conv_ref[...], na, df, gm_ref[...])):
        mixed = mixed + _dot(part.astype(BF16), w_ref[g * GROUP_W:(g + 1) * GROUP_W, :])
    o_ref[...] = x_ref[...] + mod_ref[0, 2:3, :] * mixed


def _out_proj(conv_o, na_c, na_l, df_c, df_l, gm_o, x, mod, w, layout):
    t = x.shape[0]
    ctx_blocks = layout[0] // ROW_BLOCK
    last_lat = (t - layout[0]) // ROW_BLOCK - 1
    part = pl.BlockSpec((ROW_BLOCK, GROUP_W), lambda i: (i, 0))
    ctx = pl.BlockSpec((ROW_BLOCK, GROUP_W), lambda i: (jnp.minimum(i, ctx_blocks - 1), 0))
    lat = pl.BlockSpec((ROW_BLOCK, GROUP_W), lambda i: (jnp.clip(i - ctx_blocks, 0, last_lat), 0))
    row = pl.BlockSpec((ROW_BLOCK, D_MODEL), lambda i: (i, 0))
    return pl.pallas_call(
        functools.partial(_out_proj_kernel, ctx_blocks=ctx_blocks),
        grid=(t // ROW_BLOCK,),
        in_specs=[part, ctx, lat, ctx, lat, part, row,
                  _mod_spec(ROW_BLOCK, layout),
                  pl.BlockSpec((D_MODEL, D_MODEL), lambda i: (0, 0))],
        out_specs=row,
        out_shape=jax.ShapeDtypeStruct((t, D_MODEL), F32),
        compiler_params=_params("arbitrary"),
        name="out_proj",
    )(conv_o, na_c, na_l, df_c, df_l, gm_o, x, mod, w)


def _sort_desc(vals):
    v = list(vals)
    n = len(v)
    k = 2
    while k <= n:
        j = k // 2
        while j >= 1:
            for i in range(n):
                l = i ^ j
                if l > i:
                    hi, lo = jnp.maximum(v[i], v[l]), jnp.minimum(v[i], v[l])
                    v[i], v[l] = (hi, lo) if (i & k) == 0 else (lo, hi)
            j //= 2
        k *= 2
    return v


def _merge_desc(v):
    v = list(v)
    n = len(v)
    j = n // 2
    while j >= 1:
        for i in range(n):
            l = i ^ j
            if l > i:
                v[i], v[l] = jnp.maximum(v[i], v[l]), jnp.minimum(v[i], v[l])
        j //= 2
    return v


def _top_half(a, b):
    n = len(a)
    return [jnp.maximum(a[i], b[n - 1 - i]) for i in range(n)]


def _top16_keys(s):
    k = PEER_TOPK
    v = _sort_desc([s[a * SUBLANES:(a + 1) * SUBLANES] for a in range(PEER_NKEYS // SUBLANES)])
    shift = SUBLANES // 2
    while shift >= 1:
        rolled = [pltpu.roll(x, shift, 0) for x in v]
        v = _merge_desc(_top_half(v, rolled))
        shift //= 2
    return v[:k]


def _paired_bf16(x):
    hi = pltpu.bitcast(x.astype(BF16).astype(F32), jnp.uint32)
    return hi | (hi >> 16)


def _route_kernel(x_ref, mod_ref, g_ref, wq_ref, sk_ref, h2_ref, n0_ref, r1_ref, e0_ref, e1_ref, s0_ref, s1_ref):
    x = x_ref[...]
    y = x * lax.rsqrt(jnp.mean(x * x, axis=-1, keepdims=True) + EPS) * g_ref[...]
    h2 = y * (1.0 + mod_ref[0, 4:5, :]) + mod_ref[0, 3:4, :]
    h2t = h2.T.astype(BF16)
    h2_ref[...] = h2t
    qt = _dot(wq_ref[...], h2t)
    for tile in range(x.shape[0] // LANES):
        _route_tile(qt[:, tile * LANES:(tile + 1) * LANES], tile, sk_ref, n0_ref, r1_ref, e0_ref, e1_ref,
                    s0_ref, s1_ref)


def _route_tile(qt, tile, sk_ref, n0_ref, r1_ref, e0_ref, e1_ref, s0_ref, s1_ref):
    tb = LANES
    sub = lax.broadcasted_iota(jnp.int32, (SUBLANES, tb), 0)
    k = PEER_TOPK
    tops = [[jnp.zeros((SUBLANES, tb), F32)] * k, [jnp.zeros((SUBLANES, tb), F32)] * k]
    for h in range(PEER_HEADS):
        for p in range(2):
            base = (2 * h + p) * PEER_NKEYS
            s = _dot(sk_ref[p], qt[base:base + PEER_NKEYS].astype(BF16))
            (s0_ref if p == 0 else s1_ref)[h] = s
            top = _top16_keys(s)
            tops[p] = [jnp.where(sub == h, top[a], tops[p][a]) for a in range(k)]
    cand = [tops[0][a] + tops[1][b] for a in range(k) for b in range(k) if (a + 1) * (b + 1) <= k]
    pad = [jnp.full((SUBLANES, tb), NEG, F32)] * (4 * k - len(cand))
    groups = [_sort_desc((cand + pad)[g * k:(g + 1) * k]) for g in range(4)]
    best = _top_half(_merge_desc(_top_half(groups[0], groups[1])), _merge_desc(_top_half(groups[2], groups[3])))
    thr = functools.reduce(jnp.minimum, best)
    m = tops[0][0] + tops[1][0]
    z = functools.reduce(jnp.add, [jnp.where(c >= thr, jnp.exp(c - m), 0.0) for c in cand])
    zinv = 1.0 / z
    counts, first = [], 0
    for a in range(k):
        n_b = k // (a + 1)
        counts.append(functools.reduce(jnp.add, [jnp.where(c >= thr, 1.0, 0.0) for c in cand[first:first + n_b]]))
        first += n_b
    for h in range(PEER_HEADS):
        s0 = s0_ref[h]
        s1 = s1_ref[h]
        n0 = jnp.zeros_like(s0)
        r1 = jnp.full_like(s1, float(k))
        for a in reversed(range(k)):
            n0 = jnp.where(s0 >= tops[0][a][h:h + 1], counts[a][h:h + 1], n0)
            r1 = jnp.where(s1 >= tops[1][a][h:h + 1], float(a), r1)
        n0_ref[h, tile] = _paired_bf16(n0)
        r1_ref[h, tile] = pltpu.bitcast(r1.astype(BF16), jnp.uint32)
        e0_ref[h, tile] = _paired_bf16(jnp.exp(s0 - tops[0][0][h:h + 1]) * zinv[h:h + 1])
        e1_ref[h, tile] = pltpu.bitcast(jnp.exp(s1 - tops[1][0][h:h + 1]).astype(BF16), jnp.uint32)


def _route(x1, mod, g, wq_t, sk, layout):
    t = x1.shape[0]
    tb = ROUTE_BLOCK
    tiles = t // LANES
    first_key = jax.ShapeDtypeStruct((PEER_HEADS, tiles, PEER_NKEYS, LANES), jnp.uint32)
    second_key = jax.ShapeDtypeStruct((PEER_HEADS, tiles, PEER_NKEYS // 2, LANES), jnp.uint32)
    fkb = pl.BlockSpec((PEER_HEADS, tb // LANES, PEER_NKEYS, LANES), lambda i: (0, i, 0, 0))
    skb = pl.BlockSpec((PEER_HEADS, tb // LANES, PEER_NKEYS // 2, LANES), lambda i: (0, i, 0, 0))
    scores = pltpu.VMEM((PEER_HEADS, PEER_NKEYS, LANES), F32)
    return pl.pallas_call(
        _route_kernel,
        grid=(t // tb,),
        in_specs=[pl.BlockSpec((tb, D_MODEL), lambda i: (i, 0)),
                  _mod_spec(tb, layout),
                  pl.BlockSpec((1, D_MODEL), lambda i: (0, 0)),
                  pl.BlockSpec(wq_t.shape, lambda i: (0, 0)),
                  pl.BlockSpec(sk.shape, lambda i: (0, 0, 0))],
        out_specs=[pl.BlockSpec((D_MODEL, tb), lambda i: (0, i)), fkb, skb, fkb, skb],
        out_shape=[jax.ShapeDtypeStruct((D_MODEL, t), BF16), first_key, second_key, first_key, second_key],
        scratch_shapes=[scores, scores],
        compiler_params=_params("arbitrary"),
        name="peer_route",
    )(x1, mod, g, wq_t, sk)


def _rows_as_bf16(rows):
    tile = pltpu.bitcast(rows, BF16)
    return jnp.concatenate([tile] * (PEER_NKEYS // tile.shape[0]), axis=0)


def _peer_kernel(h2_ref, pu_ref, pv_ref, pv_prev_ref, n0_ref, r1_ref, e0_ref, e1_ref, x_ref, mod_ref, o_ref,
                 acc_ref, a0_ref, a1_ref, w0_ref, w1_ref):
    e = pl.program_id(1)
    a_refs = (a0_ref, a1_ref)
    w_refs = (w0_ref, w1_ref)

    @pl.when(e == 0)
    def _():
        acc_ref[...] = jnp.zeros_like(acc_ref)
        for w_ref in w_refs:
            w_ref[...] = jnp.zeros_like(w_ref)

    tb = h2_ref.shape[1]
    n_sub = pv_ref.shape[0]
    sub_keys = MXU_DIM // PEER_NKEYS
    n_tiles = tb // LANES

    def first_matmul(k, slot):
        start = k * MXU_DIM if isinstance(k, int) else pl.multiple_of(k * MXU_DIM, MXU_DIM)
        parts = [_dot(pu_ref[pl.ds(start, MXU_DIM), j * MXU_DIM:(j + 1) * MXU_DIM],
                      h2_ref[j * MXU_DIM:(j + 1) * MXU_DIM, :]) for j in range(D_MODEL // MXU_DIM)]
        a = functools.reduce(jnp.add, parts)
        a_refs[slot][...] = pltpu.bitcast(_gelu_tanh(a).astype(BF16), jnp.uint32)
        return parts

    def second_matmul(values, slot):
        part = _dot(values, w_refs[slot][...])
        acc_ref[...] += part
        return part

    def gate(k, anchors):
        key = e * PEER_IBLOCK + sub_keys * k
        out = []
        for c in range(n_tiles):
            cols = slice(c * LANES, (c + 1) * LANES)
            words = pltpu.bitcast(anchors[c], jnp.uint32)
            gates = [_rows_as_bf16((words >> 16) >> 16)] * sub_keys
            for h in range(PEER_HEADS):
                r1 = pltpu.bitcast(r1_ref[h, c], BF16)
                e1 = pltpu.bitcast(e1_ref[h, c], BF16)
                for d in range(sub_keys):
                    row = pl.ds(key + d, SUBLANES, stride=0)
                    n0_d = _rows_as_bf16(n0_ref[h, c, row, :])
                    e0_d = _rows_as_bf16(e0_ref[h, c, row, :])
                    gates[d] = gates[d] + jnp.where(r1 < n0_d, e1 * e0_d, jnp.zeros_like(e1))
            out.append(gates)
        return out

    def write_weights(gates, slot):
        for c in range(n_tiles):
            cols = slice(c * LANES, (c + 1) * LANES)
            for d in range(sub_keys):
                rows = slice(d * PEER_NKEYS, (d + 1) * PEER_NKEYS)
                words = slice(d * PEER_NKEYS // 2, (d + 1) * PEER_NKEYS // 2)
                w_refs[slot][rows, cols] = pltpu.bitcast(a_refs[slot][words, cols], BF16) * gates[c][d]

    def trip(j, carry):
        tile = lambda x, row: x[row:row + SUBLANES, :LANES]
        assert n_tiles == 4
        anchors = []
        for slot in range(2):
            k = 2 * j + slot
            firsts = first_matmul(k, slot)
            previous = pv_prev_ref[slot] if isinstance(j, int) and j == 0 else pv_ref[k - 2]
            second = second_matmul(previous, slot)
            anchors.append([tile(firsts[0], 0), tile(firsts[2], 0), tile(second, 0), tile(second, D_MODEL // 2)])
        for slot in range(2):
            write_weights(gate(2 * j + slot, anchors[slot]), slot)
        return carry

    assert n_sub % 2 == 0
    trip(0, 0)
    lax.fori_loop(1, n_sub // 2, trip, 0)

    @pl.when(e == pl.num_programs(1) - 1)
    def _():
        for slot in range(2):
            second_matmul(pv_ref[n_sub - 2 + slot], slot)
        o_ref[...] = x_ref[...] + mod_ref[0, 5:6, :] * acc_ref[...].T


def _peer(h2t, pu, pv_t, n0, r1, e0, e1, x1, mod, layout):
    t = x1.shape[0]
    tb = PEER_TOKENS
    eb = PEER_IBLOCK * PEER_NKEYS
    n_exp = pu.shape[0]
    n_sub = eb // MXU_DIM
    assert pv_t.shape == (n_exp // MXU_DIM, D_MODEL, MXU_DIM)
    fk = pl.BlockSpec((PEER_HEADS, tb // LANES, PEER_NKEYS, LANES), lambda i, e: (0, i, 0, 0))
    sk = pl.BlockSpec((PEER_HEADS, tb // LANES, PEER_NKEYS // 2, LANES), lambda i, e: (0, i, 0, 0))
    return pl.pallas_call(
        _peer_kernel,
        grid=(t // tb, n_exp // eb),
        in_specs=[pl.BlockSpec((D_MODEL, tb), lambda i, e: (0, i)),
                  pl.BlockSpec((eb, D_MODEL), lambda i, e: (e, 0)),
                  pl.BlockSpec((n_sub, D_MODEL, MXU_DIM), lambda i, e: (e, 0, 0)),
                  pl.BlockSpec((2, D_MODEL, MXU_DIM), lambda i, e: (jnp.maximum(e * (n_sub // 2) - 1, 0), 0, 0)),
                  fk, sk, fk, sk,
                  pl.BlockSpec((tb, D_MODEL), lambda i, e: (i, 0)),
                  _mod_spec(tb, layout)],
        out_specs=pl.BlockSpec((tb, D_MODEL), lambda i, e: (i, 0)),
        out_shape=jax.ShapeDtypeStruct((t, D_MODEL), F32),
        scratch_shapes=[pltpu.VMEM((D_MODEL, tb), F32),
                        pltpu.VMEM((MXU_DIM // 2, tb), jnp.uint32), pltpu.VMEM((MXU_DIM // 2, tb), jnp.uint32),
                        pltpu.VMEM((MXU_DIM, tb), BF16), pltpu.VMEM((MXU_DIM, tb), BF16)],
        compiler_params=pltpu.CompilerParams(dimension_semantics=("arbitrary", "arbitrary"),
                                             vmem_limit_bytes=PEER_VMEM_LIMIT),
        name="peer_dense",
    )(h2t, pu, pv_t, pv_t, n0, r1, e0, e1, x1, mod)


def _merge_cache(kv):
    b, h, l, d = kv.shape
    return kv.transpose(0, 2, 1, 3).reshape(b, l, h * d).astype(BF16)


def kernel(x_prompt, x_sample, cache_na_kv, cache_diff_kv, c, c_ctx, w_mod, b_mod, norm1_g, norm2_g, w_in, conv_w, conv_b, conv_ln_g, conv_ln_b, na_qn_g, na_kn_g, na_rel_bias, diff_qn_g, diff_kn_g, diff_lambda, diff_subln_g, gmlp_ln_g, gmlp_ln_b, gmlp_ws, gmlp_bs, w_out, peer_wq, peer_sub_keys, peer_u, peer_v):
    batch, seq, _ = x_prompt.shape
    dbatch, dseq, _ = x_sample.shape
    t_ctx = batch * seq
    layout = (t_ctx, dseq)
    x = jnp.concatenate([x_prompt.reshape(t_ctx, D_MODEL), x_sample.reshape(dbatch * dseq, D_MODEL)], axis=0)
    cond = jnp.concatenate([c_ctx[None, :], c, jnp.zeros((SUBLANES - 1 - dbatch, D_MODEL), F32)], axis=0)
    mod_all = _modulation(cond, w_mod, b_mod).reshape(DEPTH, SUBLANES, 6, D_MODEL)
    cos, sin = _rope_tables(dseq)
    row = lambda v: v.reshape(1, -1)
    tile = lambda v: jnp.tile(v, GROUP_W // v.shape[0]).reshape(1, GROUP_W)
    na_states, diff_states = [], []
    for l in range(DEPTH):
        lam_init = 0.8 - 0.6 * math.exp(-0.3 * l)
        mod = mod_all[l]
        parts = _in_proj(x, mod, row(norm1_g[l]), w_in[l].astype(BF16), layout)
        conv_o = _conv_module(parts, conv_w[l], row(conv_b[l]), row(conv_ln_g[l]), row(conv_ln_b[l]), t_ctx, dseq)
        gm_bias = jnp.repeat(gmlp_bs[l].T, GROUP_W // N_HEADS, axis=1)
        gm_o = _gmlp(parts, row(gmlp_ln_g[l]), row(gmlp_ln_b[l]), gmlp_ws[l].astype(BF16), gm_bias)
        qn, kn = tile(na_qn_g[l]), tile(na_kn_g[l])
        dqn, dkn, sub = tile(diff_qn_g[l]), tile(diff_kn_g[l]), tile(diff_subln_g[l])
        na_c, df_c, na_kv, df_kv = _ctx_attention(parts, batch, seq, qn, kn, dqn, dkn, sub, diff_lambda[l], lam_init)
        lq, lk, lv, ldq, ldk, ldv = _lat_prep(parts, t_ctx, dbatch, dseq, qn, kn, dqn, dkn, cos, sin)
        na_l = _na_latent(lq, lk, lv, _merge_cache(cache_na_kv[:, l, 0]), _merge_cache(cache_na_kv[:, l, 1]),
                          _na_bias_table(na_rel_bias[l], dseq // GRID_W), dbatch, dseq)
        ldk_t = ldk.reshape(dbatch, dseq, GROUP_W).transpose(0, 2, 1)
        ck_t = _merge_cache(cache_diff_kv[:, l, 0]).transpose(0, 2, 1)
        df_l = _diff_latent(ldq, ldk_t, ldv, ck_t, _merge_cache(cache_diff_kv[:, l, 1]), sub, diff_lambda[l],
                            dbatch, dseq, lam_init)
        x1 = _out_proj(conv_o, na_c, na_l, df_c, df_l, gm_o, x, mod, w_out[l].astype(BF16), layout)
        h2t, n0, r1, e0, e1 = _route(x1, mod, row(norm2_g[l]), peer_wq[l].T.astype(BF16),
                                          peer_sub_keys[l].astype(BF16), layout)
        pv_t = peer_v[l].astype(BF16).reshape(-1, MXU_DIM, D_MODEL).transpose(0, 2, 1)
        x = _peer(h2t, peer_u[l].astype(BF16), pv_t, n0, r1, e0, e1, x1, mod, layout)
        na_states.append(na_kv)
        diff_states.append(df_kv)
    y_prompt = x[:t_ctx].reshape(batch, seq, D_MODEL)
    y_sample = x[t_ctx:].reshape(dbatch, dseq, D_MODEL)
    return (y_prompt, y_sample, jnp.stack(na_states, axis=1), jnp.stack(diff_states, axis=1))
```

```python
import functools
import math

import numpy as np
import jax
import jax.numpy as jnp
from jax import lax
from jax.experimental import pallas as pl
from jax.experimental.pallas import tpu as pltpu

F32 = jnp.float32
BF16 = jnp.bfloat16

D_MODEL = 1024
DEPTH = 2
GRID_W = 64
GROUP_W = 256
HEAD_DIM = 64
N_HEADS = 4
DIFF_SUB = 32
CONV_K = 31
CONV_HALO = 16
NA_WIN_R = 8
NA_WIN_C = 16
ROPE_BASE = 10000.0
CHUNK = 128
PEER_HEADS = 8
PEER_NKEYS = 128
PEER_TOPK = 16
EPS = 1e-6
NEG = -1e30

LANES = 128
SUBLANES = 8
MXU_DIM = 256
VMEM_LIMIT = 48 * 1024 * 1024

SEQ_BLOCK = 256
ROW_BLOCK = 512
ROUTE_BLOCK = 256
PEER_TOKENS = 512
PEER_IBLOCK = 16
PEER_VMEM_LIMIT = 56 * 1024 * 1024
CTX_SEQS_PER_STEP = 2
DIFF_QBLOCK = 128
DIFF_KCHUNK = 512


def _params(*sem):
    return pltpu.CompilerParams(dimension_semantics=sem, vmem_limit_bytes=VMEM_LIMIT)


def _dot(a, b):
    return jnp.dot(a, b, preferred_element_type=F32)


def _dot_nt(a, b):
    return lax.dot_general(a, b, (((1,), (1,)), ((), ())), preferred_element_type=F32)


def _split_dot(a, b):
    a1 = a.astype(BF16)
    r1 = a - a1.astype(F32)
    a2 = r1.astype(BF16)
    a3 = (r1 - a2.astype(F32)).astype(BF16)
    return _dot(a1, b) + _dot(a2, b) + _dot(a3, b)


def _group_ones(n, group):
    r = lax.broadcasted_iota(jnp.int32, (n, n), 0) // group
    c = lax.broadcasted_iota(jnp.int32, (n, n), 1) // group
    return (r == c).astype(BF16)


def _group_rms(x, gain, group):
    ssq = _split_dot(x * x, _group_ones(x.shape[-1], group))
    return x * lax.rsqrt(ssq * (1.0 / group) + EPS) * gain


def _gelu_tanh(x):
    k = -2.0 * math.sqrt(2.0 / math.pi) * math.log2(math.e)
    return x / (1.0 + jnp.exp2(x * (k + (k * 0.044715) * (x * x))))


def _layer_norm(y, g, b):
    mu = jnp.mean(y, axis=-1, keepdims=True)
    yc = y - mu
    return yc * lax.rsqrt(jnp.mean(yc * yc, axis=-1, keepdims=True) + EPS) * g + b


def _mod_spec(rows_per_block, layout):
    t_ctx, lat_seq = layout
    assert t_ctx % rows_per_block == 0 and lat_seq % rows_per_block == 0
    def index(i, *_):
        return (jnp.maximum((i * rows_per_block - t_ctx) // lat_seq + 1, 0), 0, 0)
    return pl.BlockSpec((1, 6, D_MODEL), index)


def _mod_kernel(c_ref, w_ref, b_ref, o_ref):
    c = c_ref[...]
    s = c * jax.nn.sigmoid(c)
    o_ref[0] = jnp.dot(s, w_ref[0], preferred_element_type=F32,
                       precision=lax.Precision.HIGHEST) + b_ref[0]


def _modulation(cond, w_mod, b_mod):
    nb = 4
    cols = 6 * D_MODEL // nb
    return pl.pallas_call(
        _mod_kernel,
        grid=(DEPTH, nb),
        in_specs=[pl.BlockSpec((SUBLANES, D_MODEL), lambda l, j: (0, 0)),
                  pl.BlockSpec((1, D_MODEL, cols), lambda l, j: (l, 0, j)),
                  pl.BlockSpec((1, 1, cols), lambda l, j: (l, 0, j))],
        out_specs=pl.BlockSpec((1, SUBLANES, cols), lambda l, j: (l, 0, j)),
        out_shape=jax.ShapeDtypeStruct((DEPTH, SUBLANES, 6 * D_MODEL), F32),
        name="modulation",
        compiler_params=_params("arbitrary", "arbitrary"),
    )(cond, w_mod, b_mod.reshape(DEPTH, 1, 6 * D_MODEL))


def _in_proj_kernel(x_ref, mod_ref, g_ref, w_ref, o_ref):
    x = x_ref[...]
    y = x * lax.rsqrt(jnp.mean(x * x, axis=-1, keepdims=True) + EPS) * g_ref[...]
    h = y * (1.0 + mod_ref[0, 1:2, :]) + mod_ref[0, 0:1, :]
    o_ref[...] = _dot(h.astype(BF16), w_ref[...])


def _in_proj(x, mod, g, w, layout):
    t = x.shape[0]
    n = w.shape[1]
    return pl.pallas_call(
        _in_proj_kernel,
        grid=(t // ROW_BLOCK,),
        in_specs=[pl.BlockSpec((ROW_BLOCK, D_MODEL), lambda i: (i, 0)),
                  _mod_spec(ROW_BLOCK, layout),
                  pl.BlockSpec((1, D_MODEL), lambda i: (0, 0)),
                  pl.BlockSpec((D_MODEL, n), lambda i: (0, 0))],
        out_specs=pl.BlockSpec((ROW_BLOCK, n), lambda i: (i, 0)),
        out_shape=jax.ShapeDtypeStruct((t, n), F32),
        name="in_proj",
        compiler_params=_params("arbitrary"),
    )(x, mod, g, w)


def _conv_kernel(ac, gc, ap, gp, an, gn, w_ref, b_ref, lg_ref, lb_ref, o_ref, pad_ref, *, ctx_blocks, seq_blocks):
    i = pl.program_id(0)
    is_ctx = i < ctx_blocks
    j = i % seq_blocks
    first = jnp.logical_or(is_ctx, j == 0)
    last = jnp.logical_or(is_ctx, j == seq_blocks - 1)
    yp = ap[...] * jax.nn.sigmoid(gp[...])
    yn = an[...] * jax.nn.sigmoid(gn[...])
    pad_ref[0:CONV_HALO, :] = jnp.where(first, 0.0, yp)
    pad_ref[CONV_HALO:CONV_HALO + SEQ_BLOCK, :] = ac[...] * jax.nn.sigmoid(gc[...])
    pad_ref[CONV_HALO + SEQ_BLOCK:, :] = jnp.where(last, 0.0, yn)
    off = CONV_HALO - CONV_K // 2
    acc = jnp.zeros((SEQ_BLOCK, GROUP_W), F32)
    for k in range(CONV_K):
        acc = acc + pad_ref[off + k:off + k + SEQ_BLOCK, :] * w_ref[k:k + 1, :]
    y = _layer_norm(acc + b_ref[...], lg_ref[...], lb_ref[...])
    o_ref[...] = y * jax.nn.sigmoid(y)


def _conv_module(parts, w, b, lg, lb, ctx_tokens, lat_seq):
    t = parts.shape[0]
    nblk = t // SEQ_BLOCK
    hb = SEQ_BLOCK // CONV_HALO
    last_halo = t // CONV_HALO - 1
    vec = pl.BlockSpec((1, GROUP_W), lambda i: (0, 0))
    kern = functools.partial(_conv_kernel, ctx_blocks=ctx_tokens // SEQ_BLOCK, seq_blocks=lat_seq // SEQ_BLOCK)
    return pl.pallas_call(
        kern,
        grid=(nblk,),
        in_specs=[pl.BlockSpec((SEQ_BLOCK, GROUP_W), lambda i: (i, 0)),
                  pl.BlockSpec((SEQ_BLOCK, GROUP_W), lambda i: (i, 1)),
                  pl.BlockSpec((CONV_HALO, GROUP_W), lambda i: (jnp.maximum(i * hb - 1, 0), 0)),
                  pl.BlockSpec((CONV_HALO, GROUP_W), lambda i: (jnp.maximum(i * hb - 1, 0), 1)),
                  pl.BlockSpec((CONV_HALO, GROUP_W), lambda i: (jnp.minimum((i + 1) * hb, last_halo), 0)),
                  pl.BlockSpec((CONV_HALO, GROUP_W), lambda i: (jnp.minimum((i + 1) * hb, last_halo), 1)),
                  pl.BlockSpec((CONV_K, GROUP_W), lambda i: (0, 0)),
                  vec, vec, vec],
        out_specs=pl.BlockSpec((SEQ_BLOCK, GROUP_W), lambda i: (i, 0)),
        out_shape=jax.ShapeDtypeStruct((t, GROUP_W), F32),
        scratch_shapes=[pltpu.VMEM((SEQ_BLOCK + 2 * CONV_HALO, GROUP_W), F32)],
        name="conv_module",
        compiler_params=_params("arbitrary"),
    )(parts, parts, parts, parts, parts, parts, w, b, lg, lb)


def _gmlp_kernel(u_ref, v_ref, lg_ref, lb_ref, ws_ref, bias_ref, o_ref):
    u = jax.nn.gelu(u_ref[...])
    v = _layer_norm(jax.nn.gelu(v_ref[...]), lg_ref[...], lb_ref[...]).astype(BF16)
    lane_g = lax.broadcasted_iota(jnp.int32, (1, GROUP_W), 1) // (GROUP_W // N_HEADS)
    for ch in range(SEQ_BLOCK // CHUNK):
        rows = slice(ch * CHUNK, (ch + 1) * CHUNK)
        s = bias_ref[...]
        for g in range(N_HEADS):
            s = s + jnp.where(lane_g == g, _dot(ws_ref[g], v[rows]), 0.0)
        o_ref[rows, :] = u[rows] * s


def _gmlp(parts, lg, lb, ws, bias):
    t = parts.shape[0]
    vec = pl.BlockSpec((1, GROUP_W), lambda i: (0, 0))
    return pl.pallas_call(
        _gmlp_kernel,
        grid=(t // SEQ_BLOCK,),
        in_specs=[pl.BlockSpec((SEQ_BLOCK, GROUP_W), lambda i: (i, 8)),
                  pl.BlockSpec((SEQ_BLOCK, GROUP_W), lambda i: (i, 9)),
                  vec, vec,
                  pl.BlockSpec((N_HEADS, CHUNK, CHUNK), lambda i: (0, 0, 0)),
                  pl.BlockSpec((CHUNK, GROUP_W), lambda i: (0, 0))],
        out_specs=pl.BlockSpec((SEQ_BLOCK, GROUP_W), lambda i: (i, 0)),
        out_shape=jax.ShapeDtypeStruct((t, GROUP_W), F32),
        compiler_params=_params("arbitrary"),
        name="gmlp",
    )(parts, parts, lg, lb, ws, bias)


def _lane_group_id(width, group):
    return lax.broadcasted_iota(jnp.int32, (1, width), 1) // group


def _stack_masked(q, group, ids):
    lane = _lane_group_id(q.shape[-1], group)
    return jnp.concatenate([jnp.where(lane == g, q, jnp.zeros_like(q)) for g in ids], axis=0)


def _diff_lambda(dl_ref, lam_init):
    dl = dl_ref[...]
    a = jnp.sum(dl[0:1, :] * dl[1:2, :], axis=-1, keepdims=True)
    b = jnp.sum(dl[2:3, :] * dl[3:4, :], axis=-1, keepdims=True)
    return jnp.exp(a) - jnp.exp(b) + lam_init


def _ctx_attn_kernel(q_ref, k_ref, v_ref, dq_ref, dk_ref, dv_ref, qn_ref, kn_ref, dqn_ref, dkn_ref, sub_ref,
                     dl_ref, na_ref, df_ref, na_kv_ref, df_kv_ref, *, lam_init, seq):
    n = seq
    lane_h = _lane_group_id(GROUP_W, HEAD_DIM)
    lam = _diff_lambda(dl_ref, lam_init)

    def write_state(ref, b, keys, values):
        for h in range(N_HEADS):
            ref[b, 0, h] = keys[:, h * HEAD_DIM:(h + 1) * HEAD_DIM]
            ref[b, 1, h] = values[:, h * HEAD_DIM:(h + 1) * HEAD_DIM]

    def head_lanes(r):
        return functools.reduce(jnp.add, [jnp.where(lane_h == h, r[h * n:(h + 1) * n], 0.0) for h in range(N_HEADS)])

    for b in range(q_ref.shape[0] // n):
        rows = slice(b * n, (b + 1) * n)
        qn = _group_rms(q_ref[rows, :], qn_ref[...], HEAD_DIM)
        kn = _group_rms(k_ref[rows, :], kn_ref[...], HEAD_DIM)
        v = v_ref[rows, :]
        write_state(na_kv_ref, b, kn, v)
        qs = _stack_masked(qn.astype(BF16), HEAD_DIM, range(N_HEADS))
        s = _dot_nt(qs, kn.astype(BF16)) * (HEAD_DIM ** -0.5)
        e = jnp.exp(s - jnp.max(s, axis=-1, keepdims=True))
        r = _dot(e.astype(BF16), v.astype(BF16)) / jnp.sum(e, axis=-1, keepdims=True)
        na_ref[rows, :] = head_lanes(r)
        dq = _group_rms(dq_ref[rows, :], dqn_ref[...], DIFF_SUB)
        dk = _group_rms(dk_ref[rows, :], dkn_ref[...], DIFF_SUB)
        dv = dv_ref[rows, :]
        write_state(df_kv_ref, b, dk, dv)
        qs = _stack_masked(dq.astype(BF16), DIFF_SUB, range(2 * N_HEADS))
        s = _dot_nt(qs, dk.astype(BF16)) * (DIFF_SUB ** -0.5)
        e = jnp.exp(s - jnp.max(s, axis=-1, keepdims=True))
        p = e * (1.0 / jnp.sum(e, axis=-1, keepdims=True))
        a = jnp.concatenate([p[2 * h * n:(2 * h + 1) * n] - lam * p[(2 * h + 1) * n:(2 * h + 2) * n]
                             for h in range(N_HEADS)], axis=0)
        o = head_lanes(_dot(a.astype(BF16), dv.astype(BF16)))
        df_ref[rows, :] = _group_rms(o, sub_ref[...], HEAD_DIM) * (1.0 - lam_init)


def _ctx_attention(parts, batch, seq, qn, kn, dqn, dkn, sub, dl, lam_init):
    t = batch * seq
    per_step = CTX_SEQS_PER_STEP
    assert batch % per_step == 0
    vec = pl.BlockSpec((1, GROUP_W), lambda b: (0, 0))
    col = lambda c: pl.BlockSpec((per_step * seq, GROUP_W), lambda b, c=c: (b, c))
    out = jax.ShapeDtypeStruct((t, GROUP_W), F32)
    ob = pl.BlockSpec((per_step * seq, GROUP_W), lambda b: (b, 0))
    state = jax.ShapeDtypeStruct((batch, 2, N_HEADS, seq, HEAD_DIM), F32)
    sb = pl.BlockSpec((per_step, 2, N_HEADS, seq, HEAD_DIM), lambda b: (b, 0, 0, 0, 0))
    return pl.pallas_call(
        functools.partial(_ctx_attn_kernel, lam_init=lam_init, seq=seq),
        grid=(batch // per_step,),
        in_specs=[col(2), col(3), col(4), col(5), col(6), col(7), vec, vec, vec, vec, vec,
                  pl.BlockSpec((4, DIFF_SUB), lambda b: (0, 0))],
        out_specs=[ob, ob, sb, sb],
        out_shape=[out, out, state, state],
        compiler_params=_params("arbitrary"),
        name="ctx_attention",
    )(parts, parts, parts, parts, parts, parts, qn, kn, dqn, dkn, sub, dl)


def _lat_prep_kernel(q_ref, k_ref, v_ref, dq_ref, dk_ref, dv_ref, qn_ref, kn_ref, dqn_ref, dkn_ref,
                     cos_ref, sin_ref, oq, ok, ov, odq, odk, odv):
    oq[...] = _group_rms(q_ref[...], qn_ref[...], HEAD_DIM).astype(BF16)
    ok[...] = _group_rms(k_ref[...], kn_ref[...], HEAD_DIM).astype(BF16)
    ov[...] = v_ref[...].astype(BF16)
    odv[...] = dv_ref[...].astype(BF16)
    half = DIFF_SUB // 4
    lane = lax.broadcasted_iota(jnp.int32, (1, GROUP_W), 1)
    lower = (lane % (2 * half)) < half
    cos = cos_ref[...]
    sin = sin_ref[...]

    def rope(x):
        partner = jnp.where(lower, pltpu.roll(x, GROUP_W - half, 1), pltpu.roll(x, half, 1))
        return x * cos + partner * sin

    odq[...] = rope(_group_rms(dq_ref[...], dqn_ref[...], DIFF_SUB)).astype(BF16)
    odk[...] = rope(_group_rms(dk_ref[...], dkn_ref[...], DIFF_SUB)).astype(BF16)


def _rope_tables(seq):
    half = DIFF_SUB // 4
    t = jnp.arange(seq)
    rows = (t // GRID_W).astype(F32)
    cols = (t % GRID_W).astype(F32)
    inv = ROPE_BASE ** (-jnp.arange(half, dtype=F32) / half)
    ang_r = rows[:, None] * inv[None, :]
    ang_c = cols[:, None] * inv[None, :]
    cos32 = jnp.concatenate([jnp.cos(ang_r), jnp.cos(ang_r), jnp.cos(ang_c), jnp.cos(ang_c)], axis=-1)
    sin32 = jnp.concatenate([-jnp.sin(ang_r), jnp.sin(ang_r), -jnp.sin(ang_c), jnp.sin(ang_c)], axis=-1)
    reps = GROUP_W // DIFF_SUB
    return jnp.tile(cos32, (1, reps)), jnp.tile(sin32, (1, reps))


def _lat_prep(parts, row0, batch, seq, qn, kn, dqn, dkn, cos, sin):
    t = batch * seq
    nb = seq // SEQ_BLOCK
    r0 = row0 // SEQ_BLOCK
    vec = pl.BlockSpec((1, GROUP_W), lambda i: (0, 0))
    col = lambda c: pl.BlockSpec((SEQ_BLOCK, GROUP_W), lambda i, c=c: (r0 + i, c))
    tab = pl.BlockSpec((SEQ_BLOCK, GROUP_W), lambda i: (i % nb, 0))
    out = jax.ShapeDtypeStruct((t, GROUP_W), BF16)
    ob = pl.BlockSpec((SEQ_BLOCK, GROUP_W), lambda i: (i, 0))
    return pl.pallas_call(
        _lat_prep_kernel,
        grid=(t // SEQ_BLOCK,),
        in_specs=[col(2), col(3), col(4), col(5), col(6), col(7), vec, vec, vec, vec, tab, tab],
        out_specs=[ob] * 6,
        out_shape=[out] * 6,
        compiler_params=_params("arbitrary"),
        name="lat_prep",
    )(parts, parts, parts, parts, parts, parts, qn, kn, dqn, dkn, cos, sin)


def _na_row_start(r, rows):
    return jnp.clip(r - NA_WIN_R // 2, 0, rows - NA_WIN_R)


def _na_lat_kernel(q_ref, k_ref, v_ref, ck_ref, cv_ref, bt_ref, o_ref, *, rows):
    r = pl.program_id(1)
    start = pl.multiple_of(_na_row_start(r, rows) * GRID_W, GRID_W)
    n_loc = NA_WIN_R * GRID_W
    kl = k_ref[pl.ds(start, n_loc), :]
    vl = v_ref[pl.ds(start, n_loc), :]
    qs = _stack_masked(q_ref[...], HEAD_DIM, range(N_HEADS))
    scale = HEAD_DIM ** -0.5
    s_loc = _dot_nt(qs, kl) * scale + bt_ref[0]
    s_ctx = _dot_nt(qs, ck_ref[0]) * scale
    m = jnp.maximum(jnp.max(s_loc, axis=-1, keepdims=True), jnp.max(s_ctx, axis=-1, keepdims=True))
    e_loc = jnp.exp(s_loc - m)
    e_ctx = jnp.exp(s_ctx - m)
    l = jnp.sum(e_loc, axis=-1, keepdims=True) + jnp.sum(e_ctx, axis=-1, keepdims=True)
    res = (_dot(e_loc.astype(BF16), vl) + _dot(e_ctx.astype(BF16), cv_ref[0])) / l
    lane_h = _lane_group_id(GROUP_W, HEAD_DIM)
    o = jnp.zeros((GRID_W, GROUP_W), F32)
    for h in range(N_HEADS):
        o = o + jnp.where(lane_h == h, res[h * GRID_W:(h + 1) * GRID_W], 0.0)
    o_ref[...] = o


def _na_bias_table(rel_bias, rows):
    wr = NA_WIN_R
    c = np.arange(GRID_W)
    c0 = np.clip(c - NA_WIN_C // 2, 0, GRID_W - NA_WIN_C)
    in_win = (c[None, :] >= c0[:, None]) & (c[None, :] < c0[:, None] + NA_WIN_C)
    dc_idx = np.clip(c[None, :] - c[:, None] + NA_WIN_C - 1, 0, 2 * NA_WIN_C - 2)
    off = np.arange(wr)
    dr_idx = (np.arange(wr)[None, :] - off[:, None]) + NA_WIN_R - 1
    rb = rel_bias[:, dr_idx]
    onehot = jnp.asarray(dc_idx[..., None] == np.arange(2 * NA_WIN_C - 1), F32)
    bias = jnp.einsum('howd,qkd->howqk', rb, onehot, precision=lax.Precision.HIGHEST)
    bias = jnp.where(jnp.asarray(in_win)[None, None, None], bias, NEG)
    bias = bias.transpose(1, 0, 3, 2, 4)
    return bias.reshape(wr, N_HEADS * GRID_W, wr * GRID_W)


def _na_latent(qn, kn, v, ck, cv, bias_tab, batch, seq):
    rows = seq // GRID_W
    nctx = ck.shape[1]
    full = pl.BlockSpec((seq, GROUP_W), lambda b, r: (b, 0))
    ctx = pl.BlockSpec((1, nctx, GROUP_W), lambda b, r: (b, 0, 0))
    return pl.pallas_call(
        functools.partial(_na_lat_kernel, rows=rows),
        grid=(batch, rows),
        in_specs=[pl.BlockSpec((GRID_W, GROUP_W), lambda b, r: (b * rows + r, 0)),
                  full, full, ctx, ctx,
                  pl.BlockSpec((1, N_HEADS * GRID_W, NA_WIN_R * GRID_W),
                               lambda b, r: (r - _na_row_start(r, rows), 0, 0))],
        out_specs=pl.BlockSpec((GRID_W, GROUP_W), lambda b, r: (b * rows + r, 0)),
        out_shape=jax.ShapeDtypeStruct((batch * seq, GROUP_W), F32),
        compiler_params=_params("arbitrary", "arbitrary"),
        name="na_latent",
    )(qn, kn, v, ck, cv, bias_tab)


def _lane_fold(fn, acc, x):
    for g in range(x.shape[-1] // LANES):
        acc = fn(acc, x[:, g * LANES:(g + 1) * LANES])
    return acc


def _diff_lat_kernel(q_ref, k_ref, v_ref, ck_ref, cv_ref, sub_ref, dl_ref, o_ref, s_ref, sc_ref, *, lam_init):
    n = q_ref.shape[0]
    n_chunks, rows, chunk = s_ref.shape
    n_sub = rows // n
    lam = _diff_lambda(dl_ref, lam_init)
    qs = _stack_masked(q_ref[...], DIFF_SUB, range(n_sub))
    c2 = (DIFF_SUB ** -0.5) * math.log2(math.e)
    step = pl.program_id(1)
    top = jnp.full((rows, LANES), NEG, F32)
    for j in range(n_chunks):
        s = _dot(qs, k_ref[0, :, j * chunk:(j + 1) * chunk])
        s_ref[(j + step) % n_chunks] = s
        top = _lane_fold(jnp.maximum, top, s)
    s = _dot(qs, ck_ref[0])
    sc_ref[...] = s
    top = _lane_fold(jnp.maximum, top, s)
    m = jnp.max(top, axis=-1, keepdims=True) * c2
    part = jnp.zeros((rows, LANES), F32)
    half = rows // 2
    res = [jnp.zeros((half, GROUP_W), F32)] * 2
    products = []
    for j in range(n_chunks + 1):
        scores = s_ref[(j + step) % n_chunks] if j < n_chunks else sc_ref[...]
        values = v_ref[j * chunk:(j + 1) * chunk, :] if j < n_chunks else cv_ref[0]
        shift = m
        if j >= 2:
            bits = pltpu.bitcast(products[j - 2][:SUBLANES, :LANES], jnp.uint32)
            shift = m + pltpu.bitcast((bits >> 16) >> 16, F32)[:1, :1]
        e = jnp.exp2(scores * c2 - shift)
        part = _lane_fold(jnp.add, part, e)
        eb = e.astype(BF16)
        prods = [_dot(eb[r * half:(r + 1) * half], values) for r in range(2)]
        products.append(prods[0])
        res = [res[r] + prods[r] for r in range(2)]
    res = jnp.concatenate(res, axis=0) * (1.0 / jnp.sum(part, axis=-1, keepdims=True))
    lane_h = _lane_group_id(GROUP_W, HEAD_DIM)
    o = jnp.zeros((n, GROUP_W), F32)
    for h in range(N_HEADS):
        first, second = res[2 * h * n:(2 * h + 1) * n], res[(2 * h + 1) * n:(2 * h + 2) * n]
        o = o + jnp.where(lane_h == h, first - lam * second, 0.0)
    o_ref[...] = _group_rms(o, sub_ref[...], HEAD_DIM) * (1.0 - lam_init)


def _diff_latent(dq, dk_t, dv, ck_t, cv, sub, dl, batch, seq, lam_init):
    nq = seq // DIFF_QBLOCK
    nctx = cv.shape[1]
    full = pl.BlockSpec((seq, GROUP_W), lambda b, i: (b, 0))
    ctx = pl.BlockSpec((1, nctx, GROUP_W), lambda b, i: (b, 0, 0))
    return pl.pallas_call(
        functools.partial(_diff_lat_kernel, lam_init=lam_init),
        grid=(batch, nq),
        in_specs=[pl.BlockSpec((DIFF_QBLOCK, GROUP_W), lambda b, i: (b * nq + i, 0)),
                  pl.BlockSpec((1, GROUP_W, seq), lambda b, i: (b, 0, 0)), full,
                  pl.BlockSpec((1, GROUP_W, nctx), lambda b, i: (b, 0, 0)), ctx,
                  pl.BlockSpec((1, GROUP_W), lambda b, i: (0, 0)),
                  pl.BlockSpec((4, DIFF_SUB), lambda b, i: (0, 0))],
        out_specs=pl.BlockSpec((DIFF_QBLOCK, GROUP_W), lambda b, i: (b * nq + i, 0)),
        out_shape=jax.ShapeDtypeStruct((batch * seq, GROUP_W), F32),
        scratch_shapes=[pltpu.VMEM((seq // DIFF_KCHUNK, 2 * N_HEADS * DIFF_QBLOCK, DIFF_KCHUNK), F32),
                        pltpu.VMEM((2 * N_HEADS * DIFF_QBLOCK, nctx), F32)],
        compiler_params=_params("arbitrary", "arbitrary"),
        name="diff_latent",
    )(dq, dk_t, dv, ck_t, cv, sub, dl)


def _out_proj_kernel(conv_ref, na_c_ref, na_l_ref, df_c_ref, df_l_ref, gm_ref, x_ref, mod_ref, w_ref, o_ref, *,
                     ctx_blocks):
    is_ctx = pl.program_id(0) < ctx_blocks
    na = jnp.where(is_ctx, na_c_ref[...], na_l_ref[...])
    df = jnp.where(is_ctx, df_c_ref[...], df_l_ref[...])
    mixed = jnp.zeros(o_ref.shape, F32)
    for g, part in enumerate((conv_ref[...], na, df, gm_ref[...])):
        mixed = mixed + _dot(part.astype(BF16), w_ref[g * GROUP_W:(g + 1) * GROUP_W, :])
    o_ref[...] = x_ref[...] + mod_ref[0, 2:3, :] * mixed


def _out_proj(conv_o, na_c, na_l, df_c, df_l, gm_o, x, mod, w, layout):
    t = x.shape[0]
    ctx_blocks = layout[0] // ROW_BLOCK
    last_lat = (t - layout[0]) // ROW_BLOCK - 1
    part = pl.BlockSpec((ROW_BLOCK, GROUP_W), lambda i: (i, 0))
    ctx = pl.BlockSpec((ROW_BLOCK, GROUP_W), lambda i: (jnp.minimum(i, ctx_blocks - 1), 0))
    lat = pl.BlockSpec((ROW_BLOCK, GROUP_W), lambda i: (jnp.clip(i - ctx_blocks, 0, last_lat), 0))
    row = pl.BlockSpec((ROW_BLOCK, D_MODEL), lambda i: (i, 0))
    return pl.pallas_call(
        functools.partial(_out_proj_kernel, ctx_blocks=ctx_blocks),
        grid=(t // ROW_BLOCK,),
        in_specs=[part, ctx, lat, ctx, lat, part, row,
                  _mod_spec(ROW_BLOCK, layout),
                  pl.BlockSpec((D_MODEL, D_MODEL), lambda i: (0, 0))],
        out_specs=row,
        out_shape=jax.ShapeDtypeStruct((t, D_MODEL), F32),
        compiler_params=_params("arbitrary"),
        name="out_proj",
    )(conv_o, na_c, na_l, df_c, df_l, gm_o, x, mod, w)


def _sort_desc(vals):
    v = list(vals)
    n = len(v)
    k = 2
    while k <= n:
        j = k // 2
        while j >= 1:
            for i in range(n):
                l = i ^ j
                if l > i:
                    hi, lo = jnp.maximum(v[i], v[l]), jnp.minimum(v[i], v[l])
                    v[i], v[l] = (hi, lo) if (i & k) == 0 else (lo, hi)
            j //= 2
        k *= 2
    return v


def _merge_desc(v):
    v = list(v)
    n = len(v)
    j = n // 2
    while j >= 1:
        for i in range(n):
            l = i ^ j
            if l > i:
                v[i], v[l] = jnp.maximum(v[i], v[l]), jnp.minimum(v[i], v[l])
        j //= 2
    return v


def _top_half(a, b):
    n = len(a)
    return [jnp.maximum(a[i], b[n - 1 - i]) for i in range(n)]


def _top16_keys(s):
    k = PEER_TOPK
    v = _sort_desc([s[a * SUBLANES:(a + 1) * SUBLANES] for a in range(PEER_NKEYS // SUBLANES)])
    shift = SUBLANES // 2
    while shift >= 1:
        rolled = [pltpu.roll(x, shift, 0) for x in v]
        v = _merge_desc(_top_half(v, rolled))
        shift //= 2
    return v[:k]


def _paired_bf16(x):
    hi = pltpu.bitcast(x.astype(BF16).astype(F32), jnp.uint32)
    return hi | (hi >> 16)


def _route_kernel(x_ref, mod_ref, g_ref, wq_ref, sk_ref, h2_ref, n0_ref, r1_ref, e0_ref, e1_ref, s0_ref, s1_ref):
    x = x_ref[...]
    y = x * lax.rsqrt(jnp.mean(x * x, axis=-1, keepdims=True) + EPS) * g_ref[...]
    h2 = y * (1.0 + mod_ref[0, 4:5, :]) + mod_ref[0, 3:4, :]
    h2t = h2.T.astype(BF16)
    h2_ref[...] = h2t
    qt = _dot(wq_ref[...], h2t)
    for tile in range(x.shape[0] // LANES):
        _route_tile(qt[:, tile * LANES:(tile + 1) * LANES], tile, sk_ref, n0_ref, r1_ref, e0_ref, e1_ref,
                    s0_ref, s1_ref)


def _route_tile(qt, tile, sk_ref, n0_ref, r1_ref, e0_ref, e1_ref, s0_ref, s1_ref):
    tb = LANES
    sub = lax.broadcasted_iota(jnp.int32, (SUBLANES, tb), 0)
    k = PEER_TOPK
    tops = [[jnp.zeros((SUBLANES, tb), F32)] * k, [jnp.zeros((SUBLANES, tb), F32)] * k]
    for h in range(PEER_HEADS):
        for p in range(2):
            base = (2 * h + p) * PEER_NKEYS
            s = _dot(sk_ref[p], qt[base:base + PEER_NKEYS].astype(BF16))
            (s0_ref if p == 0 else s1_ref)[h] = s
            top = _top16_keys(s)
            tops[p] = [jnp.where(sub == h, top[a], tops[p][a]) for a in range(k)]
    cand = [tops[0][a] + tops[1][b] for a in range(k) for b in range(k) if (a + 1) * (b + 1) <= k]
    pad = [jnp.full((SUBLANES, tb), NEG, F32)] * (4 * k - len(cand))
    groups = [_sort_desc((cand + pad)[g * k:(g + 1) * k]) for g in range(4)]
    best = _top_half(_merge_desc(_top_half(groups[0], groups[1])), _merge_desc(_top_half(groups[2], groups[3])))
    thr = functools.reduce(jnp.minimum, best)
    m = tops[0][0] + tops[1][0]
    z = functools.reduce(jnp.add, [jnp.where(c >= thr, jnp.exp(c - m), 0.0) for c in cand])
    zinv = 1.0 / z
    counts, first = [], 0
    for a in range(k):
        n_b = k // (a + 1)
        counts.append(functools.reduce(jnp.add, [jnp.where(c >= thr, 1.0, 0.0) for c in cand[first:first + n_b]]))
        first += n_b
    for h in range(PEER_HEADS):
        s0 = s0_ref[h]
        s1 = s1_ref[h]
        n0 = jnp.zeros_like(s0)
        r1 = jnp.full_like(s1, float(k))
        for a in reversed(range(k)):
            n0 = jnp.where(s0 >= tops[0][a][h:h + 1], counts[a][h:h + 1], n0)
            r1 = jnp.where(s1 >= tops[1][a][h:h + 1], float(a), r1)
        n0_ref[h, tile] = _paired_bf16(n0)
        r1_ref[h, tile] = pltpu.bitcast(r1.astype(BF16), jnp.uint32)
        e0_ref[h, tile] = _paired_bf16(jnp.exp(s0 - tops[0][0][h:h + 1]) * zinv[h:h + 1])
        e1_ref[h, tile] = pltpu.bitcast(jnp.exp(s1 - tops[1][0][h:h + 1]).astype(BF16), jnp.uint32)


def _route(x1, mod, g, wq_t, sk, layout):
    t = x1.shape[0]
    tb = ROUTE_BLOCK
    tiles = t // LANES
    first_key = jax.ShapeDtypeStruct((PEER_HEADS, tiles, PEER_NKEYS, LANES), jnp.uint32)
    second_key = jax.ShapeDtypeStruct((PEER_HEADS, tiles, PEER_NKEYS // 2, LANES), jnp.uint32)
    fkb = pl.BlockSpec((PEER_HEADS, tb // LANES, PEER_NKEYS, LANES), lambda i: (0, i, 0, 0))
    skb = pl.BlockSpec((PEER_HEADS, tb // LANES, PEER_NKEYS // 2, LANES), lambda i: (0, i, 0, 0))
    scores = pltpu.VMEM((PEER_HEADS, PEER_NKEYS, LANES), F32)
    return pl.pallas_call(
        _route_kernel,
        grid=(t // tb,),
        in_specs=[pl.BlockSpec((tb, D_MODEL), lambda i: (i, 0)),
                  _mod_spec(tb, layout),
                  pl.BlockSpec((1, D_MODEL), lambda i: (0, 0)),
                  pl.BlockSpec(wq_t.shape, lambda i: (0, 0)),
                  pl.BlockSpec(sk.shape, lambda i: (0, 0, 0))],
        out_specs=[pl.BlockSpec((D_MODEL, tb), lambda i: (0, i)), fkb, skb, fkb, skb],
        out_shape=[jax.ShapeDtypeStruct((D_MODEL, t), BF16), first_key, second_key, first_key, second_key],
        scratch_shapes=[scores, scores],
        compiler_params=_params("arbitrary"),
        name="peer_route",
    )(x1, mod, g, wq_t, sk)


def _rows_as_bf16(rows):
    tile = pltpu.bitcast(rows, BF16)
    return jnp.concatenate([tile] * (PEER_NKEYS // tile.shape[0]), axis=0)


def _peer_kernel(h2_ref, pu_ref, pv_ref, pv_prev_ref, n0_ref, r1_ref, e0_ref, e1_ref, x_ref, mod_ref, o_ref,
                 acc_ref, a0_ref, a1_ref, w0_ref, w1_ref):
    e = pl.program_id(1)
    a_refs = (a0_ref, a1_ref)
    w_refs = (w0_ref, w1_ref)

    @pl.when(e == 0)
    def _():
        acc_ref[...] = jnp.zeros_like(acc_ref)
        for w_ref in w_refs:
            w_ref[...] = jnp.zeros_like(w_ref)

    tb = h2_ref.shape[1]
    n_sub = pv_ref.shape[0]
    sub_keys = MXU_DIM // PEER_NKEYS
    n_tiles = tb // LANES

    def first_matmul(k, slot):
        start = k * MXU_DIM if isinstance(k, int) else pl.multiple_of(k * MXU_DIM, MXU_DIM)
        parts = [_dot(pu_ref[pl.ds(start, MXU_DIM), j * MXU_DIM:(j + 1) * MXU_DIM],
                      h2_ref[j * MXU_DIM:(j + 1) * MXU_DIM, :]) for j in range(D_MODEL // MXU_DIM)]
        a = functools.reduce(jnp.add, parts)
        a_refs[slot][...] = pltpu.bitcast(_gelu_tanh(a).astype(BF16), jnp.uint32)
        return parts

    def second_matmul(values, slot):
        part = _dot(values, w_refs[slot][...])
        acc_ref[...] += part
        return part

    def gate(k, anchors):
        key = e * PEER_IBLOCK + sub_keys * k
        out = []
        for c in range(n_tiles):
            cols = slice(c * LANES, (c + 1) * LANES)
            words = pltpu.bitcast(anchors[c], jnp.uint32)
            gates = [_rows_as_bf16((words >> 16) >> 16)] * sub_keys
            for h in range(PEER_HEADS):
                r1 = pltpu.bitcast(r1_ref[h, c], BF16)
                e1 = pltpu.bitcast(e1_ref[h, c], BF16)
                for d in range(sub_keys):
                    row = pl.ds(key + d, SUBLANES, stride=0)
                    n0_d = _rows_as_bf16(n0_ref[h, c, row, :])
                    e0_d = _rows_as_bf16(e0_ref[h, c, row, :])
                    gates[d] = gates[d] + jnp.where(r1 < n0_d, e1 * e0_d, jnp.zeros_like(e1))
            out.append(gates)
        return out

    def write_weights(gates, slot):
        for c in range(n_tiles):
            cols = slice(c * LANES, (c + 1) * LANES)
            for d in range(sub_keys):
                rows = slice(d * PEER_NKEYS, (d + 1) * PEER_NKEYS)
                words = slice(d * PEER_NKEYS // 2, (d + 1) * PEER_NKEYS // 2)
                w_refs[slot][rows, cols] = pltpu.bitcast(a_refs[slot][words, cols], BF16) * gates[c][d]

    def trip(j, carry):
        tile = lambda x, row: x[row:row + SUBLANES, :LANES]
        assert n_tiles == 4
        anchors = []
        for slot in range(2):
            k = 2 * j + slot
            firsts = first_matmul(k, slot)
            previous = pv_prev_ref[slot] if isinstance(j, int) and j == 0 else pv_ref[k - 2]
            second = second_matmul(previous, slot)
            anchors.append([tile(firsts[0], 0), tile(firsts[2], 0), tile(second, 0), tile(second, D_MODEL // 2)])
        for slot in range(2):
            write_weights(gate(2 * j + slot, anchors[slot]), slot)
        return carry

    assert n_sub % 2 == 0
    trip(0, 0)
    lax.fori_loop(1, n_sub // 2, trip, 0)

    @pl.when(e == pl.num_programs(1) - 1)
    def _():
        for slot in range(2):
            second_matmul(pv_ref[n_sub - 2 + slot], slot)
        o_ref[...] = x_ref[...] + mod_ref[0, 5:6, :] * acc_ref[...].T


def _peer(h2t, pu, pv_t, n0, r1, e0, e1, x1, mod, layout):
    t = x1.shape[0]
    tb = PEER_TOKENS
    eb = PEER_IBLOCK * PEER_NKEYS
    n_exp = pu.shape[0]
    n_sub = eb // MXU_DIM
    assert pv_t.shape == (n_exp // MXU_DIM, D_MODEL, MXU_DIM)
    fk = pl.BlockSpec((PEER_HEADS, tb // LANES, PEER_NKEYS, LANES), lambda i, e: (0, i, 0, 0))
    sk = pl.BlockSpec((PEER_HEADS, tb // LANES, PEER_NKEYS // 2, LANES), lambda i, e: (0, i, 0, 0))
    return pl.pallas_call(
        _peer_kernel,
        grid=(t // tb, n_exp // eb),
        in_specs=[pl.BlockSpec((D_MODEL, tb), lambda i, e: (0, i)),
                  pl.BlockSpec((eb, D_MODEL), lambda i, e: (e, 0)),
                  pl.BlockSpec((n_sub, D_MODEL, MXU_DIM), lambda i, e: (e, 0, 0)),
                  pl.BlockSpec((2, D_MODEL, MXU_DIM), lambda i, e: (jnp.maximum(e * (n_sub // 2) - 1, 0), 0, 0)),
                  fk, sk, fk, sk,
                  pl.BlockSpec((tb, D_MODEL), lambda i, e: (i, 0)),
                  _mod_spec(tb, layout)],
        out_specs=pl.BlockSpec((tb, D_MODEL), lambda i, e: (i, 0)),
        out_shape=jax.ShapeDtypeStruct((t, D_MODEL), F32),
        scratch_shapes=[pltpu.VMEM((D_MODEL, tb), F32),
                        pltpu.VMEM((MXU_DIM // 2, tb), jnp.uint32), pltpu.VMEM((MXU_DIM // 2, tb), jnp.uint32),
                        pltpu.VMEM((MXU_DIM, tb), BF16), pltpu.VMEM((MXU_DIM, tb), BF16)],
        compiler_params=pltpu.CompilerParams(dimension_semantics=("arbitrary", "arbitrary"),
                                             vmem_limit_bytes=PEER_VMEM_LIMIT),
        name="peer_dense",
    )(h2t, pu, pv_t, pv_t, n0, r1, e0, e1, x1, mod)


def _merge_cache(kv):
    b, h, l, d = kv.shape
    return kv.transpose(0, 2, 1, 3).reshape(b, l, h * d).astype(BF16)


def kernel(x_prompt, x_sample, cache_na_kv, cache_diff_kv, c, c_ctx, w_mod, b_mod, norm1_g, norm2_g, w_in, conv_w, conv_b, conv_ln_g, conv_ln_b, na_qn_g, na_kn_g, na_rel_bias, diff_qn_g, diff_kn_g, diff_lambda, diff_subln_g, gmlp_ln_g, gmlp_ln_b, gmlp_ws, gmlp_bs, w_out, peer_wq, peer_sub_keys, peer_u, peer_v):
    batch, seq, _ = x_prompt.shape
    dbatch, dseq, _ = x_sample.shape
    t_ctx = batch * seq
    layout = (t_ctx, dseq)
    x = jnp.concatenate([x_prompt.reshape(t_ctx, D_MODEL), x_sample.reshape(dbatch * dseq, D_MODEL)], axis=0)
    cond = jnp.concatenate([c_ctx[None, :], c, jnp.zeros((SUBLANES - 1 - dbatch, D_MODEL), F32)], axis=0)
    mod_all = _modulation(cond, w_mod, b_mod).reshape(DEPTH, SUBLANES, 6, D_MODEL)
    cos, sin = _rope_tables(dseq)
    row = lambda v: v.reshape(1, -1)
    tile = lambda v: jnp.tile(v, GROUP_W // v.shape[0]).reshape(1, GROUP_W)
    na_states, diff_states = [], []
    for l in range(DEPTH):
        lam_init = 0.8 - 0.6 * math.exp(-0.3 * l)
        mod = mod_all[l]
        parts = _in_proj(x, mod, row(norm1_g[l]), w_in[l].astype(BF16), layout)
        conv_o = _conv_module(parts, conv_w[l], row(conv_b[l]), row(conv_ln_g[l]), row(conv_ln_b[l]), t_ctx, dseq)
        gm_bias = jnp.repeat(gmlp_bs[l].T, GROUP_W // N_HEADS, axis=1)
        gm_o = _gmlp(parts, row(gmlp_ln_g[l]), row(gmlp_ln_b[l]), gmlp_ws[l].astype(BF16), gm_bias)
        qn, kn = tile(na_qn_g[l]), tile(na_kn_g[l])
        dqn, dkn, sub = tile(diff_qn_g[l]), tile(diff_kn_g[l]), tile(diff_subln_g[l])
        na_c, df_c, na_kv, df_kv = _ctx_attention(parts, batch, seq, qn, kn, dqn, dkn, sub, diff_lambda[l], lam_init)
        lq, lk, lv, ldq, ldk, ldv = _lat_prep(parts, t_ctx, dbatch, dseq, qn, kn, dqn, dkn, cos, sin)
        na_l = _na_latent(lq, lk, lv, _merge_cache(cache_na_kv[:, l, 0]), _merge_cache(cache_na_kv[:, l, 1]),
                          _na_bias_table(na_rel_bias[l], dseq // GRID_W), dbatch, dseq)
        ldk_t = ldk.reshape(dbatch, dseq, GROUP_W).transpose(0, 2, 1)
        ck_t = _merge_cache(cache_diff_kv[:, l, 0]).transpose(0, 2, 1)
        df_l = _diff_latent(ldq, ldk_t, ldv, ck_t, _merge_cache(cache_diff_kv[:, l, 1]), sub, diff_lambda[l],
                            dbatch, dseq, lam_init)
        x1 = _out_proj(conv_o, na_c, na_l, df_c, df_l, gm_o, x, mod, w_out[l].astype(BF16), layout)
        h2t, n0, r1, e0, e1 = _route(x1, mod, row(norm2_g[l]), peer_wq[l].T.astype(BF16),
                                          peer_sub_keys[l].astype(BF16), layout)
        pv_t = peer_v[l].astype(BF16).reshape(-1, MXU_DIM, D_MODEL).transpose(0, 2, 1)
        x = _peer(h2t, peer_u[l].astype(BF16), pv_t, n0, r1, e0, e1, x1, mod, layout)
        na_states.append(na_kv)
        diff_states.append(df_kv)
    y_prompt = x[:t_ctx].reshape(batch, seq, D_MODEL)
    y_sample = x[t_ctx:].reshape(dbatch, dseq, D_MODEL)
    return (y_prompt, y_sample, jnp.stack(na_states, axis=1), jnp.stack(diff_states, axis=1))
```

```python
import functools
import math

import numpy as np
import jax
import jax.numpy as jnp
from jax import lax
from jax.experimental import pallas as pl
from jax.experimental.pallas import tpu as pltpu

F32 = jnp.float32
BF16 = jnp.bfloat16

D_MODEL = 1024
DEPTH = 2
GRID_W = 64
GROUP_W = 256
HEAD_DIM = 64
N_HEADS = 4
DIFF_SUB = 32
CONV_K = 31
CONV_HALO = 16
NA_WIN_R = 8
NA_WIN_C = 16
ROPE_BASE = 10000.0
CHUNK = 128
PEER_HEADS = 8
PEER_NKEYS = 128
PEER_TOPK = 16
EPS = 1e-6
NEG = -1e30

LANES = 128
SUBLANES = 8
MXU_DIM = 256
VMEM_LIMIT = 48 * 1024 * 1024

SEQ_BLOCK = 256
ROW_BLOCK = 512
GMLP_BLOCK = 512
ROUTE_BLOCK = 256
PEER_TOKENS = 512
PEER_IBLOCK = 16
PEER_VMEM_LIMIT = 56 * 1024 * 1024
CTX_SEQS_PER_STEP = 2
NA_ROWS_PER_STEP = 4
DIFF_QBLOCK = 128
DIFF_KCHUNK = 512


def _params(*sem):
    return pltpu.CompilerParams(dimension_semantics=sem, vmem_limit_bytes=VMEM_LIMIT)


def _dot(a, b):
    return jnp.dot(a, b, preferred_element_type=F32)


def _dot_nt(a, b):
    return lax.dot_general(a, b, (((1,), (1,)), ((), ())), preferred_element_type=F32)


def _split_dot(a, b):
    a1 = a.astype(BF16)
    r1 = a - a1.astype(F32)
    a2 = r1.astype(BF16)
    a3 = (r1 - a2.astype(F32)).astype(BF16)
    return _dot(a1, b) + _dot(a2, b) + _dot(a3, b)


def _group_ones(n, group):
    r = lax.broadcasted_iota(jnp.int32, (n, n), 0) // group
    c = lax.broadcasted_iota(jnp.int32, (n, n), 1) // group
    return (r == c).astype(BF16)


def _group_rms(x, gain, group):
    ssq = _split_dot(x * x, _group_ones(x.shape[-1], group))
    return x * lax.rsqrt(ssq * (1.0 / group) + EPS) * gain


def _gelu_tanh(x):
    k = -2.0 * math.sqrt(2.0 / math.pi) * math.log2(math.e)
    return x / (1.0 + jnp.exp2(x * (k + (k * 0.044715) * (x * x))))


def _layer_norm(y, g, b):
    mu = jnp.mean(y, axis=-1, keepdims=True)
    yc = y - mu
    return yc * lax.rsqrt(jnp.mean(yc * yc, axis=-1, keepdims=True) + EPS) * g + b


def _mod_spec(rows_per_block, layout):
    t_ctx, lat_seq = layout
    assert t_ctx % rows_per_block == 0 and lat_seq % rows_per_block == 0
    def index(i, *_):
        return (jnp.maximum((i * rows_per_block - t_ctx) // lat_seq + 1, 0), 0, 0)
    return pl.BlockSpec((1, 6, D_MODEL), index)


def _mod_kernel(c_ref, w_ref, b_ref, o_ref):
    c = c_ref[...]
    s = c * jax.nn.sigmoid(c)
    o_ref[0] = jnp.dot(s, w_ref[0], preferred_element_type=F32,
                       precision=lax.Precision.HIGHEST) + b_ref[0]


def _modulation(cond, w_mod, b_mod):
    nb = 4
    cols = 6 * D_MODEL // nb
    return pl.pallas_call(
        _mod_kernel,
        grid=(DEPTH, nb),
        in_specs=[pl.BlockSpec((SUBLANES, D_MODEL), lambda l, j: (0, 0)),
                  pl.BlockSpec((1, D_MODEL, cols), lambda l, j: (l, 0, j)),
                  pl.BlockSpec((1, 1, cols), lambda l, j: (l, 0, j))],
        out_specs=pl.BlockSpec((1, SUBLANES, cols), lambda l, j: (l, 0, j)),
        out_shape=jax.ShapeDtypeStruct((DEPTH, SUBLANES, 6 * D_MODEL), F32),
        name="modulation",
        compiler_params=_params("arbitrary", "arbitrary"),
    )(cond, w_mod, b_mod.reshape(DEPTH, 1, 6 * D_MODEL))


def _in_proj_kernel(x_ref, mod_ref, g_ref, w_ref, o_ref):
    x = x_ref[...]
    y = x * lax.rsqrt(jnp.mean(x * x, axis=-1, keepdims=True) + EPS) * g_ref[...]
    h = y * (1.0 + mod_ref[0, 1:2, :]) + mod_ref[0, 0:1, :]
    o_ref[...] = _dot(h.astype(BF16), w_ref[...])


def _in_proj(x, mod, g, w, layout):
    t = x.shape[0]
    n = w.shape[1]
    return pl.pallas_call(
        _in_proj_kernel,
        grid=(t // ROW_BLOCK,),
        in_specs=[pl.BlockSpec((ROW_BLOCK, D_MODEL), lambda i: (i, 0)),
                  _mod_spec(ROW_BLOCK, layout),
                  pl.BlockSpec((1, D_MODEL), lambda i: (0, 0)),
                  pl.BlockSpec((D_MODEL, n), lambda i: (0, 0))],
        out_specs=pl.BlockSpec((ROW_BLOCK, n), lambda i: (i, 0)),
        out_shape=jax.ShapeDtypeStruct((t, n), F32),
        name="in_proj",
        compiler_params=_params("arbitrary"),
    )(x, mod, g, w)


def _conv_kernel(ac, gc, ap, gp, an, gn, w_ref, b_ref, lg_ref, lb_ref, o_ref, pad_ref, *, ctx_blocks, seq_blocks):
    i = pl.program_id(0)
    is_ctx = i < ctx_blocks
    j = i % seq_blocks
    first = jnp.logical_or(is_ctx, j == 0)
    last = jnp.logical_or(is_ctx, j == seq_blocks - 1)
    yp = ap[...] * jax.nn.sigmoid(gp[...])
    yn = an[...] * jax.nn.sigmoid(gn[...])
    pad_ref[0:CONV_HALO, :] = jnp.where(first, 0.0, yp)
    pad_ref[CONV_HALO:CONV_HALO + SEQ_BLOCK, :] = ac[...] * jax.nn.sigmoid(gc[...])
    pad_ref[CONV_HALO + SEQ_BLOCK:, :] = jnp.where(last, 0.0, yn)
    off = CONV_HALO - CONV_K // 2
    acc = jnp.zeros((SEQ_BLOCK, GROUP_W), F32)
    for k in range(CONV_K):
        acc = acc + pad_ref[off + k:off + k + SEQ_BLOCK, :] * w_ref[k:k + 1, :]
    y = _layer_norm(acc + b_ref[...], lg_ref[...], lb_ref[...])
    o_ref[...] = y * jax.nn.sigmoid(y)


def _conv_module(parts, w, b, lg, lb, ctx_tokens, lat_seq):
    t = parts.shape[0]
    nblk = t // SEQ_BLOCK
    hb = SEQ_BLOCK // CONV_HALO
    last_halo = t // CONV_HALO - 1
    vec = pl.BlockSpec((1, GROUP_W), lambda i: (0, 0))
    kern = functools.partial(_conv_kernel, ctx_blocks=ctx_tokens // SEQ_BLOCK, seq_blocks=lat_seq // SEQ_BLOCK)
    return pl.pallas_call(
        kern,
        grid=(nblk,),
        in_specs=[pl.BlockSpec((SEQ_BLOCK, GROUP_W), lambda i: (i, 0)),
                  pl.BlockSpec((SEQ_BLOCK, GROUP_W), lambda i: (i, 1)),
                  pl.BlockSpec((CONV_HALO, GROUP_W), lambda i: (jnp.maximum(i * hb - 1, 0), 0)),
                  pl.BlockSpec((CONV_HALO, GROUP_W), lambda i: (jnp.maximum(i * hb - 1, 0), 1)),
                  pl.BlockSpec((CONV_HALO, GROUP_W), lambda i: (jnp.minimum((i + 1) * hb, last_halo), 0)),
                  pl.BlockSpec((CONV_HALO, GROUP_W), lambda i: (jnp.minimum((i + 1) * hb, last_halo), 1)),
                  pl.BlockSpec((CONV_K, GROUP_W), lambda i: (0, 0)),
                  vec, vec, vec],
        out_specs=pl.BlockSpec((SEQ_BLOCK, GROUP_W), lambda i: (i, 0)),
        out_shape=jax.ShapeDtypeStruct((t, GROUP_W), F32),
        scratch_shapes=[pltpu.VMEM((SEQ_BLOCK + 2 * CONV_HALO, GROUP_W), F32)],
        name="conv_module",
        compiler_params=_params("arbitrary"),
    )(parts, parts, parts, parts, parts, parts, w, b, lg, lb)


def _gmlp_kernel(u_ref, v_ref, lg_ref, lb_ref, ws_ref, bias_ref, o_ref):
    u = jax.nn.gelu(u_ref[...])
    v = _layer_norm(jax.nn.gelu(v_ref[...]), lg_ref[...], lb_ref[...]).astype(BF16)
    lane_g = lax.broadcasted_iota(jnp.int32, (1, GROUP_W), 1) // (GROUP_W // N_HEADS)
    for ch in range(u_ref.shape[0] // CHUNK):
        rows = slice(ch * CHUNK, (ch + 1) * CHUNK)
        s = bias_ref[...]
        for g in range(N_HEADS):
            s = s + jnp.where(lane_g == g, _dot(ws_ref[g], v[rows]), 0.0)
        o_ref[rows, :] = u[rows] * s


def _gmlp(parts, lg, lb, ws, bias):
    t = parts.shape[0]
    vec = pl.BlockSpec((1, GROUP_W), lambda i: (0, 0))
    return pl.pallas_call(
        _gmlp_kernel,
        grid=(t // GMLP_BLOCK,),
        in_specs=[pl.BlockSpec((GMLP_BLOCK, GROUP_W), lambda i: (i, 8)),
                  pl.BlockSpec((GMLP_BLOCK, GROUP_W), lambda i: (i, 9)),
                  vec, vec,
                  pl.BlockSpec((N_HEADS, CHUNK, CHUNK), lambda i: (0, 0, 0)),
                  pl.BlockSpec((CHUNK, GROUP_W), lambda i: (0, 0))],
        out_specs=pl.BlockSpec((GMLP_BLOCK, GROUP_W), lambda i: (i, 0)),
        out_shape=jax.ShapeDtypeStruct((t, GROUP_W), F32),
        compiler_params=_params("arbitrary"),
        name="gmlp",
    )(parts, parts, lg, lb, ws, bias)


def _lane_group_id(width, group):
    return lax.broadcasted_iota(jnp.int32, (1, width), 1) // group


def _stack_masked(q, group, ids):
    lane = _lane_group_id(q.shape[-1], group)
    return jnp.concatenate([jnp.where(lane == g, q, jnp.zeros_like(q)) for g in ids], axis=0)


def _diff_lambda(dl_ref, lam_init):
    dl = dl_ref[...]
    a = jnp.sum(dl[0:1, :] * dl[1:2, :], axis=-1, keepdims=True)
    b = jnp.sum(dl[2:3, :] * dl[3:4, :], axis=-1, keepdims=True)
    return jnp.exp(a) - jnp.exp(b) + lam_init


def _ctx_attn_kernel(q_ref, k_ref, v_ref, dq_ref, dk_ref, dv_ref, qn_ref, kn_ref, dqn_ref, dkn_ref, sub_ref,
                     dl_ref, na_ref, df_ref, na_kv_ref, df_kv_ref, *, lam_init, seq):
    n = seq
    lane_h = _lane_group_id(GROUP_W, HEAD_DIM)
    lam = _diff_lambda(dl_ref, lam_init)

    def write_state(ref, b, keys, values):
        for h in range(N_HEADS):
            ref[b, 0, h] = keys[:, h * HEAD_DIM:(h + 1) * HEAD_DIM]
            ref[b, 1, h] = values[:, h * HEAD_DIM:(h + 1) * HEAD_DIM]

    def head_lanes(r):
        return functools.reduce(jnp.add, [jnp.where(lane_h == h, r[h * n:(h + 1) * n], 0.0) for h in range(N_HEADS)])

    for b in range(q_ref.shape[0] // n):
        rows = slice(b * n, (b + 1) * n)
        qn = _group_rms(q_ref[rows, :], qn_ref[...], HEAD_DIM)
        kn = _group_rms(k_ref[rows, :], kn_ref[...], HEAD_DIM)
        v = v_ref[rows, :]
        write_state(na_kv_ref, b, kn, v)
        qs = _stack_masked(qn.astype(BF16), HEAD_DIM, range(N_HEADS))
        s = _dot_nt(qs, kn.astype(BF16)) * (HEAD_DIM ** -0.5)
        e = jnp.exp(s - jnp.max(s, axis=-1, keepdims=True))
        r = _dot(e.astype(BF16), v.astype(BF16)) / jnp.sum(e, axis=-1, keepdims=True)
        na_ref[rows, :] = head_lanes(r)
        dq = _group_rms(dq_ref[rows, :], dqn_ref[...], DIFF_SUB)
        dk = _group_rms(dk_ref[rows, :], dkn_ref[...], DIFF_SUB)
        dv = dv_ref[rows, :]
        write_state(df_kv_ref, b, dk, dv)
        qs = _stack_masked(dq.astype(BF16), DIFF_SUB, range(2 * N_HEADS))
        s = _dot_nt(qs, dk.astype(BF16)) * (DIFF_SUB ** -0.5)
        e = jnp.exp(s - jnp.max(s, axis=-1, keepdims=True))
        p = e * (1.0 / jnp.sum(e, axis=-1, keepdims=True))
        a = jnp.concatenate([p[2 * h * n:(2 * h + 1) * n] - lam * p[(2 * h + 1) * n:(2 * h + 2) * n]
                             for h in range(N_HEADS)], axis=0)
        o = head_lanes(_dot(a.astype(BF16), dv.astype(BF16)))
        df_ref[rows, :] = _group_rms(o, sub_ref[...], HEAD_DIM) * (1.0 - lam_init)


def _ctx_attention(parts, batch, seq, qn, kn, dqn, dkn, sub, dl, lam_init):
    t = batch * seq
    per_step = CTX_SEQS_PER_STEP
    assert batch % per_step == 0
    vec = pl.BlockSpec((1, GROUP_W), lambda b: (0, 0))
    col = lambda c: pl.BlockSpec((per_step * seq, GROUP_W), lambda b, c=c: (b, c))
    out = jax.ShapeDtypeStruct((t, GROUP_W), F32)
    ob = pl.BlockSpec((per_step * seq, GROUP_W), lambda b: (b, 0))
    state = jax.ShapeDtypeStruct((batch, 2, N_HEADS, seq, HEAD_DIM), F32)
    sb = pl.BlockSpec((per_step, 2, N_HEADS, seq, HEAD_DIM), lambda b: (b, 0, 0, 0, 0))
    return pl.pallas_call(
        functools.partial(_ctx_attn_kernel, lam_init=lam_init, seq=seq),
        grid=(batch // per_step,),
        in_specs=[col(2), col(3), col(4), col(5), col(6), col(7), vec, vec, vec, vec, vec,
                  pl.BlockSpec((4, DIFF_SUB), lambda b: (0, 0))],
        out_specs=[ob, ob, sb, sb],
        out_shape=[out, out, state, state],
        compiler_params=_params("arbitrary"),
        name="ctx_attention",
    )(parts, parts, parts, parts, parts, parts, qn, kn, dqn, dkn, sub, dl)


def _lat_prep_kernel(q_ref, k_ref, v_ref, dq_ref, dk_ref, dv_ref, qn_ref, kn_ref, dqn_ref, dkn_ref,
                     cos_ref, sin_ref, oq, ok, ov, odq, odk, odv):
    oq[...] = _group_rms(q_ref[...], qn_ref[...], HEAD_DIM).astype(BF16)
    ok[...] = _group_rms(k_ref[...], kn_ref[...], HEAD_DIM).astype(BF16)
    ov[...] = v_ref[...].astype(BF16)
    odv[...] = dv_ref[...].astype(BF16)
    half = DIFF_SUB // 4
    lane = lax.broadcasted_iota(jnp.int32, (1, GROUP_W), 1)
    lower = (lane % (2 * half)) < half
    cos = cos_ref[...]
    sin = sin_ref[...]

    def rope(x):
        partner = jnp.where(lower, pltpu.roll(x, GROUP_W - half, 1), pltpu.roll(x, half, 1))
        return x * cos + partner * sin

    odq[...] = rope(_group_rms(dq_ref[...], dqn_ref[...], DIFF_SUB)).astype(BF16)
    odk[...] = rope(_group_rms(dk_ref[...], dkn_ref[...], DIFF_SUB)).astype(BF16)


def _rope_tables(seq):
    half = DIFF_SUB // 4
    t = jnp.arange(seq)
    rows = (t // GRID_W).astype(F32)
    cols = (t % GRID_W).astype(F32)
    inv = ROPE_BASE ** (-jnp.arange(half, dtype=F32) / half)
    ang_r = rows[:, None] * inv[None, :]
    ang_c = cols[:, None] * inv[None, :]
    cos32 = jnp.concatenate([jnp.cos(ang_r), jnp.cos(ang_r), jnp.cos(ang_c), jnp.cos(ang_c)], axis=-1)
    sin32 = jnp.concatenate([-jnp.sin(ang_r), jnp.sin(ang_r), -jnp.sin(ang_c), jnp.sin(ang_c)], axis=-1)
    reps = GROUP_W // DIFF_SUB
    return jnp.tile(cos32, (1, reps)), jnp.tile(sin32, (1, reps))


def _lat_prep(parts, row0, batch, seq, qn, kn, dqn, dkn, cos, sin):
    t = batch * seq
    nb = seq // SEQ_BLOCK
    r0 = row0 // SEQ_BLOCK
    vec = pl.BlockSpec((1, GROUP_W), lambda i: (0, 0))
    col = lambda c: pl.BlockSpec((SEQ_BLOCK, GROUP_W), lambda i, c=c: (r0 + i, c))
    tab = pl.BlockSpec((SEQ_BLOCK, GROUP_W), lambda i: (i % nb, 0))
    out = jax.ShapeDtypeStruct((t, GROUP_W), BF16)
    ob = pl.BlockSpec((SEQ_BLOCK, GROUP_W), lambda i: (i, 0))
    return pl.pallas_call(
        _lat_prep_kernel,
        grid=(t // SEQ_BLOCK,),
        in_specs=[col(2), col(3), col(4), col(5), col(6), col(7), vec, vec, vec, vec, tab, tab],
        out_specs=[ob] * 6,
        out_shape=[out] * 6,
        compiler_params=_params("arbitrary"),
        name="lat_prep",
    )(parts, parts, parts, parts, parts, parts, qn, kn, dqn, dkn, cos, sin)


def _na_row_start(r, rows):
    return jnp.clip(r - NA_WIN_R // 2, 0, rows - NA_WIN_R)


def _na_lat_kernel(q_ref, k_ref, v_ref, ck_ref, cv_ref, bt_ref, o_ref, *, rows):
    n_loc = NA_WIN_R * GRID_W
    scale = HEAD_DIM ** -0.5
    lane_h = _lane_group_id(GROUP_W, HEAD_DIM)
    for i in range(q_ref.shape[0] // GRID_W):
        r = pl.program_id(1) * (q_ref.shape[0] // GRID_W) + i
        first_row = _na_row_start(r, rows)
        start = pl.multiple_of(first_row * GRID_W, GRID_W)
        kl = k_ref[pl.ds(start, n_loc), :]
        vl = v_ref[pl.ds(start, n_loc), :]
        qs = _stack_masked(q_ref[i * GRID_W:(i + 1) * GRID_W, :], HEAD_DIM, range(N_HEADS))
        s_loc = _dot_nt(qs, kl) * scale + bt_ref[r - first_row]
        s_ctx = _dot_nt(qs, ck_ref[0]) * scale
        m = jnp.maximum(jnp.max(s_loc, axis=-1, keepdims=True), jnp.max(s_ctx, axis=-1, keepdims=True))
        e_loc = jnp.exp(s_loc - m)
        e_ctx = jnp.exp(s_ctx - m)
        l = jnp.sum(e_loc, axis=-1, keepdims=True) + jnp.sum(e_ctx, axis=-1, keepdims=True)
        res = (_dot(e_loc.astype(BF16), vl) + _dot(e_ctx.astype(BF16), cv_ref[0])) / l
        o = jnp.zeros((GRID_W, GROUP_W), F32)
        for h in range(N_HEADS):
            o = o + jnp.where(lane_h == h, res[h * GRID_W:(h + 1) * GRID_W], 0.0)
        o_ref[i * GRID_W:(i + 1) * GRID_W, :] = o


def _na_bias_table(rel_bias, rows):
    wr = NA_WIN_R
    c = np.arange(GRID_W)
    c0 = np.clip(c - NA_WIN_C // 2, 0, GRID_W - NA_WIN_C)
    in_win = (c[None, :] >= c0[:, None]) & (c[None, :] < c0[:, None] + NA_WIN_C)
    dc_idx = np.clip(c[None, :] - c[:, None] + NA_WIN_C - 1, 0, 2 * NA_WIN_C - 2)
    off = np.arange(wr)
    dr_idx = (np.arange(wr)[None, :] - off[:, None]) + NA_WIN_R - 1
    rb = rel_bias[:, dr_idx]
    onehot = jnp.asarray(dc_idx[..., None] == np.arange(2 * NA_WIN_C - 1), F32)
    bias = jnp.einsum('howd,qkd->howqk', rb, onehot, precision=lax.Precision.HIGHEST)
    bias = jnp.where(jnp.asarray(in_win)[None, None, None], bias, NEG)
    bias = bias.transpose(1, 0, 3, 2, 4)
    return bias.reshape(wr, N_HEADS * GRID_W, wr * GRID_W)


def _na_latent(qn, kn, v, ck, cv, bias_tab, batch, seq):
    rows = seq // GRID_W
    nctx = ck.shape[1]
    full = pl.BlockSpec((seq, GROUP_W), lambda b, r: (b, 0))
    ctx = pl.BlockSpec((1, nctx, GROUP_W), lambda b, r: (b, 0, 0))
    steps = rows // NA_ROWS_PER_STEP
    qb = NA_ROWS_PER_STEP * GRID_W
    return pl.pallas_call(
        functools.partial(_na_lat_kernel, rows=rows),
        grid=(batch, steps),
        in_specs=[pl.BlockSpec((qb, GROUP_W), lambda b, r: (b * steps + r, 0)),
                  full, full, ctx, ctx,
                  pl.BlockSpec(bias_tab.shape, lambda b, r: (0, 0, 0))],
        out_specs=pl.BlockSpec((qb, GROUP_W), lambda b, r: (b * steps + r, 0)),
        out_shape=jax.ShapeDtypeStruct((batch * seq, GROUP_W), F32),
        compiler_params=_params("arbitrary", "arbitrary"),
        name="na_latent",
    )(qn, kn, v, ck, cv, bias_tab)


def _lane_fold(fn, acc, x):
    for g in range(x.shape[-1] // LANES):
        acc = fn(acc, x[:, g * LANES:(g + 1) * LANES])
    return acc


def _diff_lat_kernel(q_ref, k_ref, v_ref, ck_ref, cv_ref, sub_ref, dl_ref, o_ref, s_ref, sc_ref, *, lam_init):
    n = q_ref.shape[0]
    n_chunks, rows, chunk = s_ref.shape
    n_sub = rows // n
    lam = _diff_lambda(dl_ref, lam_init)
    qs = _stack_masked(q_ref[...], DIFF_SUB, range(n_sub))
    c2 = (DIFF_SUB ** -0.5) * math.log2(math.e)
    step = pl.program_id(1)
    top = jnp.full((rows, LANES), NEG, F32)
    for j in range(n_chunks):
        s = _dot(qs, k_ref[0, :, j * chunk:(j + 1) * chunk])
        s_ref[(j + step) % n_chunks] = s
        top = _lane_fold(jnp.maximum, top, s)
    s = _dot(qs, ck_ref[0])
    sc_ref[...] = s
    top = _lane_fold(jnp.maximum, top, s)
    m = jnp.max(top, axis=-1, keepdims=True) * c2
    part = jnp.zeros((rows, LANES), F32)
    half = rows // 2
    res = [jnp.zeros((half, GROUP_W), F32)] * 2
    products = []
    for j in range(n_chunks + 1):
        scores = s_ref[(j + step) % n_chunks] if j < n_chunks else sc_ref[...]
        values = v_ref[j * chunk:(j + 1) * chunk, :] if j < n_chunks else cv_ref[0]
        shift = m
        if j >= 2:
            bits = pltpu.bitcast(products[j - 2][:SUBLANES, :LANES], jnp.uint32)
            shift = m + pltpu.bitcast((bits >> 16) >> 16, F32)[:1, :1]
        e = jnp.exp2(scores * c2 - shift)
        part = _lane_fold(jnp.add, part, e)
        eb = e.astype(BF16)
        prods = [_dot(eb[r * half:(r + 1) * half], values) for r in range(2)]
        products.append(prods[0])
        res = [res[r] + prods[r] for r in range(2)]
    res = jnp.concatenate(res, axis=0) * (1.0 / jnp.sum(part, axis=-1, keepdims=True))
    lane_h = _lane_group_id(GROUP_W, HEAD_DIM)
    o = jnp.zeros((n, GROUP_W), F32)
    for h in range(N_HEADS):
        first, second = res[2 * h * n:(2 * h + 1) * n], res[(2 * h + 1) * n:(2 * h + 2) * n]
        o = o + jnp.where(lane_h == h, first - lam * second, 0.0)
    o_ref[...] = _group_rms(o, sub_ref[...], HEAD_DIM) * (1.0 - lam_init)


def _diff_latent(dq, dk_t, dv, ck_t, cv, sub, dl, batch, seq, lam_init):
    nq = seq // DIFF_QBLOCK
    nctx = cv.shape[1]
    full = pl.BlockSpec((seq, GROUP_W), lambda b, i: (b, 0))
    ctx = pl.BlockSpec((1, nctx, GROUP_W), lambda b, i: (b, 0, 0))
    return pl.pallas_call(
        functools.partial(_diff_lat_kernel, lam_init=lam_init),
        grid=(batch, nq),
        in_specs=[pl.BlockSpec((DIFF_QBLOCK, GROUP_W), lambda b, i: (b * nq + i, 0)),
                  pl.BlockSpec((1, GROUP_W, seq), lambda b, i: (b, 0, 0)), full,
                  pl.BlockSpec((1, GROUP_W, nctx), lambda b, i: (b, 0, 0)), ctx,
                  pl.BlockSpec((1, GROUP_W), lambda b, i: (0, 0)),
                  pl.BlockSpec((4, DIFF_SUB), lambda b, i: (0, 0))],
        out_specs=pl.BlockSpec((DIFF_QBLOCK, GROUP_W), lambda b, i: (b * nq + i, 0)),
        out_shape=jax.ShapeDtypeStruct((batch * seq, GROUP_W), F32),
        scratch_shapes=[pltpu.VMEM((seq // DIFF_KCHUNK, 2 * N_HEADS * DIFF_QBLOCK, DIFF_KCHUNK), F32),
                        pltpu.VMEM((2 * N_HEADS * DIFF_QBLOCK, nctx), F32)],
        compiler_params=_params("arbitrary", "arbitrary"),
        name="diff_latent",
    )(dq, dk_t, dv, ck_t, cv, sub, dl)


def _out_proj_kernel(conv_ref, na_c_ref, na_l_ref, df_c_ref, df_l_ref, gm_ref, x_ref, mod_ref, w_ref, o_ref, *,
                     ctx_blocks):
    is_ctx = pl.program_id(0) < ctx_blocks
    na = jnp.where(is_ctx, na_c_ref[...], na_l_ref[...])
    df = jnp.where(is_ctx, df_c_ref[...], df_l_ref[...])
    mixed = jnp.zeros(o_ref.shape, F32)
    for g, part in enumerate((conv_ref[...], na, df, gm_ref[...])):
        mixed = mixed + _dot(part.astype(BF16), w_ref[g * GROUP_W:(g + 1) * GROUP_W, :])
    o_ref[...] = x_ref[...] + mod_ref[0, 2:3, :] * mixed


def _out_proj(conv_o, na_c, na_l, df_c, df_l, gm_o, x, mod, w, layout):
    t = x.shape[0]
    ctx_blocks = layout[0] // ROW_BLOCK
    last_lat = (t - layout[0]) // ROW_BLOCK - 1
    part = pl.BlockSpec((ROW_BLOCK, GROUP_W), lambda i: (i, 0))
    ctx = pl.BlockSpec((ROW_BLOCK, GROUP_W), lambda i: (jnp.minimum(i, ctx_blocks - 1), 0))
    lat = pl.BlockSpec((ROW_BLOCK, GROUP_W), lambda i: (jnp.clip(i - ctx_blocks, 0, last_lat), 0))
    row = pl.BlockSpec((ROW_BLOCK, D_MODEL), lambda i: (i, 0))
    return pl.pallas_call(
        functools.partial(_out_proj_kernel, ctx_blocks=ctx_blocks),
        grid=(t // ROW_BLOCK,),
        in_specs=[part, ctx, lat, ctx, lat, part, row,
                  _mod_spec(ROW_BLOCK, layout),
                  pl.BlockSpec((D_MODEL, D_MODEL), lambda i: (0, 0))],
        out_specs=row,
        out_shape=jax.ShapeDtypeStruct((t, D_MODEL), F32),
        compiler_params=_params("arbitrary"),
        name="out_proj",
    )(conv_o, na_c, na_l, df_c, df_l, gm_o, x, mod, w)


def _sort_desc(vals):
    v = list(vals)
    n = len(v)
    k = 2
    while k <= n:
        j = k // 2
        while j >= 1:
            for i in range(n):
                l = i ^ j
                if l > i:
                    hi, lo = jnp.maximum(v[i], v[l]), jnp.minimum(v[i], v[l])
                    v[i], v[l] = (hi, lo) if (i & k) == 0 else (lo, hi)
            j //= 2
        k *= 2
    return v


def _merge_desc(v):
    v = list(v)
    n = len(v)
    j = n // 2
    while j >= 1:
        for i in range(n):
            l = i ^ j
            if l > i:
                v[i], v[l] = jnp.maximum(v[i], v[l]), jnp.minimum(v[i], v[l])
        j //= 2
    return v


def _top_half(a, b):
    n = len(a)
    return [jnp.maximum(a[i], b[n - 1 - i]) for i in range(n)]


def _top16_keys(s):
    k = PEER_TOPK
    v = _sort_desc([s[a * SUBLANES:(a + 1) * SUBLANES] for a in range(PEER_NKEYS // SUBLANES)])
    shift = SUBLANES // 2
    while shift >= 1:
        rolled = [pltpu.roll(x, shift, 0) for x in v]
        v = _merge_desc(_top_half(v, rolled))
        shift //= 2
    return v[:k]


def _paired_bf16(x):
    hi = pltpu.bitcast(x.astype(BF16).astype(F32), jnp.uint32)
    return hi | (hi >> 16)


def _route_kernel(x_ref, mod_ref, g_ref, wq_ref, sk_ref, h2_ref, n0_ref, r1_ref, e0_ref, e1_ref, s0_ref, s1_ref):
    x = x_ref[...]
    y = x * lax.rsqrt(jnp.mean(x * x, axis=-1, keepdims=True) + EPS) * g_ref[...]
    h2 = y * (1.0 + mod_ref[0, 4:5, :]) + mod_ref[0, 3:4, :]
    h2t = h2.T.astype(BF16)
    h2_ref[...] = h2t
    qt = _dot(wq_ref[...], h2t)
    for tile in range(x.shape[0] // LANES):
        _route_tile(qt[:, tile * LANES:(tile + 1) * LANES], tile, sk_ref, n0_ref, r1_ref, e0_ref, e1_ref,
                    s0_ref, s1_ref)


def _route_tile(qt, tile, sk_ref, n0_ref, r1_ref, e0_ref, e1_ref, s0_ref, s1_ref):
    tb = LANES
    sub = lax.broadcasted_iota(jnp.int32, (SUBLANES, tb), 0)
    k = PEER_TOPK
    tops = [[jnp.zeros((SUBLANES, tb), F32)] * k, [jnp.zeros((SUBLANES, tb), F32)] * k]
    for h in range(PEER_HEADS):
        for p in range(2):
            base = (2 * h + p) * PEER_NKEYS
            s = _dot(sk_ref[p], qt[base:base + PEER_NKEYS].astype(BF16))
            (s0_ref if p == 0 else s1_ref)[h] = s
            top = _top16_keys(s)
            tops[p] = [jnp.where(sub == h, top[a], tops[p][a]) for a in range(k)]
    cand = [tops[0][a] + tops[1][b] for a in range(k) for b in range(k) if (a + 1) * (b + 1) <= k]
    pad = [jnp.full((SUBLANES, tb), NEG, F32)] * (4 * k - len(cand))
    groups = [_sort_desc((cand + pad)[g * k:(g + 1) * k]) for g in range(4)]
    best = _top_half(_merge_desc(_top_half(groups[0], groups[1])), _merge_desc(_top_half(groups[2], groups[3])))
    thr = functools.reduce(jnp.minimum, best)
    m = tops[0][0] + tops[1][0]
    z = functools.reduce(jnp.add, [jnp.where(c >= thr, jnp.exp(c - m), 0.0) for c in cand])
    zinv = 1.0 / z
    counts, first = [], 0
    for a in range(k):
        n_b = k // (a + 1)
        counts.append(functools.reduce(jnp.add, [jnp.where(c >= thr, 1.0, 0.0) for c in cand[first:first + n_b]]))
        first += n_b
    for h in range(PEER_HEADS):
        s0 = s0_ref[h]
        s1 = s1_ref[h]
        n0 = jnp.zeros_like(s0)
        r1 = jnp.full_like(s1, float(k))
        for a in reversed(range(k)):
            n0 = jnp.where(s0 >= tops[0][a][h:h + 1], counts[a][h:h + 1], n0)
            r1 = jnp.where(s1 >= tops[1][a][h:h + 1], float(a), r1)
        n0_ref[h, tile] = _paired_bf16(n0)
        r1_ref[h, tile] = pltpu.bitcast(r1.astype(BF16), jnp.uint32)
        e0_ref[h, tile] = _paired_bf16(jnp.exp(s0 - tops[0][0][h:h + 1]) * zinv[h:h + 1])
        e1_ref[h, tile] = pltpu.bitcast(jnp.exp(s1 - tops[1][0][h:h + 1]).astype(BF16), jnp.uint32)


def _route(x1, mod, g, wq_t, sk, layout):
    t = x1.shape[0]
    tb = ROUTE_BLOCK
    tiles = t // LANES
    first_key = jax.ShapeDtypeStruct((PEER_HEADS, tiles, PEER_NKEYS, LANES), jnp.uint32)
    second_key = jax.ShapeDtypeStruct((PEER_HEADS, tiles, PEER_NKEYS // 2, LANES), jnp.uint32)
    fkb = pl.BlockSpec((PEER_HEADS, tb // LANES, PEER_NKEYS, LANES), lambda i: (0, i, 0, 0))
    skb = pl.BlockSpec((PEER_HEADS, tb // LANES, PEER_NKEYS // 2, LANES), lambda i: (0, i, 0, 0))
    scores = pltpu.VMEM((PEER_HEADS, PEER_NKEYS, LANES), F32)
    return pl.pallas_call(
        _route_kernel,
        grid=(t // tb,),
        in_specs=[pl.BlockSpec((tb, D_MODEL), lambda i: (i, 0)),
                  _mod_spec(tb, layout),
                  pl.BlockSpec((1, D_MODEL), lambda i: (0, 0)),
                  pl.BlockSpec(wq_t.shape, lambda i: (0, 0)),
                  pl.BlockSpec(sk.shape, lambda i: (0, 0, 0))],
        out_specs=[pl.BlockSpec((D_MODEL, tb), lambda i: (0, i)), fkb, skb, fkb, skb],
        out_shape=[jax.ShapeDtypeStruct((D_MODEL, t), BF16), first_key, second_key, first_key, second_key],
        scratch_shapes=[scores, scores],
        compiler_params=_params("arbitrary"),
        name="peer_route",
    )(x1, mod, g, wq_t, sk)


def _rows_as_bf16(rows):
    tile = pltpu.bitcast(rows, BF16)
    return jnp.concatenate([tile] * (PEER_NKEYS // tile.shape[0]), axis=0)


def _peer_kernel(h2_ref, pu_ref, pv_ref, pv_prev_ref, n0_ref, r1_ref, e0_ref, e1_ref, x_ref, mod_ref, o_ref,
                 acc_ref, a0_ref, a1_ref, w0_ref, w1_ref):
    e = pl.program_id(1)
    a_refs = (a0_ref, a1_ref)
    w_refs = (w0_ref, w1_ref)

    @pl.when(e == 0)
    def _():
        acc_ref[...] = jnp.zeros_like(acc_ref)
        for w_ref in w_refs:
            w_ref[...] = jnp.zeros_like(w_ref)

    tb = h2_ref.shape[1]
    n_sub = pv_ref.shape[0]
    sub_keys = MXU_DIM // PEER_NKEYS
    n_tiles = tb // LANES

    def first_matmul(k, slot):
        start = k * MXU_DIM if isinstance(k, int) else pl.multiple_of(k * MXU_DIM, MXU_DIM)
        parts = [_dot(pu_ref[pl.ds(start, MXU_DIM), j * MXU_DIM:(j + 1) * MXU_DIM],
                      h2_ref[j * MXU_DIM:(j + 1) * MXU_DIM, :]) for j in range(D_MODEL // MXU_DIM)]
        a = functools.reduce(jnp.add, parts)
        a_refs[slot][...] = pltpu.bitcast(_gelu_tanh(a).astype(BF16), jnp.uint32)
        return parts

    def second_matmul(values, slot):
        part = _dot(values, w_refs[slot][...])
        acc_ref[...] += part
        return part

    def gate(k, anchors):
        key = e * PEER_IBLOCK + sub_keys * k
        out = []
        for c in range(n_tiles):
            cols = slice(c * LANES, (c + 1) * LANES)
            words = pltpu.bitcast(anchors[c], jnp.uint32)
            gates = [_rows_as_bf16((words >> 16) >> 16)] * sub_keys
            for h in range(PEER_HEADS):
                r1 = pltpu.bitcast(r1_ref[h, c], BF16)
                e1 = pltpu.bitcast(e1_ref[h, c], BF16)
                for d in range(sub_keys):
                    row = pl.ds(key + d, SUBLANES, stride=0)
                    n0_d = _rows_as_bf16(n0_ref[h, c, row, :])
                    e0_d = _rows_as_bf16(e0_ref[h, c, row, :])
                    gates[d] = gates[d] + jnp.where(r1 < n0_d, e1 * e0_d, jnp.zeros_like(e1))
            out.append(gates)
        return out

    def write_weights(gates, slot):
        for c in range(n_tiles):
            cols = slice(c * LANES, (c + 1) * LANES)
            for d in range(sub_keys):
                rows = slice(d * PEER_NKEYS, (d + 1) * PEER_NKEYS)
                words = slice(d * PEER_NKEYS // 2, (d + 1) * PEER_NKEYS // 2)
                w_refs[slot][rows, cols] = pltpu.bitcast(a_refs[slot][words, cols], BF16) * gates[c][d]

    def trip(j, carry):
        tile = lambda x, row: x[row:row + SUBLANES, :LANES]
        assert n_tiles == 4
        anchors = []
        for slot in range(2):
            k = 2 * j + slot
            firsts = first_matmul(k, slot)
            previous = pv_prev_ref[slot] if isinstance(j, int) and j == 0 else pv_ref[k - 2]
            second = second_matmul(previous, slot)
            anchors.append([tile(firsts[0], 0), tile(firsts[2], 0), tile(second, 0), tile(second, D_MODEL // 2)])
        for slot in range(2):
            write_weights(gate(2 * j + slot, anchors[slot]), slot)
        return carry

    assert n_sub % 2 == 0
    trip(0, 0)
    lax.fori_loop(1, n_sub // 2, trip, 0)

    @pl.when(e == pl.num_programs(1) - 1)
    def _():
        for slot in range(2):
            second_matmul(pv_ref[n_sub - 2 + slot], slot)
        o_ref[...] = x_ref[...] + mod_ref[0, 5:6, :] * acc_ref[...].T


def _peer(h2t, pu, pv_t, n0, r1, e0, e1, x1, mod, layout):
    t = x1.shape[0]
    tb = PEER_TOKENS
    eb = PEER_IBLOCK * PEER_NKEYS
    n_exp = pu.shape[0]
    n_sub = eb // MXU_DIM
    assert pv_t.shape == (n_exp // MXU_DIM, D_MODEL, MXU_DIM)
    fk = pl.BlockSpec((PEER_HEADS, tb // LANES, PEER_NKEYS, LANES), lambda i, e: (0, i, 0, 0))
    sk = pl.BlockSpec((PEER_HEADS, tb // LANES, PEER_NKEYS // 2, LANES), lambda i, e: (0, i, 0, 0))
    return pl.pallas_call(
        _peer_kernel,
        grid=(t // tb, n_exp // eb),
        in_specs=[pl.BlockSpec((D_MODEL, tb), lambda i, e: (0, i)),
                  pl.BlockSpec((eb, D_MODEL), lambda i, e: (e, 0)),
                  pl.BlockSpec((n_sub, D_MODEL, MXU_DIM), lambda i, e: (e, 0, 0)),
                  pl.BlockSpec((2, D_MODEL, MXU_DIM), lambda i, e: (jnp.maximum(e * (n_sub // 2) - 1, 0), 0, 0)),
                  fk, sk, fk, sk,
                  pl.BlockSpec((tb, D_MODEL), lambda i, e: (i, 0)),
                  _mod_spec(tb, layout)],
        out_specs=pl.BlockSpec((tb, D_MODEL), lambda i, e: (i, 0)),
        out_shape=jax.ShapeDtypeStruct((t, D_MODEL), F32),
        scratch_shapes=[pltpu.VMEM((D_MODEL, tb), F32),
                        pltpu.VMEM((MXU_DIM // 2, tb), jnp.uint32), pltpu.VMEM((MXU_DIM // 2, tb), jnp.uint32),
                        pltpu.VMEM((MXU_DIM, tb), BF16), pltpu.VMEM((MXU_DIM, tb), BF16)],
        compiler_params=pltpu.CompilerParams(dimension_semantics=("arbitrary", "arbitrary"),
                                             vmem_limit_bytes=PEER_VMEM_LIMIT),
        name="peer_dense",
    )(h2t, pu, pv_t, pv_t, n0, r1, e0, e1, x1, mod)


def _merge_cache(kv):
    b, h, l, d = kv.shape
    return kv.transpose(0, 2, 1, 3).reshape(b, l, h * d).astype(BF16)


def kernel(x_prompt, x_sample, cache_na_kv, cache_diff_kv, c, c_ctx, w_mod, b_mod, norm1_g, norm2_g, w_in, conv_w, conv_b, conv_ln_g, conv_ln_b, na_qn_g, na_kn_g, na_rel_bias, diff_qn_g, diff_kn_g, diff_lambda, diff_subln_g, gmlp_ln_g, gmlp_ln_b, gmlp_ws, gmlp_bs, w_out, peer_wq, peer_sub_keys, peer_u, peer_v):
    batch, seq, _ = x_prompt.shape
    dbatch, dseq, _ = x_sample.shape
    t_ctx = batch * seq
    layout = (t_ctx, dseq)
    x = jnp.concatenate([x_prompt.reshape(t_ctx, D_MODEL), x_sample.reshape(dbatch * dseq, D_MODEL)], axis=0)
    cond = jnp.concatenate([c_ctx[None, :], c, jnp.zeros((SUBLANES - 1 - dbatch, D_MODEL), F32)], axis=0)
    mod_all = _modulation(cond, w_mod, b_mod).reshape(DEPTH, SUBLANES, 6, D_MODEL)
    cos, sin = _rope_tables(dseq)
    row = lambda v: v.reshape(1, -1)
    tile = lambda v: jnp.tile(v, GROUP_W // v.shape[0]).reshape(1, GROUP_W)
    na_states, diff_states = [], []
    for l in range(DEPTH):
        lam_init = 0.8 - 0.6 * math.exp(-0.3 * l)
        mod = mod_all[l]
        parts = _in_proj(x, mod, row(norm1_g[l]), w_in[l].astype(BF16), layout)
        conv_o = _conv_module(parts, conv_w[l], row(conv_b[l]), row(conv_ln_g[l]), row(conv_ln_b[l]), t_ctx, dseq)
        gm_bias = jnp.repeat(gmlp_bs[l].T, GROUP_W // N_HEADS, axis=1)
        gm_o = _gmlp(parts, row(gmlp_ln_g[l]), row(gmlp_ln_b[l]), gmlp_ws[l].astype(BF16), gm_bias)
        qn, kn = tile(na_qn_g[l]), tile(na_kn_g[l])
        dqn, dkn, sub = tile(diff_qn_g[l]), tile(diff_kn_g[l]), tile(diff_subln_g[l])
        na_c, df_c, na_kv, df_kv = _ctx_attention(parts, batch, seq, qn, kn, dqn, dkn, sub, diff_lambda[l], lam_init)
        lq, lk, lv, ldq, ldk, ldv = _lat_prep(parts, t_ctx, dbatch, dseq, qn, kn, dqn, dkn, cos, sin)
        na_l = _na_latent(lq, lk, lv, _merge_cache(cache_na_kv[:, l, 0]), _merge_cache(cache_na_kv[:, l, 1]),
                          _na_bias_table(na_rel_bias[l], dseq // GRID_W), dbatch, dseq)
        ldk_t = ldk.reshape(dbatch, dseq, GROUP_W).transpose(0, 2, 1)
        ck_t = _merge_cache(cache_diff_kv[:, l, 0]).transpose(0, 2, 1)
        df_l = _diff_latent(ldq, ldk_t, ldv, ck_t, _merge_cache(cache_diff_kv[:, l, 1]), sub, diff_lambda[l],
                            dbatch, dseq, lam_init)
        x1 = _out_proj(conv_o, na_c, na_l, df_c, df_l, gm_o, x, mod, w_out[l].astype(BF16), layout)
        h2t, n0, r1, e0, e1 = _route(x1, mod, row(norm2_g[l]), peer_wq[l].T.astype(BF16),
                                          peer_sub_keys[l].astype(BF16), layout)
        pv_t = peer_v[l].astype(BF16).reshape(-1, MXU_DIM, D_MODEL).transpose(0, 2, 1)
        x = _peer(h2t, peer_u[l].astype(BF16), pv_t, n0, r1, e0, e1, x1, mod, layout)
        na_states.append(na_kv)
        diff_states.append(df_kv)
    y_prompt = x[:t_ctx].reshape(batch, seq, D_MODEL)
    y_sample = x[t_ctx:].reshape(dbatch, dseq, D_MODEL)
    return (y_prompt, y_sample, jnp.stack(na_states, axis=1), jnp.stack(diff_states, axis=1))
```

```python
import functools
import math

import numpy as np
import jax
import jax.numpy as jnp
from jax import lax
from jax.experimental import pallas as pl
from jax.experimental.pallas import tpu as pltpu

F32 = jnp.float32
BF16 = jnp.bfloat16

D_MODEL = 1024
DEPTH = 2
GRID_W = 64
GROUP_W = 256
HEAD_DIM = 64
N_HEADS = 4
DIFF_SUB = 32
CONV_K = 31
CONV_HALO = 16
NA_WIN_R = 8
NA_WIN_C = 16
ROPE_BASE = 10000.0
CHUNK = 128
PEER_HEADS = 8
PEER_NKEYS = 128
PEER_TOPK = 16
EPS = 1e-6
NEG = -1e30

LANES = 128
SUBLANES = 8
MXU_DIM = 256
VMEM_LIMIT = 48 * 1024 * 1024

SEQ_BLOCK = 256
ROW_BLOCK = 512
GMLP_BLOCK = 512
ROUTE_BLOCK = 256
PEER_TOKENS = 512
PEER_IBLOCK = 16
PEER_VMEM_LIMIT = 56 * 1024 * 1024
CTX_SEQS_PER_STEP = 2
NA_ROWS_PER_STEP = 4
DIFF_QBLOCK = 128
DIFF_KCHUNK = 512


def _params(*sem):
    return pltpu.CompilerParams(dimension_semantics=sem, vmem_limit_bytes=VMEM_LIMIT)


def _dot(a, b):
    return jnp.dot(a, b, preferred_element_type=F32)


def _dot_nt(a, b):
    return lax.dot_general(a, b, (((1,), (1,)), ((), ())), preferred_element_type=F32)


def _split_dot(a, b):
    a1 = a.astype(BF16)
    r1 = a - a1.astype(F32)
    a2 = r1.astype(BF16)
    a3 = (r1 - a2.astype(F32)).astype(BF16)
    return _dot(a1, b) + _dot(a2, b) + _dot(a3, b)


def _group_ones(n, group):
    r = lax.broadcasted_iota(jnp.int32, (n, n), 0) // group
    c = lax.broadcasted_iota(jnp.int32, (n, n), 1) // group
    return (r == c).astype(BF16)


def _group_rms(x, gain, group):
    ssq = _split_dot(x * x, _group_ones(x.shape[-1], group))
    return x * lax.rsqrt(ssq * (1.0 / group) + EPS) * gain


def _gelu_tanh(x):
    k = -2.0 * math.sqrt(2.0 / math.pi) * math.log2(math.e)
    return x / (1.0 + jnp.exp2(x * (k + (k * 0.044715) * (x * x))))


def _layer_norm(y, g, b):
    mu = jnp.mean(y, axis=-1, keepdims=True)
    yc = y - mu
    return yc * lax.rsqrt(jnp.mean(yc * yc, axis=-1, keepdims=True) + EPS) * g + b


def _mod_spec(rows_per_block, layout):
    t_ctx, lat_seq = layout
    assert t_ctx % rows_per_block == 0 and lat_seq % rows_per_block == 0
    def index(i, *_):
        return (jnp.maximum((i * rows_per_block - t_ctx) // lat_seq + 1, 0), 0, 0)
    return pl.BlockSpec((1, 6, D_MODEL), index)


def _mod_kernel(c_ref, w_ref, b_ref, o_ref):
    c = c_ref[...]
    s = c * jax.nn.sigmoid(c)
    o_ref[0] = jnp.dot(s, w_ref[0], preferred_element_type=F32,
                       precision=lax.Precision.HIGHEST) + b_ref[0]


def _modulation(cond, w_mod, b_mod):
    nb = 4
    cols = 6 * D_MODEL // nb
    return pl.pallas_call(
        _mod_kernel,
        grid=(DEPTH, nb),
        in_specs=[pl.BlockSpec((SUBLANES, D_MODEL), lambda l, j: (0, 0)),
                  pl.BlockSpec((1, D_MODEL, cols), lambda l, j: (l, 0, j)),
                  pl.BlockSpec((1, 1, cols), lambda l, j: (l, 0, j))],
        out_specs=pl.BlockSpec((1, SUBLANES, cols), lambda l, j: (l, 0, j)),
        out_shape=jax.ShapeDtypeStruct((DEPTH, SUBLANES, 6 * D_MODEL), F32),
        name="modulation",
        compiler_params=_params("arbitrary", "arbitrary"),
    )(cond, w_mod, b_mod.reshape(DEPTH, 1, 6 * D_MODEL))


def _in_proj_kernel(x_ref, mod_ref, g_ref, w_ref, o_ref):
    x = x_ref[...]
    y = x * lax.rsqrt(jnp.mean(x * x, axis=-1, keepdims=True) + EPS) * g_ref[...]
    h = y * (1.0 + mod_ref[0, 1:2, :]) + mod_ref[0, 0:1, :]
    o_ref[...] = _dot(h.astype(BF16), w_ref[...])


def _in_proj(x, mod, g, w, layout):
    t = x.shape[0]
    n = w.shape[1]
    return pl.pallas_call(
        _in_proj_kernel,
        grid=(t // ROW_BLOCK,),
        in_specs=[pl.BlockSpec((ROW_BLOCK, D_MODEL), lambda i: (i, 0)),
                  _mod_spec(ROW_BLOCK, layout),
                  pl.BlockSpec((1, D_MODEL), lambda i: (0, 0)),
                  pl.BlockSpec((D_MODEL, n), lambda i: (0, 0))],
        out_specs=pl.BlockSpec((ROW_BLOCK, n), lambda i: (i, 0)),
        out_shape=jax.ShapeDtypeStruct((t, n), F32),
        name="in_proj",
        compiler_params=_params("arbitrary"),
    )(x, mod, g, w)


def _conv_kernel(ac, gc, ap, gp, an, gn, w_ref, b_ref, lg_ref, lb_ref, o_ref, pad_ref, shift_ref, *,
                 ctx_blocks, seq_blocks):
    i = pl.program_id(0)
    is_ctx = i < ctx_blocks
    j = i % seq_blocks
    first = jnp.logical_or(is_ctx, j == 0)
    last = jnp.logical_or(is_ctx, j == seq_blocks - 1)
    yp = ap[...] * jax.nn.sigmoid(gp[...])
    yn = an[...] * jax.nn.sigmoid(gn[...])
    pad_ref[0:CONV_HALO, :] = jnp.where(first, 0.0, yp)
    pad_ref[CONV_HALO:CONV_HALO + SEQ_BLOCK, :] = ac[...] * jax.nn.sigmoid(gc[...])
    pad_ref[CONV_HALO + SEQ_BLOCK:, :] = jnp.where(last, 0.0, yn)
    off = CONV_HALO - CONV_K // 2
    acc = jnp.zeros((SEQ_BLOCK, GROUP_W), F32)
    for phase in range(SUBLANES):
        taps = [k for k in range(CONV_K) if (off + k) % SUBLANES == phase]
        span = max(off + k - phase for k in taps) + SEQ_BLOCK
        shift_ref[0:span, :] = pad_ref[phase:phase + span, :]
        for k in taps:
            start = off + k - phase
            acc = acc + shift_ref[start:start + SEQ_BLOCK, :] * w_ref[k:k + 1, :]
    y = _layer_norm(acc + b_ref[...], lg_ref[...], lb_ref[...])
    o_ref[...] = y * jax.nn.sigmoid(y)


def _conv_module(parts, w, b, lg, lb, ctx_tokens, lat_seq):
    t = parts.shape[0]
    nblk = t // SEQ_BLOCK
    hb = SEQ_BLOCK // CONV_HALO
    last_halo = t // CONV_HALO - 1
    vec = pl.BlockSpec((1, GROUP_W), lambda i: (0, 0))
    kern = functools.partial(_conv_kernel, ctx_blocks=ctx_tokens // SEQ_BLOCK, seq_blocks=lat_seq // SEQ_BLOCK)
    return pl.pallas_call(
        kern,
        grid=(nblk,),
        in_specs=[pl.BlockSpec((SEQ_BLOCK, GROUP_W), lambda i: (i, 0)),
                  pl.BlockSpec((SEQ_BLOCK, GROUP_W), lambda i: (i, 1)),
                  pl.BlockSpec((CONV_HALO, GROUP_W), lambda i: (jnp.maximum(i * hb - 1, 0), 0)),
                  pl.BlockSpec((CONV_HALO, GROUP_W), lambda i: (jnp.maximum(i * hb - 1, 0), 1)),
                  pl.BlockSpec((CONV_HALO, GROUP_W), lambda i: (jnp.minimum((i + 1) * hb, last_halo), 0)),
                  pl.BlockSpec((CONV_HALO, GROUP_W), lambda i: (jnp.minimum((i + 1) * hb, last_halo), 1)),
                  pl.BlockSpec((CONV_K, GROUP_W), lambda i: (0, 0)),
                  vec, vec, vec],
        out_specs=pl.BlockSpec((SEQ_BLOCK, GROUP_W), lambda i: (i, 0)),
        out_shape=jax.ShapeDtypeStruct((t, GROUP_W), F32),
        scratch_shapes=[pltpu.VMEM((SEQ_BLOCK + 2 * CONV_HALO, GROUP_W), F32),
                        pltpu.VMEM((SEQ_BLOCK + 2 * CONV_HALO, GROUP_W), F32)],
        name="conv_module",
        compiler_params=_params("arbitrary"),
    )(parts, parts, parts, parts, parts, parts, w, b, lg, lb)


def _gmlp_kernel(u_ref, v_ref, lg_ref, lb_ref, ws_ref, bias_ref, o_ref):
    u = jax.nn.gelu(u_ref[...])
    v = _layer_norm(jax.nn.gelu(v_ref[...]), lg_ref[...], lb_ref[...]).astype(BF16)
    lane_g = lax.broadcasted_iota(jnp.int32, (1, GROUP_W), 1) // (GROUP_W // N_HEADS)
    for ch in range(u_ref.shape[0] // CHUNK):
        rows = slice(ch * CHUNK, (ch + 1) * CHUNK)
        s = bias_ref[...]
        for g in range(N_HEADS):
            s = s + jnp.where(lane_g == g, _dot(ws_ref[g], v[rows]), 0.0)
        o_ref[rows, :] = u[rows] * s


def _gmlp(parts, lg, lb, ws, bias):
    t = parts.shape[0]
    vec = pl.BlockSpec((1, GROUP_W), lambda i: (0, 0))
    return pl.pallas_call(
        _gmlp_kernel,
        grid=(t // GMLP_BLOCK,),
        in_specs=[pl.BlockSpec((GMLP_BLOCK, GROUP_W), lambda i: (i, 8)),
                  pl.BlockSpec((GMLP_BLOCK, GROUP_W), lambda i: (i, 9)),
                  vec, vec,
                  pl.BlockSpec((N_HEADS, CHUNK, CHUNK), lambda i: (0, 0, 0)),
                  pl.BlockSpec((CHUNK, GROUP_W), lambda i: (0, 0))],
        out_specs=pl.BlockSpec((GMLP_BLOCK, GROUP_W), lambda i: (i, 0)),
        out_shape=jax.ShapeDtypeStruct((t, GROUP_W), F32),
        compiler_params=_params("arbitrary"),
        name="gmlp",
    )(parts, parts, lg, lb, ws, bias)


def _lane_group_id(width, group):
    return lax.broadcasted_iota(jnp.int32, (1, width), 1) // group


def _stack_masked(q, group, ids):
    lane = _lane_group_id(q.shape[-1], group)
    return jnp.concatenate([jnp.where(lane == g, q, jnp.zeros_like(q)) for g in ids], axis=0)


def _diff_lambda(dl_ref, lam_init):
    dl = dl_ref[...]
    a = jnp.sum(dl[0:1, :] * dl[1:2, :], axis=-1, keepdims=True)
    b = jnp.sum(dl[2:3, :] * dl[3:4, :], axis=-1, keepdims=True)
    return jnp.exp(a) - jnp.exp(b) + lam_init


def _ctx_attn_kernel(q_ref, k_ref, v_ref, dq_ref, dk_ref, dv_ref, qn_ref, kn_ref, dqn_ref, dkn_ref, sub_ref,
                     dl_ref, na_ref, df_ref, na_kv_ref, df_kv_ref, *, lam_init, seq):
    n = seq
    lane_h = _lane_group_id(GROUP_W, HEAD_DIM)
    lam = _diff_lambda(dl_ref, lam_init)

    def write_state(ref, b, keys, values):
        for h in range(N_HEADS):
            ref[b, 0, h] = keys[:, h * HEAD_DIM:(h + 1) * HEAD_DIM]
            ref[b, 1, h] = values[:, h * HEAD_DIM:(h + 1) * HEAD_DIM]

    def head_lanes(r):
        return functools.reduce(jnp.add, [jnp.where(lane_h == h, r[h * n:(h + 1) * n], 0.0) for h in range(N_HEADS)])

    for b in range(q_ref.shape[0] // n):
        rows = slice(b * n, (b + 1) * n)
        qn = _group_rms(q_ref[rows, :], qn_ref[...], HEAD_DIM)
        kn = _group_rms(k_ref[rows, :], kn_ref[...], HEAD_DIM)
        v = v_ref[rows, :]
        write_state(na_kv_ref, b, kn, v)
        qs = _stack_masked(qn.astype(BF16), HEAD_DIM, range(N_HEADS))
        s = _dot_nt(qs, kn.astype(BF16)) * (HEAD_DIM ** -0.5)
        e = jnp.exp(s - jnp.max(s, axis=-1, keepdims=True))
        r = _dot(e.astype(BF16), v.astype(BF16)) / jnp.sum(e, axis=-1, keepdims=True)
        na_ref[rows, :] = head_lanes(r)
        dq = _group_rms(dq_ref[rows, :], dqn_ref[...], DIFF_SUB)
        dk = _group_rms(dk_ref[rows, :], dkn_ref[...], DIFF_SUB)
        dv = dv_ref[rows, :]
        write_state(df_kv_ref, b, dk, dv)
        qs = _stack_masked(dq.astype(BF16), DIFF_SUB, range(2 * N_HEADS))
        s = _dot_nt(qs, dk.astype(BF16)) * (DIFF_SUB ** -0.5)
        e = jnp.exp(s - jnp.max(s, axis=-1, keepdims=True))
        p = e * (1.0 / jnp.sum(e, axis=-1, keepdims=True))
        a = jnp.concatenate([p[2 * h * n:(2 * h + 1) * n] - lam * p[(2 * h + 1) * n:(2 * h + 2) * n]
                             for h in range(N_HEADS)], axis=0)
        o = head_lanes(_dot(a.astype(BF16), dv.astype(BF16)))
        df_ref[rows, :] = _group_rms(o, sub_ref[...], HEAD_DIM) * (1.0 - lam_init)


def _ctx_attention(parts, batch, seq, qn, kn, dqn, dkn, sub, dl, lam_init):
    t = batch * seq
    per_step = CTX_SEQS_PER_STEP
    assert batch % per_step == 0
    vec = pl.BlockSpec((1, GROUP_W), lambda b: (0, 0))
    col = lambda c: pl.BlockSpec((per_step * seq, GROUP_W), lambda b, c=c: (b, c))
    out = jax.ShapeDtypeStruct((t, GROUP_W), F32)
    ob = pl.BlockSpec((per_step * seq, GROUP_W), lambda b: (b, 0))
    state = jax.ShapeDtypeStruct((batch, 2, N_HEADS, seq, HEAD_DIM), F32)
    sb = pl.BlockSpec((per_step, 2, N_HEADS, seq, HEAD_DIM), lambda b: (b, 0, 0, 0, 0))
    return pl.pallas_call(
        functools.partial(_ctx_attn_kernel, lam_init=lam_init, seq=seq),
        grid=(batch // per_step,),
        in_specs=[col(2), col(3), col(4), col(5), col(6), col(7), vec, vec, vec, vec, vec,
                  pl.BlockSpec((4, DIFF_SUB), lambda b: (0, 0))],
        out_specs=[ob, ob, sb, sb],
        out_shape=[out, out, state, state],
        compiler_params=_params("arbitrary"),
        name="ctx_attention",
    )(parts, parts, parts, parts, parts, parts, qn, kn, dqn, dkn, sub, dl)


def _lat_prep_kernel(q_ref, k_ref, v_ref, dq_ref, dk_ref, dv_ref, qn_ref, kn_ref, dqn_ref, dkn_ref,
                     cos_ref, sin_ref, oq, ok, ov, odq, odk, odv):
    oq[...] = _group_rms(q_ref[...], qn_ref[...], HEAD_DIM).astype(BF16)
    ok[...] = _group_rms(k_ref[...], kn_ref[...], HEAD_DIM).astype(BF16)
    ov[...] = v_ref[...].astype(BF16)
    odv[...] = dv_ref[...].astype(BF16)
    half = DIFF_SUB // 4
    lane = lax.broadcasted_iota(jnp.int32, (1, GROUP_W), 1)
    lower = (lane % (2 * half)) < half
    cos = cos_ref[...]
    sin = sin_ref[...]

    def rope(x):
        partner = jnp.where(lower, pltpu.roll(x, GROUP_W - half, 1), pltpu.roll(x, half, 1))
        return x * cos + partner * sin

    odq[...] = rope(_group_rms(dq_ref[...], dqn_ref[...], DIFF_SUB)).astype(BF16)
    odk[...] = rope(_group_rms(dk_ref[...], dkn_ref[...], DIFF_SUB)).astype(BF16)


def _rope_tables(seq):
    half = DIFF_SUB // 4
    t = jnp.arange(seq)
    rows = (t // GRID_W).astype(F32)
    cols = (t % GRID_W).astype(F32)
    inv = ROPE_BASE ** (-jnp.arange(half, dtype=F32) / half)
    ang_r = rows[:, None] * inv[None, :]
    ang_c = cols[:, None] * inv[None, :]
    cos32 = jnp.concatenate([jnp.cos(ang_r), jnp.cos(ang_r), jnp.cos(ang_c), jnp.cos(ang_c)], axis=-1)
    sin32 = jnp.concatenate([-jnp.sin(ang_r), jnp.sin(ang_r), -jnp.sin(ang_c), jnp.sin(ang_c)], axis=-1)
    reps = GROUP_W // DIFF_SUB
    return jnp.tile(cos32, (1, reps)), jnp.tile(sin32, (1, reps))


def _lat_prep(parts, row0, batch, seq, qn, kn, dqn, dkn, cos, sin):
    t = batch * seq
    nb = seq // SEQ_BLOCK
    r0 = row0 // SEQ_BLOCK
    vec = pl.BlockSpec((1, GROUP_W), lambda i: (0, 0))
    col = lambda c: pl.BlockSpec((SEQ_BLOCK, GROUP_W), lambda i, c=c: (r0 + i, c))
    tab = pl.BlockSpec((SEQ_BLOCK, GROUP_W), lambda i: (i % nb, 0))
    out = jax.ShapeDtypeStruct((t, GROUP_W), BF16)
    ob = pl.BlockSpec((SEQ_BLOCK, GROUP_W), lambda i: (i, 0))
    return pl.pallas_call(
        _lat_prep_kernel,
        grid=(t // SEQ_BLOCK,),
        in_specs=[col(2), col(3), col(4), col(5), col(6), col(7), vec, vec, vec, vec, tab, tab],
        out_specs=[ob] * 6,
        out_shape=[out] * 6,
        compiler_params=_params("arbitrary"),
        name="lat_prep",
    )(parts, parts, parts, parts, parts, parts, qn, kn, dqn, dkn, cos, sin)


def _na_row_start(r, rows):
    return jnp.clip(r - NA_WIN_R // 2, 0, rows - NA_WIN_R)


def _na_lat_kernel(q_ref, k_ref, v_ref, ck_ref, cv_ref, bt_ref, o_ref, *, rows):
    n_loc = NA_WIN_R * GRID_W
    scale = HEAD_DIM ** -0.5
    lane_h = _lane_group_id(GROUP_W, HEAD_DIM)
    for i in range(q_ref.shape[0] // GRID_W):
        r = pl.program_id(1) * (q_ref.shape[0] // GRID_W) + i
        first_row = _na_row_start(r, rows)
        start = pl.multiple_of(first_row * GRID_W, GRID_W)
        kl = k_ref[pl.ds(start, n_loc), :]
        vl = v_ref[pl.ds(start, n_loc), :]
        qs = _stack_masked(q_ref[i * GRID_W:(i + 1) * GRID_W, :], HEAD_DIM, range(N_HEADS))
        s_loc = _dot_nt(qs, kl) * scale + bt_ref[r - first_row]
        s_ctx = _dot_nt(qs, ck_ref[0]) * scale
        m = jnp.maximum(jnp.max(s_loc, axis=-1, keepdims=True), jnp.max(s_ctx, axis=-1, keepdims=True))
        e_loc = jnp.exp(s_loc - m)
        e_ctx = jnp.exp(s_ctx - m)
        l = jnp.sum(e_loc, axis=-1, keepdims=True) + jnp.sum(e_ctx, axis=-1, keepdims=True)
        res = (_dot(e_loc.astype(BF16), vl) + _dot(e_ctx.astype(BF16), cv_ref[0])) / l
        o = jnp.zeros((GRID_W, GROUP_W), F32)
        for h in range(N_HEADS):
            o = o + jnp.where(lane_h == h, res[h * GRID_W:(h + 1) * GRID_W], 0.0)
        o_ref[i * GRID_W:(i + 1) * GRID_W, :] = o


def _na_bias_table(rel_bias, rows):
    wr = NA_WIN_R
    c = np.arange(GRID_W)
    c0 = np.clip(c - NA_WIN_C // 2, 0, GRID_W - NA_WIN_C)
    in_win = (c[None, :] >= c0[:, None]) & (c[None, :] < c0[:, None] + NA_WIN_C)
    dc_idx = np.clip(c[None, :] - c[:, None] + NA_WIN_C - 1, 0, 2 * NA_WIN_C - 2)
    off = np.arange(wr)
    dr_idx = (np.arange(wr)[None, :] - off[:, None]) + NA_WIN_R - 1
    rb = rel_bias[:, dr_idx]
    onehot = jnp.asarray(dc_idx[..., None] == np.arange(2 * NA_WIN_C - 1), F32)
    bias = jnp.einsum('howd,qkd->howqk', rb, onehot, precision=lax.Precision.HIGHEST)
    bias = jnp.where(jnp.asarray(in_win)[None, None, None], bias, NEG)
    bias = bias.transpose(1, 0, 3, 2, 4)
    return bias.reshape(wr, N_HEADS * GRID_W, wr * GRID_W)


def _na_latent(qn, kn, v, ck, cv, bias_tab, batch, seq):
    rows = seq // GRID_W
    nctx = ck.shape[1]
    full = pl.BlockSpec((seq, GROUP_W), lambda b, r: (b, 0))
    ctx = pl.BlockSpec((1, nctx, GROUP_W), lambda b, r: (b, 0, 0))
    steps = rows // NA_ROWS_PER_STEP
    qb = NA_ROWS_PER_STEP * GRID_W
    return pl.pallas_call(
        functools.partial(_na_lat_kernel, rows=rows),
        grid=(batch, steps),
        in_specs=[pl.BlockSpec((qb, GROUP_W), lambda b, r: (b * steps + r, 0)),
                  full, full, ctx, ctx,
                  pl.BlockSpec(bias_tab.shape, lambda b, r: (0, 0, 0))],
        out_specs=pl.BlockSpec((qb, GROUP_W), lambda b, r: (b * steps + r, 0)),
        out_shape=jax.ShapeDtypeStruct((batch * seq, GROUP_W), F32),
        compiler_params=_params("arbitrary", "arbitrary"),
        name="na_latent",
    )(qn, kn, v, ck, cv, bias_tab)


def _lane_fold(fn, acc, x):
    for g in range(x.shape[-1] // LANES):
        acc = fn(acc, x[:, g * LANES:(g + 1) * LANES])
    return acc


def _diff_lat_kernel(q_ref, k_ref, v_ref, ck_ref, cv_ref, sub_ref, dl_ref, o_ref, s_ref, sc_ref, *, lam_init):
    n = q_ref.shape[0]
    n_chunks, rows, chunk = s_ref.shape
    n_sub = rows // n
    lam = _diff_lambda(dl_ref, lam_init)
    qs = _stack_masked(q_ref[...], DIFF_SUB, range(n_sub))
    c2 = (DIFF_SUB ** -0.5) * math.log2(math.e)
    step = pl.program_id(1)
    top = jnp.full((rows, LANES), NEG, F32)
    for j in range(n_chunks):
        s = _dot(qs, k_ref[0, :, j * chunk:(j + 1) * chunk])
        s_ref[(j + step) % n_chunks] = s
        top = _lane_fold(jnp.maximum, top, s)
    s = _dot(qs, ck_ref[0])
    sc_ref[...] = s
    top = _lane_fold(jnp.maximum, top, s)
    m = jnp.max(top, axis=-1, keepdims=True) * c2
    part = jnp.zeros((rows, LANES), F32)
    half = rows // 2
    res = [jnp.zeros((half, GROUP_W), F32)] * 2
    products = []
    for j in range(n_chunks + 1):
        scores = s_ref[(j + step) % n_chunks] if j < n_chunks else sc_ref[...]
        values = v_ref[j * chunk:(j + 1) * chunk, :] if j < n_chunks else cv_ref[0]
        shift = m
        if j >= 2:
            bits = pltpu.bitcast(products[j - 2][:SUBLANES, :LANES], jnp.uint32)
            shift = m + pltpu.bitcast((bits >> 16) >> 16, F32)[:1, :1]
        e = jnp.exp2(scores * c2 - shift)
        part = _lane_fold(jnp.add, part, e)
        eb = e.astype(BF16)
        prods = [_dot(eb[r * half:(r + 1) * half], values) for r in range(2)]
        products.append(prods[0])
        res = [res[r] + prods[r] for r in range(2)]
    res = jnp.concatenate(res, axis=0) * (1.0 / jnp.sum(part, axis=-1, keepdims=True))
    lane_h = _lane_group_id(GROUP_W, HEAD_DIM)
    o = jnp.zeros((n, GROUP_W), F32)
    for h in range(N_HEADS):
        first, second = res[2 * h * n:(2 * h + 1) * n], res[(2 * h + 1) * n:(2 * h + 2) * n]
        o = o + jnp.where(lane_h == h, first - lam * second, 0.0)
    o_ref[...] = _group_rms(o, sub_ref[...], HEAD_DIM) * (1.0 - lam_init)


def _diff_latent(dq, dk_t, dv, ck_t, cv, sub, dl, batch, seq, lam_init):
    nq = seq // DIFF_QBLOCK
    nctx = cv.shape[1]
    full = pl.BlockSpec((seq, GROUP_W), lambda b, i: (b, 0))
    ctx = pl.BlockSpec((1, nctx, GROUP_W), lambda b, i: (b, 0, 0))
    return pl.pallas_call(
        functools.partial(_diff_lat_kernel, lam_init=lam_init),
        grid=(batch, nq),
        in_specs=[pl.BlockSpec((DIFF_QBLOCK, GROUP_W), lambda b, i: (b * nq + i, 0)),
                  pl.BlockSpec((1, GROUP_W, seq), lambda b, i: (b, 0, 0)), full,
                  pl.BlockSpec((1, GROUP_W, nctx), lambda b, i: (b, 0, 0)), ctx,
                  pl.BlockSpec((1, GROUP_W), lambda b, i: (0, 0)),
                  pl.BlockSpec((4, DIFF_SUB), lambda b, i: (0, 0))],
        out_specs=pl.BlockSpec((DIFF_QBLOCK, GROUP_W), lambda b, i: (b * nq + i, 0)),
        out_shape=jax.ShapeDtypeStruct((batch * seq, GROUP_W), F32),
        scratch_shapes=[pltpu.VMEM((seq // DIFF_KCHUNK, 2 * N_HEADS * DIFF_QBLOCK, DIFF_KCHUNK), F32),
                        pltpu.VMEM((2 * N_HEADS * DIFF_QBLOCK, nctx), F32)],
        compiler_params=_params("arbitrary", "arbitrary"),
        name="diff_latent",
    )(dq, dk_t, dv, ck_t, cv, sub, dl)


def _out_proj_kernel(conv_ref, na_c_ref, na_l_ref, df_c_ref, df_l_ref, gm_ref, x_ref, mod_ref, w_ref, o_ref, *,
                     ctx_blocks):
    is_ctx = pl.program_id(0) < ctx_blocks
    na = jnp.where(is_ctx, na_c_ref[...], na_l_ref[...])
    df = jnp.where(is_ctx, df_c_ref[...], df_l_ref[...])
    mixed = jnp.zeros(o_ref.shape, F32)
    for g, part in enumerate((conv_ref[...], na, df, gm_ref[...])):
        mixed = mixed + _dot(part.astype(BF16), w_ref[g * GROUP_W:(g + 1) * GROUP_W, :])
    o_ref[...] = x_ref[...] + mod_ref[0, 2:3, :] * mixed


def _out_proj(conv_o, na_c, na_l, df_c, df_l, gm_o, x, mod, w, layout):
    t = x.shape[0]
    ctx_blocks = layout[0] // ROW_BLOCK
    last_lat = (t - layout[0]) // ROW_BLOCK - 1
    part = pl.BlockSpec((ROW_BLOCK, GROUP_W), lambda i: (i, 0))
    ctx = pl.BlockSpec((ROW_BLOCK, GROUP_W), lambda i: (jnp.minimum(i, ctx_blocks - 1), 0))
    lat = pl.BlockSpec((ROW_BLOCK, GROUP_W), lambda i: (jnp.clip(i - ctx_blocks, 0, last_lat), 0))
    row = pl.BlockSpec((ROW_BLOCK, D_MODEL), lambda i: (i, 0))
    return pl.pallas_call(
        functools.partial(_out_proj_kernel, ctx_blocks=ctx_blocks),
        grid=(t // ROW_BLOCK,),
        in_specs=[part, ctx, lat, ctx, lat, part, row,
                  _mod_spec(ROW_BLOCK, layout),
                  pl.BlockSpec((D_MODEL, D_MODEL), lambda i: (0, 0))],
        out_specs=row,
        out_shape=jax.ShapeDtypeStruct((t, D_MODEL), F32),
        compiler_params=_params("arbitrary"),
        name="out_proj",
    )(conv_o, na_c, na_l, df_c, df_l, gm_o, x, mod, w)


def _sort_desc(vals):
    v = list(vals)
    n = len(v)
    k = 2
    while k <= n:
        j = k // 2
        while j >= 1:
            for i in range(n):
                l = i ^ j
                if l > i:
                    hi, lo = jnp.maximum(v[i], v[l]), jnp.minimum(v[i], v[l])
                    v[i], v[l] = (hi, lo) if (i & k) == 0 else (lo, hi)
            j //= 2
        k *= 2
    return v


def _merge_desc(v):
    v = list(v)
    n = len(v)
    j = n // 2
    while j >= 1:
        for i in range(n):
            l = i ^ j
            if l > i:
                v[i], v[l] = jnp.maximum(v[i], v[l]), jnp.minimum(v[i], v[l])
        j //= 2
    return v


def _top_half(a, b):
    n = len(a)
    return [jnp.maximum(a[i], b[n - 1 - i]) for i in range(n)]


def _top16_keys(s):
    k = PEER_TOPK
    v = _sort_desc([s[a * SUBLANES:(a + 1) * SUBLANES] for a in range(PEER_NKEYS // SUBLANES)])
    shift = SUBLANES // 2
    while shift >= 1:
        rolled = [pltpu.roll(x, shift, 0) for x in v]
        v = _merge_desc(_top_half(v, rolled))
        shift //= 2
    return v[:k]


def _paired_bf16(x):
    hi = pltpu.bitcast(x.astype(BF16).astype(F32), jnp.uint32)
    return hi | (hi >> 16)


def _route_kernel(x_ref, mod_ref, g_ref, wq_ref, sk_ref, h2_ref, n0_ref, r1_ref, e0_ref, e1_ref, s0_ref, s1_ref):
    x = x_ref[...]
    y = x * lax.rsqrt(jnp.mean(x * x, axis=-1, keepdims=True) + EPS) * g_ref[...]
    h2 = y * (1.0 + mod_ref[0, 4:5, :]) + mod_ref[0, 3:4, :]
    h2t = h2.T.astype(BF16)
    h2_ref[...] = h2t
    qt = _dot(wq_ref[...], h2t)
    for tile in range(x.shape[0] // LANES):
        _route_tile(qt[:, tile * LANES:(tile + 1) * LANES], tile, sk_ref, n0_ref, r1_ref, e0_ref, e1_ref,
                    s0_ref, s1_ref)


def _route_tile(qt, tile, sk_ref, n0_ref, r1_ref, e0_ref, e1_ref, s0_ref, s1_ref):
    tb = LANES
    sub = lax.broadcasted_iota(jnp.int32, (SUBLANES, tb), 0)
    k = PEER_TOPK
    tops = [[jnp.zeros((SUBLANES, tb), F32)] * k, [jnp.zeros((SUBLANES, tb), F32)] * k]
    for h in range(PEER_HEADS):
        for p in range(2):
            base = (2 * h + p) * PEER_NKEYS
            s = _dot(sk_ref[p], qt[base:base + PEER_NKEYS].astype(BF16))
            (s0_ref if p == 0 else s1_ref)[h] = s
            top = _top16_keys(s)
            tops[p] = [jnp.where(sub == h, top[a], tops[p][a]) for a in range(k)]
    cand = [tops[0][a] + tops[1][b] for a in range(k) for b in range(k) if (a + 1) * (b + 1) <= k]
    pad = [jnp.full((SUBLANES, tb), NEG, F32)] * (4 * k - len(cand))
    groups = [_sort_desc((cand + pad)[g * k:(g + 1) * k]) for g in range(4)]
    best = _top_half(_merge_desc(_top_half(groups[0], groups[1])), _merge_desc(_top_half(groups[2], groups[3])))
    thr = functools.reduce(jnp.minimum, best)
    m = tops[0][0] + tops[1][0]
    z = functools.reduce(jnp.add, [jnp.where(c >= thr, jnp.exp(c - m), 0.0) for c in cand])
    zinv = 1.0 / z
    counts, first = [], 0
    for a in range(k):
        n_b = k // (a + 1)
        counts.append(functools.reduce(jnp.add, [jnp.where(c >= thr, 1.0, 0.0) for c in cand[first:first + n_b]]))
        first += n_b
    for h in range(PEER_HEADS):
        s0 = s0_ref[h]
        s1 = s1_ref[h]
        n0 = jnp.zeros_like(s0)
        r1 = jnp.full_like(s1, float(k))
        for a in reversed(range(k)):
            n0 = jnp.where(s0 >= tops[0][a][h:h + 1], counts[a][h:h + 1], n0)
            r1 = jnp.where(s1 >= tops[1][a][h:h + 1], float(a), r1)
        n0_ref[h, tile] = _paired_bf16(n0)
        r1_ref[h, tile] = pltpu.bitcast(r1.astype(BF16), jnp.uint32)
        e0_ref[h, tile] = _paired_bf16(jnp.exp(s0 - tops[0][0][h:h + 1]) * zinv[h:h + 1])
        e1_ref[h, tile] = pltpu.bitcast(jnp.exp(s1 - tops[1][0][h:h + 1]).astype(BF16), jnp.uint32)


def _route(x1, mod, g, wq_t, sk, layout):
    t = x1.shape[0]
    tb = ROUTE_BLOCK
    tiles = t // LANES
    first_key = jax.ShapeDtypeStruct((PEER_HEADS, tiles, PEER_NKEYS, LANES), jnp.uint32)
    second_key = jax.ShapeDtypeStruct((PEER_HEADS, tiles, PEER_NKEYS // 2, LANES), jnp.uint32)
    fkb = pl.BlockSpec((PEER_HEADS, tb // LANES, PEER_NKEYS, LANES), lambda i: (0, i, 0, 0))
    skb = pl.BlockSpec((PEER_HEADS, tb // LANES, PEER_NKEYS // 2, LANES), lambda i: (0, i, 0, 0))
    scores = pltpu.VMEM((PEER_HEADS, PEER_NKEYS, LANES), F32)
    return pl.pallas_call(
        _route_kernel,
        grid=(t // tb,),
        in_specs=[pl.BlockSpec((tb, D_MODEL), lambda i: (i, 0)),
                  _mod_spec(tb, layout),
                  pl.BlockSpec((1, D_MODEL), lambda i: (0, 0)),
                  pl.BlockSpec(wq_t.shape, lambda i: (0, 0)),
                  pl.BlockSpec(sk.shape, lambda i: (0, 0, 0))],
        out_specs=[pl.BlockSpec((D_MODEL, tb), lambda i: (0, i)), fkb, skb, fkb, skb],
        out_shape=[jax.ShapeDtypeStruct((D_MODEL, t), BF16), first_key, second_key, first_key, second_key],
        scratch_shapes=[scores, scores],
        compiler_params=_params("arbitrary"),
        name="peer_route",
    )(x1, mod, g, wq_t, sk)


def _rows_as_bf16(rows):
    tile = pltpu.bitcast(rows, BF16)
    return jnp.concatenate([tile] * (PEER_NKEYS // tile.shape[0]), axis=0)


def _peer_kernel(h2_ref, pu_ref, pv_ref, pv_prev_ref, n0_ref, r1_ref, e0_ref, e1_ref, x_ref, mod_ref, o_ref,
                 acc_ref, a0_ref, a1_ref, w0_ref, w1_ref):
    e = pl.program_id(1)
    a_refs = (a0_ref, a1_ref)
    w_refs = (w0_ref, w1_ref)

    @pl.when(e == 0)
    def _():
        acc_ref[...] = jnp.zeros_like(acc_ref)
        for w_ref in w_refs:
            w_ref[...] = jnp.zeros_like(w_ref)

    tb = h2_ref.shape[1]
    n_sub = pv_ref.shape[0]
    sub_keys = MXU_DIM // PEER_NKEYS
    n_tiles = tb // LANES

    def first_matmul(k, slot):
        start = k * MXU_DIM if isinstance(k, int) else pl.multiple_of(k * MXU_DIM, MXU_DIM)
        parts = [_dot(pu_ref[pl.ds(start, MXU_DIM), j * MXU_DIM:(j + 1) * MXU_DIM],
                      h2_ref[j * MXU_DIM:(j + 1) * MXU_DIM, :]) for j in range(D_MODEL // MXU_DIM)]
        a = functools.reduce(jnp.add, parts)
        a_refs[slot][...] = pltpu.bitcast(_gelu_tanh(a).astype(BF16), jnp.uint32)
        return parts

    def second_matmul(values, slot):
        part = _dot(values, w_refs[slot][...])
        acc_ref[...] += part
        return part

    def gate(k, anchors):
        key = e * PEER_IBLOCK + sub_keys * k
        out = []
        for c in range(n_tiles):
            cols = slice(c * LANES, (c + 1) * LANES)
            words = pltpu.bitcast(anchors[c], jnp.uint32)
            gates = [_rows_as_bf16((words >> 16) >> 16)] * sub_keys
            for h in range(PEER_HEADS):
                r1 = pltpu.bitcast(r1_ref[h, c], BF16)
                e1 = pltpu.bitcast(e1_ref[h, c], BF16)
                for d in range(sub_keys):
                    row = pl.ds(key + d, SUBLANES, stride=0)
                    n0_d = _rows_as_bf16(n0_ref[h, c, row, :])
                    e0_d = _rows_as_bf16(e0_ref[h, c, row, :])
                    gates[d] = gates[d] + jnp.where(r1 < n0_d, e1 * e0_d, jnp.zeros_like(e1))
            out.append(gates)
        return out

    def write_weights(gates, slot):
        for c in range(n_tiles):
            cols = slice(c * LANES, (c + 1) * LANES)
            for d in range(sub_keys):
                rows = slice(d * PEER_NKEYS, (d + 1) * PEER_NKEYS)
                words = slice(d * PEER_NKEYS // 2, (d + 1) * PEER_NKEYS // 2)
                w_refs[slot][rows, cols] = pltpu.bitcast(a_refs[slot][words, cols], BF16) * gates[c][d]

    def trip(j, carry):
        tile = lambda x, row: x[row:row + SUBLANES, :LANES]
        assert n_tiles == 4
        anchors = []
        for slot in range(2):
            k = 2 * j + slot
            firsts = first_matmul(k, slot)
            previous = pv_prev_ref[slot] if isinstance(j, int) and j == 0 else pv_ref[k - 2]
            second = second_matmul(previous, slot)
            anchors.append([tile(firsts[0], 0), tile(firsts[2], 0), tile(second, 0), tile(second, D_MODEL // 2)])
        for slot in range(2):
            write_weights(gate(2 * j + slot, anchors[slot]), slot)
        return carry

    assert n_sub % 2 == 0
    trip(0, 0)
    lax.fori_loop(1, n_sub // 2, trip, 0)

    @pl.when(e == pl.num_programs(1) - 1)
    def _():
        for slot in range(2):
            second_matmul(pv_ref[n_sub - 2 + slot], slot)
        o_ref[...] = x_ref[...] + mod_ref[0, 5:6, :] * acc_ref[...].T


def _peer(h2t, pu, pv_t, n0, r1, e0, e1, x1, mod, layout):
    t = x1.shape[0]
    tb = PEER_TOKENS
    eb = PEER_IBLOCK * PEER_NKEYS
    n_exp = pu.shape[0]
    n_sub = eb // MXU_DIM
    assert pv_t.shape == (n_exp // MXU_DIM, D_MODEL, MXU_DIM)
    fk = pl.BlockSpec((PEER_HEADS, tb // LANES, PEER_NKEYS, LANES), lambda i, e: (0, i, 0, 0))
    sk = pl.BlockSpec((PEER_HEADS, tb // LANES, PEER_NKEYS // 2, LANES), lambda i, e: (0, i, 0, 0))
    return pl.pallas_call(
        _peer_kernel,
        grid=(t // tb, n_exp // eb),
        in_specs=[pl.BlockSpec((D_MODEL, tb), lambda i, e: (0, i)),
                  pl.BlockSpec((eb, D_MODEL), lambda i, e: (e, 0)),
                  pl.BlockSpec((n_sub, D_MODEL, MXU_DIM), lambda i, e: (e, 0, 0)),
                  pl.BlockSpec((2, D_MODEL, MXU_DIM), lambda i, e: (jnp.maximum(e * (n_sub // 2) - 1, 0), 0, 0)),
                  fk, sk, fk, sk,
                  pl.BlockSpec((tb, D_MODEL), lambda i, e: (i, 0)),
                  _mod_spec(tb, layout)],
        out_specs=pl.BlockSpec((tb, D_MODEL), lambda i, e: (i, 0)),
        out_shape=jax.ShapeDtypeStruct((t, D_MODEL), F32),
        scratch_shapes=[pltpu.VMEM((D_MODEL, tb), F32),
                        pltpu.VMEM((MXU_DIM // 2, tb), jnp.uint32), pltpu.VMEM((MXU_DIM // 2, tb), jnp.uint32),
                        pltpu.VMEM((MXU_DIM, tb), BF16), pltpu.VMEM((MXU_DIM, tb), BF16)],
        compiler_params=pltpu.CompilerParams(dimension_semantics=("arbitrary", "arbitrary"),
                                             vmem_limit_bytes=PEER_VMEM_LIMIT),
        name="peer_dense",
    )(h2t, pu, pv_t, pv_t, n0, r1, e0, e1, x1, mod)


def _merge_cache(kv):
    b, h, l, d = kv.shape
    return kv.transpose(0, 2, 1, 3).reshape(b, l, h * d).astype(BF16)


def kernel(x_prompt, x_sample, cache_na_kv, cache_diff_kv, c, c_ctx, w_mod, b_mod, norm1_g, norm2_g, w_in, conv_w, conv_b, conv_ln_g, conv_ln_b, na_qn_g, na_kn_g, na_rel_bias, diff_qn_g, diff_kn_g, diff_lambda, diff_subln_g, gmlp_ln_g, gmlp_ln_b, gmlp_ws, gmlp_bs, w_out, peer_wq, peer_sub_keys, peer_u, peer_v):
    batch, seq, _ = x_prompt.shape
    dbatch, dseq, _ = x_sample.shape
    t_ctx = batch * seq
    layout = (t_ctx, dseq)
    x = jnp.concatenate([x_prompt.reshape(t_ctx, D_MODEL), x_sample.reshape(dbatch * dseq, D_MODEL)], axis=0)
    cond = jnp.concatenate([c_ctx[None, :], c, jnp.zeros((SUBLANES - 1 - dbatch, D_MODEL), F32)], axis=0)
    mod_all = _modulation(cond, w_mod, b_mod).reshape(DEPTH, SUBLANES, 6, D_MODEL)
    cos, sin = _rope_tables(dseq)
    row = lambda v: v.reshape(1, -1)
    tile = lambda v: jnp.tile(v, GROUP_W // v.shape[0]).reshape(1, GROUP_W)
    na_states, diff_states = [], []
    for l in range(DEPTH):
        lam_init = 0.8 - 0.6 * math.exp(-0.3 * l)
        mod = mod_all[l]
        parts = _in_proj(x, mod, row(norm1_g[l]), w_in[l].astype(BF16), layout)
        conv_o = _conv_module(parts, conv_w[l], row(conv_b[l]), row(conv_ln_g[l]), row(conv_ln_b[l]), t_ctx, dseq)
        gm_bias = jnp.repeat(gmlp_bs[l].T, GROUP_W // N_HEADS, axis=1)
        gm_o = _gmlp(parts, row(gmlp_ln_g[l]), row(gmlp_ln_b[l]), gmlp_ws[l].astype(BF16), gm_bias)
        qn, kn = tile(na_qn_g[l]), tile(na_kn_g[l])
        dqn, dkn, sub = tile(diff_qn_g[l]), tile(diff_kn_g[l]), tile(diff_subln_g[l])
        na_c, df_c, na_kv, df_kv = _ctx_attention(parts, batch, seq, qn, kn, dqn, dkn, sub, diff_lambda[l], lam_init)
        lq, lk, lv, ldq, ldk, ldv = _lat_prep(parts, t_ctx, dbatch, dseq, qn, kn, dqn, dkn, cos, sin)
        na_l = _na_latent(lq, lk, lv, _merge_cache(cache_na_kv[:, l, 0]), _merge_cache(cache_na_kv[:, l, 1]),
                          _na_bias_table(na_rel_bias[l], dseq // GRID_W), dbatch, dseq)
        ldk_t = ldk.reshape(dbatch, dseq, GROUP_W).transpose(0, 2, 1)
        ck_t = _merge_cache(cache_diff_kv[:, l, 0]).transpose(0, 2, 1)
        df_l = _diff_latent(ldq, ldk_t, ldv, ck_t, _merge_cache(cache_diff_kv[:, l, 1]), sub, diff_lambda[l],
                            dbatch, dseq, lam_init)
        x1 = _out_proj(conv_o, na_c, na_l, df_c, df_l, gm_o, x, mod, w_out[l].astype(BF16), layout)
        h2t, n0, r1, e0, e1 = _route(x1, mod, row(norm2_g[l]), peer_wq[l].T.astype(BF16),
                                          peer_sub_keys[l].astype(BF16), layout)
        pv_t = peer_v[l].astype(BF16).reshape(-1, MXU_DIM, D_MODEL).transpose(0, 2, 1)
        x = _peer(h2t, peer_u[l].astype(BF16), pv_t, n0, r1, e0, e1, x1, mod, layout)
        na_states.append(na_kv)
        diff_states.append(df_kv)
    y_prompt = x[:t_ctx].reshape(batch, seq, D_MODEL)
    y_sample = x[t_ctx:].reshape(dbatch, dseq, D_MODEL)
    return (y_prompt, y_sample, jnp.stack(na_states, axis=1), jnp.stack(diff_states, axis=1))
```

```python
import functools
import math

import numpy as np
import jax
import jax.numpy as jnp
from jax import lax
from jax.experimental import pallas as pl
from jax.experimental.pallas import tpu as pltpu

F32 = jnp.float32
BF16 = jnp.bfloat16

D_MODEL = 1024
DEPTH = 2
GRID_W = 64
GROUP_W = 256
HEAD_DIM = 64
N_HEADS = 4
DIFF_SUB = 32
CONV_K = 31
CONV_HALO = 16
NA_WIN_R = 8
NA_WIN_C = 16
ROPE_BASE = 10000.0
CHUNK = 128
PEER_HEADS = 8
PEER_NKEYS = 128
PEER_TOPK = 16
EPS = 1e-6
NEG = -1e30

LANES = 128
SUBLANES = 8
MXU_DIM = 256
VMEM_LIMIT = 48 * 1024 * 1024

SEQ_BLOCK = 256
ROW_BLOCK = 512
GMLP_BLOCK = 512
ROUTE_BLOCK = 256
PEER_TOKENS = 512
PEER_IBLOCK = 16
PEER_VMEM_LIMIT = 56 * 1024 * 1024
CTX_SEQS_PER_STEP = 2
NA_ROWS_PER_STEP = 4
DIFF_QBLOCK = 128
DIFF_KCHUNK = 512


def _params(*sem):
    return pltpu.CompilerParams(dimension_semantics=sem, vmem_limit_bytes=VMEM_LIMIT)


def _dot(a, b):
    return jnp.dot(a, b, preferred_element_type=F32)


def _dot_nt(a, b):
    return lax.dot_general(a, b, (((1,), (1,)), ((), ())), preferred_element_type=F32)


def _split_dot(a, b):
    a1 = a.astype(BF16)
    r1 = a - a1.astype(F32)
    a2 = r1.astype(BF16)
    a3 = (r1 - a2.astype(F32)).astype(BF16)
    return _dot(a1, b) + _dot(a2, b) + _dot(a3, b)


def _group_ones(n, group):
    r = lax.broadcasted_iota(jnp.int32, (n, n), 0) // group
    c = lax.broadcasted_iota(jnp.int32, (n, n), 1) // group
    return (r == c).astype(BF16)


def _group_rms(x, gain, group):
    ssq = _split_dot(x * x, _group_ones(x.shape[-1], group))
    return x * lax.rsqrt(ssq * (1.0 / group) + EPS) * gain


def _gelu_tanh(x):
    k = -2.0 * math.sqrt(2.0 / math.pi) * math.log2(math.e)
    return x / (1.0 + jnp.exp2(x * (k + (k * 0.044715) * (x * x))))


def _layer_norm(y, g, b):
    mu = jnp.mean(y, axis=-1, keepdims=True)
    yc = y - mu
    return yc * lax.rsqrt(jnp.mean(yc * yc, axis=-1, keepdims=True) + EPS) * g + b


def _mod_spec(rows_per_block, layout, first_block=0):
    t_ctx, lat_seq = layout
    assert t_ctx % rows_per_block == 0 and lat_seq % rows_per_block == 0
    def index(i, *_):
        return (jnp.maximum(((i + first_block) * rows_per_block - t_ctx) // lat_seq + 1, 0), 0, 0)
    return pl.BlockSpec((1, 6, D_MODEL), index)


def _mod_kernel(c_ref, w_ref, b_ref, o_ref):
    c = c_ref[...]
    s = c * jax.nn.sigmoid(c)
    o_ref[0] = jnp.dot(s, w_ref[0], preferred_element_type=F32,
                       precision=lax.Precision.HIGHEST) + b_ref[0]


def _modulation(cond, w_mod, b_mod):
    nb = 4
    cols = 6 * D_MODEL // nb
    return pl.pallas_call(
        _mod_kernel,
        grid=(DEPTH, nb),
        in_specs=[pl.BlockSpec((SUBLANES, D_MODEL), lambda l, j: (0, 0)),
                  pl.BlockSpec((1, D_MODEL, cols), lambda l, j: (l, 0, j)),
                  pl.BlockSpec((1, 1, cols), lambda l, j: (l, 0, j))],
        out_specs=pl.BlockSpec((1, SUBLANES, cols), lambda l, j: (l, 0, j)),
        out_shape=jax.ShapeDtypeStruct((DEPTH, SUBLANES, 6 * D_MODEL), F32),
        name="modulation",
        compiler_params=_params("arbitrary", "arbitrary"),
    )(cond, w_mod, b_mod.reshape(DEPTH, 1, 6 * D_MODEL))


def _in_proj_kernel(x_ref, mod_ref, g_ref, w_ref, o_ref):
    x = x_ref[...]
    y = x * lax.rsqrt(jnp.mean(x * x, axis=-1, keepdims=True) + EPS) * g_ref[...]
    h = y * (1.0 + mod_ref[0, 1:2, :]) + mod_ref[0, 0:1, :]
    o_ref[...] = _dot(h.astype(BF16), w_ref[...])


def _in_proj(x, mod, g, w, layout):
    t = x.shape[0]
    n = w.shape[1]
    return pl.pallas_call(
        _in_proj_kernel,
        grid=(t // ROW_BLOCK,),
        in_specs=[pl.BlockSpec((ROW_BLOCK, D_MODEL), lambda i: (i, 0)),
                  _mod_spec(ROW_BLOCK, layout),
                  pl.BlockSpec((1, D_MODEL), lambda i: (0, 0)),
                  pl.BlockSpec((D_MODEL, n), lambda i: (0, 0))],
        out_specs=pl.BlockSpec((ROW_BLOCK, n), lambda i: (i, 0)),
        out_shape=jax.ShapeDtypeStruct((t, n), F32),
        name="in_proj",
        compiler_params=_params("arbitrary"),
    )(x, mod, g, w)


def _conv_kernel(ac, gc, ap, gp, an, gn, w_ref, b_ref, lg_ref, lb_ref, o_ref, pad_ref, shift_ref, *,
                 ctx_blocks, seq_blocks):
    i = pl.program_id(0)
    is_ctx = i < ctx_blocks
    j = i % seq_blocks
    first = jnp.logical_or(is_ctx, j == 0)
    last = jnp.logical_or(is_ctx, j == seq_blocks - 1)
    yp = ap[...] * jax.nn.sigmoid(gp[...])
    yn = an[...] * jax.nn.sigmoid(gn[...])
    pad_ref[0:CONV_HALO, :] = jnp.where(first, 0.0, yp)
    pad_ref[CONV_HALO:CONV_HALO + SEQ_BLOCK, :] = ac[...] * jax.nn.sigmoid(gc[...])
    pad_ref[CONV_HALO + SEQ_BLOCK:, :] = jnp.where(last, 0.0, yn)
    off = CONV_HALO - CONV_K // 2
    acc = jnp.zeros((SEQ_BLOCK, GROUP_W), F32)
    for phase in range(SUBLANES):
        taps = [k for k in range(CONV_K) if (off + k) % SUBLANES == phase]
        span = max(off + k - phase for k in taps) + SEQ_BLOCK
        shift_ref[0:span, :] = pad_ref[phase:phase + span, :]
        for k in taps:
            start = off + k - phase
            acc = acc + shift_ref[start:start + SEQ_BLOCK, :] * w_ref[k:k + 1, :]
    y = _layer_norm(acc + b_ref[...], lg_ref[...], lb_ref[...])
    o_ref[...] = y * jax.nn.sigmoid(y)


def _conv_module(parts, w, b, lg, lb, ctx_tokens, lat_seq):
    t = parts.shape[0]
    nblk = t // SEQ_BLOCK
    hb = SEQ_BLOCK // CONV_HALO
    last_halo = t // CONV_HALO - 1
    vec = pl.BlockSpec((1, GROUP_W), lambda i: (0, 0))
    kern = functools.partial(_conv_kernel, ctx_blocks=ctx_tokens // SEQ_BLOCK, seq_blocks=lat_seq // SEQ_BLOCK)
    return pl.pallas_call(
        kern,
        grid=(nblk,),
        in_specs=[pl.BlockSpec((SEQ_BLOCK, GROUP_W), lambda i: (i, 0)),
                  pl.BlockSpec((SEQ_BLOCK, GROUP_W), lambda i: (i, 1)),
                  pl.BlockSpec((CONV_HALO, GROUP_W), lambda i: (jnp.maximum(i * hb - 1, 0), 0)),
                  pl.BlockSpec((CONV_HALO, GROUP_W), lambda i: (jnp.maximum(i * hb - 1, 0), 1)),
                  pl.BlockSpec((CONV_HALO, GROUP_W), lambda i: (jnp.minimum((i + 1) * hb, last_halo), 0)),
                  pl.BlockSpec((CONV_HALO, GROUP_W), lambda i: (jnp.minimum((i + 1) * hb, last_halo), 1)),
                  pl.BlockSpec((CONV_K, GROUP_W), lambda i: (0, 0)),
                  vec, vec, vec],
        out_specs=pl.BlockSpec((SEQ_BLOCK, GROUP_W), lambda i: (i, 0)),
        out_shape=jax.ShapeDtypeStruct((t, GROUP_W), F32),
        scratch_shapes=[pltpu.VMEM((SEQ_BLOCK + 2 * CONV_HALO, GROUP_W), F32),
                        pltpu.VMEM((SEQ_BLOCK + 2 * CONV_HALO, GROUP_W), F32)],
        name="conv_module",
        compiler_params=_params("arbitrary"),
    )(parts, parts, parts, parts, parts, parts, w, b, lg, lb)


def _gmlp_kernel(u_ref, v_ref, lg_ref, lb_ref, ws_ref, bias_ref, o_ref):
    u = jax.nn.gelu(u_ref[...])
    v = _layer_norm(jax.nn.gelu(v_ref[...]), lg_ref[...], lb_ref[...]).astype(BF16)
    lane_g = lax.broadcasted_iota(jnp.int32, (1, GROUP_W), 1) // (GROUP_W // N_HEADS)
    for ch in range(u_ref.shape[0] // CHUNK):
        rows = slice(ch * CHUNK, (ch + 1) * CHUNK)
        s = bias_ref[...]
        for g in range(N_HEADS):
            s = s + jnp.where(lane_g == g, _dot(ws_ref[g], v[rows]), 0.0)
        o_ref[rows, :] = u[rows] * s


def _gmlp(parts, lg, lb, ws, bias):
    t = parts.shape[0]
    vec = pl.BlockSpec((1, GROUP_W), lambda i: (0, 0))
    return pl.pallas_call(
        _gmlp_kernel,
        grid=(t // GMLP_BLOCK,),
        in_specs=[pl.BlockSpec((GMLP_BLOCK, GROUP_W), lambda i: (i, 8)),
                  pl.BlockSpec((GMLP_BLOCK, GROUP_W), lambda i: (i, 9)),
                  vec, vec,
                  pl.BlockSpec((N_HEADS, CHUNK, CHUNK), lambda i: (0, 0, 0)),
                  pl.BlockSpec((CHUNK, GROUP_W), lambda i: (0, 0))],
        out_specs=pl.BlockSpec((GMLP_BLOCK, GROUP_W), lambda i: (i, 0)),
        out_shape=jax.ShapeDtypeStruct((t, GROUP_W), F32),
        compiler_params=_params("arbitrary"),
        name="gmlp",
    )(parts, parts, lg, lb, ws, bias)


def _lane_group_id(width, group):
    return lax.broadcasted_iota(jnp.int32, (1, width), 1) // group


def _stack_masked(q, group, ids):
    lane = _lane_group_id(q.shape[-1], group)
    return jnp.concatenate([jnp.where(lane == g, q, jnp.zeros_like(q)) for g in ids], axis=0)


def _diff_lambda(dl_ref, lam_init):
    dl = dl_ref[...]
    a = jnp.sum(dl[0:1, :] * dl[1:2, :], axis=-1, keepdims=True)
    b = jnp.sum(dl[2:3, :] * dl[3:4, :], axis=-1, keepdims=True)
    return jnp.exp(a) - jnp.exp(b) + lam_init


def _ctx_attn_kernel(q_ref, k_ref, v_ref, dq_ref, dk_ref, dv_ref, qn_ref, kn_ref, dqn_ref, dkn_ref, sub_ref,
                     dl_ref, na_ref, df_ref, na_kv_ref, df_kv_ref, *, lam_init, seq):
    n = seq
    lane_h = _lane_group_id(GROUP_W, HEAD_DIM)
    lam = _diff_lambda(dl_ref, lam_init)

    def write_state(ref, b, keys, values):
        for h in range(N_HEADS):
            ref[b, 0, h] = keys[:, h * HEAD_DIM:(h + 1) * HEAD_DIM]
            ref[b, 1, h] = values[:, h * HEAD_DIM:(h + 1) * HEAD_DIM]

    def head_lanes(r):
        return functools.reduce(jnp.add, [jnp.where(lane_h == h, r[h * n:(h + 1) * n], 0.0) for h in range(N_HEADS)])

    for b in range(q_ref.shape[0] // n):
        rows = slice(b * n, (b + 1) * n)
        qn = _group_rms(q_ref[rows, :], qn_ref[...], HEAD_DIM)
        kn = _group_rms(k_ref[rows, :], kn_ref[...], HEAD_DIM)
        v = v_ref[rows, :]
        write_state(na_kv_ref, b, kn, v)
        qs = _stack_masked(qn.astype(BF16), HEAD_DIM, range(N_HEADS))
        s = _dot_nt(qs, kn.astype(BF16)) * (HEAD_DIM ** -0.5)
        e = jnp.exp(s - jnp.max(s, axis=-1, keepdims=True))
        r = _dot(e.astype(BF16), v.astype(BF16)) / jnp.sum(e, axis=-1, keepdims=True)
        na_ref[rows, :] = head_lanes(r)
        dq = _group_rms(dq_ref[rows, :], dqn_ref[...], DIFF_SUB)
        dk = _group_rms(dk_ref[rows, :], dkn_ref[...], DIFF_SUB)
        dv = dv_ref[rows, :]
        write_state(df_kv_ref, b, dk, dv)
        qs = _stack_masked(dq.astype(BF16), DIFF_SUB, range(2 * N_HEADS))
        s = _dot_nt(qs, dk.astype(BF16)) * (DIFF_SUB ** -0.5)
        e = jnp.exp(s - jnp.max(s, axis=-1, keepdims=True))
        p = e * (1.0 / jnp.sum(e, axis=-1, keepdims=True))
        a = jnp.concatenate([p[2 * h * n:(2 * h + 1) * n] - lam * p[(2 * h + 1) * n:(2 * h + 2) * n]
                             for h in range(N_HEADS)], axis=0)
        o = head_lanes(_dot(a.astype(BF16), dv.astype(BF16)))
        df_ref[rows, :] = _group_rms(o, sub_ref[...], HEAD_DIM) * (1.0 - lam_init)


def _ctx_attention(parts, batch, seq, qn, kn, dqn, dkn, sub, dl, lam_init):
    t = batch * seq
    per_step = CTX_SEQS_PER_STEP
    assert batch % per_step == 0
    vec = pl.BlockSpec((1, GROUP_W), lambda b: (0, 0))
    col = lambda c: pl.BlockSpec((per_step * seq, GROUP_W), lambda b, c=c: (b, c))
    out = jax.ShapeDtypeStruct((t, GROUP_W), F32)
    ob = pl.BlockSpec((per_step * seq, GROUP_W), lambda b: (b, 0))
    state = jax.ShapeDtypeStruct((batch, 2, N_HEADS, seq, HEAD_DIM), F32)
    sb = pl.BlockSpec((per_step, 2, N_HEADS, seq, HEAD_DIM), lambda b: (b, 0, 0, 0, 0))
    return pl.pallas_call(
        functools.partial(_ctx_attn_kernel, lam_init=lam_init, seq=seq),
        grid=(batch // per_step,),
        in_specs=[col(2), col(3), col(4), col(5), col(6), col(7), vec, vec, vec, vec, vec,
                  pl.BlockSpec((4, DIFF_SUB), lambda b: (0, 0))],
        out_specs=[ob, ob, sb, sb],
        out_shape=[out, out, state, state],
        compiler_params=_params("arbitrary"),
        name="ctx_attention",
    )(parts, parts, parts, parts, parts, parts, qn, kn, dqn, dkn, sub, dl)


def _lat_prep_kernel(q_ref, k_ref, v_ref, dq_ref, dk_ref, dv_ref, qn_ref, kn_ref, dqn_ref, dkn_ref,
                     cos_ref, sin_ref, oq, ok, ov, odq, odk, odv):
    oq[...] = _group_rms(q_ref[...], qn_ref[...], HEAD_DIM).astype(BF16)
    ok[...] = _group_rms(k_ref[...], kn_ref[...], HEAD_DIM).astype(BF16)
    ov[...] = v_ref[...].astype(BF16)
    odv[...] = dv_ref[...].astype(BF16)
    half = DIFF_SUB // 4
    lane = lax.broadcasted_iota(jnp.int32, (1, GROUP_W), 1)
    lower = (lane % (2 * half)) < half
    cos = cos_ref[...]
    sin = sin_ref[...]

    def rope(x):
        partner = jnp.where(lower, pltpu.roll(x, GROUP_W - half, 1), pltpu.roll(x, half, 1))
        return x * cos + partner * sin

    odq[...] = rope(_group_rms(dq_ref[...], dqn_ref[...], DIFF_SUB)).astype(BF16)
    odk[...] = rope(_group_rms(dk_ref[...], dkn_ref[...], DIFF_SUB)).astype(BF16)


def _rope_tables(seq):
    half = DIFF_SUB // 4
    t = jnp.arange(seq)
    rows = (t // GRID_W).astype(F32)
    cols = (t % GRID_W).astype(F32)
    inv = ROPE_BASE ** (-jnp.arange(half, dtype=F32) / half)
    ang_r = rows[:, None] * inv[None, :]
    ang_c = cols[:, None] * inv[None, :]
    cos32 = jnp.concatenate([jnp.cos(ang_r), jnp.cos(ang_r), jnp.cos(ang_c), jnp.cos(ang_c)], axis=-1)
    sin32 = jnp.concatenate([-jnp.sin(ang_r), jnp.sin(ang_r), -jnp.sin(ang_c), jnp.sin(ang_c)], axis=-1)
    reps = GROUP_W // DIFF_SUB
    return jnp.tile(cos32, (1, reps)), jnp.tile(sin32, (1, reps))


def _lat_prep(parts, row0, batch, seq, qn, kn, dqn, dkn, cos, sin):
    t = batch * seq
    nb = seq // SEQ_BLOCK
    r0 = row0 // SEQ_BLOCK
    vec = pl.BlockSpec((1, GROUP_W), lambda i: (0, 0))
    col = lambda c: pl.BlockSpec((SEQ_BLOCK, GROUP_W), lambda i, c=c: (r0 + i, c))
    tab = pl.BlockSpec((SEQ_BLOCK, GROUP_W), lambda i: (i % nb, 0))
    out = jax.ShapeDtypeStruct((t, GROUP_W), BF16)
    ob = pl.BlockSpec((SEQ_BLOCK, GROUP_W), lambda i: (i, 0))
    return pl.pallas_call(
        _lat_prep_kernel,
        grid=(t // SEQ_BLOCK,),
        in_specs=[col(2), col(3), col(4), col(5), col(6), col(7), vec, vec, vec, vec, tab, tab],
        out_specs=[ob] * 6,
        out_shape=[out] * 6,
        compiler_params=_params("arbitrary"),
        name="lat_prep",
    )(parts, parts, parts, parts, parts, parts, qn, kn, dqn, dkn, cos, sin)


def _na_row_start(r, rows):
    return jnp.clip(r - NA_WIN_R // 2, 0, rows - NA_WIN_R)


def _na_lat_kernel(q_ref, k_ref, v_ref, ck_ref, cv_ref, bt_ref, o_ref, *, rows):
    n_loc = NA_WIN_R * GRID_W
    scale = HEAD_DIM ** -0.5
    lane_h = _lane_group_id(GROUP_W, HEAD_DIM)
    for i in range(q_ref.shape[0] // GRID_W):
        r = pl.program_id(1) * (q_ref.shape[0] // GRID_W) + i
        first_row = _na_row_start(r, rows)
        start = pl.multiple_of(first_row * GRID_W, GRID_W)
        kl = k_ref[pl.ds(start, n_loc), :]
        vl = v_ref[pl.ds(start, n_loc), :]
        qs = _stack_masked(q_ref[i * GRID_W:(i + 1) * GRID_W, :], HEAD_DIM, range(N_HEADS))
        s_loc = _dot_nt(qs, kl) * scale + bt_ref[r - first_row]
        s_ctx = _dot_nt(qs, ck_ref[0]) * scale
        m = jnp.maximum(jnp.max(s_loc, axis=-1, keepdims=True), jnp.max(s_ctx, axis=-1, keepdims=True))
        e_loc = jnp.exp(s_loc - m)
        e_ctx = jnp.exp(s_ctx - m)
        l = jnp.sum(e_loc, axis=-1, keepdims=True) + jnp.sum(e_ctx, axis=-1, keepdims=True)
        res = (_dot(e_loc.astype(BF16), vl) + _dot(e_ctx.astype(BF16), cv_ref[0])) / l
        o = jnp.zeros((GRID_W, GROUP_W), F32)
        for h in range(N_HEADS):
            o = o + jnp.where(lane_h == h, res[h * GRID_W:(h + 1) * GRID_W], 0.0)
        o_ref[i * GRID_W:(i + 1) * GRID_W, :] = o


def _na_bias_table(rel_bias, rows):
    wr = NA_WIN_R
    c = np.arange(GRID_W)
    c0 = np.clip(c - NA_WIN_C // 2, 0, GRID_W - NA_WIN_C)
    in_win = (c[None, :] >= c0[:, None]) & (c[None, :] < c0[:, None] + NA_WIN_C)
    dc_idx = np.clip(c[None, :] - c[:, None] + NA_WIN_C - 1, 0, 2 * NA_WIN_C - 2)
    off = np.arange(wr)
    dr_idx = (np.arange(wr)[None, :] - off[:, None]) + NA_WIN_R - 1
    rb = rel_bias[:, dr_idx]
    onehot = jnp.asarray(dc_idx[..., None] == np.arange(2 * NA_WIN_C - 1), F32)
    bias = jnp.einsum('howd,qkd->howqk', rb, onehot, precision=lax.Precision.HIGHEST)
    bias = jnp.where(jnp.asarray(in_win)[None, None, None], bias, NEG)
    bias = bias.transpose(1, 0, 3, 2, 4)
    return bias.reshape(wr, N_HEADS * GRID_W, wr * GRID_W)


def _na_latent(qn, kn, v, ck, cv, bias_tab, batch, seq):
    rows = seq // GRID_W
    nctx = ck.shape[1]
    full = pl.BlockSpec((seq, GROUP_W), lambda b, r: (b, 0))
    ctx = pl.BlockSpec((1, nctx, GROUP_W), lambda b, r: (b, 0, 0))
    steps = rows // NA_ROWS_PER_STEP
    qb = NA_ROWS_PER_STEP * GRID_W
    return pl.pallas_call(
        functools.partial(_na_lat_kernel, rows=rows),
        grid=(batch, steps),
        in_specs=[pl.BlockSpec((qb, GROUP_W), lambda b, r: (b * steps + r, 0)),
                  full, full, ctx, ctx,
                  pl.BlockSpec(bias_tab.shape, lambda b, r: (0, 0, 0))],
        out_specs=pl.BlockSpec((qb, GROUP_W), lambda b, r: (b * steps + r, 0)),
        out_shape=jax.ShapeDtypeStruct((batch * seq, GROUP_W), F32),
        compiler_params=_params("arbitrary", "arbitrary"),
        name="na_latent",
    )(qn, kn, v, ck, cv, bias_tab)


def _lane_fold(fn, acc, x):
    for g in range(x.shape[-1] // LANES):
        acc = fn(acc, x[:, g * LANES:(g + 1) * LANES])
    return acc


def _diff_lat_kernel(q_ref, k_ref, v_ref, ck_ref, cv_ref, sub_ref, dl_ref, o_ref, s_ref, sc_ref, *, lam_init):
    n = q_ref.shape[0]
    n_chunks, rows, chunk = s_ref.shape
    n_sub = rows // n
    lam = _diff_lambda(dl_ref, lam_init)
    qs = _stack_masked(q_ref[...], DIFF_SUB, range(n_sub))
    c2 = (DIFF_SUB ** -0.5) * math.log2(math.e)
    step = pl.program_id(1)
    top = jnp.full((rows, LANES), NEG, F32)
    for j in range(n_chunks):
        s = _dot(qs, k_ref[0, :, j * chunk:(j + 1) * chunk])
        s_ref[(j + step) % n_chunks] = s
        top = _lane_fold(jnp.maximum, top, s)
    s = _dot(qs, ck_ref[0])
    sc_ref[...] = s
    top = _lane_fold(jnp.maximum, top, s)
    m = jnp.max(top, axis=-1, keepdims=True) * c2
    part = jnp.zeros((rows, LANES), F32)
    half = rows // 2
    res = [jnp.zeros((half, GROUP_W), F32)] * 2
    products = []
    for j in range(n_chunks + 1):
        scores = s_ref[(j + step) % n_chunks] if j < n_chunks else sc_ref[...]
        values = v_ref[j * chunk:(j + 1) * chunk, :] if j < n_chunks else cv_ref[0]
        shift = m
        if j >= 2:
            bits = pltpu.bitcast(products[j - 2][:SUBLANES, :LANES], jnp.uint32)
            shift = m + pltpu.bitcast((bits >> 16) >> 16, F32)[:1, :1]
        e = jnp.exp2(scores * c2 - shift)
        part = _lane_fold(jnp.add, part, e)
        eb = e.astype(BF16)
        prods = [_dot(eb[r * half:(r + 1) * half], values) for r in range(2)]
        products.append(prods[0])
        res = [res[r] + prods[r] for r in range(2)]
    res = jnp.concatenate(res, axis=0) * (1.0 / jnp.sum(part, axis=-1, keepdims=True))
    lane_h = _lane_group_id(GROUP_W, HEAD_DIM)
    o = jnp.zeros((n, GROUP_W), F32)
    for h in range(N_HEADS):
        first, second = res[2 * h * n:(2 * h + 1) * n], res[(2 * h + 1) * n:(2 * h + 2) * n]
        o = o + jnp.where(lane_h == h, first - lam * second, 0.0)
    o_ref[...] = _group_rms(o, sub_ref[...], HEAD_DIM) * (1.0 - lam_init)


def _diff_latent(dq, dk_t, dv, ck_t, cv, sub, dl, batch, seq, lam_init):
    nq = seq // DIFF_QBLOCK
    nctx = cv.shape[1]
    full = pl.BlockSpec((seq, GROUP_W), lambda b, i: (b, 0))
    ctx = pl.BlockSpec((1, nctx, GROUP_W), lambda b, i: (b, 0, 0))
    return pl.pallas_call(
        functools.partial(_diff_lat_kernel, lam_init=lam_init),
        grid=(batch, nq),
        in_specs=[pl.BlockSpec((DIFF_QBLOCK, GROUP_W), lambda b, i: (b * nq + i, 0)),
                  pl.BlockSpec((1, GROUP_W, seq), lambda b, i: (b, 0, 0)), full,
                  pl.BlockSpec((1, GROUP_W, nctx), lambda b, i: (b, 0, 0)), ctx,
                  pl.BlockSpec((1, GROUP_W), lambda b, i: (0, 0)),
                  pl.BlockSpec((4, DIFF_SUB), lambda b, i: (0, 0))],
        out_specs=pl.BlockSpec((DIFF_QBLOCK, GROUP_W), lambda b, i: (b * nq + i, 0)),
        out_shape=jax.ShapeDtypeStruct((batch * seq, GROUP_W), F32),
        scratch_shapes=[pltpu.VMEM((seq // DIFF_KCHUNK, 2 * N_HEADS * DIFF_QBLOCK, DIFF_KCHUNK), F32),
                        pltpu.VMEM((2 * N_HEADS * DIFF_QBLOCK, nctx), F32)],
        compiler_params=_params("arbitrary", "arbitrary"),
        name="diff_latent",
    )(dq, dk_t, dv, ck_t, cv, sub, dl)


def _out_proj_kernel(conv_ref, na_c_ref, na_l_ref, df_c_ref, df_l_ref, gm_ref, x_ref, mod_ref, w_ref, o_ref, *,
                     ctx_blocks):
    is_ctx = pl.program_id(0) < ctx_blocks
    na = jnp.where(is_ctx, na_c_ref[...], na_l_ref[...])
    df = jnp.where(is_ctx, df_c_ref[...], df_l_ref[...])
    mixed = jnp.zeros(o_ref.shape, F32)
    for g, part in enumerate((conv_ref[...], na, df, gm_ref[...])):
        mixed = mixed + _dot(part.astype(BF16), w_ref[g * GROUP_W:(g + 1) * GROUP_W, :])
    o_ref[...] = x_ref[...] + mod_ref[0, 2:3, :] * mixed


def _out_proj(conv_o, na_c, na_l, df_c, df_l, gm_o, x, mod, w, layout):
    t = x.shape[0]
    ctx_blocks = layout[0] // ROW_BLOCK
    last_lat = (t - layout[0]) // ROW_BLOCK - 1
    part = pl.BlockSpec((ROW_BLOCK, GROUP_W), lambda i: (i, 0))
    ctx = pl.BlockSpec((ROW_BLOCK, GROUP_W), lambda i: (jnp.minimum(i, ctx_blocks - 1), 0))
    lat = pl.BlockSpec((ROW_BLOCK, GROUP_W), lambda i: (jnp.clip(i - ctx_blocks, 0, last_lat), 0))
    row = pl.BlockSpec((ROW_BLOCK, D_MODEL), lambda i: (i, 0))
    return pl.pallas_call(
        functools.partial(_out_proj_kernel, ctx_blocks=ctx_blocks),
        grid=(t // ROW_BLOCK,),
        in_specs=[part, ctx, lat, ctx, lat, part, row,
                  _mod_spec(ROW_BLOCK, layout),
                  pl.BlockSpec((D_MODEL, D_MODEL), lambda i: (0, 0))],
        out_specs=row,
        out_shape=jax.ShapeDtypeStruct((t, D_MODEL), F32),
        compiler_params=_params("arbitrary"),
        name="out_proj",
    )(conv_o, na_c, na_l, df_c, df_l, gm_o, x, mod, w)


def _sort_desc(vals):
    v = list(vals)
    n = len(v)
    k = 2
    while k <= n:
        j = k // 2
        while j >= 1:
            for i in range(n):
                l = i ^ j
                if l > i:
                    hi, lo = jnp.maximum(v[i], v[l]), jnp.minimum(v[i], v[l])
                    v[i], v[l] = (hi, lo) if (i & k) == 0 else (lo, hi)
            j //= 2
        k *= 2
    return v


def _merge_desc(v):
    v = list(v)
    n = len(v)
    j = n // 2
    while j >= 1:
        for i in range(n):
            l = i ^ j
            if l > i:
                v[i], v[l] = jnp.maximum(v[i], v[l]), jnp.minimum(v[i], v[l])
        j //= 2
    return v


def _top_half(a, b):
    n = len(a)
    return [jnp.maximum(a[i], b[n - 1 - i]) for i in range(n)]


def _top16_keys(s):
    k = PEER_TOPK
    v = _sort_desc([s[a * SUBLANES:(a + 1) * SUBLANES] for a in range(PEER_NKEYS // SUBLANES)])
    shift = SUBLANES // 2
    while shift >= 1:
        rolled = [pltpu.roll(x, shift, 0) for x in v]
        v = _merge_desc(_top_half(v, rolled))
        shift //= 2
    return v[:k]


def _paired_bf16(x):
    hi = pltpu.bitcast(x.astype(BF16).astype(F32), jnp.uint32)
    return hi | (hi >> 16)


def _route_kernel(x_ref, mod_ref, g_ref, wq_ref, sk_ref, h2_ref, n0_ref, r1_ref, e0_ref, e1_ref, s0_ref, s1_ref):
    x = x_ref[...]
    y = x * lax.rsqrt(jnp.mean(x * x, axis=-1, keepdims=True) + EPS) * g_ref[...]
    h2 = y * (1.0 + mod_ref[0, 4:5, :]) + mod_ref[0, 3:4, :]
    h2t = h2.T.astype(BF16)
    h2_ref[...] = h2t
    qt = _dot(wq_ref[...], h2t)
    for tile in range(x.shape[0] // LANES):
        _route_tile(qt[:, tile * LANES:(tile + 1) * LANES], tile, sk_ref, n0_ref, r1_ref, e0_ref, e1_ref,
                    s0_ref, s1_ref)


def _route_tile(qt, tile, sk_ref, n0_ref, r1_ref, e0_ref, e1_ref, s0_ref, s1_ref):
    tb = LANES
    sub = lax.broadcasted_iota(jnp.int32, (SUBLANES, tb), 0)
    k = PEER_TOPK
    tops = [[jnp.zeros((SUBLANES, tb), F32)] * k, [jnp.zeros((SUBLANES, tb), F32)] * k]
    for h in range(PEER_HEADS):
        for p in range(2):
            base = (2 * h + p) * PEER_NKEYS
            s = _dot(sk_ref[p], qt[base:base + PEER_NKEYS].astype(BF16))
            (s0_ref if p == 0 else s1_ref)[h] = s
            top = _top16_keys(s)
            tops[p] = [jnp.where(sub == h, top[a], tops[p][a]) for a in range(k)]
    cand = [tops[0][a] + tops[1][b] for a in range(k) for b in range(k) if (a + 1) * (b + 1) <= k]
    pad = [jnp.full((SUBLANES, tb), NEG, F32)] * (4 * k - len(cand))
    groups = [_sort_desc((cand + pad)[g * k:(g + 1) * k]) for g in range(4)]
    best = _top_half(_merge_desc(_top_half(groups[0], groups[1])), _merge_desc(_top_half(groups[2], groups[3])))
    thr = functools.reduce(jnp.minimum, best)
    m = tops[0][0] + tops[1][0]
    z = functools.reduce(jnp.add, [jnp.where(c >= thr, jnp.exp(c - m), 0.0) for c in cand])
    zinv = 1.0 / z
    counts, first = [], 0
    for a in range(k):
        n_b = k // (a + 1)
        counts.append(functools.reduce(jnp.add, [jnp.where(c >= thr, 1.0, 0.0) for c in cand[first:first + n_b]]))
        first += n_b
    for h in range(PEER_HEADS):
        s0 = s0_ref[h]
        s1 = s1_ref[h]
        n0 = jnp.zeros_like(s0)
        r1 = jnp.full_like(s1, float(k))
        for a in reversed(range(k)):
            n0 = jnp.where(s0 >= tops[0][a][h:h + 1], counts[a][h:h + 1], n0)
            r1 = jnp.where(s1 >= tops[1][a][h:h + 1], float(a), r1)
        n0_ref[h, tile] = _paired_bf16(n0)
        r1_ref[h, tile] = pltpu.bitcast(r1.astype(BF16), jnp.uint32)
        e0_ref[h, tile] = _paired_bf16(jnp.exp(s0 - tops[0][0][h:h + 1]) * zinv[h:h + 1])
        e1_ref[h, tile] = pltpu.bitcast(jnp.exp(s1 - tops[1][0][h:h + 1]).astype(BF16), jnp.uint32)


def _route(x1, mod, g, wq_t, sk, layout):
    t = x1.shape[0]
    tb = ROUTE_BLOCK
    tiles = t // LANES
    first_key = jax.ShapeDtypeStruct((PEER_HEADS, tiles, PEER_NKEYS, LANES), jnp.uint32)
    second_key = jax.ShapeDtypeStruct((PEER_HEADS, tiles, PEER_NKEYS // 2, LANES), jnp.uint32)
    fkb = pl.BlockSpec((PEER_HEADS, tb // LANES, PEER_NKEYS, LANES), lambda i: (0, i, 0, 0))
    skb = pl.BlockSpec((PEER_HEADS, tb // LANES, PEER_NKEYS // 2, LANES), lambda i: (0, i, 0, 0))
    scores = pltpu.VMEM((PEER_HEADS, PEER_NKEYS, LANES), F32)
    return pl.pallas_call(
        _route_kernel,
        grid=(t // tb,),
        in_specs=[pl.BlockSpec((tb, D_MODEL), lambda i: (i, 0)),
                  _mod_spec(tb, layout),
                  pl.BlockSpec((1, D_MODEL), lambda i: (0, 0)),
                  pl.BlockSpec(wq_t.shape, lambda i: (0, 0)),
                  pl.BlockSpec(sk.shape, lambda i: (0, 0, 0))],
        out_specs=[pl.BlockSpec((D_MODEL, tb), lambda i: (0, i)), fkb, skb, fkb, skb],
        out_shape=[jax.ShapeDtypeStruct((D_MODEL, t), BF16), first_key, second_key, first_key, second_key],
        scratch_shapes=[scores, scores],
        compiler_params=_params("arbitrary"),
        name="peer_route",
    )(x1, mod, g, wq_t, sk)


def _rows_as_bf16(rows):
    tile = pltpu.bitcast(rows, BF16)
    return jnp.concatenate([tile] * (PEER_NKEYS // tile.shape[0]), axis=0)


def _peer_kernel(h2_ref, pu_ref, pv_ref, pv_prev_ref, n0_ref, r1_ref, e0_ref, e1_ref, x_ref, mod_ref, o_ref,
                 acc_ref, a0_ref, a1_ref, w0_ref, w1_ref):
    e = pl.program_id(1)
    a_refs = (a0_ref, a1_ref)
    w_refs = (w0_ref, w1_ref)

    @pl.when(e == 0)
    def _():
        acc_ref[...] = jnp.zeros_like(acc_ref)
        for w_ref in w_refs:
            w_ref[...] = jnp.zeros_like(w_ref)

    tb = h2_ref.shape[1]
    n_sub = pv_ref.shape[0]
    sub_keys = MXU_DIM // PEER_NKEYS
    n_tiles = tb // LANES

    def first_matmul(k, slot):
        start = k * MXU_DIM if isinstance(k, int) else pl.multiple_of(k * MXU_DIM, MXU_DIM)
        parts = [_dot(pu_ref[pl.ds(start, MXU_DIM), j * MXU_DIM:(j + 1) * MXU_DIM],
                      h2_ref[j * MXU_DIM:(j + 1) * MXU_DIM, :]) for j in range(D_MODEL // MXU_DIM)]
        a = functools.reduce(jnp.add, parts)
        a_refs[slot][...] = pltpu.bitcast(_gelu_tanh(a).astype(BF16), jnp.uint32)
        return parts

    def second_matmul(values, slot):
        part = _dot(values, w_refs[slot][...])
        acc_ref[...] += part
        return part

    def gate(k, anchors):
        key = e * PEER_IBLOCK + sub_keys * k
        out = []
        for c in range(n_tiles):
            cols = slice(c * LANES, (c + 1) * LANES)
            words = pltpu.bitcast(anchors[c], jnp.uint32)
            gates = [_rows_as_bf16((words >> 16) >> 16)] * sub_keys
            for h in range(PEER_HEADS):
                r1 = pltpu.bitcast(r1_ref[h, c], BF16)
                e1 = pltpu.bitcast(e1_ref[h, c], BF16)
                for d in range(sub_keys):
                    row = pl.ds(key + d, SUBLANES, stride=0)
                    n0_d = _rows_as_bf16(n0_ref[h, c, row, :])
                    e0_d = _rows_as_bf16(e0_ref[h, c, row, :])
                    gates[d] = gates[d] + jnp.where(r1 < n0_d, e1 * e0_d, jnp.zeros_like(e1))
            out.append(gates)
        return out

    def write_weights(gates, slot):
        for c in range(n_tiles):
            cols = slice(c * LANES, (c + 1) * LANES)
            for d in range(sub_keys):
                rows = slice(d * PEER_NKEYS, (d + 1) * PEER_NKEYS)
                words = slice(d * PEER_NKEYS // 2, (d + 1) * PEER_NKEYS // 2)
                w_refs[slot][rows, cols] = pltpu.bitcast(a_refs[slot][words, cols], BF16) * gates[c][d]

    def trip(j, carry):
        tile = lambda x, row: x[row:row + SUBLANES, :LANES]
        assert n_tiles == 4
        anchors = []
        for slot in range(2):
            k = 2 * j + slot
            firsts = first_matmul(k, slot)
            previous = pv_prev_ref[slot] if isinstance(j, int) and j == 0 else pv_ref[k - 2]
            second = second_matmul(previous, slot)
            anchors.append([tile(firsts[0], 0), tile(firsts[2], 0), tile(second, 0), tile(second, D_MODEL // 2)])
        for slot in range(2):
            write_weights(gate(2 * j + slot, anchors[slot]), slot)
        return carry

    assert n_sub % 2 == 0
    trip(0, 0)
    lax.fori_loop(1, n_sub // 2, trip, 0)

    @pl.when(e == pl.num_programs(1) - 1)
    def _():
        for slot in range(2):
            second_matmul(pv_ref[n_sub - 2 + slot], slot)
        o_ref[...] = x_ref[...] + mod_ref[0, 5:6, :] * acc_ref[...].T


def _peer(h2t, pu, pv_t, n0, r1, e0, e1, x1, mod, layout, token_rows=None):
    tb = PEER_TOKENS
    first_row, t = token_rows if token_rows is not None else (0, x1.shape[0])
    assert first_row % tb == 0 and t % tb == 0
    b0 = first_row // tb
    eb = PEER_IBLOCK * PEER_NKEYS
    n_exp = pu.shape[0]
    n_sub = eb // MXU_DIM
    assert pv_t.shape == (n_exp // MXU_DIM, D_MODEL, MXU_DIM)
    fk = pl.BlockSpec((PEER_HEADS, tb // LANES, PEER_NKEYS, LANES), lambda i, e: (0, i + b0, 0, 0))
    sk = pl.BlockSpec((PEER_HEADS, tb // LANES, PEER_NKEYS // 2, LANES), lambda i, e: (0, i + b0, 0, 0))
    return pl.pallas_call(
        _peer_kernel,
        grid=(t // tb, n_exp // eb),
        in_specs=[pl.BlockSpec((D_MODEL, tb), lambda i, e: (0, i + b0)),
                  pl.BlockSpec((eb, D_MODEL), lambda i, e: (e, 0)),
                  pl.BlockSpec((n_sub, D_MODEL, MXU_DIM), lambda i, e: (e, 0, 0)),
                  pl.BlockSpec((2, D_MODEL, MXU_DIM), lambda i, e: (jnp.maximum(e * (n_sub // 2) - 1, 0), 0, 0)),
                  fk, sk, fk, sk,
                  pl.BlockSpec((tb, D_MODEL), lambda i, e: (i + b0, 0)),
                  _mod_spec(tb, layout, b0)],
        out_specs=pl.BlockSpec((tb, D_MODEL), lambda i, e: (i, 0)),
        out_shape=jax.ShapeDtypeStruct((t, D_MODEL), F32),
        scratch_shapes=[pltpu.VMEM((D_MODEL, tb), F32),
                        pltpu.VMEM((MXU_DIM // 2, tb), jnp.uint32), pltpu.VMEM((MXU_DIM // 2, tb), jnp.uint32),
                        pltpu.VMEM((MXU_DIM, tb), BF16), pltpu.VMEM((MXU_DIM, tb), BF16)],
        compiler_params=pltpu.CompilerParams(dimension_semantics=("arbitrary", "arbitrary"),
                                             vmem_limit_bytes=PEER_VMEM_LIMIT),
        name="peer_dense",
    )(h2t, pu, pv_t, pv_t, n0, r1, e0, e1, x1, mod)


def _merge_cache(kv):
    b, h, l, d = kv.shape
    return kv.transpose(0, 2, 1, 3).reshape(b, l, h * d).astype(BF16)


def kernel(x_prompt, x_sample, cache_na_kv, cache_diff_kv, c, c_ctx, w_mod, b_mod, norm1_g, norm2_g, w_in, conv_w, conv_b, conv_ln_g, conv_ln_b, na_qn_g, na_kn_g, na_rel_bias, diff_qn_g, diff_kn_g, diff_lambda, diff_subln_g, gmlp_ln_g, gmlp_ln_b, gmlp_ws, gmlp_bs, w_out, peer_wq, peer_sub_keys, peer_u, peer_v):
    batch, seq, _ = x_prompt.shape
    dbatch, dseq, _ = x_sample.shape
    t_ctx = batch * seq
    layout = (t_ctx, dseq)
    x = jnp.concatenate([x_prompt.reshape(t_ctx, D_MODEL), x_sample.reshape(dbatch * dseq, D_MODEL)], axis=0)
    cond = jnp.concatenate([c_ctx[None, :], c, jnp.zeros((SUBLANES - 1 - dbatch, D_MODEL), F32)], axis=0)
    mod_all = _modulation(cond, w_mod, b_mod).reshape(DEPTH, SUBLANES, 6, D_MODEL)
    cos, sin = _rope_tables(dseq)
    row = lambda v: v.reshape(1, -1)
    tile = lambda v: jnp.tile(v, GROUP_W // v.shape[0]).reshape(1, GROUP_W)
    na_states, diff_states = [], []
    for l in range(DEPTH):
        lam_init = 0.8 - 0.6 * math.exp(-0.3 * l)
        mod = mod_all[l]
        parts = _in_proj(x, mod, row(norm1_g[l]), w_in[l].astype(BF16), layout)
        conv_o = _conv_module(parts, conv_w[l], row(conv_b[l]), row(conv_ln_g[l]), row(conv_ln_b[l]), t_ctx, dseq)
        gm_bias = jnp.repeat(gmlp_bs[l].T, GROUP_W // N_HEADS, axis=1)
        gm_o = _gmlp(parts, row(gmlp_ln_g[l]), row(gmlp_ln_b[l]), gmlp_ws[l].astype(BF16), gm_bias)
        qn, kn = tile(na_qn_g[l]), tile(na_kn_g[l])
        dqn, dkn, sub = tile(diff_qn_g[l]), tile(diff_kn_g[l]), tile(diff_subln_g[l])
        na_c, df_c, na_kv, df_kv = _ctx_attention(parts, batch, seq, qn, kn, dqn, dkn, sub, diff_lambda[l], lam_init)
        lq, lk, lv, ldq, ldk, ldv = _lat_prep(parts, t_ctx, dbatch, dseq, qn, kn, dqn, dkn, cos, sin)
        na_l = _na_latent(lq, lk, lv, _merge_cache(cache_na_kv[:, l, 0]), _merge_cache(cache_na_kv[:, l, 1]),
                          _na_bias_table(na_rel_bias[l], dseq // GRID_W), dbatch, dseq)
        ldk_t = ldk.reshape(dbatch, dseq, GROUP_W).transpose(0, 2, 1)
        ck_t = _merge_cache(cache_diff_kv[:, l, 0]).transpose(0, 2, 1)
        df_l = _diff_latent(ldq, ldk_t, ldv, ck_t, _merge_cache(cache_diff_kv[:, l, 1]), sub, diff_lambda[l],
                            dbatch, dseq, lam_init)
        x1 = _out_proj(conv_o, na_c, na_l, df_c, df_l, gm_o, x, mod, w_out[l].astype(BF16), layout)
        h2t, n0, r1, e0, e1 = _route(x1, mod, row(norm2_g[l]), peer_wq[l].T.astype(BF16),
                                          peer_sub_keys[l].astype(BF16), layout)
        pv_t = peer_v[l].astype(BF16).reshape(-1, MXU_DIM, D_MODEL).transpose(0, 2, 1)
        sweep = functools.partial(_peer, h2t, peer_u[l].astype(BF16), pv_t, n0, r1, e0, e1, x1, mod, layout)
        if l + 1 < DEPTH:
            x = sweep()
        else:
            y_ctx = sweep(token_rows=(0, t_ctx))
            y_lat = sweep(token_rows=(t_ctx, dbatch * dseq))
        na_states.append(na_kv)
        diff_states.append(df_kv)
    y_prompt = y_ctx.reshape(batch, seq, D_MODEL)
    y_sample = y_lat.reshape(dbatch, dseq, D_MODEL)
    return (y_prompt, y_sample, jnp.stack(na_states, axis=1), jnp.stack(diff_states, axis=1))
```
